```python
import math
import jax, jax.numpy as jnp
from jax import lax
import numpy as np

D_MODEL = 2048
BATCH = 4
SEQ = 2048
DEPTH = 1
DEC_BATCH = 32
DEC_SEQ = 8
PAST_LEN = 16384
PAGE_SIZE = 128

NSA_HEADS = 16
NSA_KV_HEADS = 2
HEAD_DIM = 64
NSA_WIDTH = NSA_HEADS * HEAD_DIM
KV_WIDTH = NSA_KV_HEADS * HEAD_DIM
N_PAGED = 4
N_KV = 6
SSM_WIDTH = D_MODEL - NSA_WIDTH
SSM_CH = 16
SSM_GROUPS = SSM_WIDTH // SSM_CH
SSM_STATE = 64
IN_WIDTH = NSA_WIDTH + N_KV * KV_WIDTH + 3 * NSA_HEADS + SSM_WIDTH
CMP_BLOCK = 32
CMP_STRIDE = 16
CMP_HIDDEN = 2 * HEAD_DIM
SEL_BLOCK = 64
SEL_TOP_N = 16
WINDOW = 512
Q_BLOCK = 64
FORCE_SCORE = 1e6
REL_BUCKETS = 32
REL_MAX_DIST = 4096
N_EXPERTS = 32
TOP_K = 4
D_FF = D_MODEL
SWIGLU_LIMIT = 7.0
SWIGLU_ALPHA = 1.702
RMS_EPS = 1e-5
NEG_INF = -1e30

kernel_name = 'nsa_s5_moe_hybrid_decode'


def rms_norm(x, g):
    xf = x.astype(jnp.float32)
    y = xf * lax.rsqrt(jnp.mean(xf * xf, axis=-1, keepdims=True) + RMS_EPS)
    return (y * g.astype(jnp.float32)).astype(x.dtype)


def masked_softmax(s, mask):
    s = jnp.where(mask, s.astype(jnp.float32), NEG_INF)
    m = jnp.max(s, axis=-1, keepdims=True)
    e = jnp.where(mask, jnp.exp(s - m), 0.0)
    return e / jnp.maximum(jnp.sum(e, axis=-1, keepdims=True), 1e-30)


def rel_bucket(dist):
    n = jnp.maximum(dist, 0)
    exact = REL_BUCKETS // 2
    nf = jnp.maximum(n, 1).astype(jnp.float32)
    large = exact + (jnp.log(nf / exact) / math.log(REL_MAX_DIST / exact) * (REL_BUCKETS - exact)).astype(jnp.int32)
    return jnp.where(n < exact, n, jnp.minimum(large, REL_BUCKETS - 1))


def compress_rows(x, pe, w1, b1, w2, b2, n_rows):
    b = x.shape[0]
    nc = (n_rows - CMP_BLOCK) // CMP_STRIDE + 1
    r = CMP_BLOCK // CMP_STRIDE
    nch = nc + r - 1
    chunks = x[:, :nch * CMP_STRIDE].reshape(b, nch, CMP_STRIDE, NSA_KV_HEADS, HEAD_DIM)
    part = jnp.einsum('bcjhd,rjdf->bcrhf', chunks, w1.reshape(r, CMP_STRIDE, HEAD_DIM, CMP_HIDDEN))
    h1 = jnp.einsum('ld,ldf->f', pe, w1) + b1
    for o in range(r):
        h1 = h1 + part[:, o:o + nc, o]
    return jnp.einsum('bchf,fd->bchd', jax.nn.gelu(h1), w2) + b2


def nsa_query_block(q, qpos, gates, kw, vw, kwpos, kc, vc, c_end, ks_b, vs_b, cover, rel_bias):
    b, nq = q.shape[:2]
    grp = NSA_HEADS // NSA_KV_HEADS
    qg = (q * HEAD_DIM ** -0.5).reshape(b, nq, NSA_KV_HEADS, grp, HEAD_DIM)
    rb = rel_bias.reshape(REL_BUCKETS, NSA_KV_HEADS, grp)
    hi = jnp.arange(NSA_KV_HEADS)[None, None, :, None]
    bi = jnp.arange(b)[:, None, None, None]
    dist_c = qpos[:, None] - c_end[None, :]
    bias_c = jnp.moveaxis(rb[rel_bucket(dist_c)], 1, -1)
    s_c = jnp.einsum('bqhgd,bchd->bqhgc', qg, kc) + bias_c
    p_c = masked_softmax(s_c, (dist_c >= 0)[:, None, None, :])
    o_c = jnp.einsum('bqhgc,bchd->bqhgd', p_c.astype(vc.dtype), vc)
    imp = jnp.einsum('bqhgc,cs->bqhs', p_c, cover)
    ns = ks_b.shape[1]
    blk = jnp.arange(ns)[None, :]
    cur = (qpos // SEL_BLOCK)[:, None]
    forced = ((blk == 0) | (blk == cur) | (blk == cur - 1))[:, None, :]
    valid = (blk * SEL_BLOCK <= qpos[:, None])[:, None, :]
    score = jnp.where(forced, FORCE_SCORE, jnp.where(valid, imp, -1.0))
    _, idx = lax.top_k(score, min(SEL_TOP_N, ns))
    k_s = ks_b[bi, idx, :, hi].reshape(b, nq, NSA_KV_HEADS, -1, HEAD_DIM)
    v_s = vs_b[bi, idx, :, hi].reshape(b, nq, NSA_KV_HEADS, -1, HEAD_DIM)
    s_pos = (idx[..., None] * SEL_BLOCK + jnp.arange(SEL_BLOCK)).reshape(b, nq, NSA_KV_HEADS, -1)
    dist_s = qpos[None, :, None, None] - s_pos
    bias_s = jnp.moveaxis(rb[rel_bucket(dist_s), hi], -1, 3)
    s_s = jnp.einsum('bqhgd,bqhkd->bqhgk', qg, k_s) + bias_s
    p_s = masked_softmax(s_s, (dist_s >= 0)[:, :, :, None, :])
    o_s = jnp.einsum('bqhgk,bqhkd->bqhgd', p_s.astype(v_s.dtype), v_s)
    dist_w = qpos[:, None] - kwpos[None, :]
    mask_w = (dist_w >= 0) & (dist_w < WINDOW) & (kwpos >= 0)[None, :]
    bias_w = jnp.moveaxis(rb[rel_bucket(dist_w)], 1, -1)
    s_w = jnp.einsum('bqhgd,bkhd->bqhgk', qg, kw) + bias_w
    p_w = masked_softmax(s_w, mask_w[:, None, None, :])
    o_w = jnp.einsum('bqhgk,bkhd->bqhgd', p_w.astype(vw.dtype), vw)
    g = gates.reshape(b, nq, NSA_KV_HEADS, grp, 3)
    o = g[..., 0:1] * o_c + g[..., 1:2] * o_s + g[..., 2:3] * o_w
    return o.reshape(b, nq, NSA_WIDTH)


def s5_mixer(u, h0_re, h0_im, lam_re, lam_im, log_dt, b_re, b_im, c_re, c_im, d_skip, w_glu, b_glu):
    bsz, t, _ = u.shape
    f32 = jnp.float32
    uf = u.astype(f32).reshape(bsz, t, SSM_GROUPS, SSM_CH)
    dt = jnp.exp(log_dt.astype(f32))[:, None]
    lr, li = lam_re.astype(f32), lam_im.astype(f32)
    mag = jnp.exp(lr * dt)
    a_re, a_im = mag * jnp.cos(li * dt), mag * jnp.sin(li * dt)
    den = lr * lr + li * li
    z_re = ((a_re - 1.0) * lr + a_im * li) / den
    z_im = (a_im * lr - (a_re - 1.0) * li) / den
    br, bim = b_re.astype(f32), b_im.astype(f32)
    bb_re = z_re[..., None] * br - z_im[..., None] * bim
    bb_im = z_re[..., None] * bim + z_im[..., None] * br
    x_re = jnp.einsum('btgc,gpc->btgp', uf, bb_re)
    x_im = jnp.einsum('btgc,gpc->btgp', uf, bb_im)
    ar = jnp.broadcast_to(a_re, x_re.shape)
    ai = jnp.broadcast_to(a_im, x_re.shape)

    def combine(e1, e2):
        a1r, a1i, b1r, b1i = e1
        a2r, a2i, b2r, b2i = e2
        return (a2r * a1r - a2i * a1i, a2r * a1i + a2i * a1r,
                a2r * b1r - a2i * b1i + b2r, a2r * b1i + a2i * b1r + b2i)

    pr, pim, sr, si = lax.associative_scan(combine, (ar, ai, x_re, x_im), axis=1)
    h0r = h0_re.astype(f32)[:, None]
    h0i = h0_im.astype(f32)[:, None]
    h_re = pr * h0r - pim * h0i + sr
    h_im = pr * h0i + pim * h0r + si
    y = jnp.einsum('btgp,gcp->btgc', h_re, c_re.astype(f32)) - jnp.einsum('btgp,gcp->btgc', h_im, c_im.astype(f32))
    y = jax.nn.gelu(y.reshape(bsz, t, SSM_WIDTH) + d_skip.astype(f32) * u.astype(f32))
    out = y * jax.nn.sigmoid(y @ w_glu.astype(f32) + b_glu.astype(f32))
    return out.astype(u.dtype), h_re[:, -1].astype(u.dtype), h_im[:, -1].astype(u.dtype)


def moe_ffn(x, router_w, router_b, w_gu, b_gu, w_down, b_down):
    n = x.shape[0]
    logits = (x @ router_w).astype(jnp.float32) + router_b.astype(jnp.float32)
    top_val, top_idx = lax.top_k(logits, TOP_K)
    gate = jax.nn.softmax(top_val, axis=-1)
    nk = n * TOP_K
    blk = max(8, min(256, nk // N_EXPERTS))
    n_blocks = -(-nk // blk) + N_EXPERTS
    flat_e = top_idx.reshape(nk)
    order = jnp.argsort(flat_e)
    sorted_e = flat_e[order]
    counts = jnp.bincount(flat_e, length=N_EXPERTS)
    padded = (counts + blk - 1) // blk * blk
    pad_end = jnp.cumsum(padded)
    pad_start = pad_end - padded
    grp_start = jnp.cumsum(counts) - counts
    dest_sorted = (pad_start[sorted_e] + jnp.arange(nk) - grp_start[sorted_e]).astype(jnp.int32)
    dest = jnp.zeros(nk, jnp.int32).at[order].set(dest_sorted)
    row_tok = jnp.zeros(n_blocks * blk, jnp.int32).at[dest_sorted].set((order // TOP_K).astype(jnp.int32))
    block_e = jnp.minimum(jnp.searchsorted(pad_end, jnp.arange(n_blocks) * blk, side='right'), N_EXPERTS - 1)
    xb = x[row_tok].reshape(n_blocks, blk, x.shape[-1])

    def expert_block(args):
        xe, e = args
        hgu = xe @ w_gu[e] + b_gu[e]
        hg = jnp.minimum(hgu[:, :D_FF], SWIGLU_LIMIT)
        hl = jnp.clip(hgu[:, D_FF:], -SWIGLU_LIMIT, SWIGLU_LIMIT)
        act = hg * jax.nn.sigmoid(SWIGLU_ALPHA * hg) * (hl + 1.0)
        return act @ w_down[e] + b_down[e]

    yb = lax.map(expert_block, (xb, block_e)).reshape(n_blocks * blk, -1)
    y = yb[dest].reshape(n, TOP_K, -1)
    return jnp.einsum('nk,nkd->nd', gate.astype(y.dtype), y)


def layer_forward(x, pos0, past_kv, win_buf, h0_re, h0_im, rel_bias, lw):
    b, t, _ = x.shape
    h = rms_norm(x, lw['norm_mix'])
    proj = h @ lw['w_in']
    o1 = NSA_WIDTH
    o2 = o1 + N_KV * KV_WIDTH
    o3 = o2 + 3 * NSA_HEADS
    q = proj[..., :o1].reshape(b, t, NSA_HEADS, HEAD_DIM)
    kv_new = proj[..., o1:o2].reshape(b, t, N_KV, NSA_KV_HEADS, HEAD_DIM)
    gates = jax.nn.sigmoid(proj[..., o2:o3].astype(jnp.float32)).reshape(b, t, NSA_HEADS, 3)
    u = proj[..., o3:]
    paged_new = kv_new[:, :, :N_PAGED]
    win_new = kv_new[:, :, N_PAGED:]
    qpos = pos0 + jnp.arange(t, dtype=jnp.int32)

    n_rows = pos0 + t
    ns = -(-n_rows // SEL_BLOCK)
    pad = jnp.zeros((b, ns * SEL_BLOCK - n_rows) + paged_new.shape[2:], paged_new.dtype)
    rows = jnp.concatenate(([] if past_kv is None else [past_kv]) + [paged_new, pad], axis=1)
    kc = compress_rows(rows[:, :, 0], lw['cmp_pe'][0], lw['cmp_w1'][0], lw['cmp_b1'][0], lw['cmp_w2'][0], lw['cmp_b2'][0], n_rows)
    vc = compress_rows(rows[:, :, 1], lw['cmp_pe'][1], lw['cmp_w1'][1], lw['cmp_b1'][1], lw['cmp_w2'][1], lw['cmp_b2'][1], n_rows)
    nc = kc.shape[1]
    c_start = jnp.arange(nc) * CMP_STRIDE
    c_end = c_start + CMP_BLOCK - 1
    s_start = jnp.arange(ns) * SEL_BLOCK
    cover = ((c_start[:, None] < s_start[None, :] + SEL_BLOCK) & (c_start[:, None] + CMP_BLOCK > s_start[None, :])).astype(jnp.float32)
    ks_b = rows[:, :, 2].reshape(b, ns, SEL_BLOCK, NSA_KV_HEADS, HEAD_DIM)
    vs_b = rows[:, :, 3].reshape(b, ns, SEL_BLOCK, NSA_KV_HEADS, HEAD_DIM)

    if win_buf is None:
        qb = min(Q_BLOCK, t)
        nb = t // qb
        wpad = jnp.concatenate([jnp.zeros((b, WINDOW) + win_new.shape[2:], win_new.dtype), win_new], axis=1)
        widx = jnp.arange(nb)[:, None] * qb + jnp.arange(qb + WINDOW)[None, :]
        w_blocks = jnp.moveaxis(wpad[:, widx], 1, 0)
        w_pos = widx - WINDOW
        w_hist = win_new
    else:
        qb, nb = t, 1
        w_hist = jnp.concatenate([win_buf, win_new], axis=1)
        w_blocks = w_hist[None]
        w_pos = (pos0 - win_buf.shape[1] + jnp.arange(w_hist.shape[1], dtype=jnp.int32))[None]
    new_win = w_hist[:, w_hist.shape[1] - min(WINDOW, n_rows):]

    def run(args):
        q_b, qpos_b, g_b, w_b, wpos_b = args
        return nsa_query_block(q_b, qpos_b, g_b, w_b[:, :, 0], w_b[:, :, 1], wpos_b,
                               kc, vc, c_end, ks_b, vs_b, cover, rel_bias)

    q_blocks = jnp.moveaxis(q.reshape(b, nb, qb, NSA_HEADS, HEAD_DIM), 1, 0)
    g_blocks = jnp.moveaxis(gates.reshape(b, nb, qb, NSA_HEADS, 3), 1, 0)
    attn = lax.map(run, (q_blocks, qpos.reshape(nb, qb), g_blocks, w_blocks, w_pos))
    attn = jnp.moveaxis(attn, 0, 1).reshape(b, t, NSA_WIDTH).astype(x.dtype)

    ssm, h_re, h_im = s5_mixer(u, h0_re, h0_im, lw['lam_re'], lw['lam_im'], lw['log_dt'], lw['b_re'], lw['b_im'],
                               lw['c_re'], lw['c_im'], lw['d_skip'], lw['w_glu'], lw['b_glu'])
    merged = jnp.concatenate([rms_norm(attn, lw['norm_attn_out']), rms_norm(ssm, lw['norm_ssm_out'])], axis=-1)
    x = x + merged @ lw['w_out']
    hm = rms_norm(x, lw['norm_ffn']).reshape(b * t, D_MODEL)
    x = x + moe_ffn(hm, lw['router_w'], lw['router_b'], lw['w_gu'], lw['b_gu'], lw['w_down'], lw['b_down']).reshape(b, t, D_MODEL).astype(x.dtype)
    return x, paged_new, new_win, h_re, h_im


def setup_inputs(seed: int = 0) -> dict:
    key = jax.random.key(seed)
    k = jax.random.split(key, 36)
    f32 = jnp.float32

    def nrm(i, shape, scale=1.0):
        return scale * jax.random.normal(k[i], shape, f32)

    n_pages = PAST_LEN // PAGE_SIZE
    n_pool = (DEC_BATCH * n_pages * 5) // 4
    win_len = min(WINDOW, PAST_LEN)
    page_table = jax.random.permutation(k[6], n_pool)[:DEC_BATCH * n_pages].reshape(DEC_BATCH, n_pages).astype(jnp.int32)
    nvec = jnp.arange(SSM_STATE, dtype=f32)
    return {
        'x_prompt': nrm(0, (BATCH, SEQ, D_MODEL)),
        'x_sample': nrm(1, (DEC_BATCH, DEC_SEQ, D_MODEL)),
        'cache_nsa_kv': nrm(2, (DEPTH, n_pool, PAGE_SIZE, N_PAGED, NSA_KV_HEADS, HEAD_DIM)),
        'cache_win_kv': nrm(3, (DEPTH, DEC_BATCH, win_len, 2, NSA_KV_HEADS, HEAD_DIM)),
        'state_ssm_re': nrm(4, (DEPTH, DEC_BATCH, SSM_GROUPS, SSM_STATE), 0.5),
        'state_ssm_im': nrm(5, (DEPTH, DEC_BATCH, SSM_GROUPS, SSM_STATE), 0.5),
        'page_table': page_table,
        'rel_bias': nrm(7, (REL_BUCKETS, NSA_HEADS), 0.5),
        'norm_mix': 1.0 + nrm(8, (DEPTH, D_MODEL), 0.01),
        'w_in': nrm(9, (DEPTH, D_MODEL, IN_WIDTH), D_MODEL ** -0.5),
        'cmp_pe': nrm(10, (DEPTH, 2, CMP_BLOCK, HEAD_DIM), 0.1),
        'cmp_w1': nrm(11, (DEPTH, 2, CMP_BLOCK, HEAD_DIM, CMP_HIDDEN), (CMP_BLOCK * HEAD_DIM) ** -0.5),
        'cmp_b1': nrm(12, (DEPTH, 2, CMP_HIDDEN), 0.01),
        'cmp_w2': nrm(13, (DEPTH, 2, CMP_HIDDEN, HEAD_DIM), CMP_HIDDEN ** -0.5),
        'cmp_b2': nrm(14, (DEPTH, 2, HEAD_DIM), 0.01),
        'ssm_lam_re': -0.5 + nrm(15, (DEPTH, SSM_GROUPS, SSM_STATE), 0.01),
        'ssm_lam_im': math.pi * nvec + nrm(16, (DEPTH, SSM_GROUPS, SSM_STATE), 0.01),
        'ssm_log_dt': jax.random.uniform(k[17], (DEPTH, SSM_GROUPS), f32, math.log(1e-3), math.log(1e-1)),
        'ssm_b_re': nrm(18, (DEPTH, SSM_GROUPS, SSM_STATE, SSM_CH), (2 * SSM_CH) ** -0.5),
        'ssm_b_im': nrm(19, (DEPTH, SSM_GROUPS, SSM_STATE, SSM_CH), (2 * SSM_CH) ** -0.5),
        'ssm_c_re': nrm(20, (DEPTH, SSM_GROUPS, SSM_CH, SSM_STATE), (2 * SSM_STATE) ** -0.5),
        'ssm_c_im': nrm(21, (DEPTH, SSM_GROUPS, SSM_CH, SSM_STATE), (2 * SSM_STATE) ** -0.5),
        'ssm_d': nrm(22, (DEPTH, SSM_WIDTH), 0.5),
        'ssm_w_glu': nrm(23, (DEPTH, SSM_WIDTH, SSM_WIDTH), SSM_WIDTH ** -0.5),
        'ssm_b_glu': nrm(24, (DEPTH, SSM_WIDTH), 0.01),
        'norm_attn_out': 1.0 + nrm(25, (DEPTH, NSA_WIDTH), 0.01),
        'norm_ssm_out': 1.0 + nrm(26, (DEPTH, SSM_WIDTH), 0.01),
        'w_out': nrm(27, (DEPTH, D_MODEL, D_MODEL), D_MODEL ** -0.5),
        'norm_ffn': 1.0 + nrm(28, (DEPTH, D_MODEL), 0.01),
        'router_w': nrm(29, (DEPTH, D_MODEL, N_EXPERTS), D_MODEL ** -0.5),
        'router_b': nrm(30, (DEPTH, N_EXPERTS), 0.01),
        'w_gu': nrm(31, (DEPTH, N_EXPERTS, D_MODEL, 2 * D_FF), D_MODEL ** -0.5),
        'b_gu': nrm(32, (DEPTH, N_EXPERTS, 2 * D_FF), 0.01),
        'w_down': nrm(33, (DEPTH, N_EXPERTS, D_FF, D_MODEL), D_FF ** -0.5),
        'b_down': nrm(34, (DEPTH, N_EXPERTS, D_MODEL), 0.01),
        'norm_final': 1.0 + nrm(35, (D_MODEL,), 0.01),
    }


def reference(x_prompt, x_sample, cache_nsa_kv, cache_win_kv, state_ssm_re, state_ssm_im, page_table, rel_bias,
              norm_mix, w_in, cmp_pe, cmp_w1, cmp_b1, cmp_w2, cmp_b2, ssm_lam_re, ssm_lam_im, ssm_log_dt,
              ssm_b_re, ssm_b_im, ssm_c_re, ssm_c_im, ssm_d, ssm_w_glu, ssm_b_glu, norm_attn_out, norm_ssm_out,
              w_out, norm_ffn, router_w, router_b, w_gu, b_gu, w_down, b_down, norm_final):
    n_seq, n_pages = page_table.shape
    past_len = n_pages * cache_nsa_kv.shape[2]
    xp, xs = x_prompt, x_sample
    kv_p, win_p, sre_p, sim_p = [], [], [], []
    kv_s, win_s, sre_s, sim_s = [], [], [], []
    for i in range(DEPTH):
        lw = dict(norm_mix=norm_mix[i], w_in=w_in[i], cmp_pe=cmp_pe[i], cmp_w1=cmp_w1[i], cmp_b1=cmp_b1[i],
                  cmp_w2=cmp_w2[i], cmp_b2=cmp_b2[i], lam_re=ssm_lam_re[i], lam_im=ssm_lam_im[i],
                  log_dt=ssm_log_dt[i], b_re=ssm_b_re[i], b_im=ssm_b_im[i], c_re=ssm_c_re[i], c_im=ssm_c_im[i],
                  d_skip=ssm_d[i], w_glu=ssm_w_glu[i], b_glu=ssm_b_glu[i], norm_attn_out=norm_attn_out[i],
                  norm_ssm_out=norm_ssm_out[i], w_out=w_out[i], norm_ffn=norm_ffn[i], router_w=router_w[i],
                  router_b=router_b[i], w_gu=w_gu[i], b_gu=b_gu[i], w_down=w_down[i], b_down=b_down[i])
        h0 = jnp.zeros((xp.shape[0], SSM_GROUPS, SSM_STATE), xp.dtype)
        xp, kv1, w1, r1, m1 = layer_forward(xp, 0, None, None, h0, h0, rel_bias, lw)
        past = cache_nsa_kv[i][page_table].reshape(n_seq, past_len, N_PAGED, NSA_KV_HEADS, HEAD_DIM)
        xs, kv2, w2, r2, m2 = layer_forward(xs, past_len, past, cache_win_kv[i], state_ssm_re[i], state_ssm_im[i], rel_bias, lw)
        kv_p.append(kv1); win_p.append(w1); sre_p.append(r1); sim_p.append(m1)
        kv_s.append(kv2); win_s.append(w2); sre_s.append(r2); sim_s.append(m2)
    y_prompt = rms_norm(xp, norm_final)
    y_sample = rms_norm(xs, norm_final)
    return (y_prompt, y_sample, jnp.stack(kv_p), jnp.stack(win_p), jnp.stack(sre_p), jnp.stack(sim_p),
            jnp.stack(kv_s), jnp.stack(win_s), jnp.stack(sre_s), jnp.stack(sim_s))
```

```python
import functools
import math
import jax, jax.numpy as jnp
from jax import lax
from jax.experimental import pallas as pl
from jax.experimental.pallas import tpu as pltpu

D_MODEL = 2048
DEPTH = 1
NSA_HEADS = 16
NSA_KV_HEADS = 2
HEAD_DIM = 64
NSA_WIDTH = NSA_HEADS * HEAD_DIM
KV_WIDTH = NSA_KV_HEADS * HEAD_DIM
N_PAGED = 4
N_KV = 6
SSM_WIDTH = D_MODEL - NSA_WIDTH
SSM_CH = 16
SSM_GROUPS = SSM_WIDTH // SSM_CH
SSM_STATE = 64
IN_WIDTH = NSA_WIDTH + N_KV * KV_WIDTH + 3 * NSA_HEADS + SSM_WIDTH
CMP_BLOCK = 32
CMP_STRIDE = 16
CMP_HIDDEN = 2 * HEAD_DIM
SEL_BLOCK = 64
SEL_TOP_N = 16
WINDOW = 512
Q_BLOCK = 64
FORCE_SCORE = 1e6
REL_BUCKETS = 32
REL_MAX_DIST = 4096
N_EXPERTS = 32
TOP_K = 4
D_FF = D_MODEL
SWIGLU_LIMIT = 7.0
SWIGLU_ALPHA = 1.702
RMS_EPS = 1e-5
NEG_INF = -1e30


def _mm_body(x_ref, w_ref, o_ref):
    o_ref[...] = jnp.dot(x_ref[...].astype(jnp.bfloat16), w_ref[...].astype(jnp.bfloat16),
                         preferred_element_type=jnp.float32)


def pallas_matmul(x, w, tm=256, tn=512):
    m, k = x.shape
    n = w.shape[1]
    tm = min(tm, m)
    tn = min(tn, n)
    return pl.pallas_call(
        _mm_body,
        grid=(m // tm, n // tn),
        in_specs=[pl.BlockSpec((tm, k), lambda i, j: (i, 0)),
                  pl.BlockSpec((k, tn), lambda i, j: (0, j))],
        out_specs=pl.BlockSpec((tm, tn), lambda i, j: (i, j)),
        out_shape=jax.ShapeDtypeStruct((m, n), jnp.float32),
        name="matmul",
    )(x, w)


def rms_norm(x, g):
    xf = x.astype(jnp.float32)
    y = xf * lax.rsqrt(jnp.mean(xf * xf, axis=-1, keepdims=True) + RMS_EPS)
    return (y * g.astype(jnp.float32)).astype(x.dtype)


def masked_softmax(s, mask):
    s = jnp.where(mask, s.astype(jnp.float32), NEG_INF)
    m = jnp.max(s, axis=-1, keepdims=True)
    e = jnp.where(mask, jnp.exp(s - m), 0.0)
    return e / jnp.maximum(jnp.sum(e, axis=-1, keepdims=True), 1e-30)


def rel_bucket(dist):
    n = jnp.maximum(dist, 0)
    exact = REL_BUCKETS // 2
    nf = jnp.maximum(n, 1).astype(jnp.float32)
    large = exact + (jnp.log(nf / exact) / math.log(REL_MAX_DIST / exact) * (REL_BUCKETS - exact)).astype(jnp.int32)
    return jnp.where(n < exact, n, jnp.minimum(large, REL_BUCKETS - 1))


def compress_rows(x, pe, w1, b1, w2, b2, n_rows):
    b = x.shape[0]
    nc = (n_rows - CMP_BLOCK) // CMP_STRIDE + 1
    r = CMP_BLOCK // CMP_STRIDE
    nch = nc + r - 1
    chunks = x[:, :nch * CMP_STRIDE].reshape(b, nch, CMP_STRIDE, NSA_KV_HEADS, HEAD_DIM)
    part = jnp.einsum('bcjhd,rjdf->bcrhf', chunks, w1.reshape(r, CMP_STRIDE, HEAD_DIM, CMP_HIDDEN))
    h1 = jnp.einsum('ld,ldf->f', pe, w1) + b1
    for o in range(r):
        h1 = h1 + part[:, o:o + nc, o]
    return jnp.einsum('bchf,fd->bchd', jax.nn.gelu(h1), w2) + b2


def nsa_query_block(q, qpos, gates, kw, vw, kwpos, kc, vc, c_end, ks_b, vs_b, cover, rel_bias):
    b, nq = q.shape[:2]
    grp = NSA_HEADS // NSA_KV_HEADS
    qg = (q * HEAD_DIM ** -0.5).reshape(b, nq, NSA_KV_HEADS, grp, HEAD_DIM)
    rb = rel_bias.reshape(REL_BUCKETS, NSA_KV_HEADS, grp)
    hi = jnp.arange(NSA_KV_HEADS)[None, None, :, None]
    bi = jnp.arange(b)[:, None, None, None]
    dist_c = qpos[:, None] - c_end[None, :]
    bias_c = jnp.moveaxis(rb[rel_bucket(dist_c)], 1, -1)
    s_c = jnp.einsum('bqhgd,bchd->bqhgc', qg, kc) + bias_c
    p_c = masked_softmax(s_c, (dist_c >= 0)[:, None, None, :])
    o_c = jnp.einsum('bqhgc,bchd->bqhgd', p_c.astype(vc.dtype), vc)
    imp = jnp.einsum('bqhgc,cs->bqhs', p_c, cover)
    ns = ks_b.shape[1]
    blk = jnp.arange(ns)[None, :]
    cur = (qpos // SEL_BLOCK)[:, None]
    forced = ((blk == 0) | (blk == cur) | (blk == cur - 1))[:, None, :]
    valid = (blk * SEL_BLOCK <= qpos[:, None])[:, None, :]
    score = jnp.where(forced, FORCE_SCORE, jnp.where(valid, imp, -1.0))
    _, idx = lax.top_k(score, min(SEL_TOP_N, ns))
    k_s = ks_b[bi, idx, :, hi].reshape(b, nq, NSA_KV_HEADS, -1, HEAD_DIM)
    v_s = vs_b[bi, idx, :, hi].reshape(b, nq, NSA_KV_HEADS, -1, HEAD_DIM)
    s_pos = (idx[..., None] * SEL_BLOCK + jnp.arange(SEL_BLOCK)).reshape(b, nq, NSA_KV_HEADS, -1)
    dist_s = qpos[None, :, None, None] - s_pos
    bias_s = jnp.moveaxis(rb[rel_bucket(dist_s), hi], -1, 3)
    s_s = jnp.einsum('bqhgd,bqhkd->bqhgk', qg, k_s) + bias_s
    p_s = masked_softmax(s_s, (dist_s >= 0)[:, :, :, None, :])
    o_s = jnp.einsum('bqhgk,bqhkd->bqhgd', p_s.astype(v_s.dtype), v_s)
    dist_w = qpos[:, None] - kwpos[None, :]
    mask_w = (dist_w >= 0) & (dist_w < WINDOW) & (kwpos >= 0)[None, :]
    bias_w = jnp.moveaxis(rb[rel_bucket(dist_w)], 1, -1)
    s_w = jnp.einsum('bqhgd,bkhd->bqhgk', qg, kw) + bias_w
    p_w = masked_softmax(s_w, mask_w[:, None, None, :])
    o_w = jnp.einsum('bqhgk,bkhd->bqhgd', p_w.astype(vw.dtype), vw)
    g = gates.reshape(b, nq, NSA_KV_HEADS, grp, 3)
    o = g[..., 0:1] * o_c + g[..., 1:2] * o_s + g[..., 2:3] * o_w
    return o.reshape(b, nq, NSA_WIDTH)


def s5_mixer(u, h0_re, h0_im, lam_re, lam_im, log_dt, b_re, b_im, c_re, c_im, d_skip, w_glu, b_glu):
    bsz, t, _ = u.shape
    f32 = jnp.float32
    uf = u.astype(f32).reshape(bsz, t, SSM_GROUPS, SSM_CH)
    dt = jnp.exp(log_dt.astype(f32))[:, None]
    lr, li = lam_re.astype(f32), lam_im.astype(f32)
    mag = jnp.exp(lr * dt)
    a_re, a_im = mag * jnp.cos(li * dt), mag * jnp.sin(li * dt)
    den = lr * lr + li * li
    z_re = ((a_re - 1.0) * lr + a_im * li) / den
    z_im = (a_im * lr - (a_re - 1.0) * li) / den
    br, bim = b_re.astype(f32), b_im.astype(f32)
    bb_re = z_re[..., None] * br - z_im[..., None] * bim
    bb_im = z_re[..., None] * bim + z_im[..., None] * br
    x_re = jnp.einsum('btgc,gpc->btgp', uf, bb_re)
    x_im = jnp.einsum('btgc,gpc->btgp', uf, bb_im)
    ar = jnp.broadcast_to(a_re, x_re.shape)
    ai = jnp.broadcast_to(a_im, x_re.shape)

    def combine(e1, e2):
        a1r, a1i, b1r, b1i = e1
        a2r, a2i, b2r, b2i = e2
        return (a2r * a1r - a2i * a1i, a2r * a1i + a2i * a1r,
                a2r * b1r - a2i * b1i + b2r, a2r * b1i + a2i * b1r + b2i)

    pr, pim, sr, si = lax.associative_scan(combine, (ar, ai, x_re, x_im), axis=1)
    h0r = h0_re.astype(f32)[:, None]
    h0i = h0_im.astype(f32)[:, None]
    h_re = pr * h0r - pim * h0i + sr
    h_im = pr * h0i + pim * h0r + si
    y = jnp.einsum('btgp,gcp->btgc', h_re, c_re.astype(f32)) - jnp.einsum('btgp,gcp->btgc', h_im, c_im.astype(f32))
    y = jax.nn.gelu(y.reshape(bsz, t, SSM_WIDTH) + d_skip.astype(f32) * u.astype(f32))
    out = y * jax.nn.sigmoid(y @ w_glu.astype(f32) + b_glu.astype(f32))
    return out.astype(u.dtype), h_re[:, -1].astype(u.dtype), h_im[:, -1].astype(u.dtype)


def moe_ffn(x, router_w, router_b, w_gu, b_gu, w_down, b_down):
    n = x.shape[0]
    logits = (x @ router_w).astype(jnp.float32) + router_b.astype(jnp.float32)
    top_val, top_idx = lax.top_k(logits, TOP_K)
    gate = jax.nn.softmax(top_val, axis=-1)
    nk = n * TOP_K
    blk = max(8, min(256, nk // N_EXPERTS))
    n_blocks = -(-nk // blk) + N_EXPERTS
    flat_e = top_idx.reshape(nk)
    order = jnp.argsort(flat_e)
    sorted_e = flat_e[order]
    counts = jnp.bincount(flat_e, length=N_EXPERTS)
    padded = (counts + blk - 1) // blk * blk
    pad_end = jnp.cumsum(padded)
    pad_start = pad_end - padded
    grp_start = jnp.cumsum(counts) - counts
    dest_sorted = (pad_start[sorted_e] + jnp.arange(nk) - grp_start[sorted_e]).astype(jnp.int32)
    dest = jnp.zeros(nk, jnp.int32).at[order].set(dest_sorted)
    row_tok = jnp.zeros(n_blocks * blk, jnp.int32).at[dest_sorted].set((order // TOP_K).astype(jnp.int32))
    block_e = jnp.minimum(jnp.searchsorted(pad_end, jnp.arange(n_blocks) * blk, side='right'), N_EXPERTS - 1)
    xb = x[row_tok].reshape(n_blocks, blk, x.shape[-1])

    def expert_block(args):
        xe, e = args
        hgu = xe @ w_gu[e] + b_gu[e]
        hg = jnp.minimum(hgu[:, :D_FF], SWIGLU_LIMIT)
        hl = jnp.clip(hgu[:, D_FF:], -SWIGLU_LIMIT, SWIGLU_LIMIT)
        act = hg * jax.nn.sigmoid(SWIGLU_ALPHA * hg) * (hl + 1.0)
        return act @ w_down[e] + b_down[e]

    yb = lax.map(expert_block, (xb, block_e)).reshape(n_blocks * blk, -1)
    y = yb[dest].reshape(n, TOP_K, -1)
    return jnp.einsum('nk,nkd->nd', gate.astype(y.dtype), y)


def layer_forward(x, pos0, past_kv, win_buf, h0_re, h0_im, rel_bias, lw):
    b, t, _ = x.shape
    h = rms_norm(x, lw['norm_mix'])
    proj = h @ lw['w_in']
    o1 = NSA_WIDTH
    o2 = o1 + N_KV * KV_WIDTH
    o3 = o2 + 3 * NSA_HEADS
    q = proj[..., :o1].reshape(b, t, NSA_HEADS, HEAD_DIM)
    kv_new = proj[..., o1:o2].reshape(b, t, N_KV, NSA_KV_HEADS, HEAD_DIM)
    gates = jax.nn.sigmoid(proj[..., o2:o3].astype(jnp.float32)).reshape(b, t, NSA_HEADS, 3)
    u = proj[..., o3:]
    paged_new = kv_new[:, :, :N_PAGED]
    win_new = kv_new[:, :, N_PAGED:]
    qpos = pos0 + jnp.arange(t, dtype=jnp.int32)

    n_rows = pos0 + t
    ns = -(-n_rows // SEL_BLOCK)
    pad = jnp.zeros((b, ns * SEL_BLOCK - n_rows) + paged_new.shape[2:], paged_new.dtype)
    rows = jnp.concatenate(([] if past_kv is None else [past_kv]) + [paged_new, pad], axis=1)
    kc = compress_rows(rows[:, :, 0], lw['cmp_pe'][0], lw['cmp_w1'][0], lw['cmp_b1'][0], lw['cmp_w2'][0], lw['cmp_b2'][0], n_rows)
    vc = compress_rows(rows[:, :, 1], lw['cmp_pe'][1], lw['cmp_w1'][1], lw['cmp_b1'][1], lw['cmp_w2'][1], lw['cmp_b2'][1], n_rows)
    nc = kc.shape[1]
    c_start = jnp.arange(nc) * CMP_STRIDE
    c_end = c_start + CMP_BLOCK - 1
    s_start = jnp.arange(ns) * SEL_BLOCK
    cover = ((c_start[:, None] < s_start[None, :] + SEL_BLOCK) & (c_start[:, None] + CMP_BLOCK > s_start[None, :])).astype(jnp.float32)
    ks_b = rows[:, :, 2].reshape(b, ns, SEL_BLOCK, NSA_KV_HEADS, HEAD_DIM)
    vs_b = rows[:, :, 3].reshape(b, ns, SEL_BLOCK, NSA_KV_HEADS, HEAD_DIM)

    if win_buf is None:
        qb = min(Q_BLOCK, t)
        nb = t // qb
        wpad = jnp.concatenate([jnp.zeros((b, WINDOW) + win_new.shape[2:], win_new.dtype), win_new], axis=1)
        widx = jnp.arange(nb)[:, None] * qb + jnp.arange(qb + WINDOW)[None, :]
        w_blocks = jnp.moveaxis(wpad[:, widx], 1, 0)
        w_pos = widx - WINDOW
        w_hist = win_new
    else:
        qb, nb = t, 1
        w_hist = jnp.concatenate([win_buf, win_new], axis=1)
        w_blocks = w_hist[None]
        w_pos = (pos0 - win_buf.shape[1] + jnp.arange(w_hist.shape[1], dtype=jnp.int32))[None]
    new_win = w_hist[:, w_hist.shape[1] - min(WINDOW, n_rows):]

    def run(args):
        q_b, qpos_b, g_b, w_b, wpos_b = args
        return nsa_query_block(q_b, qpos_b, g_b, w_b[:, :, 0], w_b[:, :, 1], wpos_b,
                               kc, vc, c_end, ks_b, vs_b, cover, rel_bias)

    q_blocks = jnp.moveaxis(q.reshape(b, nb, qb, NSA_HEADS, HEAD_DIM), 1, 0)
    g_blocks = jnp.moveaxis(gates.reshape(b, nb, qb, NSA_HEADS, 3), 1, 0)
    attn = lax.map(run, (q_blocks, qpos.reshape(nb, qb), g_blocks, w_blocks, w_pos))
    attn = jnp.moveaxis(attn, 0, 1).reshape(b, t, NSA_WIDTH).astype(x.dtype)

    ssm, h_re, h_im = s5_mixer(u, h0_re, h0_im, lw['lam_re'], lw['lam_im'], lw['log_dt'], lw['b_re'], lw['b_im'],
                               lw['c_re'], lw['c_im'], lw['d_skip'], lw['w_glu'], lw['b_glu'])
    merged = jnp.concatenate([rms_norm(attn, lw['norm_attn_out']), rms_norm(ssm, lw['norm_ssm_out'])], axis=-1)
    x = x + pallas_matmul(merged.reshape(b * t, D_MODEL), lw['w_out']).reshape(b, t, D_MODEL)
    hm = rms_norm(x, lw['norm_ffn']).reshape(b * t, D_MODEL)
    x = x + moe_ffn(hm, lw['router_w'], lw['router_b'], lw['w_gu'], lw['b_gu'], lw['w_down'], lw['b_down']).reshape(b, t, D_MODEL).astype(x.dtype)
    return x, paged_new, new_win, h_re, h_im


def kernel(x_prompt, x_sample, cache_nsa_kv, cache_win_kv, state_ssm_re, state_ssm_im, page_table, rel_bias,
           norm_mix, w_in, cmp_pe, cmp_w1, cmp_b1, cmp_w2, cmp_b2, ssm_lam_re, ssm_lam_im, ssm_log_dt,
           ssm_b_re, ssm_b_im, ssm_c_re, ssm_c_im, ssm_d, ssm_w_glu, ssm_b_glu, norm_attn_out, norm_ssm_out,
           w_out, norm_ffn, router_w, router_b, w_gu, b_gu, w_down, b_down, norm_final):
    n_seq, n_pages = page_table.shape
    past_len = n_pages * cache_nsa_kv.shape[2]
    xp, xs = x_prompt, x_sample
    kv_p, win_p, sre_p, sim_p = [], [], [], []
    kv_s, win_s, sre_s, sim_s = [], [], [], []
    for i in range(DEPTH):
        lw = dict(norm_mix=norm_mix[i], w_in=w_in[i], cmp_pe=cmp_pe[i], cmp_w1=cmp_w1[i], cmp_b1=cmp_b1[i],
                  cmp_w2=cmp_w2[i], cmp_b2=cmp_b2[i], lam_re=ssm_lam_re[i], lam_im=ssm_lam_im[i],
                  log_dt=ssm_log_dt[i], b_re=ssm_b_re[i], b_im=ssm_b_im[i], c_re=ssm_c_re[i], c_im=ssm_c_im[i],
                  d_skip=ssm_d[i], w_glu=ssm_w_glu[i], b_glu=ssm_b_glu[i], norm_attn_out=norm_attn_out[i],
                  norm_ssm_out=norm_ssm_out[i], w_out=w_out[i], norm_ffn=norm_ffn[i], router_w=router_w[i],
                  router_b=router_b[i], w_gu=w_gu[i], b_gu=b_gu[i], w_down=w_down[i], b_down=b_down[i])
        h0 = jnp.zeros((xp.shape[0], SSM_GROUPS, SSM_STATE), xp.dtype)
        xp, kv1, w1, r1, m1 = layer_forward(xp, 0, None, None, h0, h0, rel_bias, lw)
        past = cache_nsa_kv[i][page_table].reshape(n_seq, past_len, N_PAGED, NSA_KV_HEADS, HEAD_DIM)
        xs, kv2, w2, r2, m2 = layer_forward(xs, past_len, past, cache_win_kv[i], state_ssm_re[i], state_ssm_im[i], rel_bias, lw)
        kv_p.append(kv1); win_p.append(w1); sre_p.append(r1); sim_p.append(m1)
        kv_s.append(kv2); win_s.append(w2); sre_s.append(r2); sim_s.append(m2)
    y_prompt = rms_norm(xp, norm_final)
    y_sample = rms_norm(xs, norm_final)
    return (y_prompt, y_sample, jnp.stack(kv_p), jnp.stack(win_p), jnp.stack(sre_p), jnp.stack(sim_p),
            jnp.stack(kv_s), jnp.stack(win_s), jnp.stack(sre_s), jnp.stack(sim_s))
```

```python
import functools
import math
import jax, jax.numpy as jnp
from jax import lax
from jax.experimental import pallas as pl
from jax.experimental.pallas import tpu as pltpu

D_MODEL = 2048
DEPTH = 1
NSA_HEADS = 16
NSA_KV_HEADS = 2
HEAD_DIM = 64
NSA_WIDTH = NSA_HEADS * HEAD_DIM
KV_WIDTH = NSA_KV_HEADS * HEAD_DIM
N_PAGED = 4
N_KV = 6
SSM_WIDTH = D_MODEL - NSA_WIDTH
SSM_CH = 16
SSM_GROUPS = SSM_WIDTH // SSM_CH
SSM_STATE = 64
IN_WIDTH = NSA_WIDTH + N_KV * KV_WIDTH + 3 * NSA_HEADS + SSM_WIDTH
CMP_BLOCK = 32
CMP_STRIDE = 16
CMP_HIDDEN = 2 * HEAD_DIM
SEL_BLOCK = 64
SEL_TOP_N = 16
WINDOW = 512
Q_BLOCK = 64
FORCE_SCORE = 1e6
REL_BUCKETS = 32
REL_MAX_DIST = 4096
N_EXPERTS = 32
TOP_K = 4
D_FF = D_MODEL
SWIGLU_LIMIT = 7.0
SWIGLU_ALPHA = 1.702
RMS_EPS = 1e-5
NEG_INF = -1e30


def _mm_body(x_ref, w_ref, o_ref):
    o_ref[...] = jnp.dot(x_ref[...].astype(jnp.bfloat16), w_ref[...].astype(jnp.bfloat16),
                         preferred_element_type=jnp.float32)


def pallas_matmul(x, w, tm=256, tn=512):
    m, k = x.shape
    n = w.shape[1]
    tm = min(tm, m)
    tn = min(tn, n)
    return pl.pallas_call(
        _mm_body,
        grid=(m // tm, n // tn),
        in_specs=[pl.BlockSpec((tm, k), lambda i, j: (i, 0)),
                  pl.BlockSpec((k, tn), lambda i, j: (0, j))],
        out_specs=pl.BlockSpec((tm, tn), lambda i, j: (i, j)),
        out_shape=jax.ShapeDtypeStruct((m, n), jnp.float32),
        name="matmul",
    )(x, w)


GRP = NSA_HEADS // NSA_KV_HEADS
TQ = 128
TK = 128
LANES = 128
BUCKET_TABLE_LEN = 32768
VMEM_LIMIT = 56 * 1024 * 1024


def rel_bucket(dist):
    n = jnp.maximum(dist, 0)
    exact = REL_BUCKETS // 2
    nf = jnp.maximum(n, 1).astype(jnp.float32)
    large = exact + (jnp.log(nf / exact) / math.log(REL_MAX_DIST / exact) * (REL_BUCKETS - exact)).astype(jnp.int32)
    return jnp.where(n < exact, n, jnp.minimum(large, REL_BUCKETS - 1))


def bucket_thresholds():
    tab = rel_bucket(jnp.arange(BUCKET_TABLE_LEN, dtype=jnp.int32))
    return jnp.sum(tab[None, :] < jnp.arange(REL_BUCKETS, dtype=jnp.int32)[:, None], axis=1).astype(jnp.int32)


def _bias_table_body(thr_ref, rb_ref, d_ref, o_ref):
    n = jnp.maximum(d_ref[0], 0)
    for h in range(NSA_HEADS):
        val = jnp.full(n.shape, rb_ref[h], jnp.float32)
        for k in range(1, REL_BUCKETS):
            val = jnp.where(n >= thr_ref[k], rb_ref[k * NSA_HEADS + h], val)
        o_ref[h // GRP, 0, h % GRP] = val


def bias_tables(dist, thr, rel_bias):
    n, r, c = dist.shape
    return pl.pallas_call(
        _bias_table_body,
        grid_spec=pltpu.PrefetchScalarGridSpec(
            num_scalar_prefetch=2,
            grid=(n,),
            in_specs=[pl.BlockSpec((1, r, c), lambda i, *_: (i, 0, 0))],
            out_specs=pl.BlockSpec((NSA_KV_HEADS, 1, GRP, r, c), lambda i, *_: (0, i, 0, 0, 0)),
        ),
        out_shape=jax.ShapeDtypeStruct((NSA_KV_HEADS, n, GRP, r, c), jnp.float32),
        name="bias_tables",
    )(thr, rel_bias.reshape(-1), dist)


def _softmax_tile(s3, msk, m, l, acc, v):
    s3 = jnp.where(msk[None], s3, NEG_INF)
    m_new = jnp.maximum(m, jnp.max(s3, axis=-1, keepdims=True))
    p = jnp.where(msk[None], jnp.exp(s3 - m_new), 0.0)
    alpha = jnp.exp(m - m_new)
    l = alpha * l + jnp.sum(p, axis=-1, keepdims=True)
    pv = jnp.dot(p.reshape(GRP * TQ, p.shape[-1]).astype(jnp.bfloat16), v, preferred_element_type=jnp.float32)
    acc = alpha * acc + pv.reshape(GRP, TQ, HEAD_DIM)
    return m_new, l, acc


def _nsa_prompt_body(q_ref, g_ref, kc_ref, vc_ref, ks_ref, vs_ref, kw_ref, vw_ref, bt_ref, bc_ref, cov_ref, exp_ref,
                     o_ref, mask_ref, *, n_cmp, n_sel):
    f32, bf16 = jnp.float32, jnp.bfloat16
    qi = pl.program_id(2)
    nt_dims = (((1,), (1,)), ((), ()))
    q = (q_ref[0] * (HEAD_DIM ** -0.5)).reshape(GRP * TQ, HEAD_DIM).astype(bf16)
    qpos = qi * TQ + lax.broadcasted_iota(jnp.int32, (TQ, LANES), 0)
    lane = lax.broadcasted_iota(jnp.int32, (TQ, LANES), 1)

    ok_c = (qpos >= CMP_STRIDE * lane + (CMP_BLOCK - 1)) & (lane < n_cmp)
    s_c = lax.dot_general(q, kc_ref[0, 0].astype(bf16), nt_dims, preferred_element_type=f32) + bc_ref[0, 0].reshape(GRP * TQ, LANES)
    s_c = jnp.where(ok_c[None], s_c.reshape(GRP, TQ, LANES), NEG_INF)
    m_c = jnp.max(s_c, axis=-1, keepdims=True)
    e_c = jnp.where(ok_c[None], jnp.exp(s_c - m_c), 0.0)
    p_c = e_c / jnp.maximum(jnp.sum(e_c, axis=-1, keepdims=True), 1e-30)
    o_c = jnp.dot(p_c.reshape(GRP * TQ, LANES).astype(bf16), vc_ref[0, 0].astype(bf16),
                  preferred_element_type=f32).reshape(GRP, TQ, HEAD_DIM)

    imp = jnp.dot(jnp.sum(p_c, axis=0).astype(bf16), cov_ref[...], preferred_element_type=f32)
    cur = qpos // SEL_BLOCK
    forced = (lane == 0) | (lane == cur) | (lane == cur - 1)
    score = jnp.where(forced, FORCE_SCORE, jnp.where(lane * SEL_BLOCK <= qpos, imp, -1.0))
    score = jnp.where(lane < n_sel, score, -jnp.inf)
    sel = jnp.zeros((TQ, LANES), f32)
    for _ in range(min(SEL_TOP_N, n_sel)):
        best = jnp.max(score, axis=-1, keepdims=True)
        idx = jnp.min(jnp.where(score == best, lane, LANES), axis=-1, keepdims=True)
        hit = lane == idx
        sel = jnp.where(hit, 1.0, sel)
        score = jnp.where(hit, -jnp.inf, score)
    mask_ref[...] = jnp.dot(sel.astype(bf16), exp_ref[...], preferred_element_type=f32)

    row = lax.broadcasted_iota(jnp.int32, (TQ, TK), 0)
    col = lax.broadcasted_iota(jnp.int32, (TQ, TK), 1)
    init = (jnp.full((GRP, TQ, 1), NEG_INF, f32), jnp.zeros((GRP, TQ, 1), f32), jnp.zeros((GRP, TQ, HEAD_DIM), f32))

    def scores(k_ref, ki):
        k = k_ref[0, 0, pl.ds(pl.multiple_of(ki * TK, TK), TK), :].astype(bf16)
        s = lax.dot_general(q, k, nt_dims, preferred_element_type=f32)
        return s.reshape(GRP, TQ, TK) + bt_ref[0, qi - ki]

    def sel_step(ki, carry):
        dist = (qi - ki) * TQ + row - col
        msk = (mask_ref[:, pl.ds(pl.multiple_of(ki * TK, TK), TK)] > 0.5) & (dist >= 0)
        v = vs_ref[0, 0, pl.ds(pl.multiple_of(ki * TK, TK), TK), :].astype(bf16)
        return _softmax_tile(scores(ks_ref, ki), msk, *carry, v)

    def win_step(ki, carry):
        dist = (qi - ki) * TQ + row - col
        msk = (dist >= 0) & (dist < WINDOW)
        v = vw_ref[0, 0, pl.ds(pl.multiple_of(ki * TK, TK), TK), :].astype(bf16)
        return _softmax_tile(scores(kw_ref, ki), msk, *carry, v)

    _, l_s, acc_s = lax.fori_loop(0, qi + 1, sel_step, init)
    _, l_w, acc_w = lax.fori_loop(jnp.maximum(qi - WINDOW // TK, 0), qi + 1, win_step, init)
    o_s = acc_s / jnp.maximum(l_s, 1e-30)
    o_w = acc_w / jnp.maximum(l_w, 1e-30)
    g = jax.nn.sigmoid(g_ref[0])
    o_ref[0] = g[:, :, 0:1] * o_c + g[:, :, 1:2] * o_s + g[:, :, 2:3] * o_w


def nsa_prompt(q4, g4, kc, vc, ks, vs, kw, vw, thr, rel_bias, n_cmp):
    b, _, t, _ = q4.shape
    nq = t // TQ
    n_sel = t // SEL_BLOCK
    assert t % TQ == 0 and n_sel <= LANES and n_cmp <= LANES and TQ == TK
    i = jnp.arange(TQ, dtype=jnp.int32)
    d_toep = jnp.arange(nq, dtype=jnp.int32)[:, None, None] * TQ + i[:, None] - i[None, :]
    d_cmp = (jnp.arange(t, dtype=jnp.int32)[:, None] - (CMP_STRIDE * jnp.arange(LANES, dtype=jnp.int32)[None, :] + CMP_BLOCK - 1)).reshape(nq, TQ, LANES)
    tabs = bias_tables(jnp.concatenate([d_toep, d_cmp], axis=0), thr, rel_bias)
    bt, bc = tabs[:, :nq], tabs[:, nq:]
    c_start = CMP_STRIDE * jnp.arange(LANES)[:, None]
    s_start = SEL_BLOCK * jnp.arange(LANES)[None, :]
    cover = ((c_start < s_start + SEL_BLOCK) & (c_start + CMP_BLOCK > s_start)
             & (jnp.arange(LANES)[:, None] < n_cmp) & (jnp.arange(LANES)[None, :] < n_sel)).astype(jnp.bfloat16)
    expand = (jnp.arange(t)[None, :] // SEL_BLOCK == jnp.arange(LANES)[:, None]).astype(jnp.bfloat16)
    kv_spec = lambda rows: pl.BlockSpec((1, 1, rows, HEAD_DIM), lambda h, bb, qq: (bb, h, 0, 0))
    return pl.pallas_call(
        functools.partial(_nsa_prompt_body, n_cmp=n_cmp, n_sel=n_sel),
        grid=(NSA_KV_HEADS, b, nq),
        in_specs=[pl.BlockSpec((1, GRP, TQ, HEAD_DIM), lambda h, bb, qq: (bb, h, qq, 0)),
                  pl.BlockSpec((1, GRP, TQ, 3), lambda h, bb, qq: (bb, h, qq, 0)),
                  kv_spec(LANES), kv_spec(LANES), kv_spec(t), kv_spec(t), kv_spec(t), kv_spec(t),
                  pl.BlockSpec((1, nq, GRP, TQ, TK), lambda h, bb, qq: (h, 0, 0, 0, 0)),
                  pl.BlockSpec((1, 1, GRP, TQ, LANES), lambda h, bb, qq: (h, qq, 0, 0, 0)),
                  pl.BlockSpec((LANES, LANES), lambda h, bb, qq: (0, 0)),
                  pl.BlockSpec((LANES, t), lambda h, bb, qq: (0, 0))],
        out_specs=pl.BlockSpec((1, GRP, TQ, HEAD_DIM), lambda h, bb, qq: (bb, h, qq, 0)),
        out_shape=jax.ShapeDtypeStruct(q4.shape, jnp.float32),
        scratch_shapes=[pltpu.VMEM((TQ, t), jnp.float32)],
        compiler_params=pltpu.CompilerParams(vmem_limit_bytes=VMEM_LIMIT),
        name="nsa_prompt",
    )(q4, g4, kc, vc, ks, vs, kw, vw, bt, bc, cover, expand)


S5_GB = 8
S5_CH = S5_GB * SSM_CH
S5_ST = S5_GB * SSM_STATE


def _cmul(a_re, a_im, b_re, b_im):
    return a_re * b_re - a_im * b_im, a_re * b_im + a_im * b_re


def _s5_body(u_ref, h0re_ref, h0im_ref, a_ref, bcat_ref, ccat_ref, d_ref, y_ref, hre_ref, him_ref, xs_ref, hin_ref,
             *, chained):
    f32, bf16 = jnp.float32, jnp.bfloat16
    n_l, n_r = u_ref.shape[1], u_ref.shape[2]
    u2 = u_ref[0].reshape(n_l * n_r, S5_CH)
    xs_ref[...] = jnp.dot(u2.astype(bf16), bcat_ref[0].astype(bf16),
                          preferred_element_type=f32).reshape(n_l, n_r, 2 * S5_ST)
    a_re, a_im = a_ref[0:1, :], a_ref[1:2, :]

    def scan_step(j, carry):
        h_re, h_im, p_re, p_im = carry
        x = xs_ref[j]
        t_re, t_im = _cmul(a_re, a_im, h_re, h_im)
        h_re, h_im = t_re + x[:, :S5_ST], t_im + x[:, S5_ST:]
        xs_ref[j] = jnp.concatenate([h_re, h_im], axis=1)
        return (h_re, h_im) + _cmul(a_re, a_im, p_re, p_im)

    if chained:
        start = (jnp.zeros((n_r, S5_ST), f32), jnp.zeros((n_r, S5_ST), f32))
    else:
        start = (h0re_ref[0], h0im_ref[0])
    ones = (jnp.ones((1, S5_ST), f32), jnp.zeros((1, S5_ST), f32))
    h_re, h_im, al_re, al_im = lax.fori_loop(0, n_l, scan_step, start + ones)

    if chained:
        hin_ref[0:1, :] = jnp.concatenate([h0re_ref[0], h0im_ref[0]], axis=1)

        def chain_step(c, carry):
            z = xs_ref[n_l - 1, pl.ds(c - 1, 1), :]
            t_re, t_im = _cmul(al_re, al_im, *carry)
            n_re, n_im = t_re + z[:, :S5_ST], t_im + z[:, S5_ST:]
            hin_ref[pl.ds(c, 1), :] = jnp.concatenate([n_re, n_im], axis=1)
            return n_re, n_im

        h_re, h_im = lax.fori_loop(1, n_r + 1, chain_step, (h0re_ref[0], h0im_ref[0]))

        def fix_step(j, carry):
            p_re, p_im = carry
            hin = hin_ref[0:n_r, :]
            t_re, t_im = _cmul(p_re, p_im, hin[:, :S5_ST], hin[:, S5_ST:])
            xs_ref[j] = xs_ref[j] + jnp.concatenate([t_re, t_im], axis=1)
            return _cmul(a_re, a_im, p_re, p_im)

        lax.fori_loop(0, n_l, fix_step, (a_re, a_im))

    hre_ref[0] = h_re
    him_ref[0] = h_im
    hs = xs_ref[...].reshape(n_l * n_r, 2 * S5_ST).astype(bf16)
    y = jnp.dot(hs, ccat_ref[0].astype(bf16), preferred_element_type=f32) + d_ref[...] * u2
    y_ref[0] = jax.nn.gelu(y).reshape(n_l, n_r, S5_CH)


def s5_discretize(lam_re, lam_im, log_dt, b_re, b_im, c_re, c_im):
    f32 = jnp.float32
    dt = jnp.exp(log_dt.astype(f32))[:, None]
    lr, li = lam_re.astype(f32), lam_im.astype(f32)
    mag = jnp.exp(lr * dt)
    a_re, a_im = mag * jnp.cos(li * dt), mag * jnp.sin(li * dt)
    den = lr * lr + li * li
    z_re = ((a_re - 1.0) * lr + a_im * li) / den
    z_im = (a_im * lr - (a_re - 1.0) * li) / den
    br, bim = b_re.astype(f32), b_im.astype(f32)
    bb_re = z_re[..., None] * br - z_im[..., None] * bim
    bb_im = z_re[..., None] * bim + z_im[..., None] * br
    ngb = SSM_GROUPS // S5_GB
    eye = jnp.eye(S5_GB, dtype=f32)

    def block_diag(w):
        wd = w[:, :, :, None, :] * eye[None, :, None, :, None]
        return wd.reshape(ngb, S5_GB * w.shape[2], S5_GB * w.shape[3])

    def pack_b(bb):
        return block_diag(jnp.swapaxes(bb.reshape(ngb, S5_GB, SSM_STATE, SSM_CH), 2, 3))

    def pack_c(cc):
        return block_diag(jnp.swapaxes(cc.reshape(ngb, S5_GB, SSM_CH, SSM_STATE), 2, 3))

    a = jnp.stack([a_re.reshape(-1), a_im.reshape(-1)])
    bcat = jnp.concatenate([pack_b(bb_re), pack_b(bb_im)], axis=2)
    ccat = jnp.concatenate([pack_c(c_re.astype(f32)), -pack_c(c_im.astype(f32))], axis=1)
    return a, bcat, ccat


def s5_scan(u4, h0_re, h0_im, a, bcat, ccat, d_skip, chained):
    nb, n_l, n_r, _ = u4.shape
    rh = h0_re.shape[1]
    ngb = SSM_GROUPS // S5_GB
    st_spec = pl.BlockSpec((1, rh, S5_ST), lambda i, j: (i, 0, j))
    return pl.pallas_call(
        functools.partial(_s5_body, chained=chained),
        grid=(nb, ngb),
        in_specs=[pl.BlockSpec((1, n_l, n_r, S5_CH), lambda i, j: (i, 0, 0, j)), st_spec, st_spec,
                  pl.BlockSpec((2, S5_ST), lambda i, j: (0, j)),
                  pl.BlockSpec((1, S5_CH, 2 * S5_ST), lambda i, j: (j, 0, 0)),
                  pl.BlockSpec((1, 2 * S5_ST, S5_CH), lambda i, j: (j, 0, 0)),
                  pl.BlockSpec((1, S5_CH), lambda i, j: (0, j))],
        out_specs=[pl.BlockSpec((1, n_l, n_r, S5_CH), lambda i, j: (i, 0, 0, j)), st_spec, st_spec],
        out_shape=[jax.ShapeDtypeStruct(u4.shape, jnp.float32),
                   jax.ShapeDtypeStruct(h0_re.shape, jnp.float32), jax.ShapeDtypeStruct(h0_re.shape, jnp.float32)],
        scratch_shapes=[pltpu.VMEM((n_l, n_r, 2 * S5_ST), jnp.float32), pltpu.VMEM((n_r + 8, 2 * S5_ST), jnp.float32)],
        compiler_params=pltpu.CompilerParams(vmem_limit_bytes=VMEM_LIMIT),
        name="s5_scan",
    )(u4, h0_re, h0_im, a, bcat, ccat, d_skip.reshape(1, SSM_WIDTH))


S5_CHUNK = 64


def s5_mixer_pallas(u, h0_re, h0_im, lam_re, lam_im, log_dt, b_re, b_im, c_re, c_im, d_skip, w_glu, b_glu, chained):
    bsz, t, _ = u.shape
    f32 = jnp.float32
    a, bcat, ccat = s5_discretize(lam_re, lam_im, log_dt, b_re, b_im, c_re, c_im)
    flat = lambda h: h.astype(f32).reshape(bsz, SSM_GROUPS * SSM_STATE)
    if chained:
        n_r = t // S5_CHUNK
        u4 = jnp.transpose(u.astype(f32).reshape(bsz, n_r, S5_CHUNK, SSM_WIDTH), (0, 2, 1, 3))
        y4, h_re, h_im = s5_scan(u4, flat(h0_re)[:, None], flat(h0_im)[:, None], a, bcat, ccat, d_skip.astype(f32), True)
        y = jnp.transpose(y4, (0, 2, 1, 3)).reshape(bsz, t, SSM_WIDTH)
        h_re, h_im = h_re[:, 0], h_im[:, 0]
    else:
        u4 = jnp.transpose(u.astype(f32), (1, 0, 2))[None]
        y4, h_re, h_im = s5_scan(u4, flat(h0_re)[None], flat(h0_im)[None], a, bcat, ccat, d_skip.astype(f32), False)
        y = jnp.transpose(y4[0], (1, 0, 2))
        h_re, h_im = h_re[0], h_im[0]
    out = y * jax.nn.sigmoid(y @ w_glu.astype(f32) + b_glu.astype(f32))
    st = lambda h: h.reshape(bsz, SSM_GROUPS, SSM_STATE).astype(u.dtype)
    return out.astype(u.dtype), st(h_re), st(h_im)


def rms_norm(x, g):
    xf = x.astype(jnp.float32)
    y = xf * lax.rsqrt(jnp.mean(xf * xf, axis=-1, keepdims=True) + RMS_EPS)
    return (y * g.astype(jnp.float32)).astype(x.dtype)


def masked_softmax(s, mask):
    s = jnp.where(mask, s.astype(jnp.float32), NEG_INF)
    m = jnp.max(s, axis=-1, keepdims=True)
    e = jnp.where(mask, jnp.exp(s - m), 0.0)
    return e / jnp.maximum(jnp.sum(e, axis=-1, keepdims=True), 1e-30)


def compress_rows(x, pe, w1, b1, w2, b2, n_rows):
    b = x.shape[0]
    nc = (n_rows - CMP_BLOCK) // CMP_STRIDE + 1
    r = CMP_BLOCK // CMP_STRIDE
    nch = nc + r - 1
    chunks = x[:, :nch * CMP_STRIDE].reshape(b, nch, CMP_STRIDE, NSA_KV_HEADS, HEAD_DIM)
    part = jnp.einsum('bcjhd,rjdf->bcrhf', chunks, w1.reshape(r, CMP_STRIDE, HEAD_DIM, CMP_HIDDEN))
    h1 = jnp.einsum('ld,ldf->f', pe, w1) + b1
    for o in range(r):
        h1 = h1 + part[:, o:o + nc, o]
    return jnp.einsum('bchf,fd->bchd', jax.nn.gelu(h1), w2) + b2


def nsa_query_block(q, qpos, gates, kw, vw, kwpos, kc, vc, c_end, ks_b, vs_b, cover, rel_bias):
    b, nq = q.shape[:2]
    grp = NSA_HEADS // NSA_KV_HEADS
    qg = (q * HEAD_DIM ** -0.5).reshape(b, nq, NSA_KV_HEADS, grp, HEAD_DIM)
    rb = rel_bias.reshape(REL_BUCKETS, NSA_KV_HEADS, grp)
    hi = jnp.arange(NSA_KV_HEADS)[None, None, :, None]
    bi = jnp.arange(b)[:, None, None, None]
    dist_c = qpos[:, None] - c_end[None, :]
    bias_c = jnp.moveaxis(rb[rel_bucket(dist_c)], 1, -1)
    s_c = jnp.einsum('bqhgd,bchd->bqhgc', qg, kc) + bias_c
    p_c = masked_softmax(s_c, (dist_c >= 0)[:, None, None, :])
    o_c = jnp.einsum('bqhgc,bchd->bqhgd', p_c.astype(vc.dtype), vc)
    imp = jnp.einsum('bqhgc,cs->bqhs', p_c, cover)
    ns = ks_b.shape[1]
    blk = jnp.arange(ns)[None, :]
    cur = (qpos // SEL_BLOCK)[:, None]
    forced = ((blk == 0) | (blk == cur) | (blk == cur - 1))[:, None, :]
    valid = (blk * SEL_BLOCK <= qpos[:, None])[:, None, :]
    score = jnp.where(forced, FORCE_SCORE, jnp.where(valid, imp, -1.0))
    _, idx = lax.top_k(score, min(SEL_TOP_N, ns))
    k_s = ks_b[bi, idx, :, hi].reshape(b, nq, NSA_KV_HEADS, -1, HEAD_DIM)
    v_s = vs_b[bi, idx, :, hi].reshape(b, nq, NSA_KV_HEADS, -1, HEAD_DIM)
    s_pos = (idx[..., None] * SEL_BLOCK + jnp.arange(SEL_BLOCK)).reshape(b, nq, NSA_KV_HEADS, -1)
    dist_s = qpos[None, :, None, None] - s_pos
    bias_s = jnp.moveaxis(rb[rel_bucket(dist_s), hi], -1, 3)
    s_s = jnp.einsum('bqhgd,bqhkd->bqhgk', qg, k_s) + bias_s
    p_s = masked_softmax(s_s, (dist_s >= 0)[:, :, :, None, :])
    o_s = jnp.einsum('bqhgk,bqhkd->bqhgd', p_s.astype(v_s.dtype), v_s)
    dist_w = qpos[:, None] - kwpos[None, :]
    mask_w = (dist_w >= 0) & (dist_w < WINDOW) & (kwpos >= 0)[None, :]
    bias_w = jnp.moveaxis(rb[rel_bucket(dist_w)], 1, -1)
    s_w = jnp.einsum('bqhgd,bkhd->bqhgk', qg, kw) + bias_w
    p_w = masked_softmax(s_w, mask_w[:, None, None, :])
    o_w = jnp.einsum('bqhgk,bkhd->bqhgd', p_w.astype(vw.dtype), vw)
    g = gates.reshape(b, nq, NSA_KV_HEADS, grp, 3)
    o = g[..., 0:1] * o_c + g[..., 1:2] * o_s + g[..., 2:3] * o_w
    return o.reshape(b, nq, NSA_WIDTH)


def s5_mixer(u, h0_re, h0_im, lam_re, lam_im, log_dt, b_re, b_im, c_re, c_im, d_skip, w_glu, b_glu):
    bsz, t, _ = u.shape
    f32 = jnp.float32
    uf = u.astype(f32).reshape(bsz, t, SSM_GROUPS, SSM_CH)
    dt = jnp.exp(log_dt.astype(f32))[:, None]
    lr, li = lam_re.astype(f32), lam_im.astype(f32)
    mag = jnp.exp(lr * dt)
    a_re, a_im = mag * jnp.cos(li * dt), mag * jnp.sin(li * dt)
    den = lr * lr + li * li
    z_re = ((a_re - 1.0) * lr + a_im * li) / den
    z_im = (a_im * lr - (a_re - 1.0) * li) / den
    br, bim = b_re.astype(f32), b_im.astype(f32)
    bb_re = z_re[..., None] * br - z_im[..., None] * bim
    bb_im = z_re[..., None] * bim + z_im[..., None] * br
    x_re = jnp.einsum('btgc,gpc->btgp', uf, bb_re)
    x_im = jnp.einsum('btgc,gpc->btgp', uf, bb_im)
    ar = jnp.broadcast_to(a_re, x_re.shape)
    ai = jnp.broadcast_to(a_im, x_re.shape)

    def combine(e1, e2):
        a1r, a1i, b1r, b1i = e1
        a2r, a2i, b2r, b2i = e2
        return (a2r * a1r - a2i * a1i, a2r * a1i + a2i * a1r,
                a2r * b1r - a2i * b1i + b2r, a2r * b1i + a2i * b1r + b2i)

    pr, pim, sr, si = lax.associative_scan(combine, (ar, ai, x_re, x_im), axis=1)
    h0r = h0_re.astype(f32)[:, None]
    h0i = h0_im.astype(f32)[:, None]
    h_re = pr * h0r - pim * h0i + sr
    h_im = pr * h0i + pim * h0r + si
    y = jnp.einsum('btgp,gcp->btgc', h_re, c_re.astype(f32)) - jnp.einsum('btgp,gcp->btgc', h_im, c_im.astype(f32))
    y = jax.nn.gelu(y.reshape(bsz, t, SSM_WIDTH) + d_skip.astype(f32) * u.astype(f32))
    out = y * jax.nn.sigmoid(y @ w_glu.astype(f32) + b_glu.astype(f32))
    return out.astype(u.dtype), h_re[:, -1].astype(u.dtype), h_im[:, -1].astype(u.dtype)


def moe_ffn(x, router_w, router_b, w_gu, b_gu, w_down, b_down):
    n = x.shape[0]
    logits = (x @ router_w).astype(jnp.float32) + router_b.astype(jnp.float32)
    top_val, top_idx = lax.top_k(logits, TOP_K)
    gate = jax.nn.softmax(top_val, axis=-1)
    nk = n * TOP_K
    blk = max(8, min(256, nk // N_EXPERTS))
    n_blocks = -(-nk // blk) + N_EXPERTS
    flat_e = top_idx.reshape(nk)
    order = jnp.argsort(flat_e)
    sorted_e = flat_e[order]
    counts = jnp.bincount(flat_e, length=N_EXPERTS)
    padded = (counts + blk - 1) // blk * blk
    pad_end = jnp.cumsum(padded)
    pad_start = pad_end - padded
    grp_start = jnp.cumsum(counts) - counts
    dest_sorted = (pad_start[sorted_e] + jnp.arange(nk) - grp_start[sorted_e]).astype(jnp.int32)
    dest = jnp.zeros(nk, jnp.int32).at[order].set(dest_sorted)
    row_tok = jnp.zeros(n_blocks * blk, jnp.int32).at[dest_sorted].set((order // TOP_K).astype(jnp.int32))
    block_e = jnp.minimum(jnp.searchsorted(pad_end, jnp.arange(n_blocks) * blk, side='right'), N_EXPERTS - 1)
    xb = x[row_tok].reshape(n_blocks, blk, x.shape[-1])

    def expert_block(args):
        xe, e = args
        hgu = xe @ w_gu[e] + b_gu[e]
        hg = jnp.minimum(hgu[:, :D_FF], SWIGLU_LIMIT)
        hl = jnp.clip(hgu[:, D_FF:], -SWIGLU_LIMIT, SWIGLU_LIMIT)
        act = hg * jax.nn.sigmoid(SWIGLU_ALPHA * hg) * (hl + 1.0)
        return act @ w_down[e] + b_down[e]

    yb = lax.map(expert_block, (xb, block_e)).reshape(n_blocks * blk, -1)
    y = yb[dest].reshape(n, TOP_K, -1)
    return jnp.einsum('nk,nkd->nd', gate.astype(y.dtype), y)


def layer_forward(x, pos0, past_kv, win_buf, h0_re, h0_im, rel_bias, lw):
    b, t, _ = x.shape
    h = rms_norm(x, lw['norm_mix'])
    proj = h @ lw['w_in']
    o1 = NSA_WIDTH
    o2 = o1 + N_KV * KV_WIDTH
    o3 = o2 + 3 * NSA_HEADS
    q = proj[..., :o1].reshape(b, t, NSA_HEADS, HEAD_DIM)
    kv_new = proj[..., o1:o2].reshape(b, t, N_KV, NSA_KV_HEADS, HEAD_DIM)
    gates = jax.nn.sigmoid(proj[..., o2:o3].astype(jnp.float32)).reshape(b, t, NSA_HEADS, 3)
    u = proj[..., o3:]
    paged_new = kv_new[:, :, :N_PAGED]
    win_new = kv_new[:, :, N_PAGED:]
    qpos = pos0 + jnp.arange(t, dtype=jnp.int32)

    n_rows = pos0 + t
    ns = -(-n_rows // SEL_BLOCK)
    pad = jnp.zeros((b, ns * SEL_BLOCK - n_rows) + paged_new.shape[2:], paged_new.dtype)
    rows = jnp.concatenate(([] if past_kv is None else [past_kv]) + [paged_new, pad], axis=1)
    kc = compress_rows(rows[:, :, 0], lw['cmp_pe'][0], lw['cmp_w1'][0], lw['cmp_b1'][0], lw['cmp_w2'][0], lw['cmp_b2'][0], n_rows)
    vc = compress_rows(rows[:, :, 1], lw['cmp_pe'][1], lw['cmp_w1'][1], lw['cmp_b1'][1], lw['cmp_w2'][1], lw['cmp_b2'][1], n_rows)
    nc = kc.shape[1]
    c_start = jnp.arange(nc) * CMP_STRIDE
    c_end = c_start + CMP_BLOCK - 1
    s_start = jnp.arange(ns) * SEL_BLOCK
    cover = ((c_start[:, None] < s_start[None, :] + SEL_BLOCK) & (c_start[:, None] + CMP_BLOCK > s_start[None, :])).astype(jnp.float32)
    ks_b = rows[:, :, 2].reshape(b, ns, SEL_BLOCK, NSA_KV_HEADS, HEAD_DIM)
    vs_b = rows[:, :, 3].reshape(b, ns, SEL_BLOCK, NSA_KV_HEADS, HEAD_DIM)

    if win_buf is None:
        new_win = win_new[:, t - min(WINDOW, n_rows):]
        tr = lambda a: jnp.transpose(a, (0, 2, 1, 3))
        pad_c = lambda a: tr(jnp.pad(a, ((0, 0), (0, LANES - nc), (0, 0), (0, 0))))
        g4 = tr(proj[..., o2:o3].reshape(b, t, NSA_HEADS, 3))
        attn4 = nsa_prompt(tr(q), g4, pad_c(kc), pad_c(vc), tr(kv_new[:, :, 2]), tr(kv_new[:, :, 3]),
                           tr(kv_new[:, :, 4]), tr(kv_new[:, :, 5]), bucket_thresholds(), rel_bias, nc)
        attn = tr(attn4).reshape(b, t, NSA_WIDTH).astype(x.dtype)
        return _layer_tail(x, attn, u, h0_re, h0_im, lw, paged_new, new_win, True)
    if win_buf is None:
        qb = min(Q_BLOCK, t)
        nb = t // qb
        wpad = jnp.concatenate([jnp.zeros((b, WINDOW) + win_new.shape[2:], win_new.dtype), win_new], axis=1)
        widx = jnp.arange(nb)[:, None] * qb + jnp.arange(qb + WINDOW)[None, :]
        w_blocks = jnp.moveaxis(wpad[:, widx], 1, 0)
        w_pos = widx - WINDOW
        w_hist = win_new
    else:
        qb, nb = t, 1
        w_hist = jnp.concatenate([win_buf, win_new], axis=1)
        w_blocks = w_hist[None]
        w_pos = (pos0 - win_buf.shape[1] + jnp.arange(w_hist.shape[1], dtype=jnp.int32))[None]
    new_win = w_hist[:, w_hist.shape[1] - min(WINDOW, n_rows):]

    def run(args):
        q_b, qpos_b, g_b, w_b, wpos_b = args
        return nsa_query_block(q_b, qpos_b, g_b, w_b[:, :, 0], w_b[:, :, 1], wpos_b,
                               kc, vc, c_end, ks_b, vs_b, cover, rel_bias)

    q_blocks = jnp.moveaxis(q.reshape(b, nb, qb, NSA_HEADS, HEAD_DIM), 1, 0)
    g_blocks = jnp.moveaxis(gates.reshape(b, nb, qb, NSA_HEADS, 3), 1, 0)
    attn = lax.map(run, (q_blocks, qpos.reshape(nb, qb), g_blocks, w_blocks, w_pos))
    attn = jnp.moveaxis(attn, 0, 1).reshape(b, t, NSA_WIDTH).astype(x.dtype)
    return _layer_tail(x, attn, u, h0_re, h0_im, lw, paged_new, new_win, False)


def _layer_tail(x, attn, u, h0_re, h0_im, lw, paged_new, new_win, chained):
    b, t, _ = x.shape
    ssm, h_re, h_im = s5_mixer_pallas(u, h0_re, h0_im, lw['lam_re'], lw['lam_im'], lw['log_dt'], lw['b_re'], lw['b_im'],
                                      lw['c_re'], lw['c_im'], lw['d_skip'], lw['w_glu'], lw['b_glu'], chained)
    merged = jnp.concatenate([rms_norm(attn, lw['norm_attn_out']), rms_norm(ssm, lw['norm_ssm_out'])], axis=-1)
    x = x + pallas_matmul(merged.reshape(b * t, D_MODEL), lw['w_out']).reshape(b, t, D_MODEL)
    hm = rms_norm(x, lw['norm_ffn']).reshape(b * t, D_MODEL)
    x = x + moe_ffn(hm, lw['router_w'], lw['router_b'], lw['w_gu'], lw['b_gu'], lw['w_down'], lw['b_down']).reshape(b, t, D_MODEL).astype(x.dtype)
    return x, paged_new, new_win, h_re, h_im


def kernel(x_prompt, x_sample, cache_nsa_kv, cache_win_kv, state_ssm_re, state_ssm_im, page_table, rel_bias,
           norm_mix, w_in, cmp_pe, cmp_w1, cmp_b1, cmp_w2, cmp_b2, ssm_lam_re, ssm_lam_im, ssm_log_dt,
           ssm_b_re, ssm_b_im, ssm_c_re, ssm_c_im, ssm_d, ssm_w_glu, ssm_b_glu, norm_attn_out, norm_ssm_out,
           w_out, norm_ffn, router_w, router_b, w_gu, b_gu, w_down, b_down, norm_final):
    n_seq, n_pages = page_table.shape
    past_len = n_pages * cache_nsa_kv.shape[2]
    xp, xs = x_prompt, x_sample
    kv_p, win_p, sre_p, sim_p = [], [], [], []
    kv_s, win_s, sre_s, sim_s = [], [], [], []
    for i in range(DEPTH):
        lw = dict(norm_mix=norm_mix[i], w_in=w_in[i], cmp_pe=cmp_pe[i], cmp_w1=cmp_w1[i], cmp_b1=cmp_b1[i],
                  cmp_w2=cmp_w2[i], cmp_b2=cmp_b2[i], lam_re=ssm_lam_re[i], lam_im=ssm_lam_im[i],
                  log_dt=ssm_log_dt[i], b_re=ssm_b_re[i], b_im=ssm_b_im[i], c_re=ssm_c_re[i], c_im=ssm_c_im[i],
                  d_skip=ssm_d[i], w_glu=ssm_w_glu[i], b_glu=ssm_b_glu[i], norm_attn_out=norm_attn_out[i],
                  norm_ssm_out=norm_ssm_out[i], w_out=w_out[i], norm_ffn=norm_ffn[i], router_w=router_w[i],
                  router_b=router_b[i], w_gu=w_gu[i], b_gu=b_gu[i], w_down=w_down[i], b_down=b_down[i])
        h0 = jnp.zeros((xp.shape[0], SSM_GROUPS, SSM_STATE), xp.dtype)
        xp, kv1, w1, r1, m1 = layer_forward(xp, 0, None, None, h0, h0, rel_bias, lw)
        past = cache_nsa_kv[i][page_table].reshape(n_seq, past_len, N_PAGED, NSA_KV_HEADS, HEAD_DIM)
        xs, kv2, w2, r2, m2 = layer_forward(xs, past_len, past, cache_win_kv[i], state_ssm_re[i], state_ssm_im[i], rel_bias, lw)
        kv_p.append(kv1); win_p.append(w1); sre_p.append(r1); sim_p.append(m1)
        kv_s.append(kv2); win_s.append(w2); sre_s.append(r2); sim_s.append(m2)
    y_prompt = rms_norm(xp, norm_final)
    y_sample = rms_norm(xs, norm_final)
    return (y_prompt, y_sample, jnp.stack(kv_p), jnp.stack(win_p), jnp.stack(sre_p), jnp.stack(sim_p),
            jnp.stack(kv_s), jnp.stack(win_s), jnp.stack(sre_s), jnp.stack(sim_s))
```

```python
import functools
import math
import jax, jax.numpy as jnp
from jax import lax
from jax.experimental import pallas as pl
from jax.experimental.pallas import tpu as pltpu

D_MODEL = 2048
DEPTH = 1
NSA_HEADS = 16
NSA_KV_HEADS = 2
HEAD_DIM = 64
NSA_WIDTH = NSA_HEADS * HEAD_DIM
KV_WIDTH = NSA_KV_HEADS * HEAD_DIM
N_PAGED = 4
N_KV = 6
SSM_WIDTH = D_MODEL - NSA_WIDTH
SSM_CH = 16
SSM_GROUPS = SSM_WIDTH // SSM_CH
SSM_STATE = 64
IN_WIDTH = NSA_WIDTH + N_KV * KV_WIDTH + 3 * NSA_HEADS + SSM_WIDTH
CMP_BLOCK = 32
CMP_STRIDE = 16
CMP_HIDDEN = 2 * HEAD_DIM
SEL_BLOCK = 64
SEL_TOP_N = 16
WINDOW = 512
Q_BLOCK = 64
FORCE_SCORE = 1e6
REL_BUCKETS = 32
REL_MAX_DIST = 4096
N_EXPERTS = 32
TOP_K = 4
D_FF = D_MODEL
SWIGLU_LIMIT = 7.0
SWIGLU_ALPHA = 1.702
RMS_EPS = 1e-5
NEG_INF = -1e30


def _mm_body(x_ref, w_ref, o_ref):
    o_ref[...] = jnp.dot(x_ref[...].astype(jnp.bfloat16), w_ref[...].astype(jnp.bfloat16),
                         preferred_element_type=jnp.float32)


def pallas_matmul(x, w, tm=256, tn=512):
    m, k = x.shape
    n = w.shape[1]
    tm = min(tm, m)
    tn = min(tn, n)
    return pl.pallas_call(
        _mm_body,
        grid=(m // tm, n // tn),
        in_specs=[pl.BlockSpec((tm, k), lambda i, j: (i, 0)),
                  pl.BlockSpec((k, tn), lambda i, j: (0, j))],
        out_specs=pl.BlockSpec((tm, tn), lambda i, j: (i, j)),
        out_shape=jax.ShapeDtypeStruct((m, n), jnp.float32),
        name="matmul",
    )(x, w)


GRP = NSA_HEADS // NSA_KV_HEADS
TQ = 128
TK = 128
LANES = 128
BUCKET_TABLE_LEN = 32768
BIAS_TILE_ELEMS = 16384
VMEM_LIMIT = 56 * 1024 * 1024


def rel_bucket(dist):
    n = jnp.maximum(dist, 0)
    exact = REL_BUCKETS // 2
    nf = jnp.maximum(n, 1).astype(jnp.float32)
    large = exact + (jnp.log(nf / exact) / math.log(REL_MAX_DIST / exact) * (REL_BUCKETS - exact)).astype(jnp.int32)
    return jnp.where(n < exact, n, jnp.minimum(large, REL_BUCKETS - 1))


def bucket_thresholds():
    tab = rel_bucket(jnp.arange(BUCKET_TABLE_LEN, dtype=jnp.int32))
    return jnp.sum(tab[None, :] < jnp.arange(REL_BUCKETS, dtype=jnp.int32)[:, None], axis=1).astype(jnp.int32)


def _bias_table_body(thr_ref, rb_ref, d_ref, o_ref):
    n = jnp.maximum(d_ref[0], 0)
    for h in range(NSA_HEADS):
        val = jnp.full(n.shape, rb_ref[h], jnp.float32)
        for k in range(1, REL_BUCKETS):
            val = jnp.where(n >= thr_ref[k], rb_ref[k * NSA_HEADS + h], val)
        o_ref[h // GRP, 0, h % GRP] = val


def bias_tables(dist, thr, rel_bias):
    n, r, c = dist.shape
    ct = min(c, BIAS_TILE_ELEMS // r)
    assert c % ct == 0
    return pl.pallas_call(
        _bias_table_body,
        grid_spec=pltpu.PrefetchScalarGridSpec(
            num_scalar_prefetch=2,
            grid=(n, c // ct),
            in_specs=[pl.BlockSpec((1, r, ct), lambda i, j, *_: (i, 0, j))],
            out_specs=pl.BlockSpec((NSA_KV_HEADS, 1, GRP, r, ct), lambda i, j, *_: (0, i, 0, 0, j)),
        ),
        out_shape=jax.ShapeDtypeStruct((NSA_KV_HEADS, n, GRP, r, c), jnp.float32),
        name="bias_tables",
    )(thr, rel_bias.reshape(-1), dist)


def _softmax_tile(s3, msk, m, l, acc, v):
    s3 = jnp.where(msk[None], s3, NEG_INF)
    m_new = jnp.maximum(m, jnp.max(s3, axis=-1, keepdims=True))
    p = jnp.where(msk[None], jnp.exp(s3 - m_new), 0.0)
    alpha = jnp.exp(m - m_new)
    l = alpha * l + jnp.sum(p, axis=-1, keepdims=True)
    pv = jnp.dot(p.reshape(GRP * TQ, p.shape[-1]).astype(jnp.bfloat16), v, preferred_element_type=jnp.float32)
    acc = alpha * acc + pv.reshape(GRP, TQ, HEAD_DIM)
    return m_new, l, acc


def _nsa_prompt_body(q_ref, g_ref, kc_ref, vc_ref, ks_ref, vs_ref, kw_ref, vw_ref, bt_ref, bc_ref, cov_ref, exp_ref,
                     o_ref, mask_ref, *, n_cmp, n_sel):
    f32, bf16 = jnp.float32, jnp.bfloat16
    qi = pl.program_id(2)
    nt_dims = (((1,), (1,)), ((), ()))
    q = (q_ref[0] * (HEAD_DIM ** -0.5)).reshape(GRP * TQ, HEAD_DIM).astype(bf16)
    qpos = qi * TQ + lax.broadcasted_iota(jnp.int32, (TQ, LANES), 0)
    lane = lax.broadcasted_iota(jnp.int32, (TQ, LANES), 1)

    ok_c = (qpos >= CMP_STRIDE * lane + (CMP_BLOCK - 1)) & (lane < n_cmp)
    s_c = lax.dot_general(q, kc_ref[0, 0].astype(bf16), nt_dims, preferred_element_type=f32) + bc_ref[0, 0].reshape(GRP * TQ, LANES)
    s_c = jnp.where(ok_c[None], s_c.reshape(GRP, TQ, LANES), NEG_INF)
    m_c = jnp.max(s_c, axis=-1, keepdims=True)
    e_c = jnp.where(ok_c[None], jnp.exp(s_c - m_c), 0.0)
    p_c = e_c / jnp.maximum(jnp.sum(e_c, axis=-1, keepdims=True), 1e-30)
    o_c = jnp.dot(p_c.reshape(GRP * TQ, LANES).astype(bf16), vc_ref[0, 0].astype(bf16),
                  preferred_element_type=f32).reshape(GRP, TQ, HEAD_DIM)

    imp = jnp.dot(jnp.sum(p_c, axis=0).astype(bf16), cov_ref[...], preferred_element_type=f32)
    cur = qpos // SEL_BLOCK
    forced = (lane == 0) | (lane == cur) | (lane == cur - 1)
    score = jnp.where(forced, FORCE_SCORE, jnp.where(lane * SEL_BLOCK <= qpos, imp, -1.0))
    score = jnp.where(lane < n_sel, score, -jnp.inf)
    sel = jnp.zeros((TQ, LANES), f32)
    for _ in range(min(SEL_TOP_N, n_sel)):
        best = jnp.max(score, axis=-1, keepdims=True)
        idx = jnp.min(jnp.where(score == best, lane, LANES), axis=-1, keepdims=True)
        hit = lane == idx
        sel = jnp.where(hit, 1.0, sel)
        score = jnp.where(hit, -jnp.inf, score)
    mask_ref[...] = jnp.dot(sel.astype(bf16), exp_ref[...], preferred_element_type=f32)

    row = lax.broadcasted_iota(jnp.int32, (TQ, TK), 0)
    col = lax.broadcasted_iota(jnp.int32, (TQ, TK), 1)
    init = (jnp.full((GRP, TQ, 1), NEG_INF, f32), jnp.zeros((GRP, TQ, 1), f32), jnp.zeros((GRP, TQ, HEAD_DIM), f32))

    def scores(k_ref, ki):
        k = k_ref[0, 0, pl.ds(pl.multiple_of(ki * TK, TK), TK), :].astype(bf16)
        s = lax.dot_general(q, k, nt_dims, preferred_element_type=f32)
        return s.reshape(GRP, TQ, TK) + bt_ref[0, qi - ki]

    def sel_step(ki, carry):
        dist = (qi - ki) * TQ + row - col
        msk = (mask_ref[:, pl.ds(pl.multiple_of(ki * TK, TK), TK)] > 0.5) & (dist >= 0)
        v = vs_ref[0, 0, pl.ds(pl.multiple_of(ki * TK, TK), TK), :].astype(bf16)
        return _softmax_tile(scores(ks_ref, ki), msk, *carry, v)

    def win_step(ki, carry):
        dist = (qi - ki) * TQ + row - col
        msk = (dist >= 0) & (dist < WINDOW)
        v = vw_ref[0, 0, pl.ds(pl.multiple_of(ki * TK, TK), TK), :].astype(bf16)
        return _softmax_tile(scores(kw_ref, ki), msk, *carry, v)

    _, l_s, acc_s = lax.fori_loop(0, qi + 1, sel_step, init)
    _, l_w, acc_w = lax.fori_loop(jnp.maximum(qi - WINDOW // TK, 0), qi + 1, win_step, init)
    o_s = acc_s / jnp.maximum(l_s, 1e-30)
    o_w = acc_w / jnp.maximum(l_w, 1e-30)
    g = jax.nn.sigmoid(g_ref[0])
    o_ref[0] = g[:, :, 0:1] * o_c + g[:, :, 1:2] * o_s + g[:, :, 2:3] * o_w


def nsa_prompt(q4, g4, kc, vc, ks, vs, kw, vw, thr, rel_bias, n_cmp):
    b, _, t, _ = q4.shape
    nq = t // TQ
    n_sel = t // SEL_BLOCK
    assert t % TQ == 0 and n_sel <= LANES and n_cmp <= LANES and TQ == TK
    i = jnp.arange(TQ, dtype=jnp.int32)
    d_toep = jnp.arange(nq, dtype=jnp.int32)[:, None, None] * TQ + i[:, None] - i[None, :]
    d_cmp = (jnp.arange(t, dtype=jnp.int32)[:, None] - (CMP_STRIDE * jnp.arange(LANES, dtype=jnp.int32)[None, :] + CMP_BLOCK - 1)).reshape(nq, TQ, LANES)
    tabs = bias_tables(jnp.concatenate([d_toep, d_cmp], axis=0), thr, rel_bias)
    bt, bc = tabs[:, :nq], tabs[:, nq:]
    c_start = CMP_STRIDE * jnp.arange(LANES)[:, None]
    s_start = SEL_BLOCK * jnp.arange(LANES)[None, :]
    cover = ((c_start < s_start + SEL_BLOCK) & (c_start + CMP_BLOCK > s_start)
             & (jnp.arange(LANES)[:, None] < n_cmp) & (jnp.arange(LANES)[None, :] < n_sel)).astype(jnp.bfloat16)
    expand = (jnp.arange(t)[None, :] // SEL_BLOCK == jnp.arange(LANES)[:, None]).astype(jnp.bfloat16)
    kv_spec = lambda rows: pl.BlockSpec((1, 1, rows, HEAD_DIM), lambda h, bb, qq: (bb, h, 0, 0))
    return pl.pallas_call(
        functools.partial(_nsa_prompt_body, n_cmp=n_cmp, n_sel=n_sel),
        grid=(NSA_KV_HEADS, b, nq),
        in_specs=[pl.BlockSpec((1, GRP, TQ, HEAD_DIM), lambda h, bb, qq: (bb, h, qq, 0)),
                  pl.BlockSpec((1, GRP, TQ, 3), lambda h, bb, qq: (bb, h, qq, 0)),
                  kv_spec(LANES), kv_spec(LANES), kv_spec(t), kv_spec(t), kv_spec(t), kv_spec(t),
                  pl.BlockSpec((1, nq, GRP, TQ, TK), lambda h, bb, qq: (h, 0, 0, 0, 0)),
                  pl.BlockSpec((1, 1, GRP, TQ, LANES), lambda h, bb, qq: (h, qq, 0, 0, 0)),
                  pl.BlockSpec((LANES, LANES), lambda h, bb, qq: (0, 0)),
                  pl.BlockSpec((LANES, t), lambda h, bb, qq: (0, 0))],
        out_specs=pl.BlockSpec((1, GRP, TQ, HEAD_DIM), lambda h, bb, qq: (bb, h, qq, 0)),
        out_shape=jax.ShapeDtypeStruct(q4.shape, jnp.float32),
        scratch_shapes=[pltpu.VMEM((TQ, t), jnp.float32)],
        compiler_params=pltpu.CompilerParams(vmem_limit_bytes=VMEM_LIMIT),
        name="nsa_prompt",
    )(q4, g4, kc, vc, ks, vs, kw, vw, bt, bc, cover, expand)


S5_GB = 8
S5_CH = S5_GB * SSM_CH
S5_ST = S5_GB * SSM_STATE


def _cmul(a_re, a_im, b_re, b_im):
    return a_re * b_re - a_im * b_im, a_re * b_im + a_im * b_re


def _s5_body(u_ref, h0re_ref, h0im_ref, a_ref, bcat_ref, ccat_ref, d_ref, y_ref, hre_ref, him_ref, xs_ref, hin_ref,
             *, chained):
    f32, bf16 = jnp.float32, jnp.bfloat16
    n_l, n_r = u_ref.shape[1], u_ref.shape[2]
    u2 = u_ref[0].reshape(n_l * n_r, S5_CH)
    xs_ref[...] = jnp.dot(u2.astype(bf16), bcat_ref[0].astype(bf16),
                          preferred_element_type=f32).reshape(n_l, n_r, 2 * S5_ST)
    a_re, a_im = a_ref[0:1, :], a_ref[1:2, :]

    def scan_step(j, carry):
        h_re, h_im, p_re, p_im = carry
        x = xs_ref[j]
        t_re, t_im = _cmul(a_re, a_im, h_re, h_im)
        h_re, h_im = t_re + x[:, :S5_ST], t_im + x[:, S5_ST:]
        xs_ref[j] = jnp.concatenate([h_re, h_im], axis=1)
        return (h_re, h_im) + _cmul(a_re, a_im, p_re, p_im)

    if chained:
        start = (jnp.zeros((n_r, S5_ST), f32), jnp.zeros((n_r, S5_ST), f32))
    else:
        start = (h0re_ref[0], h0im_ref[0])
    ones = (jnp.ones((1, S5_ST), f32), jnp.zeros((1, S5_ST), f32))
    h_re, h_im, al_re, al_im = lax.fori_loop(0, n_l, scan_step, start + ones)

    if chained:
        hin_ref[0:1, :] = jnp.concatenate([h0re_ref[0], h0im_ref[0]], axis=1)

        def chain_step(c, carry):
            z = xs_ref[n_l - 1, pl.ds(c - 1, 1), :]
            t_re, t_im = _cmul(al_re, al_im, *carry)
            n_re, n_im = t_re + z[:, :S5_ST], t_im + z[:, S5_ST:]
            hin_ref[pl.ds(c, 1), :] = jnp.concatenate([n_re, n_im], axis=1)
            return n_re, n_im

        h_re, h_im = lax.fori_loop(1, n_r + 1, chain_step, (h0re_ref[0], h0im_ref[0]))

        def fix_step(j, carry):
            p_re, p_im = carry
            hin = hin_ref[0:n_r, :]
            t_re, t_im = _cmul(p_re, p_im, hin[:, :S5_ST], hin[:, S5_ST:])
            xs_ref[j] = xs_ref[j] + jnp.concatenate([t_re, t_im], axis=1)
            return _cmul(a_re, a_im, p_re, p_im)

        lax.fori_loop(0, n_l, fix_step, (a_re, a_im))

    hre_ref[0] = h_re
    him_ref[0] = h_im
    hs = xs_ref[...].reshape(n_l * n_r, 2 * S5_ST).astype(bf16)
    y = jnp.dot(hs, ccat_ref[0].astype(bf16), preferred_element_type=f32) + d_ref[...] * u2
    y_ref[0] = jax.nn.gelu(y).reshape(n_l, n_r, S5_CH)


def s5_discretize(lam_re, lam_im, log_dt, b_re, b_im, c_re, c_im):
    f32 = jnp.float32
    dt = jnp.exp(log_dt.astype(f32))[:, None]
    lr, li = lam_re.astype(f32), lam_im.astype(f32)
    mag = jnp.exp(lr * dt)
    a_re, a_im = mag * jnp.cos(li * dt), mag * jnp.sin(li * dt)
    den = lr * lr + li * li
    z_re = ((a_re - 1.0) * lr + a_im * li) / den
    z_im = (a_im * lr - (a_re - 1.0) * li) / den
    br, bim = b_re.astype(f32), b_im.astype(f32)
    bb_re = z_re[..., None] * br - z_im[..., None] * bim
    bb_im = z_re[..., None] * bim + z_im[..., None] * br
    ngb = SSM_GROUPS // S5_GB
    eye = jnp.eye(S5_GB, dtype=f32)

    def block_diag(w):
        wd = w[:, :, :, None, :] * eye[None, :, None, :, None]
        return wd.reshape(ngb, S5_GB * w.shape[2], S5_GB * w.shape[3])

    def pack_b(bb):
        return block_diag(jnp.swapaxes(bb.reshape(ngb, S5_GB, SSM_STATE, SSM_CH), 2, 3))

    def pack_c(cc):
        return block_diag(jnp.swapaxes(cc.reshape(ngb, S5_GB, SSM_CH, SSM_STATE), 2, 3))

    a = jnp.stack([a_re.reshape(-1), a_im.reshape(-1)])
    bcat = jnp.concatenate([pack_b(bb_re), pack_b(bb_im)], axis=2)
    ccat = jnp.concatenate([pack_c(c_re.astype(f32)), -pack_c(c_im.astype(f32))], axis=1)
    return a, bcat, ccat


def s5_scan(u4, h0_re, h0_im, a, bcat, ccat, d_skip, chained):
    nb, n_l, n_r, _ = u4.shape
    rh = h0_re.shape[1]
    ngb = SSM_GROUPS // S5_GB
    st_spec = pl.BlockSpec((1, rh, S5_ST), lambda i, j: (i, 0, j))
    return pl.pallas_call(
        functools.partial(_s5_body, chained=chained),
        grid=(nb, ngb),
        in_specs=[pl.BlockSpec((1, n_l, n_r, S5_CH), lambda i, j: (i, 0, 0, j)), st_spec, st_spec,
                  pl.BlockSpec((2, S5_ST), lambda i, j: (0, j)),
                  pl.BlockSpec((1, S5_CH, 2 * S5_ST), lambda i, j: (j, 0, 0)),
                  pl.BlockSpec((1, 2 * S5_ST, S5_CH), lambda i, j: (j, 0, 0)),
                  pl.BlockSpec((1, S5_CH), lambda i, j: (0, j))],
        out_specs=[pl.BlockSpec((1, n_l, n_r, S5_CH), lambda i, j: (i, 0, 0, j)), st_spec, st_spec],
        out_shape=[jax.ShapeDtypeStruct(u4.shape, jnp.float32),
                   jax.ShapeDtypeStruct(h0_re.shape, jnp.float32), jax.ShapeDtypeStruct(h0_re.shape, jnp.float32)],
        scratch_shapes=[pltpu.VMEM((n_l, n_r, 2 * S5_ST), jnp.float32), pltpu.VMEM((n_r + 8, 2 * S5_ST), jnp.float32)],
        compiler_params=pltpu.CompilerParams(vmem_limit_bytes=VMEM_LIMIT),
        name="s5_scan",
    )(u4, h0_re, h0_im, a, bcat, ccat, d_skip.reshape(1, SSM_WIDTH))


S5_CHUNK = 64


def s5_mixer_pallas(u, h0_re, h0_im, lam_re, lam_im, log_dt, b_re, b_im, c_re, c_im, d_skip, w_glu, b_glu, chained):
    bsz, t, _ = u.shape
    f32 = jnp.float32
    a, bcat, ccat = s5_discretize(lam_re, lam_im, log_dt, b_re, b_im, c_re, c_im)
    flat = lambda h: h.astype(f32).reshape(bsz, SSM_GROUPS * SSM_STATE)
    if chained:
        n_r = t // S5_CHUNK
        u4 = jnp.transpose(u.astype(f32).reshape(bsz, n_r, S5_CHUNK, SSM_WIDTH), (0, 2, 1, 3))
        y4, h_re, h_im = s5_scan(u4, flat(h0_re)[:, None], flat(h0_im)[:, None], a, bcat, ccat, d_skip.astype(f32), True)
        y = jnp.transpose(y4, (0, 2, 1, 3)).reshape(bsz, t, SSM_WIDTH)
        h_re, h_im = h_re[:, 0], h_im[:, 0]
    else:
        u4 = jnp.transpose(u.astype(f32), (1, 0, 2))[None]
        y4, h_re, h_im = s5_scan(u4, flat(h0_re)[None], flat(h0_im)[None], a, bcat, ccat, d_skip.astype(f32), False)
        y = jnp.transpose(y4[0], (1, 0, 2))
        h_re, h_im = h_re[0], h_im[0]
    out = y * jax.nn.sigmoid(y @ w_glu.astype(f32) + b_glu.astype(f32))
    st = lambda h: h.reshape(bsz, SSM_GROUPS, SSM_STATE).astype(u.dtype)
    return out.astype(u.dtype), st(h_re), st(h_im)


ROW_LANES = N_PAGED * KV_WIDTH
CHUNKS_PER_PAGE = 128 // CMP_STRIDE
CMP_HID2 = NSA_KV_HEADS * CMP_HIDDEN


def _compress_body(pt_ref, *refs, n_g):
    f32, bf16 = jnp.float32, jnp.bfloat16
    page_refs = (refs[:n_g], refs[n_g:2 * n_g])
    w1_ref, c1_ref, w2_ref, b2_ref, o_ref, carry_ref = refs[2 * n_g:]
    m_rows = n_g * CHUNKS_PER_PAGE
    first = pl.program_id(1) == 0
    row_id = lax.broadcasted_iota(jnp.int32, (m_rows, CMP_HID2), 0)
    outs = []
    for t in range(2):
        cols = []
        for j in range(CMP_STRIDE):
            pieces = [r[0, pl.ds(j, CHUNKS_PER_PAGE, stride=CMP_STRIDE), :] for r in page_refs[t]]
            cols.append(jnp.concatenate(pieces, axis=0))
        x = jnp.concatenate(cols, axis=1).astype(bf16)
        part = jnp.dot(x, w1_ref[t], preferred_element_type=f32)
        p0, p1 = part[:, :CMP_HID2], part[:, CMP_HID2:]
        prev = jnp.where(first, 0.0, carry_ref[t, 0:1, :])
        shifted = jnp.where(row_id == 0, prev, pltpu.roll(p0, 1, axis=0))
        carry_ref[t, 0:1, :] = p0[m_rows - 1:m_rows, :]
        h1 = (c1_ref[t] + shifted) + p1
        outs.append(jnp.dot(jax.nn.gelu(h1).astype(bf16), w2_ref[t], preferred_element_type=f32) + b2_ref[t])
    o_ref[0] = jnp.concatenate(outs, axis=1)


def compress_pages(pool, page_table, cmp_pe, cmp_w1, cmp_b1, cmp_w2, cmp_b2, n_g):
    n_seq, n_pages = page_table.shape
    assert n_pages % n_g == 0 and pool.shape[1] == 128
    f32, bf16 = jnp.float32, jnp.bfloat16
    r = CMP_BLOCK // CMP_STRIDE
    eye = jnp.eye(NSA_KV_HEADS, dtype=f32)
    w1 = cmp_w1.astype(f32).reshape(2, r, CMP_STRIDE, HEAD_DIM, CMP_HIDDEN)
    w1 = jnp.transpose(w1, (0, 2, 3, 1, 4))[:, :, None, :, :, None, :] * eye[None, None, :, None, None, :, None]
    w1 = w1.reshape(2, CMP_STRIDE * KV_WIDTH, r * CMP_HID2).astype(bf16)
    c1 = jnp.stack([jnp.einsum('ld,ldf->f', cmp_pe[t], cmp_w1[t]) + cmp_b1[t] for t in range(2)])
    c1 = jnp.tile(c1[:, None, :], (1, 1, NSA_KV_HEADS))
    w2 = (cmp_w2.astype(f32)[:, None, :, None, :] * eye[None, :, None, :, None]).reshape(2, CMP_HID2, KV_WIDTH).astype(bf16)
    b2 = jnp.tile(cmp_b2.astype(f32)[:, None, :], (1, 1, NSA_KV_HEADS))
    m_rows = n_g * CHUNKS_PER_PAGE
    page_spec = lambda k, t: pl.BlockSpec((1, 128, KV_WIDTH), lambda b, g, pt: (pt[b * n_pages + g * n_g + k], 0, t))
    full = lambda a: pl.BlockSpec(a.shape, lambda b, g, pt: (0,) * a.ndim)
    return pl.pallas_call(
        functools.partial(_compress_body, n_g=n_g),
        grid_spec=pltpu.PrefetchScalarGridSpec(
            num_scalar_prefetch=1,
            grid=(n_seq, n_pages // n_g),
            in_specs=[page_spec(k, t) for t in range(2) for k in range(n_g)] + [full(w1), full(c1), full(w2), full(b2)],
            out_specs=pl.BlockSpec((1, m_rows, 2 * KV_WIDTH), lambda b, g, pt: (b, g, 0)),
            scratch_shapes=[pltpu.VMEM((2, 8, CMP_HID2), f32)],
        ),
        out_shape=jax.ShapeDtypeStruct((n_seq, n_pages * CHUNKS_PER_PAGE, 2 * KV_WIDTH), f32),
        compiler_params=pltpu.CompilerParams(vmem_limit_bytes=VMEM_LIMIT),
        name="compress_pages",
    )(page_table.reshape(-1).astype(jnp.int32), *([pool] * (2 * n_g)), w1, c1, w2, b2)


SD_PAGES = 64
SD_KEYS = SD_PAGES * 128
SD_SUB = 16
SD_ROWS = NSA_HEADS * 8
WIN_PAD = 640


def _masked_softmax(s, ok):
    s = jnp.where(ok, s, NEG_INF)
    e = jnp.where(ok, jnp.exp(s - jnp.max(s, axis=-1, keepdims=True)), 0.0)
    return e / jnp.maximum(jnp.sum(e, axis=-1, keepdims=True), 1e-30)


def _rows_from_group(a):
    t = a.shape[0] // NSA_KV_HEADS
    a4 = jnp.broadcast_to(a.reshape(NSA_KV_HEADS, 1, t, a.shape[1]), (NSA_KV_HEADS, GRP, t, a.shape[1]))
    return a4.reshape(NSA_KV_HEADS * GRP * t, a.shape[1])


def _nsa_decode_body(pt_ref, *refs, pos0, n_cmp, n_sel, n_new, win_len):
    f32, bf16 = jnp.float32, jnp.bfloat16
    page_refs = refs[:SD_PAGES]
    (q_ref, g_ref, kcvc_ref, knew_ref, wh_ref, bc_ref, bs_ref, bn_ref, bw_ref, cov_ref, exp_ref,
     o_ref, sel_ref, oc_ref, m_ref, l_ref, acc_ref, mask_ref) = refs[SD_PAGES:]
    tile, n_tiles = pl.program_id(1), pl.num_programs(1)
    nt_dims = (((1,), (1,)), ((), ()))
    n_tok = SD_ROWS // NSA_HEADS
    q = (q_ref[0] * (HEAD_DIM ** -0.5)).astype(bf16)
    qpos = pos0 + lax.broadcasted_iota(jnp.int32, (SD_ROWS, 1), 0) % n_tok

    @pl.when(tile == 0)
    def _():
        n_c = kcvc_ref.shape[1]
        m_idx = lax.broadcasted_iota(jnp.int32, (SD_ROWS, n_c), 1)
        ok_c = (m_idx >= 1) & (m_idx <= n_cmp) & (qpos >= CMP_STRIDE * m_idx + (CMP_BLOCK - 1 - CMP_STRIDE))
        s_c = lax.dot_general(q, kcvc_ref[0, :, :KV_WIDTH].astype(bf16), nt_dims, preferred_element_type=f32) + bc_ref[...]
        p_c = _masked_softmax(s_c, ok_c)
        oc_ref[...] = jnp.dot(p_c.astype(bf16), kcvc_ref[0, :, KV_WIDTH:].astype(bf16), preferred_element_type=f32)
        p_sum = jnp.sum(p_c.reshape(NSA_KV_HEADS, GRP, n_tok, n_c), axis=1).reshape(NSA_KV_HEADS * n_tok, n_c)
        imp = jnp.dot(p_sum.astype(bf16), cov_ref[...], preferred_element_type=f32)
        n_l = imp.shape[1]
        lane = lax.broadcasted_iota(jnp.int32, (NSA_KV_HEADS * n_tok, n_l), 1)
        qp = pos0 + lax.broadcasted_iota(jnp.int32, (NSA_KV_HEADS * n_tok, n_l), 0) % n_tok
        cur = qp // SEL_BLOCK
        forced = (lane == 0) | (lane == cur) | (lane == cur - 1)
        score = jnp.where(forced, FORCE_SCORE, jnp.where(lane * SEL_BLOCK <= qp, imp, -1.0))
        score = jnp.where(lane < n_sel, score, -jnp.inf)
        sel = jnp.zeros(score.shape, f32)
        for _ in range(min(SEL_TOP_N, n_sel)):
            best = jnp.max(score, axis=-1, keepdims=True)
            idx = jnp.min(jnp.where(score == best, lane, n_l), axis=-1, keepdims=True)
            hit = lane == idx
            sel = jnp.where(hit, 1.0, sel)
            score = jnp.where(hit, -jnp.inf, score)
        sel_ref[...] = sel
        m_ref[...] = jnp.full(m_ref.shape, NEG_INF, f32)
        l_ref[...] = jnp.zeros(l_ref.shape, f32)
        acc_ref[...] = jnp.zeros(acc_ref.shape, f32)

    def online(s, ok, v):
        m_old = m_ref[...]
        s = jnp.where(ok, s, NEG_INF)
        m_new = jnp.maximum(m_old, jnp.max(s, axis=-1, keepdims=True))
        p = jnp.where(ok, jnp.exp(s - m_new), 0.0)
        alpha = jnp.exp(m_old - m_new)
        l_ref[...] = alpha * l_ref[...] + jnp.sum(p, axis=-1, keepdims=True)
        acc_ref[...] = alpha * acc_ref[...] + jnp.dot(p.astype(bf16), v, preferred_element_type=f32)
        m_ref[...] = m_new

    sel_tile = sel_ref[:, pl.ds(pl.multiple_of(tile * LANES, LANES), LANES)]
    mask_ref[...] = jnp.dot(sel_tile.astype(bf16), exp_ref[...], preferred_element_type=f32)
    sub_keys = SD_SUB * 128
    for sub in range(SD_PAGES // SD_SUB):
        pages = page_refs[sub * SD_SUB:(sub + 1) * SD_SUB]
        k = jnp.concatenate([r[0, :, :KV_WIDTH] for r in pages], axis=0).astype(bf16)
        v = jnp.concatenate([r[0, :, KV_WIDTH:] for r in pages], axis=0).astype(bf16)
        s = lax.dot_general(q, k, nt_dims, preferred_element_type=f32) + bs_ref[:, sub * sub_keys:(sub + 1) * sub_keys]
        ok = _rows_from_group(mask_ref[:, sub * sub_keys:(sub + 1) * sub_keys]) > 0.5
        online(s, ok, v)

    @pl.when(tile == n_tiles - 1)
    def _():
        lane = lax.broadcasted_iota(jnp.int32, (SD_ROWS, LANES), 1)
        new_blk = pos0 // SEL_BLOCK
        sel_new = _rows_from_group(sel_ref[:, new_blk:new_blk + 1]) > 0.5
        ok_n = sel_new & (lane < n_new) & (pos0 + lane <= qpos)
        s_n = lax.dot_general(q, knew_ref[0, :, :KV_WIDTH].astype(bf16), nt_dims, preferred_element_type=f32) + bn_ref[...]
        online(s_n, ok_n, knew_ref[0, :, KV_WIDTH:].astype(bf16))
        o_s = acc_ref[...] / jnp.maximum(l_ref[...], 1e-30)
        j = lax.broadcasted_iota(jnp.int32, (SD_ROWS, WIN_PAD), 1)
        kwpos = pos0 - win_len + j
        dist = qpos - kwpos
        ok_w = (dist >= 0) & (dist < WINDOW) & (kwpos >= 0) & (j < win_len + n_new)
        s_w = lax.dot_general(q, wh_ref[0, :, :KV_WIDTH].astype(bf16), nt_dims, preferred_element_type=f32) + bw_ref[...]
        o_w = jnp.dot(_masked_softmax(s_w, ok_w).astype(bf16), wh_ref[0, :, KV_WIDTH:].astype(bf16), preferred_element_type=f32)
        g = jax.nn.sigmoid(g_ref[0])
        o = g[:, 0:1] * oc_ref[...] + g[:, 1:2] * o_s + g[:, 2:3] * o_w
        row = lax.broadcasted_iota(jnp.int32, (SD_ROWS, HEAD_DIM), 0)
        o_ref[0] = jnp.where(row < SD_ROWS // NSA_KV_HEADS, o[:, :HEAD_DIM], o[:, HEAD_DIM:])


def nsa_decode(q, glog, kv_new, kcvc, pool, page_table, cache_win, thr, rel_bias, pos0):
    f32, bf16 = jnp.float32, jnp.bfloat16
    b, t = q.shape[:2]
    n_pages = page_table.shape[1]
    win_len = cache_win.shape[1]
    assert t * NSA_HEADS == SD_ROWS and pos0 == n_pages * 128 and n_pages % SD_PAGES == 0 and pos0 % SEL_BLOCK == 0
    assert win_len + t <= WIN_PAD and t <= SEL_BLOCK and SD_KEYS == LANES * SEL_BLOCK
    n_rows = pos0 + t
    n_cmp = (n_rows - CMP_BLOCK) // CMP_STRIDE + 1
    n_sel = -(-n_rows // SEL_BLOCK)
    n_c = kcvc.shape[1]
    sel_lanes = -(-n_sel // LANES) * LANES
    q5 = jnp.transpose(q.reshape(b, t, NSA_KV_HEADS, GRP, HEAD_DIM), (0, 2, 3, 1, 4))
    qz = (q5[:, :, :, :, None, :] * jnp.eye(NSA_KV_HEADS, dtype=f32)[None, :, None, None, :, None]).reshape(b, SD_ROWS, KV_WIDTH)
    g3 = jnp.transpose(glog.reshape(b, t, NSA_HEADS, 3), (0, 2, 1, 3)).reshape(b, SD_ROWS, 3)
    knew = jnp.pad(kv_new[:, :, 2:4].reshape(b, t, 2 * KV_WIDTH), ((0, 0), (0, LANES - t), (0, 0)))
    whist = jnp.concatenate([cache_win.reshape(b, win_len, 2 * KV_WIDTH), kv_new[:, :, 4:6].reshape(b, t, 2 * KV_WIDTH),
                             jnp.zeros((b, WIN_PAD - win_len - t, 2 * KV_WIDTH), f32)], axis=1)
    qp = pos0 + jnp.arange(t, dtype=jnp.int32)[:, None]
    tab = lambda dist: bias_tables(dist[None], thr, rel_bias).reshape(SD_ROWS, dist.shape[1])
    bias_c = tab(qp - (CMP_STRIDE * jnp.arange(n_c, dtype=jnp.int32)[None, :] + CMP_BLOCK - 1 - CMP_STRIDE))
    bias_s = tab(qp - jnp.arange(pos0, dtype=jnp.int32)[None, :])
    bias_n = tab(qp - (pos0 + jnp.arange(LANES, dtype=jnp.int32)[None, :]))
    bias_w = tab(qp - (pos0 - win_len + jnp.arange(WIN_PAD, dtype=jnp.int32)[None, :]))
    m_idx = jnp.arange(n_c)[:, None]
    c_start = CMP_STRIDE * (m_idx - 1)
    s_start = SEL_BLOCK * jnp.arange(sel_lanes)[None, :]
    cover = ((c_start < s_start + SEL_BLOCK) & (c_start + CMP_BLOCK > s_start) & (m_idx >= 1) & (m_idx <= n_cmp)
             & (jnp.arange(sel_lanes)[None, :] < n_sel)).astype(bf16)
    expand = (jnp.arange(SD_KEYS)[None, :] // SEL_BLOCK == jnp.arange(LANES)[:, None]).astype(bf16)
    page_spec = lambda k: pl.BlockSpec((1, 128, 2 * KV_WIDTH), lambda bb, g, pt: (pt[bb * n_pages + g * SD_PAGES + k], 0, 1))
    per_seq = lambda a: pl.BlockSpec((1,) + a.shape[1:], lambda bb, g, pt: (bb,) + (0,) * (a.ndim - 1))
    full = lambda a: pl.BlockSpec(a.shape, lambda bb, g, pt: (0,) * a.ndim)
    kv_rows = NSA_KV_HEADS * t
    out = pl.pallas_call(
        functools.partial(_nsa_decode_body, pos0=pos0, n_cmp=n_cmp, n_sel=n_sel, n_new=t, win_len=win_len),
        grid_spec=pltpu.PrefetchScalarGridSpec(
            num_scalar_prefetch=1,
            grid=(b, n_pages // SD_PAGES),
            in_specs=[page_spec(k) for k in range(SD_PAGES)]
            + [per_seq(qz), per_seq(g3), per_seq(kcvc), per_seq(knew), per_seq(whist), full(bias_c),
               pl.BlockSpec((SD_ROWS, SD_KEYS), lambda bb, g, pt: (0, g)), full(bias_n), full(bias_w), full(cover), full(expand)],
            out_specs=pl.BlockSpec((1, SD_ROWS, HEAD_DIM), lambda bb, g, pt: (bb, 0, 0)),
            scratch_shapes=[pltpu.VMEM((kv_rows, sel_lanes), f32), pltpu.VMEM((SD_ROWS, KV_WIDTH), f32),
                            pltpu.VMEM((SD_ROWS, 1), f32), pltpu.VMEM((SD_ROWS, 1), f32), pltpu.VMEM((SD_ROWS, KV_WIDTH), f32),
                            pltpu.VMEM((kv_rows, SD_KEYS), f32)],
        ),
        out_shape=jax.ShapeDtypeStruct((b, SD_ROWS, HEAD_DIM), f32),
        compiler_params=pltpu.CompilerParams(vmem_limit_bytes=VMEM_LIMIT),
        name="nsa_decode",
    )(page_table.reshape(-1).astype(jnp.int32), *([pool] * SD_PAGES), qz, g3, kcvc, knew, whist,
      bias_c, bias_s, bias_n, bias_w, cover, expand)
    out = jnp.transpose(out.reshape(b, NSA_KV_HEADS, GRP, t, HEAD_DIM), (0, 3, 1, 2, 4))
    return out.reshape(b, t, NSA_WIDTH)


def rms_norm(x, g):
    xf = x.astype(jnp.float32)
    y = xf * lax.rsqrt(jnp.mean(xf * xf, axis=-1, keepdims=True) + RMS_EPS)
    return (y * g.astype(jnp.float32)).astype(x.dtype)


def masked_softmax(s, mask):
    s = jnp.where(mask, s.astype(jnp.float32), NEG_INF)
    m = jnp.max(s, axis=-1, keepdims=True)
    e = jnp.where(mask, jnp.exp(s - m), 0.0)
    return e / jnp.maximum(jnp.sum(e, axis=-1, keepdims=True), 1e-30)


def compress_rows(x, pe, w1, b1, w2, b2, n_rows):
    b = x.shape[0]
    nc = (n_rows - CMP_BLOCK) // CMP_STRIDE + 1
    r = CMP_BLOCK // CMP_STRIDE
    nch = nc + r - 1
    chunks = x[:, :nch * CMP_STRIDE].reshape(b, nch, CMP_STRIDE, NSA_KV_HEADS, HEAD_DIM)
    part = jnp.einsum('bcjhd,rjdf->bcrhf', chunks, w1.reshape(r, CMP_STRIDE, HEAD_DIM, CMP_HIDDEN))
    h1 = jnp.einsum('ld,ldf->f', pe, w1) + b1
    for o in range(r):
        h1 = h1 + part[:, o:o + nc, o]
    return jnp.einsum('bchf,fd->bchd', jax.nn.gelu(h1), w2) + b2


def nsa_query_block(q, qpos, gates, kw, vw, kwpos, kc, vc, c_end, ks_b, vs_b, cover, rel_bias):
    b, nq = q.shape[:2]
    grp = NSA_HEADS // NSA_KV_HEADS
    qg = (q * HEAD_DIM ** -0.5).reshape(b, nq, NSA_KV_HEADS, grp, HEAD_DIM)
    rb = rel_bias.reshape(REL_BUCKETS, NSA_KV_HEADS, grp)
    hi = jnp.arange(NSA_KV_HEADS)[None, None, :, None]
    bi = jnp.arange(b)[:, None, None, None]
    dist_c = qpos[:, None] - c_end[None, :]
    bias_c = jnp.moveaxis(rb[rel_bucket(dist_c)], 1, -1)
    s_c = jnp.einsum('bqhgd,bchd->bqhgc', qg, kc) + bias_c
    p_c = masked_softmax(s_c, (dist_c >= 0)[:, None, None, :])
    o_c = jnp.einsum('bqhgc,bchd->bqhgd', p_c.astype(vc.dtype), vc)
    imp = jnp.einsum('bqhgc,cs->bqhs', p_c, cover)
    ns = ks_b.shape[1]
    blk = jnp.arange(ns)[None, :]
    cur = (qpos // SEL_BLOCK)[:, None]
    forced = ((blk == 0) | (blk == cur) | (blk == cur - 1))[:, None, :]
    valid = (blk * SEL_BLOCK <= qpos[:, None])[:, None, :]
    score = jnp.where(forced, FORCE_SCORE, jnp.where(valid, imp, -1.0))
    _, idx = lax.top_k(score, min(SEL_TOP_N, ns))
    k_s = ks_b[bi, idx, :, hi].reshape(b, nq, NSA_KV_HEADS, -1, HEAD_DIM)
    v_s = vs_b[bi, idx, :, hi].reshape(b, nq, NSA_KV_HEADS, -1, HEAD_DIM)
    s_pos = (idx[..., None] * SEL_BLOCK + jnp.arange(SEL_BLOCK)).reshape(b, nq, NSA_KV_HEADS, -1)
    dist_s = qpos[None, :, None, None] - s_pos
    bias_s = jnp.moveaxis(rb[rel_bucket(dist_s), hi], -1, 3)
    s_s = jnp.einsum('bqhgd,bqhkd->bqhgk', qg, k_s) + bias_s
    p_s = masked_softmax(s_s, (dist_s >= 0)[:, :, :, None, :])
    o_s = jnp.einsum('bqhgk,bqhkd->bqhgd', p_s.astype(v_s.dtype), v_s)
    dist_w = qpos[:, None] - kwpos[None, :]
    mask_w = (dist_w >= 0) & (dist_w < WINDOW) & (kwpos >= 0)[None, :]
    bias_w = jnp.moveaxis(rb[rel_bucket(dist_w)], 1, -1)
    s_w = jnp.einsum('bqhgd,bkhd->bqhgk', qg, kw) + bias_w
    p_w = masked_softmax(s_w, mask_w[:, None, None, :])
    o_w = jnp.einsum('bqhgk,bkhd->bqhgd', p_w.astype(vw.dtype), vw)
    g = gates.reshape(b, nq, NSA_KV_HEADS, grp, 3)
    o = g[..., 0:1] * o_c + g[..., 1:2] * o_s + g[..., 2:3] * o_w
    return o.reshape(b, nq, NSA_WIDTH)


def s5_mixer(u, h0_re, h0_im, lam_re, lam_im, log_dt, b_re, b_im, c_re, c_im, d_skip, w_glu, b_glu):
    bsz, t, _ = u.shape
    f32 = jnp.float32
    uf = u.astype(f32).reshape(bsz, t, SSM_GROUPS, SSM_CH)
    dt = jnp.exp(log_dt.astype(f32))[:, None]
    lr, li = lam_re.astype(f32), lam_im.astype(f32)
    mag = jnp.exp(lr * dt)
    a_re, a_im = mag * jnp.cos(li * dt), mag * jnp.sin(li * dt)
    den = lr * lr + li * li
    z_re = ((a_re - 1.0) * lr + a_im * li) / den
    z_im = (a_im * lr - (a_re - 1.0) * li) / den
    br, bim = b_re.astype(f32), b_im.astype(f32)
    bb_re = z_re[..., None] * br - z_im[..., None] * bim
    bb_im = z_re[..., None] * bim + z_im[..., None] * br
    x_re = jnp.einsum('btgc,gpc->btgp', uf, bb_re)
    x_im = jnp.einsum('btgc,gpc->btgp', uf, bb_im)
    ar = jnp.broadcast_to(a_re, x_re.shape)
    ai = jnp.broadcast_to(a_im, x_re.shape)

    def combine(e1, e2):
        a1r, a1i, b1r, b1i = e1
        a2r, a2i, b2r, b2i = e2
        return (a2r * a1r - a2i * a1i, a2r * a1i + a2i * a1r,
                a2r * b1r - a2i * b1i + b2r, a2r * b1i + a2i * b1r + b2i)

    pr, pim, sr, si = lax.associative_scan(combine, (ar, ai, x_re, x_im), axis=1)
    h0r = h0_re.astype(f32)[:, None]
    h0i = h0_im.astype(f32)[:, None]
    h_re = pr * h0r - pim * h0i + sr
    h_im = pr * h0i + pim * h0r + si
    y = jnp.einsum('btgp,gcp->btgc', h_re, c_re.astype(f32)) - jnp.einsum('btgp,gcp->btgc', h_im, c_im.astype(f32))
    y = jax.nn.gelu(y.reshape(bsz, t, SSM_WIDTH) + d_skip.astype(f32) * u.astype(f32))
    out = y * jax.nn.sigmoid(y @ w_glu.astype(f32) + b_glu.astype(f32))
    return out.astype(u.dtype), h_re[:, -1].astype(u.dtype), h_im[:, -1].astype(u.dtype)


def moe_ffn(x, router_w, router_b, w_gu, b_gu, w_down, b_down):
    n = x.shape[0]
    router_pad = jnp.pad(router_w, ((0, 0), (0, LANES - N_EXPERTS)))
    logits = pallas_matmul(x, router_pad)[:, :N_EXPERTS] + router_b.astype(jnp.float32)
    top_val, top_idx = lax.top_k(logits, TOP_K)
    gate = jax.nn.softmax(top_val, axis=-1)
    nk = n * TOP_K
    n_items = N_EXPERTS + nk // MOE_ROWS
    flat_e = top_idx.reshape(nk)
    onehot = (flat_e[:, None] == jnp.arange(N_EXPERTS, dtype=flat_e.dtype)[None, :]).astype(jnp.int32)
    running = jnp.cumsum(onehot, axis=0)
    counts = running[-1]
    pos_in_e = jnp.sum(onehot * running, axis=1) - 1
    items_e = (counts + MOE_ROWS - 1) // MOE_ROWS
    item_end = jnp.cumsum(items_e)
    item_start = item_end - items_e
    total = item_end[-1]
    dest = (item_start[flat_e] * MOE_ROWS + pos_in_e).astype(jnp.int32)
    row_tok = jnp.zeros(n_items * MOE_ROWS, jnp.int32).at[dest].set(jnp.arange(nk, dtype=jnp.int32) // TOP_K)
    item = jnp.minimum(jnp.arange(n_items, dtype=jnp.int32), total - 1)
    item_e = jnp.minimum(jnp.searchsorted(item_end, item, side='right'), N_EXPERTS - 1).astype(jnp.int32)
    rows = jnp.clip(counts[item_e] - (item - item_start[item_e]) * MOE_ROWS, 0, MOE_ROWS)
    item_sub = jnp.where(jnp.arange(n_items) < total, (rows + MOE_SUB - 1) // MOE_SUB, 0).astype(jnp.int32)
    xb = x.astype(jnp.bfloat16)[row_tok]
    item_ok = (jnp.arange(n_items) < total).astype(jnp.int32)
    yb = moe_experts(item_e, item_sub, item.astype(jnp.int32), item_ok, xb, w_gu, b_gu, w_down, b_down)
    y = yb[dest].reshape(n, TOP_K, -1)
    return jnp.sum(gate[:, :, None] * y, axis=1)


MOE_ROWS = 1280
MOE_SUB = 256
MOE_TF = 256


def _moe_body(e_ref, sub_ref, blk_ref, ok_ref, x_ref, wg_ref, wl_ref, bg_ref, bl_ref, wd_ref, bd_ref, o_ref):
    bf16 = jnp.bfloat16
    i, f = pl.program_id(0), pl.program_id(1)
    n_sub = sub_ref[i]
    wg = wg_ref[0].astype(bf16)
    wl = wl_ref[0].astype(bf16)
    wd = wd_ref[0].astype(bf16)

    def sub_block(j, _):
        rows = pl.ds(pl.multiple_of(j * MOE_SUB, MOE_SUB), MOE_SUB)
        x = x_ref[rows, :]
        hg = jnp.dot(x, wg, preferred_element_type=jnp.float32) + bg_ref[0]
        hl = jnp.dot(x, wl, preferred_element_type=jnp.float32) + bl_ref[0]
        hg = jnp.minimum(hg, SWIGLU_LIMIT)
        hl = jnp.clip(hl, -SWIGLU_LIMIT, SWIGLU_LIMIT)
        act = hg * jax.nn.sigmoid(SWIGLU_ALPHA * hg) * (hl + 1.0)
        y = jnp.dot(act.astype(bf16), wd, preferred_element_type=jnp.float32)

        @pl.when(f == 0)
        def _():
            o_ref[rows, :] = y + bd_ref[0]

        @pl.when(f > 0)
        def _():
            o_ref[rows, :] += y

        return 0

    lax.fori_loop(0, n_sub, sub_block, 0)

    @pl.when(f == 0)
    def _():
        def zero_block(j, _):
            o_ref[pl.ds(pl.multiple_of(j * MOE_SUB, MOE_SUB), MOE_SUB), :] = jnp.zeros((MOE_SUB, o_ref.shape[1]), jnp.float32)
            return 0

        lax.fori_loop(n_sub, MOE_ROWS // MOE_SUB, zero_block, 0)


def moe_experts(item_e, item_sub, item_blk, item_ok, xb, w_gu, b_gu, w_down, b_down):
    n_items = item_e.shape[0]
    n_f = D_FF // MOE_TF
    d = xb.shape[1]

    def col(f, ok, i):
        return f * ok[i] + (n_f - 1) * (1 - ok[i])

    return pl.pallas_call(
        _moe_body,
        grid_spec=pltpu.PrefetchScalarGridSpec(
            num_scalar_prefetch=4,
            grid=(n_items, n_f),
            in_specs=[pl.BlockSpec((MOE_ROWS, d), lambda i, f, e, s, b, ok: (b[i], 0)),
                      pl.BlockSpec((1, d, MOE_TF), lambda i, f, e, s, b, ok: (e[i], 0, col(f, ok, i))),
                      pl.BlockSpec((1, d, MOE_TF), lambda i, f, e, s, b, ok: (e[i], 0, n_f + col(f, ok, i))),
                      pl.BlockSpec((1, 1, MOE_TF), lambda i, f, e, s, b, ok: (e[i], 0, col(f, ok, i))),
                      pl.BlockSpec((1, 1, MOE_TF), lambda i, f, e, s, b, ok: (e[i], 0, n_f + col(f, ok, i))),
                      pl.BlockSpec((1, MOE_TF, d), lambda i, f, e, s, b, ok: (e[i], col(f, ok, i), 0)),
                      pl.BlockSpec((1, 1, d), lambda i, f, e, s, b, ok: (e[i], 0, 0))],
            out_specs=pl.BlockSpec((MOE_ROWS, d), lambda i, f, e, s, b, ok: (i, 0)),
        ),
        out_shape=jax.ShapeDtypeStruct((n_items * MOE_ROWS, d), jnp.float32),
        compiler_params=pltpu.CompilerParams(vmem_limit_bytes=VMEM_LIMIT),
        name="moe_experts",
    )(item_e, item_sub, item_blk, item_ok, xb, w_gu, w_gu, b_gu[:, None, :], b_gu[:, None, :], w_down, b_down[:, None, :])


def layer_forward(x, pos0, past_kv, win_buf, h0_re, h0_im, rel_bias, lw):
    b, t, _ = x.shape
    h = rms_norm(x, lw['norm_mix'])
    proj = h @ lw['w_in']
    o1 = NSA_WIDTH
    o2 = o1 + N_KV * KV_WIDTH
    o3 = o2 + 3 * NSA_HEADS
    q = proj[..., :o1].reshape(b, t, NSA_HEADS, HEAD_DIM)
    kv_new = proj[..., o1:o2].reshape(b, t, N_KV, NSA_KV_HEADS, HEAD_DIM)
    gates = jax.nn.sigmoid(proj[..., o2:o3].astype(jnp.float32)).reshape(b, t, NSA_HEADS, 3)
    u = proj[..., o3:]
    paged_new = kv_new[:, :, :N_PAGED]
    win_new = kv_new[:, :, N_PAGED:]
    qpos = pos0 + jnp.arange(t, dtype=jnp.int32)

    n_rows = pos0 + t
    ns = -(-n_rows // SEL_BLOCK)
    pad = jnp.zeros((b, ns * SEL_BLOCK - n_rows) + paged_new.shape[2:], paged_new.dtype)
    rows = jnp.concatenate(([] if past_kv is None else [past_kv]) + [paged_new, pad], axis=1)
    kc = compress_rows(rows[:, :, 0], lw['cmp_pe'][0], lw['cmp_w1'][0], lw['cmp_b1'][0], lw['cmp_w2'][0], lw['cmp_b2'][0], n_rows)
    vc = compress_rows(rows[:, :, 1], lw['cmp_pe'][1], lw['cmp_w1'][1], lw['cmp_b1'][1], lw['cmp_w2'][1], lw['cmp_b2'][1], n_rows)
    nc = kc.shape[1]
    c_start = jnp.arange(nc) * CMP_STRIDE
    c_end = c_start + CMP_BLOCK - 1
    s_start = jnp.arange(ns) * SEL_BLOCK
    cover = ((c_start[:, None] < s_start[None, :] + SEL_BLOCK) & (c_start[:, None] + CMP_BLOCK > s_start[None, :])).astype(jnp.float32)
    ks_b = rows[:, :, 2].reshape(b, ns, SEL_BLOCK, NSA_KV_HEADS, HEAD_DIM)
    vs_b = rows[:, :, 3].reshape(b, ns, SEL_BLOCK, NSA_KV_HEADS, HEAD_DIM)

    if win_buf is None:
        new_win = win_new[:, t - min(WINDOW, n_rows):]
        tr = lambda a: jnp.transpose(a, (0, 2, 1, 3))
        pad_c = lambda a: tr(jnp.pad(a, ((0, 0), (0, LANES - nc), (0, 0), (0, 0))))
        g4 = tr(proj[..., o2:o3].reshape(b, t, NSA_HEADS, 3))
        attn4 = nsa_prompt(tr(q), g4, pad_c(kc), pad_c(vc), tr(kv_new[:, :, 2]), tr(kv_new[:, :, 3]),
                           tr(kv_new[:, :, 4]), tr(kv_new[:, :, 5]), bucket_thresholds(), rel_bias, nc)
        attn = tr(attn4).reshape(b, t, NSA_WIDTH).astype(x.dtype)
        return _layer_tail(x, attn, u, h0_re, h0_im, lw, paged_new, new_win, True)
    if win_buf is None:
        qb = min(Q_BLOCK, t)
        nb = t // qb
        wpad = jnp.concatenate([jnp.zeros((b, WINDOW) + win_new.shape[2:], win_new.dtype), win_new], axis=1)
        widx = jnp.arange(nb)[:, None] * qb + jnp.arange(qb + WINDOW)[None, :]
        w_blocks = jnp.moveaxis(wpad[:, widx], 1, 0)
        w_pos = widx - WINDOW
        w_hist = win_new
    else:
        qb, nb = t, 1
        w_hist = jnp.concatenate([win_buf, win_new], axis=1)
        w_blocks = w_hist[None]
        w_pos = (pos0 - win_buf.shape[1] + jnp.arange(w_hist.shape[1], dtype=jnp.int32))[None]
    new_win = w_hist[:, w_hist.shape[1] - min(WINDOW, n_rows):]

    def run(args):
        q_b, qpos_b, g_b, w_b, wpos_b = args
        return nsa_query_block(q_b, qpos_b, g_b, w_b[:, :, 0], w_b[:, :, 1], wpos_b,
                               kc, vc, c_end, ks_b, vs_b, cover, rel_bias)

    q_blocks = jnp.moveaxis(q.reshape(b, nb, qb, NSA_HEADS, HEAD_DIM), 1, 0)
    g_blocks = jnp.moveaxis(gates.reshape(b, nb, qb, NSA_HEADS, 3), 1, 0)
    attn = lax.map(run, (q_blocks, qpos.reshape(nb, qb), g_blocks, w_blocks, w_pos))
    attn = jnp.moveaxis(attn, 0, 1).reshape(b, t, NSA_WIDTH).astype(x.dtype)
    return _layer_tail(x, attn, u, h0_re, h0_im, lw, paged_new, new_win, False)


def mixer_layer(x, pos0, pool, page_table, win_buf, h0_re, h0_im, rel_bias, lw):
    b, t, _ = x.shape
    h = rms_norm(x, lw['norm_mix'])
    proj = h @ lw['w_in']
    o1 = NSA_WIDTH
    o2 = o1 + N_KV * KV_WIDTH
    o3 = o2 + 3 * NSA_HEADS
    q = proj[..., :o1].reshape(b, t, NSA_HEADS, HEAD_DIM)
    kv_new = proj[..., o1:o2].reshape(b, t, N_KV, NSA_KV_HEADS, HEAD_DIM)
    u = proj[..., o3:]
    paged_new = kv_new[:, :, :N_PAGED]
    win_new = kv_new[:, :, N_PAGED:]
    n_rows = pos0 + t
    n_cmp = (n_rows - CMP_BLOCK) // CMP_STRIDE + 1
    thr = bucket_thresholds()
    cmp_w = (lw['cmp_pe'], lw['cmp_w1'], lw['cmp_b1'], lw['cmp_w2'], lw['cmp_b2'])
    if pool is None:
        assert pos0 == 0 and t % 128 == 0
        pages = t // 128
        own_pool = paged_new.reshape(b * pages, 128, ROW_LANES)
        own_table = jnp.arange(b * pages, dtype=jnp.int32).reshape(b, pages)
        kcvc = compress_pages(own_pool, own_table, *cmp_w, n_g=pages)
        tr = lambda a: jnp.transpose(a, (0, 2, 1, 3))
        cmp_tok = lambda a: tr(jnp.pad(a[:, 1:], ((0, 0), (0, 1), (0, 0))).reshape(b, LANES, NSA_KV_HEADS, HEAD_DIM))
        g4 = tr(proj[..., o2:o3].reshape(b, t, NSA_HEADS, 3))
        attn4 = nsa_prompt(tr(q), g4, cmp_tok(kcvc[:, :, :KV_WIDTH]), cmp_tok(kcvc[:, :, KV_WIDTH:]),
                           tr(kv_new[:, :, 2]), tr(kv_new[:, :, 3]), tr(kv_new[:, :, 4]), tr(kv_new[:, :, 5]),
                           thr, rel_bias, n_cmp)
        attn = tr(attn4).reshape(b, t, NSA_WIDTH).astype(x.dtype)
        w_hist = win_new
    else:
        kcvc = compress_pages(pool, page_table, *cmp_w, n_g=32)
        attn = nsa_decode(q, proj[..., o2:o3], kv_new, kcvc, pool, page_table, win_buf, thr, rel_bias, pos0).astype(x.dtype)
        w_hist = jnp.concatenate([win_buf, win_new], axis=1)
    new_win = w_hist[:, w_hist.shape[1] - min(WINDOW, n_rows):]
    return _layer_tail(x, attn, u, h0_re, h0_im, lw, paged_new, new_win, pool is None)


def _layer_tail(x, attn, u, h0_re, h0_im, lw, paged_new, new_win, chained):
    b, t, _ = x.shape
    ssm, h_re, h_im = s5_mixer_pallas(u, h0_re, h0_im, lw['lam_re'], lw['lam_im'], lw['log_dt'], lw['b_re'], lw['b_im'],
                                      lw['c_re'], lw['c_im'], lw['d_skip'], lw['w_glu'], lw['b_glu'], chained)
    merged = jnp.concatenate([rms_norm(attn, lw['norm_attn_out']), rms_norm(ssm, lw['norm_ssm_out'])], axis=-1)
    x = x + pallas_matmul(merged.reshape(b * t, D_MODEL), lw['w_out']).reshape(b, t, D_MODEL)
    return x, paged_new, new_win, h_re, h_im


def moe_residual(xs, lw):
    sizes = [x.shape[0] * x.shape[1] for x in xs]
    flat = jnp.concatenate([x.reshape(-1, D_MODEL) for x in xs], axis=0)
    hm = rms_norm(flat, lw['norm_ffn'])
    flat = flat + moe_ffn(hm, lw['router_w'], lw['router_b'], lw['w_gu'], lw['b_gu'], lw['w_down'], lw['b_down']).astype(flat.dtype)
    outs, start = [], 0
    for x, n in zip(xs, sizes):
        outs.append(flat[start:start + n].reshape(x.shape))
        start += n
    return outs


def kernel(x_prompt, x_sample, cache_nsa_kv, cache_win_kv, state_ssm_re, state_ssm_im, page_table, rel_bias,
           norm_mix, w_in, cmp_pe, cmp_w1, cmp_b1, cmp_w2, cmp_b2, ssm_lam_re, ssm_lam_im, ssm_log_dt,
           ssm_b_re, ssm_b_im, ssm_c_re, ssm_c_im, ssm_d, ssm_w_glu, ssm_b_glu, norm_attn_out, norm_ssm_out,
           w_out, norm_ffn, router_w, router_b, w_gu, b_gu, w_down, b_down, norm_final):
    n_seq, n_pages = page_table.shape
    past_len = n_pages * cache_nsa_kv.shape[2]
    xp, xs = x_prompt, x_sample
    kv_p, win_p, sre_p, sim_p = [], [], [], []
    kv_s, win_s, sre_s, sim_s = [], [], [], []
    for i in range(DEPTH):
        lw = dict(norm_mix=norm_mix[i], w_in=w_in[i], cmp_pe=cmp_pe[i], cmp_w1=cmp_w1[i], cmp_b1=cmp_b1[i],
                  cmp_w2=cmp_w2[i], cmp_b2=cmp_b2[i], lam_re=ssm_lam_re[i], lam_im=ssm_lam_im[i],
                  log_dt=ssm_log_dt[i], b_re=ssm_b_re[i], b_im=ssm_b_im[i], c_re=ssm_c_re[i], c_im=ssm_c_im[i],
                  d_skip=ssm_d[i], w_glu=ssm_w_glu[i], b_glu=ssm_b_glu[i], norm_attn_out=norm_attn_out[i],
                  norm_ssm_out=norm_ssm_out[i], w_out=w_out[i], norm_ffn=norm_ffn[i], router_w=router_w[i],
                  router_b=router_b[i], w_gu=w_gu[i], b_gu=b_gu[i], w_down=w_down[i], b_down=b_down[i])
        h0 = jnp.zeros((xp.shape[0], SSM_GROUPS, SSM_STATE), xp.dtype)
        xp, kv1, w1, r1, m1 = mixer_layer(xp, 0, None, None, None, h0, h0, rel_bias, lw)
        pool = cache_nsa_kv[i].reshape(cache_nsa_kv.shape[1], cache_nsa_kv.shape[2], ROW_LANES)
        xs, kv2, w2, r2, m2 = mixer_layer(xs, past_len, pool, page_table, cache_win_kv[i], state_ssm_re[i], state_ssm_im[i], rel_bias, lw)
        xp, xs = moe_residual([xp, xs], lw)
        kv_p.append(kv1); win_p.append(w1); sre_p.append(r1); sim_p.append(m1)
        kv_s.append(kv2); win_s.append(w2); sre_s.append(r2); sim_s.append(m2)
    y_prompt = rms_norm(xp, norm_final)
    y_sample = rms_norm(xs, norm_final)
    return (y_prompt, y_sample, jnp.stack(kv_p), jnp.stack(win_p), jnp.stack(sre_p), jnp.stack(sim_p),
            jnp.stack(kv_s), jnp.stack(win_s), jnp.stack(sre_s), jnp.stack(sim_s))
```

```python
import functools
import math
import jax, jax.numpy as jnp
from jax import lax
from jax.experimental import pallas as pl
from jax.experimental.pallas import tpu as pltpu

D_MODEL = 2048
DEPTH = 1
NSA_HEADS = 16
NSA_KV_HEADS = 2
HEAD_DIM = 64
NSA_WIDTH = NSA_HEADS * HEAD_DIM
KV_WIDTH = NSA_KV_HEADS * HEAD_DIM
N_PAGED = 4
N_KV = 6
SSM_WIDTH = D_MODEL - NSA_WIDTH
SSM_CH = 16
SSM_GROUPS = SSM_WIDTH // SSM_CH
SSM_STATE = 64
IN_WIDTH = NSA_WIDTH + N_KV * KV_WIDTH + 3 * NSA_HEADS + SSM_WIDTH
CMP_BLOCK = 32
CMP_STRIDE = 16
CMP_HIDDEN = 2 * HEAD_DIM
SEL_BLOCK = 64
SEL_TOP_N = 16
WINDOW = 512
Q_BLOCK = 64
FORCE_SCORE = 1e6
REL_BUCKETS = 32
REL_MAX_DIST = 4096
N_EXPERTS = 32
TOP_K = 4
D_FF = D_MODEL
SWIGLU_LIMIT = 7.0
SWIGLU_ALPHA = 1.702
RMS_EPS = 1e-5
NEG_INF = -1e30


def _mm_body(x_ref, w_ref, o_ref):
    o_ref[...] = jnp.dot(x_ref[...].astype(jnp.bfloat16), w_ref[...].astype(jnp.bfloat16),
                         preferred_element_type=jnp.float32)


def pallas_matmul(x, w, tm=256, tn=512):
    m, k = x.shape
    n = w.shape[1]
    tm = min(tm, m)
    tn = min(tn, n)
    return pl.pallas_call(
        _mm_body,
        grid=(m // tm, n // tn),
        in_specs=[pl.BlockSpec((tm, k), lambda i, j: (i, 0)),
                  pl.BlockSpec((k, tn), lambda i, j: (0, j))],
        out_specs=pl.BlockSpec((tm, tn), lambda i, j: (i, j)),
        out_shape=jax.ShapeDtypeStruct((m, n), jnp.float32),
        name="matmul",
    )(x, w)


GRP = NSA_HEADS // NSA_KV_HEADS
TQ = 128
TK = 128
LANES = 128
BUCKET_TABLE_LEN = 32768
BIAS_TILE_ELEMS = 16384
VMEM_LIMIT = 56 * 1024 * 1024


def rel_bucket(dist):
    n = jnp.maximum(dist, 0)
    exact = REL_BUCKETS // 2
    nf = jnp.maximum(n, 1).astype(jnp.float32)
    large = exact + (jnp.log(nf / exact) / math.log(REL_MAX_DIST / exact) * (REL_BUCKETS - exact)).astype(jnp.int32)
    return jnp.where(n < exact, n, jnp.minimum(large, REL_BUCKETS - 1))


def bucket_thresholds():
    tab = rel_bucket(jnp.arange(BUCKET_TABLE_LEN, dtype=jnp.int32))
    return jnp.sum(tab[None, :] < jnp.arange(REL_BUCKETS, dtype=jnp.int32)[:, None], axis=1).astype(jnp.int32)


def _bias_table_body(thr_ref, rb_ref, d_ref, o_ref):
    n = jnp.maximum(d_ref[0], 0)
    for h in range(NSA_HEADS):
        val = jnp.full(n.shape, rb_ref[h], jnp.float32)
        for k in range(1, REL_BUCKETS):
            val = jnp.where(n >= thr_ref[k], rb_ref[k * NSA_HEADS + h], val)
        o_ref[h // GRP, 0, h % GRP] = val


def bias_tables(dist, thr, rel_bias):
    n, r, c = dist.shape
    ct = min(c, BIAS_TILE_ELEMS // r)
    assert c % ct == 0
    return pl.pallas_call(
        _bias_table_body,
        grid_spec=pltpu.PrefetchScalarGridSpec(
            num_scalar_prefetch=2,
            grid=(n, c // ct),
            in_specs=[pl.BlockSpec((1, r, ct), lambda i, j, *_: (i, 0, j))],
            out_specs=pl.BlockSpec((NSA_KV_HEADS, 1, GRP, r, ct), lambda i, j, *_: (0, i, 0, 0, j)),
        ),
        out_shape=jax.ShapeDtypeStruct((NSA_KV_HEADS, n, GRP, r, c), jnp.float32),
        name="bias_tables",
    )(thr, rel_bias.reshape(-1), dist)


def _softmax_tile(s3, msk, m, l, acc, v):
    s3 = jnp.where(msk[None], s3, NEG_INF)
    m_new = jnp.maximum(m, jnp.max(s3, axis=-1, keepdims=True))
    p = jnp.where(msk[None], jnp.exp(s3 - m_new), 0.0)
    alpha = jnp.exp(m - m_new)
    l = alpha * l + jnp.sum(p, axis=-1, keepdims=True)
    pv = jnp.dot(p.reshape(GRP * TQ, p.shape[-1]).astype(jnp.bfloat16), v, preferred_element_type=jnp.float32)
    acc = alpha * acc + pv.reshape(GRP, TQ, HEAD_DIM)
    return m_new, l, acc


def _nsa_prompt_body(q_ref, g_ref, kc_ref, vc_ref, ks_ref, vs_ref, kw_ref, vw_ref, bt_ref, bc_ref, cov_ref, exp_ref,
                     o_ref, mask_ref, *, n_cmp, n_sel):
    f32, bf16 = jnp.float32, jnp.bfloat16
    qi = pl.program_id(2)
    nt_dims = (((1,), (1,)), ((), ()))
    q = (q_ref[0] * (HEAD_DIM ** -0.5)).reshape(GRP * TQ, HEAD_DIM).astype(bf16)
    qpos = qi * TQ + lax.broadcasted_iota(jnp.int32, (TQ, LANES), 0)
    lane = lax.broadcasted_iota(jnp.int32, (TQ, LANES), 1)

    ok_c = (qpos >= CMP_STRIDE * lane + (CMP_BLOCK - 1)) & (lane < n_cmp)
    s_c = lax.dot_general(q, kc_ref[0, 0].astype(bf16), nt_dims, preferred_element_type=f32) + bc_ref[0, 0].reshape(GRP * TQ, LANES)
    s_c = jnp.where(ok_c[None], s_c.reshape(GRP, TQ, LANES), NEG_INF)
    m_c = jnp.max(s_c, axis=-1, keepdims=True)
    e_c = jnp.where(ok_c[None], jnp.exp(s_c - m_c), 0.0)
    p_c = e_c / jnp.maximum(jnp.sum(e_c, axis=-1, keepdims=True), 1e-30)
    o_c = jnp.dot(p_c.reshape(GRP * TQ, LANES).astype(bf16), vc_ref[0, 0].astype(bf16),
                  preferred_element_type=f32).reshape(GRP, TQ, HEAD_DIM)

    imp = jnp.dot(jnp.sum(p_c, axis=0).astype(bf16), cov_ref[...], preferred_element_type=f32)
    cur = qpos // SEL_BLOCK
    forced = (lane == 0) | (lane == cur) | (lane == cur - 1)
    score = jnp.where(forced, FORCE_SCORE, jnp.where(lane * SEL_BLOCK <= qpos, imp, -1.0))
    score = jnp.where(lane < n_sel, score, -jnp.inf)
    sel = jnp.zeros((TQ, LANES), f32)
    for _ in range(min(SEL_TOP_N, n_sel)):
        best = jnp.max(score, axis=-1, keepdims=True)
        idx = jnp.min(jnp.where(score == best, lane, LANES), axis=-1, keepdims=True)
        hit = lane == idx
        sel = jnp.where(hit, 1.0, sel)
        score = jnp.where(hit, -jnp.inf, score)
    mask_ref[...] = jnp.dot(sel.astype(bf16), exp_ref[...], preferred_element_type=f32)

    row = lax.broadcasted_iota(jnp.int32, (TQ, TK), 0)
    col = lax.broadcasted_iota(jnp.int32, (TQ, TK), 1)
    init = (jnp.full((GRP, TQ, 1), NEG_INF, f32), jnp.zeros((GRP, TQ, 1), f32), jnp.zeros((GRP, TQ, HEAD_DIM), f32))

    def scores(k_ref, ki):
        k = k_ref[0, 0, pl.ds(pl.multiple_of(ki * TK, TK), TK), :].astype(bf16)
        s = lax.dot_general(q, k, nt_dims, preferred_element_type=f32)
        return s.reshape(GRP, TQ, TK) + bt_ref[0, qi - ki]

    def sel_step(ki, carry):
        dist = (qi - ki) * TQ + row - col
        msk = (mask_ref[:, pl.ds(pl.multiple_of(ki * TK, TK), TK)] > 0.5) & (dist >= 0)
        v = vs_ref[0, 0, pl.ds(pl.multiple_of(ki * TK, TK), TK), :].astype(bf16)
        return _softmax_tile(scores(ks_ref, ki), msk, *carry, v)

    def win_step(ki, carry):
        dist = (qi - ki) * TQ + row - col
        msk = (dist >= 0) & (dist < WINDOW)
        v = vw_ref[0, 0, pl.ds(pl.multiple_of(ki * TK, TK), TK), :].astype(bf16)
        return _softmax_tile(scores(kw_ref, ki), msk, *carry, v)

    _, l_s, acc_s = lax.fori_loop(0, qi + 1, sel_step, init)
    _, l_w, acc_w = lax.fori_loop(jnp.maximum(qi - WINDOW // TK, 0), qi + 1, win_step, init)
    o_s = acc_s / jnp.maximum(l_s, 1e-30)
    o_w = acc_w / jnp.maximum(l_w, 1e-30)
    g = jax.nn.sigmoid(g_ref[0])
    o_ref[0] = g[:, :, 0:1] * o_c + g[:, :, 1:2] * o_s + g[:, :, 2:3] * o_w


def nsa_prompt(q4, g4, kc, vc, ks, vs, kw, vw, thr, rel_bias, n_cmp):
    b, _, t, _ = q4.shape
    nq = t // TQ
    n_sel = t // SEL_BLOCK
    assert t % TQ == 0 and n_sel <= LANES and n_cmp <= LANES and TQ == TK
    i = jnp.arange(TQ, dtype=jnp.int32)
    d_toep = jnp.arange(nq, dtype=jnp.int32)[:, None, None] * TQ + i[:, None] - i[None, :]
    d_cmp = (jnp.arange(t, dtype=jnp.int32)[:, None] - (CMP_STRIDE * jnp.arange(LANES, dtype=jnp.int32)[None, :] + CMP_BLOCK - 1)).reshape(nq, TQ, LANES)
    tabs = bias_tables(jnp.concatenate([d_toep, d_cmp], axis=0), thr, rel_bias)
    bt, bc = tabs[:, :nq], tabs[:, nq:]
    c_start = CMP_STRIDE * jnp.arange(LANES)[:, None]
    s_start = SEL_BLOCK * jnp.arange(LANES)[None, :]
    cover = ((c_start < s_start + SEL_BLOCK) & (c_start + CMP_BLOCK > s_start)
             & (jnp.arange(LANES)[:, None] < n_cmp) & (jnp.arange(LANES)[None, :] < n_sel)).astype(jnp.bfloat16)
    expand = (jnp.arange(t)[None, :] // SEL_BLOCK == jnp.arange(LANES)[:, None]).astype(jnp.bfloat16)
    kv_spec = lambda rows: pl.BlockSpec((1, 1, rows, HEAD_DIM), lambda h, bb, qq: (bb, h, 0, 0))
    return pl.pallas_call(
        functools.partial(_nsa_prompt_body, n_cmp=n_cmp, n_sel=n_sel),
        grid=(NSA_KV_HEADS, b, nq),
        in_specs=[pl.BlockSpec((1, GRP, TQ, HEAD_DIM), lambda h, bb, qq: (bb, h, qq, 0)),
                  pl.BlockSpec((1, GRP, TQ, 3), lambda h, bb, qq: (bb, h, qq, 0)),
                  kv_spec(LANES), kv_spec(LANES), kv_spec(t), kv_spec(t), kv_spec(t), kv_spec(t),
                  pl.BlockSpec((1, nq, GRP, TQ, TK), lambda h, bb, qq: (h, 0, 0, 0, 0)),
                  pl.BlockSpec((1, 1, GRP, TQ, LANES), lambda h, bb, qq: (h, qq, 0, 0, 0)),
                  pl.BlockSpec((LANES, LANES), lambda h, bb, qq: (0, 0)),
                  pl.BlockSpec((LANES, t), lambda h, bb, qq: (0, 0))],
        out_specs=pl.BlockSpec((1, GRP, TQ, HEAD_DIM), lambda h, bb, qq: (bb, h, qq, 0)),
        out_shape=jax.ShapeDtypeStruct(q4.shape, jnp.float32),
        scratch_shapes=[pltpu.VMEM((TQ, t), jnp.float32)],
        compiler_params=pltpu.CompilerParams(vmem_limit_bytes=VMEM_LIMIT),
        name="nsa_prompt",
    )(q4, g4, kc, vc, ks, vs, kw, vw, bt, bc, cover, expand)


S5_GB = 8
S5_CH = S5_GB * SSM_CH
S5_ST = S5_GB * SSM_STATE


def _cmul(a_re, a_im, b_re, b_im):
    return a_re * b_re - a_im * b_im, a_re * b_im + a_im * b_re


def _s5_body(u_ref, h0re_ref, h0im_ref, a_ref, bcat_ref, ccat_ref, d_ref, y_ref, hre_ref, him_ref, xs_ref, hin_ref,
             *, chained):
    f32, bf16 = jnp.float32, jnp.bfloat16
    n_l, n_r = u_ref.shape[1], u_ref.shape[2]
    u2 = u_ref[0].reshape(n_l * n_r, S5_CH)
    xs_ref[...] = jnp.dot(u2.astype(bf16), bcat_ref[0].astype(bf16),
                          preferred_element_type=f32).reshape(n_l, n_r, 2 * S5_ST)
    a_re, a_im = a_ref[0:1, :], a_ref[1:2, :]

    def scan_step(j, carry):
        h_re, h_im, p_re, p_im = carry
        x = xs_ref[j]
        t_re, t_im = _cmul(a_re, a_im, h_re, h_im)
        h_re, h_im = t_re + x[:, :S5_ST], t_im + x[:, S5_ST:]
        xs_ref[j] = jnp.concatenate([h_re, h_im], axis=1)
        return (h_re, h_im) + _cmul(a_re, a_im, p_re, p_im)

    if chained:
        start = (jnp.zeros((n_r, S5_ST), f32), jnp.zeros((n_r, S5_ST), f32))
    else:
        start = (h0re_ref[0], h0im_ref[0])
    ones = (jnp.ones((1, S5_ST), f32), jnp.zeros((1, S5_ST), f32))
    h_re, h_im, al_re, al_im = lax.fori_loop(0, n_l, scan_step, start + ones)

    if chained:
        hin_ref[0:1, :] = jnp.concatenate([h0re_ref[0], h0im_ref[0]], axis=1)

        def chain_step(c, carry):
            z = xs_ref[n_l - 1, pl.ds(c - 1, 1), :]
            t_re, t_im = _cmul(al_re, al_im, *carry)
            n_re, n_im = t_re + z[:, :S5_ST], t_im + z[:, S5_ST:]
            hin_ref[pl.ds(c, 1), :] = jnp.concatenate([n_re, n_im], axis=1)
            return n_re, n_im

        h_re, h_im = lax.fori_loop(1, n_r + 1, chain_step, (h0re_ref[0], h0im_ref[0]))

        def fix_step(j, carry):
            p_re, p_im = carry
            hin = hin_ref[0:n_r, :]
            t_re, t_im = _cmul(p_re, p_im, hin[:, :S5_ST], hin[:, S5_ST:])
            xs_ref[j] = xs_ref[j] + jnp.concatenate([t_re, t_im], axis=1)
            return _cmul(a_re, a_im, p_re, p_im)

        lax.fori_loop(0, n_l, fix_step, (a_re, a_im))

    hre_ref[0] = h_re
    him_ref[0] = h_im
    hs = xs_ref[...].reshape(n_l * n_r, 2 * S5_ST).astype(bf16)
    y = jnp.dot(hs, ccat_ref[0].astype(bf16), preferred_element_type=f32) + d_ref[...] * u2
    y_ref[0] = jax.nn.gelu(y).reshape(n_l, n_r, S5_CH)


def s5_discretize(lam_re, lam_im, log_dt, b_re, b_im, c_re, c_im):
    f32 = jnp.float32
    dt = jnp.exp(log_dt.astype(f32))[:, None]
    lr, li = lam_re.astype(f32), lam_im.astype(f32)
    mag = jnp.exp(lr * dt)
    a_re, a_im = mag * jnp.cos(li * dt), mag * jnp.sin(li * dt)
    den = lr * lr + li * li
    z_re = ((a_re - 1.0) * lr + a_im * li) / den
    z_im = (a_im * lr - (a_re - 1.0) * li) / den
    br, bim = b_re.astype(f32), b_im.astype(f32)
    bb_re = z_re[..., None] * br - z_im[..., None] * bim
    bb_im = z_re[..., None] * bim + z_im[..., None] * br
    ngb = SSM_GROUPS // S5_GB
    eye = jnp.eye(S5_GB, dtype=f32)

    def block_diag(w):
        wd = w[:, :, :, None, :] * eye[None, :, None, :, None]
        return wd.reshape(ngb, S5_GB * w.shape[2], S5_GB * w.shape[3])

    def pack_b(bb):
        return block_diag(jnp.swapaxes(bb.reshape(ngb, S5_GB, SSM_STATE, SSM_CH), 2, 3))

    def pack_c(cc):
        return block_diag(jnp.swapaxes(cc.reshape(ngb, S5_GB, SSM_CH, SSM_STATE), 2, 3))

    a = jnp.stack([a_re.reshape(-1), a_im.reshape(-1)])
    bcat = jnp.concatenate([pack_b(bb_re), pack_b(bb_im)], axis=2)
    ccat = jnp.concatenate([pack_c(c_re.astype(f32)), -pack_c(c_im.astype(f32))], axis=1)
    return a, bcat, ccat


def s5_scan(u4, h0_re, h0_im, a, bcat, ccat, d_skip, chained):
    nb, n_l, n_r, _ = u4.shape
    rh = h0_re.shape[1]
    ngb = SSM_GROUPS // S5_GB
    st_spec = pl.BlockSpec((1, rh, S5_ST), lambda i, j: (i, 0, j))
    return pl.pallas_call(
        functools.partial(_s5_body, chained=chained),
        grid=(nb, ngb),
        in_specs=[pl.BlockSpec((1, n_l, n_r, S5_CH), lambda i, j: (i, 0, 0, j)), st_spec, st_spec,
                  pl.BlockSpec((2, S5_ST), lambda i, j: (0, j)),
                  pl.BlockSpec((1, S5_CH, 2 * S5_ST), lambda i, j: (j, 0, 0)),
                  pl.BlockSpec((1, 2 * S5_ST, S5_CH), lambda i, j: (j, 0, 0)),
                  pl.BlockSpec((1, S5_CH), lambda i, j: (0, j))],
        out_specs=[pl.BlockSpec((1, n_l, n_r, S5_CH), lambda i, j: (i, 0, 0, j)), st_spec, st_spec],
        out_shape=[jax.ShapeDtypeStruct(u4.shape, jnp.float32),
                   jax.ShapeDtypeStruct(h0_re.shape, jnp.float32), jax.ShapeDtypeStruct(h0_re.shape, jnp.float32)],
        scratch_shapes=[pltpu.VMEM((n_l, n_r, 2 * S5_ST), jnp.float32), pltpu.VMEM((n_r + 8, 2 * S5_ST), jnp.float32)],
        compiler_params=pltpu.CompilerParams(vmem_limit_bytes=VMEM_LIMIT),
        name="s5_scan",
    )(u4, h0_re, h0_im, a, bcat, ccat, d_skip.reshape(1, SSM_WIDTH))


S5_CHUNK = 64


def s5_mixer_pallas(u, h0_re, h0_im, lam_re, lam_im, log_dt, b_re, b_im, c_re, c_im, d_skip, w_glu, b_glu, chained):
    bsz, t, _ = u.shape
    f32 = jnp.float32
    a, bcat, ccat = s5_discretize(lam_re, lam_im, log_dt, b_re, b_im, c_re, c_im)
    flat = lambda h: h.astype(f32).reshape(bsz, SSM_GROUPS * SSM_STATE)
    if chained:
        n_r = t // S5_CHUNK
        u4 = jnp.transpose(u.astype(f32).reshape(bsz, n_r, S5_CHUNK, SSM_WIDTH), (0, 2, 1, 3))
        y4, h_re, h_im = s5_scan(u4, flat(h0_re)[:, None], flat(h0_im)[:, None], a, bcat, ccat, d_skip.astype(f32), True)
        y = jnp.transpose(y4, (0, 2, 1, 3)).reshape(bsz, t, SSM_WIDTH)
        h_re, h_im = h_re[:, 0], h_im[:, 0]
    else:
        u4 = jnp.transpose(u.astype(f32), (1, 0, 2))[None]
        y4, h_re, h_im = s5_scan(u4, flat(h0_re)[None], flat(h0_im)[None], a, bcat, ccat, d_skip.astype(f32), False)
        y = jnp.transpose(y4[0], (1, 0, 2))
        h_re, h_im = h_re[0], h_im[0]
    out = y * jax.nn.sigmoid(y @ w_glu.astype(f32) + b_glu.astype(f32))
    st = lambda h: h.reshape(bsz, SSM_GROUPS, SSM_STATE).astype(u.dtype)
    return out.astype(u.dtype), st(h_re), st(h_im)


ROW_LANES = N_PAGED * KV_WIDTH
CHUNKS_PER_PAGE = 128 // CMP_STRIDE
CMP_HID2 = NSA_KV_HEADS * CMP_HIDDEN


def _compress_body(pt_ref, *refs, n_g, transposed):
    f32, bf16 = jnp.float32, jnp.bfloat16
    page_refs = (refs[:n_g], refs[n_g:2 * n_g])
    w1_ref, c1_ref, w2_ref, b2_ref, o_ref, carry_ref, rows_ref = refs[2 * n_g:]
    m_rows = n_g * CHUNKS_PER_PAGE
    first = pl.program_id(1) == 0
    row_id = lax.broadcasted_iota(jnp.int32, (m_rows, CMP_HID2), 0)
    outs = []
    for t in range(2):
        for k, r in enumerate(page_refs[t]):
            rows_ref[k * 128:(k + 1) * 128, :] = r[0].T if transposed else r[0]
        cols = [rows_ref[pl.ds(j, m_rows, stride=CMP_STRIDE), :] for j in range(CMP_STRIDE)]
        x = jnp.concatenate(cols, axis=1).astype(bf16)
        part = jnp.dot(x, w1_ref[t], preferred_element_type=f32)
        p0, p1 = part[:, :CMP_HID2], part[:, CMP_HID2:]
        prev = jnp.where(first, 0.0, carry_ref[t, 0:1, :])
        shifted = jnp.where(row_id == 0, prev, pltpu.roll(p0, 1, axis=0))
        carry_ref[t, 0:1, :] = p0[m_rows - 1:m_rows, :]
        h1 = (c1_ref[t] + shifted) + p1
        outs.append(jnp.dot(jax.nn.gelu(h1).astype(bf16), w2_ref[t], preferred_element_type=f32) + b2_ref[t])
    o_ref[0] = jnp.concatenate(outs, axis=1)


def compress_pages(pool, page_table, cmp_pe, cmp_w1, cmp_b1, cmp_w2, cmp_b2, n_g, transposed):
    n_seq, n_pages = page_table.shape
    assert n_pages % n_g == 0 and pool.shape[1:] == ((ROW_LANES, 128) if transposed else (128, ROW_LANES))
    f32, bf16 = jnp.float32, jnp.bfloat16
    r = CMP_BLOCK // CMP_STRIDE
    eye = jnp.eye(NSA_KV_HEADS, dtype=f32)
    w1 = cmp_w1.astype(f32).reshape(2, r, CMP_STRIDE, HEAD_DIM, CMP_HIDDEN)
    w1 = jnp.transpose(w1, (0, 2, 3, 1, 4))[:, :, None, :, :, None, :] * eye[None, None, :, None, None, :, None]
    w1 = w1.reshape(2, CMP_STRIDE * KV_WIDTH, r * CMP_HID2).astype(bf16)
    c1 = jnp.stack([jnp.einsum('ld,ldf->f', cmp_pe[t], cmp_w1[t]) + cmp_b1[t] for t in range(2)])
    c1 = jnp.tile(c1[:, None, :], (1, 1, NSA_KV_HEADS))
    w2 = (cmp_w2.astype(f32)[:, None, :, None, :] * eye[None, :, None, :, None]).reshape(2, CMP_HID2, KV_WIDTH).astype(bf16)
    b2 = jnp.tile(cmp_b2.astype(f32)[:, None, :], (1, 1, NSA_KV_HEADS))
    m_rows = n_g * CHUNKS_PER_PAGE
    page_spec = lambda k, t: pl.BlockSpec((1, 128, KV_WIDTH), lambda b, g, pt: (
        (pt[b * n_pages + g * n_g + k], t, 0) if transposed else (pt[b * n_pages + g * n_g + k], 0, t)))
    full = lambda a: pl.BlockSpec(a.shape, lambda b, g, pt: (0,) * a.ndim)
    return pl.pallas_call(
        functools.partial(_compress_body, n_g=n_g, transposed=transposed),
        grid_spec=pltpu.PrefetchScalarGridSpec(
            num_scalar_prefetch=1,
            grid=(n_seq, n_pages // n_g),
            in_specs=[page_spec(k, t) for t in range(2) for k in range(n_g)] + [full(w1), full(c1), full(w2), full(b2)],
            out_specs=pl.BlockSpec((1, m_rows, 2 * KV_WIDTH), lambda b, g, pt: (b, g, 0)),
            scratch_shapes=[pltpu.VMEM((2, 8, CMP_HID2), f32), pltpu.VMEM((n_g * 128, KV_WIDTH), f32)],
        ),
        out_shape=jax.ShapeDtypeStruct((n_seq, n_pages * CHUNKS_PER_PAGE, 2 * KV_WIDTH), f32),
        compiler_params=pltpu.CompilerParams(vmem_limit_bytes=VMEM_LIMIT),
        name="compress_pages",
    )(page_table.reshape(-1).astype(jnp.int32), *([pool] * (2 * n_g)), w1, c1, w2, b2)


SD_PAGES = 64
SD_KEYS = SD_PAGES * 128
SD_SUB = 16
SD_ROWS = NSA_HEADS * 8
WIN_PAD = 640


def _masked_softmax(s, ok):
    s = jnp.where(ok, s, NEG_INF)
    e = jnp.where(ok, jnp.exp(s - jnp.max(s, axis=-1, keepdims=True)), 0.0)
    return e / jnp.maximum(jnp.sum(e, axis=-1, keepdims=True), 1e-30)


def _rows_from_group(a):
    t = a.shape[0] // NSA_KV_HEADS
    a4 = jnp.broadcast_to(a.reshape(NSA_KV_HEADS, 1, t, a.shape[1]), (NSA_KV_HEADS, GRP, t, a.shape[1]))
    return a4.reshape(NSA_KV_HEADS * GRP * t, a.shape[1])


def _nsa_decode_body(pt_ref, *refs, pos0, n_cmp, n_sel, n_new, win_len):
    f32, bf16 = jnp.float32, jnp.bfloat16
    kt_refs, vt_refs = refs[:SD_PAGES], refs[SD_PAGES:2 * SD_PAGES]
    (q_ref, g_ref, kcvc_ref, knew_ref, wh_ref, bc_ref, bs_ref, bn_ref, bw_ref, cov_ref, exp_ref,
     o_ref, sel_ref, oc_ref, m_ref, l_ref, acc_ref, mask_ref) = refs[2 * SD_PAGES:]
    tile, n_tiles = pl.program_id(1), pl.num_programs(1)
    nt_dims = (((1,), (1,)), ((), ()))
    nn_dims = (((1,), (0,)), ((), ()))
    n_tok = SD_ROWS // NSA_HEADS
    q = (q_ref[0] * (HEAD_DIM ** -0.5)).astype(bf16)
    qpos = pos0 + lax.broadcasted_iota(jnp.int32, (SD_ROWS, 1), 0) % n_tok

    @pl.when(tile == 0)
    def _():
        n_c = kcvc_ref.shape[1]
        m_idx = lax.broadcasted_iota(jnp.int32, (SD_ROWS, n_c), 1)
        ok_c = (m_idx >= 1) & (m_idx <= n_cmp) & (qpos >= CMP_STRIDE * m_idx + (CMP_BLOCK - 1 - CMP_STRIDE))
        s_c = lax.dot_general(q, kcvc_ref[0, :, :KV_WIDTH].astype(bf16), nt_dims, preferred_element_type=f32) + bc_ref[...]
        p_c = _masked_softmax(s_c, ok_c)
        oc_ref[...] = jnp.dot(p_c.astype(bf16), kcvc_ref[0, :, KV_WIDTH:].astype(bf16), preferred_element_type=f32)
        p_sum = jnp.sum(p_c.reshape(NSA_KV_HEADS, GRP, n_tok, n_c), axis=1).reshape(NSA_KV_HEADS * n_tok, n_c)
        imp = jnp.dot(p_sum.astype(bf16), cov_ref[...], preferred_element_type=f32)
        n_l = imp.shape[1]
        lane = lax.broadcasted_iota(jnp.int32, (NSA_KV_HEADS * n_tok, n_l), 1)
        qp = pos0 + lax.broadcasted_iota(jnp.int32, (NSA_KV_HEADS * n_tok, n_l), 0) % n_tok
        cur = qp // SEL_BLOCK
        forced = (lane == 0) | (lane == cur) | (lane == cur - 1)
        score = jnp.where(forced, FORCE_SCORE, jnp.where(lane * SEL_BLOCK <= qp, imp, -1.0))
        score = jnp.where(lane < n_sel, score, -jnp.inf)
        sel = jnp.zeros(score.shape, f32)
        for _ in range(min(SEL_TOP_N, n_sel)):
            best = jnp.max(score, axis=-1, keepdims=True)
            idx = jnp.min(jnp.where(score == best, lane, n_l), axis=-1, keepdims=True)
            hit = lane == idx
            sel = jnp.where(hit, 1.0, sel)
            score = jnp.where(hit, -jnp.inf, score)
        sel_ref[...] = sel
        m_ref[...] = jnp.full(m_ref.shape, NEG_INF, f32)
        l_ref[...] = jnp.zeros(l_ref.shape, f32)
        acc_ref[...] = jnp.zeros(acc_ref.shape, f32)

    def online(s, ok, v, v_dims):
        m_old = m_ref[...]
        s = jnp.where(ok, s, NEG_INF)
        m_new = jnp.maximum(m_old, jnp.max(s, axis=-1, keepdims=True))
        p = jnp.where(ok, jnp.exp(s - m_new), 0.0)
        alpha = jnp.exp(m_old - m_new)
        l_ref[...] = alpha * l_ref[...] + jnp.sum(p, axis=-1, keepdims=True)
        acc_ref[...] = alpha * acc_ref[...] + lax.dot_general(p.astype(bf16), v, v_dims, preferred_element_type=f32)
        m_ref[...] = m_new

    sel_tile = sel_ref[:, pl.ds(pl.multiple_of(tile * LANES, LANES), LANES)]
    mask_ref[...] = jnp.dot(sel_tile.astype(bf16), exp_ref[...], preferred_element_type=f32)
    sub_keys = SD_SUB * 128
    for sub in range(SD_PAGES // SD_SUB):
        pages = slice(sub * SD_SUB, (sub + 1) * SD_SUB)
        kt = jnp.concatenate([r[0] for r in kt_refs[pages]], axis=1).astype(bf16)
        vt = jnp.concatenate([r[0] for r in vt_refs[pages]], axis=1).astype(bf16)
        s = jnp.dot(q, kt, preferred_element_type=f32) + bs_ref[:, sub * sub_keys:(sub + 1) * sub_keys]
        ok = _rows_from_group(mask_ref[:, sub * sub_keys:(sub + 1) * sub_keys]) > 0.5
        online(s, ok, vt, nt_dims)

    @pl.when(tile == n_tiles - 1)
    def _():
        lane = lax.broadcasted_iota(jnp.int32, (SD_ROWS, LANES), 1)
        new_blk = pos0 // SEL_BLOCK
        sel_new = _rows_from_group(sel_ref[:, new_blk:new_blk + 1]) > 0.5
        ok_n = sel_new & (lane < n_new) & (pos0 + lane <= qpos)
        s_n = lax.dot_general(q, knew_ref[0, :, :KV_WIDTH].astype(bf16), nt_dims, preferred_element_type=f32) + bn_ref[...]
        online(s_n, ok_n, knew_ref[0, :, KV_WIDTH:].astype(bf16), nn_dims)
        o_s = acc_ref[...] / jnp.maximum(l_ref[...], 1e-30)
        j = lax.broadcasted_iota(jnp.int32, (SD_ROWS, WIN_PAD), 1)
        kwpos = pos0 - win_len + j
        dist = qpos - kwpos
        ok_w = (dist >= 0) & (dist < WINDOW) & (kwpos >= 0) & (j < win_len + n_new)
        s_w = lax.dot_general(q, wh_ref[0, :, :KV_WIDTH].astype(bf16), nt_dims, preferred_element_type=f32) + bw_ref[...]
        o_w = jnp.dot(_masked_softmax(s_w, ok_w).astype(bf16), wh_ref[0, :, KV_WIDTH:].astype(bf16), preferred_element_type=f32)
        g = jax.nn.sigmoid(g_ref[0])
        o = g[:, 0:1] * oc_ref[...] + g[:, 1:2] * o_s + g[:, 2:3] * o_w
        row = lax.broadcasted_iota(jnp.int32, (SD_ROWS, HEAD_DIM), 0)
        o_ref[0] = jnp.where(row < SD_ROWS // NSA_KV_HEADS, o[:, :HEAD_DIM], o[:, HEAD_DIM:])


def nsa_decode(q, glog, kv_new, kcvc, pool_t, page_table, cache_win, thr, rel_bias, pos0):
    f32, bf16 = jnp.float32, jnp.bfloat16
    b, t = q.shape[:2]
    n_pages = page_table.shape[1]
    win_len = cache_win.shape[1]
    assert t * NSA_HEADS == SD_ROWS and pos0 == n_pages * 128 and n_pages % SD_PAGES == 0 and pos0 % SEL_BLOCK == 0
    assert win_len + t <= WIN_PAD and t <= SEL_BLOCK and SD_KEYS == LANES * SEL_BLOCK
    n_rows = pos0 + t
    n_cmp = (n_rows - CMP_BLOCK) // CMP_STRIDE + 1
    n_sel = -(-n_rows // SEL_BLOCK)
    n_c = kcvc.shape[1]
    sel_lanes = -(-n_sel // LANES) * LANES
    q5 = jnp.transpose(q.reshape(b, t, NSA_KV_HEADS, GRP, HEAD_DIM), (0, 2, 3, 1, 4))
    qz = (q5[:, :, :, :, None, :] * jnp.eye(NSA_KV_HEADS, dtype=f32)[None, :, None, None, :, None]).reshape(b, SD_ROWS, KV_WIDTH)
    g3 = jnp.transpose(glog.reshape(b, t, NSA_HEADS, 3), (0, 2, 1, 3)).reshape(b, SD_ROWS, 3)
    knew = jnp.pad(kv_new[:, :, 2:4].reshape(b, t, 2 * KV_WIDTH), ((0, 0), (0, LANES - t), (0, 0)))
    whist = jnp.concatenate([cache_win.reshape(b, win_len, 2 * KV_WIDTH), kv_new[:, :, 4:6].reshape(b, t, 2 * KV_WIDTH),
                             jnp.zeros((b, WIN_PAD - win_len - t, 2 * KV_WIDTH), f32)], axis=1)
    qp = pos0 + jnp.arange(t, dtype=jnp.int32)[:, None]
    tab = lambda dist: bias_tables(dist[None], thr, rel_bias).reshape(SD_ROWS, dist.shape[1])
    bias_c = tab(qp - (CMP_STRIDE * jnp.arange(n_c, dtype=jnp.int32)[None, :] + CMP_BLOCK - 1 - CMP_STRIDE))
    bias_s = tab(qp - jnp.arange(pos0, dtype=jnp.int32)[None, :])
    bias_n = tab(qp - (pos0 + jnp.arange(LANES, dtype=jnp.int32)[None, :]))
    bias_w = tab(qp - (pos0 - win_len + jnp.arange(WIN_PAD, dtype=jnp.int32)[None, :]))
    m_idx = jnp.arange(n_c)[:, None]
    c_start = CMP_STRIDE * (m_idx - 1)
    s_start = SEL_BLOCK * jnp.arange(sel_lanes)[None, :]
    cover = ((c_start < s_start + SEL_BLOCK) & (c_start + CMP_BLOCK > s_start) & (m_idx >= 1) & (m_idx <= n_cmp)
             & (jnp.arange(sel_lanes)[None, :] < n_sel)).astype(bf16)
    expand = (jnp.arange(SD_KEYS)[None, :] // SEL_BLOCK == jnp.arange(LANES)[:, None]).astype(bf16)
    page_spec = lambda k, blk: pl.BlockSpec((1, KV_WIDTH, 128), lambda bb, g, pt: (pt[bb * n_pages + g * SD_PAGES + k], blk, 0))
    per_seq = lambda a: pl.BlockSpec((1,) + a.shape[1:], lambda bb, g, pt: (bb,) + (0,) * (a.ndim - 1))
    full = lambda a: pl.BlockSpec(a.shape, lambda bb, g, pt: (0,) * a.ndim)
    kv_rows = NSA_KV_HEADS * t
    out = pl.pallas_call(
        functools.partial(_nsa_decode_body, pos0=pos0, n_cmp=n_cmp, n_sel=n_sel, n_new=t, win_len=win_len),
        grid_spec=pltpu.PrefetchScalarGridSpec(
            num_scalar_prefetch=1,
            grid=(b, n_pages // SD_PAGES),
            in_specs=[page_spec(k, blk) for blk in (2, 3) for k in range(SD_PAGES)]
            + [per_seq(qz), per_seq(g3), per_seq(kcvc), per_seq(knew), per_seq(whist), full(bias_c),
               pl.BlockSpec((SD_ROWS, SD_KEYS), lambda bb, g, pt: (0, g)), full(bias_n), full(bias_w), full(cover), full(expand)],
            out_specs=pl.BlockSpec((1, SD_ROWS, HEAD_DIM), lambda bb, g, pt: (bb, 0, 0)),
            scratch_shapes=[pltpu.VMEM((kv_rows, sel_lanes), f32), pltpu.VMEM((SD_ROWS, KV_WIDTH), f32),
                            pltpu.VMEM((SD_ROWS, 1), f32), pltpu.VMEM((SD_ROWS, 1), f32), pltpu.VMEM((SD_ROWS, KV_WIDTH), f32),
                            pltpu.VMEM((kv_rows, SD_KEYS), f32)],
        ),
        out_shape=jax.ShapeDtypeStruct((b, SD_ROWS, HEAD_DIM), f32),
        compiler_params=pltpu.CompilerParams(vmem_limit_bytes=VMEM_LIMIT),
        name="nsa_decode",
    )(page_table.reshape(-1).astype(jnp.int32), *([pool_t] * (2 * SD_PAGES)), qz, g3, kcvc, knew, whist,
      bias_c, bias_s, bias_n, bias_w, cover, expand)
    out = jnp.transpose(out.reshape(b, NSA_KV_HEADS, GRP, t, HEAD_DIM), (0, 3, 1, 2, 4))
    return out.reshape(b, t, NSA_WIDTH)


def rms_norm(x, g):
    xf = x.astype(jnp.float32)
    y = xf * lax.rsqrt(jnp.mean(xf * xf, axis=-1, keepdims=True) + RMS_EPS)
    return (y * g.astype(jnp.float32)).astype(x.dtype)


def masked_softmax(s, mask):
    s = jnp.where(mask, s.astype(jnp.float32), NEG_INF)
    m = jnp.max(s, axis=-1, keepdims=True)
    e = jnp.where(mask, jnp.exp(s - m), 0.0)
    return e / jnp.maximum(jnp.sum(e, axis=-1, keepdims=True), 1e-30)


def compress_rows(x, pe, w1, b1, w2, b2, n_rows):
    b = x.shape[0]
    nc = (n_rows - CMP_BLOCK) // CMP_STRIDE + 1
    r = CMP_BLOCK // CMP_STRIDE
    nch = nc + r - 1
    chunks = x[:, :nch * CMP_STRIDE].reshape(b, nch, CMP_STRIDE, NSA_KV_HEADS, HEAD_DIM)
    part = jnp.einsum('bcjhd,rjdf->bcrhf', chunks, w1.reshape(r, CMP_STRIDE, HEAD_DIM, CMP_HIDDEN))
    h1 = jnp.einsum('ld,ldf->f', pe, w1) + b1
    for o in range(r):
        h1 = h1 + part[:, o:o + nc, o]
    return jnp.einsum('bchf,fd->bchd', jax.nn.gelu(h1), w2) + b2


def nsa_query_block(q, qpos, gates, kw, vw, kwpos, kc, vc, c_end, ks_b, vs_b, cover, rel_bias):
    b, nq = q.shape[:2]
    grp = NSA_HEADS // NSA_KV_HEADS
    qg = (q * HEAD_DIM ** -0.5).reshape(b, nq, NSA_KV_HEADS, grp, HEAD_DIM)
    rb = rel_bias.reshape(REL_BUCKETS, NSA_KV_HEADS, grp)
    hi = jnp.arange(NSA_KV_HEADS)[None, None, :, None]
    bi = jnp.arange(b)[:, None, None, None]
    dist_c = qpos[:, None] - c_end[None, :]
    bias_c = jnp.moveaxis(rb[rel_bucket(dist_c)], 1, -1)
    s_c = jnp.einsum('bqhgd,bchd->bqhgc', qg, kc) + bias_c
    p_c = masked_softmax(s_c, (dist_c >= 0)[:, None, None, :])
    o_c = jnp.einsum('bqhgc,bchd->bqhgd', p_c.astype(vc.dtype), vc)
    imp = jnp.einsum('bqhgc,cs->bqhs', p_c, cover)
    ns = ks_b.shape[1]
    blk = jnp.arange(ns)[None, :]
    cur = (qpos // SEL_BLOCK)[:, None]
    forced = ((blk == 0) | (blk == cur) | (blk == cur - 1))[:, None, :]
    valid = (blk * SEL_BLOCK <= qpos[:, None])[:, None, :]
    score = jnp.where(forced, FORCE_SCORE, jnp.where(valid, imp, -1.0))
    _, idx = lax.top_k(score, min(SEL_TOP_N, ns))
    k_s = ks_b[bi, idx, :, hi].reshape(b, nq, NSA_KV_HEADS, -1, HEAD_DIM)
    v_s = vs_b[bi, idx, :, hi].reshape(b, nq, NSA_KV_HEADS, -1, HEAD_DIM)
    s_pos = (idx[..., None] * SEL_BLOCK + jnp.arange(SEL_BLOCK)).reshape(b, nq, NSA_KV_HEADS, -1)
    dist_s = qpos[None, :, None, None] - s_pos
    bias_s = jnp.moveaxis(rb[rel_bucket(dist_s), hi], -1, 3)
    s_s = jnp.einsum('bqhgd,bqhkd->bqhgk', qg, k_s) + bias_s
    p_s = masked_softmax(s_s, (dist_s >= 0)[:, :, :, None, :])
    o_s = jnp.einsum('bqhgk,bqhkd->bqhgd', p_s.astype(v_s.dtype), v_s)
    dist_w = qpos[:, None] - kwpos[None, :]
    mask_w = (dist_w >= 0) & (dist_w < WINDOW) & (kwpos >= 0)[None, :]
    bias_w = jnp.moveaxis(rb[rel_bucket(dist_w)], 1, -1)
    s_w = jnp.einsum('bqhgd,bkhd->bqhgk', qg, kw) + bias_w
    p_w = masked_softmax(s_w, mask_w[:, None, None, :])
    o_w = jnp.einsum('bqhgk,bkhd->bqhgd', p_w.astype(vw.dtype), vw)
    g = gates.reshape(b, nq, NSA_KV_HEADS, grp, 3)
    o = g[..., 0:1] * o_c + g[..., 1:2] * o_s + g[..., 2:3] * o_w
    return o.reshape(b, nq, NSA_WIDTH)


def s5_mixer(u, h0_re, h0_im, lam_re, lam_im, log_dt, b_re, b_im, c_re, c_im, d_skip, w_glu, b_glu):
    bsz, t, _ = u.shape
    f32 = jnp.float32
    uf = u.astype(f32).reshape(bsz, t, SSM_GROUPS, SSM_CH)
    dt = jnp.exp(log_dt.astype(f32))[:, None]
    lr, li = lam_re.astype(f32), lam_im.astype(f32)
    mag = jnp.exp(lr * dt)
    a_re, a_im = mag * jnp.cos(li * dt), mag * jnp.sin(li * dt)
    den = lr * lr + li * li
    z_re = ((a_re - 1.0) * lr + a_im * li) / den
    z_im = (a_im * lr - (a_re - 1.0) * li) / den
    br, bim = b_re.astype(f32), b_im.astype(f32)
    bb_re = z_re[..., None] * br - z_im[..., None] * bim
    bb_im = z_re[..., None] * bim + z_im[..., None] * br
    x_re = jnp.einsum('btgc,gpc->btgp', uf, bb_re)
    x_im = jnp.einsum('btgc,gpc->btgp', uf, bb_im)
    ar = jnp.broadcast_to(a_re, x_re.shape)
    ai = jnp.broadcast_to(a_im, x_re.shape)

    def combine(e1, e2):
        a1r, a1i, b1r, b1i = e1
        a2r, a2i, b2r, b2i = e2
        return (a2r * a1r - a2i * a1i, a2r * a1i + a2i * a1r,
                a2r * b1r - a2i * b1i + b2r, a2r * b1i + a2i * b1r + b2i)

    pr, pim, sr, si = lax.associative_scan(combine, (ar, ai, x_re, x_im), axis=1)
    h0r = h0_re.astype(f32)[:, None]
    h0i = h0_im.astype(f32)[:, None]
    h_re = pr * h0r - pim * h0i + sr
    h_im = pr * h0i + pim * h0r + si
    y = jnp.einsum('btgp,gcp->btgc', h_re, c_re.astype(f32)) - jnp.einsum('btgp,gcp->btgc', h_im, c_im.astype(f32))
    y = jax.nn.gelu(y.reshape(bsz, t, SSM_WIDTH) + d_skip.astype(f32) * u.astype(f32))
    out = y * jax.nn.sigmoid(y @ w_glu.astype(f32) + b_glu.astype(f32))
    return out.astype(u.dtype), h_re[:, -1].astype(u.dtype), h_im[:, -1].astype(u.dtype)


def moe_ffn(x, router_w, router_b, w_gu, b_gu, w_down, b_down):
    n = x.shape[0]
    router_pad = jnp.pad(router_w, ((0, 0), (0, LANES - N_EXPERTS)))
    logits = pallas_matmul(x, router_pad)[:, :N_EXPERTS] + router_b.astype(jnp.float32)
    top_val, top_idx = lax.top_k(logits, TOP_K)
    gate = jax.nn.softmax(top_val, axis=-1)
    nk = n * TOP_K
    n_items = N_EXPERTS + nk // MOE_ROWS
    flat_e = top_idx.reshape(nk)
    onehot = (flat_e[:, None] == jnp.arange(N_EXPERTS, dtype=flat_e.dtype)[None, :]).astype(jnp.int32)
    running = jnp.cumsum(onehot, axis=0)
    counts = running[-1]
    pos_in_e = jnp.sum(onehot * running, axis=1) - 1
    items_e = (counts + MOE_ROWS - 1) // MOE_ROWS
    item_end = jnp.cumsum(items_e)
    item_start = item_end - items_e
    total = item_end[-1]
    dest = (item_start[flat_e] * MOE_ROWS + pos_in_e).astype(jnp.int32)
    row_tok = jnp.zeros(n_items * MOE_ROWS, jnp.int32).at[dest].set(jnp.arange(nk, dtype=jnp.int32) // TOP_K)
    item = jnp.minimum(jnp.arange(n_items, dtype=jnp.int32), total - 1)
    item_e = jnp.minimum(jnp.searchsorted(item_end, item, side='right'), N_EXPERTS - 1).astype(jnp.int32)
    rows = jnp.clip(counts[item_e] - (item - item_start[item_e]) * MOE_ROWS, 0, MOE_ROWS)
    item_sub = jnp.where(jnp.arange(n_items) < total, (rows + MOE_SUB - 1) // MOE_SUB, 0).astype(jnp.int32)
    half = x.shape[1] // 2
    bits = lax.bitcast_convert_type(x.astype(jnp.bfloat16), jnp.uint16).astype(jnp.uint32)
    xb = ((bits[:, :half] << 16) | bits[:, half:])[row_tok]
    item_ok = (jnp.arange(n_items) < total).astype(jnp.int32)
    yb = moe_experts(item_e, item_sub, item.astype(jnp.int32), item_ok, xb, w_gu, b_gu, w_down, b_down)
    dest_k = dest.reshape(n, TOP_K)
    out = gate[:, 0:1] * yb[dest_k[:, 0]]
    for k in range(1, TOP_K):
        out = out + gate[:, k:k + 1] * yb[dest_k[:, k]]
    return out


MOE_ROWS = 1280
MOE_SUB = 256
MOE_TF = 256


def _moe_body(e_ref, sub_ref, blk_ref, ok_ref, x_ref, wg_ref, wl_ref, bg_ref, bl_ref, wd_ref, bd_ref, o_ref, xs_ref):
    bf16 = jnp.bfloat16
    i, f = pl.program_id(0), pl.program_id(1)
    n_sub = sub_ref[i]
    wg = wg_ref[0].astype(bf16)
    wl = wl_ref[0].astype(bf16)
    wd = wd_ref[0].astype(bf16)

    @pl.when(f == 0)
    def _():
        def unpack_block(j, _):
            rows = pl.ds(pl.multiple_of(j * MOE_SUB, MOE_SUB), MOE_SUB)
            w = x_ref[rows, :]
            hi = lax.bitcast_convert_type(w & jnp.uint32(0xFFFF0000), jnp.float32)
            lo = lax.bitcast_convert_type(w << 16, jnp.float32)
            xs_ref[rows, :] = jnp.concatenate([hi, lo], axis=1).astype(bf16)
            return 0

        lax.fori_loop(0, n_sub, unpack_block, 0)

    def sub_block(j, _):
        rows = pl.ds(pl.multiple_of(j * MOE_SUB, MOE_SUB), MOE_SUB)
        x = xs_ref[rows, :]
        hg = jnp.dot(x, wg, preferred_element_type=jnp.float32) + bg_ref[0]
        hl = jnp.dot(x, wl, preferred_element_type=jnp.float32) + bl_ref[0]
        hg = jnp.minimum(hg, SWIGLU_LIMIT)
        hl = jnp.clip(hl, -SWIGLU_LIMIT, SWIGLU_LIMIT)
        act = hg * jax.nn.sigmoid(SWIGLU_ALPHA * hg) * (hl + 1.0)
        y = jnp.dot(act.astype(bf16), wd, preferred_element_type=jnp.float32)

        @pl.when(f == 0)
        def _():
            o_ref[rows, :] = y + bd_ref[0]

        @pl.when(f > 0)
        def _():
            o_ref[rows, :] += y

        return 0

    lax.fori_loop(0, n_sub, sub_block, 0)

    @pl.when(f == 0)
    def _():
        def zero_block(j, _):
            o_ref[pl.ds(pl.multiple_of(j * MOE_SUB, MOE_SUB), MOE_SUB), :] = jnp.zeros((MOE_SUB, o_ref.shape[1]), jnp.float32)
            return 0

        lax.fori_loop(n_sub, MOE_ROWS // MOE_SUB, zero_block, 0)


def moe_experts(item_e, item_sub, item_blk, item_ok, xb, w_gu, b_gu, w_down, b_down):
    n_items = item_e.shape[0]
    n_f = D_FF // MOE_TF
    d = 2 * xb.shape[1]

    def col(f, ok, i):
        return f * ok[i] + (n_f - 1) * (1 - ok[i])

    return pl.pallas_call(
        _moe_body,
        grid_spec=pltpu.PrefetchScalarGridSpec(
            num_scalar_prefetch=4,
            grid=(n_items, n_f),
            in_specs=[pl.BlockSpec((MOE_ROWS, d // 2), lambda i, f, e, s, b, ok: (b[i], 0)),
                      pl.BlockSpec((1, d, MOE_TF), lambda i, f, e, s, b, ok: (e[i], 0, col(f, ok, i))),
                      pl.BlockSpec((1, d, MOE_TF), lambda i, f, e, s, b, ok: (e[i], 0, n_f + col(f, ok, i))),
                      pl.BlockSpec((1, 1, MOE_TF), lambda i, f, e, s, b, ok: (e[i], 0, col(f, ok, i))),
                      pl.BlockSpec((1, 1, MOE_TF), lambda i, f, e, s, b, ok: (e[i], 0, n_f + col(f, ok, i))),
                      pl.BlockSpec((1, MOE_TF, d), lambda i, f, e, s, b, ok: (e[i], col(f, ok, i), 0)),
                      pl.BlockSpec((1, 1, d), lambda i, f, e, s, b, ok: (e[i], 0, 0))],
            out_specs=pl.BlockSpec((MOE_ROWS, d), lambda i, f, e, s, b, ok: (i, 0)),
            scratch_shapes=[pltpu.VMEM((MOE_ROWS, d), jnp.bfloat16)],
        ),
        out_shape=jax.ShapeDtypeStruct((n_items * MOE_ROWS, d), jnp.float32),
        compiler_params=pltpu.CompilerParams(vmem_limit_bytes=VMEM_LIMIT),
        name="moe_experts",
    )(item_e, item_sub, item_blk, item_ok, xb, w_gu, w_gu, b_gu[:, None, :], b_gu[:, None, :], w_down, b_down[:, None, :])


def layer_forward(x, pos0, past_kv, win_buf, h0_re, h0_im, rel_bias, lw):
    b, t, _ = x.shape
    h = rms_norm(x, lw['norm_mix'])
    proj = h @ lw['w_in']
    o1 = NSA_WIDTH
    o2 = o1 + N_KV * KV_WIDTH
    o3 = o2 + 3 * NSA_HEADS
    q = proj[..., :o1].reshape(b, t, NSA_HEADS, HEAD_DIM)
    kv_new = proj[..., o1:o2].reshape(b, t, N_KV, NSA_KV_HEADS, HEAD_DIM)
    gates = jax.nn.sigmoid(proj[..., o2:o3].astype(jnp.float32)).reshape(b, t, NSA_HEADS, 3)
    u = proj[..., o3:]
    paged_new = kv_new[:, :, :N_PAGED]
    win_new = kv_new[:, :, N_PAGED:]
    qpos = pos0 + jnp.arange(t, dtype=jnp.int32)

    n_rows = pos0 + t
    ns = -(-n_rows // SEL_BLOCK)
    pad = jnp.zeros((b, ns * SEL_BLOCK - n_rows) + paged_new.shape[2:], paged_new.dtype)
    rows = jnp.concatenate(([] if past_kv is None else [past_kv]) + [paged_new, pad], axis=1)
    kc = compress_rows(rows[:, :, 0], lw['cmp_pe'][0], lw['cmp_w1'][0], lw['cmp_b1'][0], lw['cmp_w2'][0], lw['cmp_b2'][0], n_rows)
    vc = compress_rows(rows[:, :, 1], lw['cmp_pe'][1], lw['cmp_w1'][1], lw['cmp_b1'][1], lw['cmp_w2'][1], lw['cmp_b2'][1], n_rows)
    nc = kc.shape[1]
    c_start = jnp.arange(nc) * CMP_STRIDE
    c_end = c_start + CMP_BLOCK - 1
    s_start = jnp.arange(ns) * SEL_BLOCK
    cover = ((c_start[:, None] < s_start[None, :] + SEL_BLOCK) & (c_start[:, None] + CMP_BLOCK > s_start[None, :])).astype(jnp.float32)
    ks_b = rows[:, :, 2].reshape(b, ns, SEL_BLOCK, NSA_KV_HEADS, HEAD_DIM)
    vs_b = rows[:, :, 3].reshape(b, ns, SEL_BLOCK, NSA_KV_HEADS, HEAD_DIM)

    if win_buf is None:
        new_win = win_new[:, t - min(WINDOW, n_rows):]
        tr = lambda a: jnp.transpose(a, (0, 2, 1, 3))
        pad_c = lambda a: tr(jnp.pad(a, ((0, 0), (0, LANES - nc), (0, 0), (0, 0))))
        g4 = tr(proj[..., o2:o3].reshape(b, t, NSA_HEADS, 3))
        attn4 = nsa_prompt(tr(q), g4, pad_c(kc), pad_c(vc), tr(kv_new[:, :, 2]), tr(kv_new[:, :, 3]),
                           tr(kv_new[:, :, 4]), tr(kv_new[:, :, 5]), bucket_thresholds(), rel_bias, nc)
        attn = tr(attn4).reshape(b, t, NSA_WIDTH).astype(x.dtype)
        return _layer_tail(x, attn, u, h0_re, h0_im, lw, paged_new, new_win, True)
    if win_buf is None:
        qb = min(Q_BLOCK, t)
        nb = t // qb
        wpad = jnp.concatenate([jnp.zeros((b, WINDOW) + win_new.shape[2:], win_new.dtype), win_new], axis=1)
        widx = jnp.arange(nb)[:, None] * qb + jnp.arange(qb + WINDOW)[None, :]
        w_blocks = jnp.moveaxis(wpad[:, widx], 1, 0)
        w_pos = widx - WINDOW
        w_hist = win_new
    else:
        qb, nb = t, 1
        w_hist = jnp.concatenate([win_buf, win_new], axis=1)
        w_blocks = w_hist[None]
        w_pos = (pos0 - win_buf.shape[1] + jnp.arange(w_hist.shape[1], dtype=jnp.int32))[None]
    new_win = w_hist[:, w_hist.shape[1] - min(WINDOW, n_rows):]

    def run(args):
        q_b, qpos_b, g_b, w_b, wpos_b = args
        return nsa_query_block(q_b, qpos_b, g_b, w_b[:, :, 0], w_b[:, :, 1], wpos_b,
                               kc, vc, c_end, ks_b, vs_b, cover, rel_bias)

    q_blocks = jnp.moveaxis(q.reshape(b, nb, qb, NSA_HEADS, HEAD_DIM), 1, 0)
    g_blocks = jnp.moveaxis(gates.reshape(b, nb, qb, NSA_HEADS, 3), 1, 0)
    attn = lax.map(run, (q_blocks, qpos.reshape(nb, qb), g_blocks, w_blocks, w_pos))
    attn = jnp.moveaxis(attn, 0, 1).reshape(b, t, NSA_WIDTH).astype(x.dtype)
    return _layer_tail(x, attn, u, h0_re, h0_im, lw, paged_new, new_win, False)


def mixer_layer(x, pos0, pool, page_table, win_buf, h0_re, h0_im, rel_bias, lw):
    b, t, _ = x.shape
    h = rms_norm(x, lw['norm_mix'])
    proj = h @ lw['w_in']
    o1 = NSA_WIDTH
    o2 = o1 + N_KV * KV_WIDTH
    o3 = o2 + 3 * NSA_HEADS
    q = proj[..., :o1].reshape(b, t, NSA_HEADS, HEAD_DIM)
    kv_new = proj[..., o1:o2].reshape(b, t, N_KV, NSA_KV_HEADS, HEAD_DIM)
    u = proj[..., o3:]
    paged_new = kv_new[:, :, :N_PAGED]
    win_new = kv_new[:, :, N_PAGED:]
    n_rows = pos0 + t
    n_cmp = (n_rows - CMP_BLOCK) // CMP_STRIDE + 1
    thr = bucket_thresholds()
    cmp_w = (lw['cmp_pe'], lw['cmp_w1'], lw['cmp_b1'], lw['cmp_w2'], lw['cmp_b2'])
    if pool is None:
        assert pos0 == 0 and t % 128 == 0
        pages = t // 128
        own_pool = paged_new.reshape(b * pages, 128, ROW_LANES)
        own_table = jnp.arange(b * pages, dtype=jnp.int32).reshape(b, pages)
        kcvc = compress_pages(own_pool, own_table, *cmp_w, n_g=pages, transposed=False)
        tr = lambda a: jnp.transpose(a, (0, 2, 1, 3))
        cmp_tok = lambda a: tr(jnp.pad(a[:, 1:], ((0, 0), (0, 1), (0, 0))).reshape(b, LANES, NSA_KV_HEADS, HEAD_DIM))
        g4 = tr(proj[..., o2:o3].reshape(b, t, NSA_HEADS, 3))
        attn4 = nsa_prompt(tr(q), g4, cmp_tok(kcvc[:, :, :KV_WIDTH]), cmp_tok(kcvc[:, :, KV_WIDTH:]),
                           tr(kv_new[:, :, 2]), tr(kv_new[:, :, 3]), tr(kv_new[:, :, 4]), tr(kv_new[:, :, 5]),
                           thr, rel_bias, n_cmp)
        attn = tr(attn4).reshape(b, t, NSA_WIDTH).astype(x.dtype)
        w_hist = win_new
    else:
        kcvc = compress_pages(pool, page_table, *cmp_w, n_g=32, transposed=True)
        attn = nsa_decode(q, proj[..., o2:o3], kv_new, kcvc, pool, page_table, win_buf, thr, rel_bias, pos0).astype(x.dtype)
        w_hist = jnp.concatenate([win_buf, win_new], axis=1)
    new_win = w_hist[:, w_hist.shape[1] - min(WINDOW, n_rows):]
    return _layer_tail(x, attn, u, h0_re, h0_im, lw, paged_new, new_win, pool is None)


def _layer_tail(x, attn, u, h0_re, h0_im, lw, paged_new, new_win, chained):
    b, t, _ = x.shape
    ssm, h_re, h_im = s5_mixer_pallas(u, h0_re, h0_im, lw['lam_re'], lw['lam_im'], lw['log_dt'], lw['b_re'], lw['b_im'],
                                      lw['c_re'], lw['c_im'], lw['d_skip'], lw['w_glu'], lw['b_glu'], chained)
    merged = jnp.concatenate([rms_norm(attn, lw['norm_attn_out']), rms_norm(ssm, lw['norm_ssm_out'])], axis=-1)
    x = x + pallas_matmul(merged.reshape(b * t, D_MODEL), lw['w_out']).reshape(b, t, D_MODEL)
    return x, paged_new, new_win, h_re, h_im


def moe_residual(xs, lw):
    sizes = [x.shape[0] * x.shape[1] for x in xs]
    flat = jnp.concatenate([x.reshape(-1, D_MODEL) for x in xs], axis=0)
    hm = rms_norm(flat, lw['norm_ffn'])
    flat = flat + moe_ffn(hm, lw['router_w'], lw['router_b'], lw['w_gu'], lw['b_gu'], lw['w_down'], lw['b_down']).astype(flat.dtype)
    outs, start = [], 0
    for x, n in zip(xs, sizes):
        outs.append(flat[start:start + n].reshape(x.shape))
        start += n
    return outs


def kernel(x_prompt, x_sample, cache_nsa_kv, cache_win_kv, state_ssm_re, state_ssm_im, page_table, rel_bias,
           norm_mix, w_in, cmp_pe, cmp_w1, cmp_b1, cmp_w2, cmp_b2, ssm_lam_re, ssm_lam_im, ssm_log_dt,
           ssm_b_re, ssm_b_im, ssm_c_re, ssm_c_im, ssm_d, ssm_w_glu, ssm_b_glu, norm_attn_out, norm_ssm_out,
           w_out, norm_ffn, router_w, router_b, w_gu, b_gu, w_down, b_down, norm_final):
    n_seq, n_pages = page_table.shape
    past_len = n_pages * cache_nsa_kv.shape[2]
    xp, xs = x_prompt, x_sample
    kv_p, win_p, sre_p, sim_p = [], [], [], []
    kv_s, win_s, sre_s, sim_s = [], [], [], []
    for i in range(DEPTH):
        lw = dict(norm_mix=norm_mix[i], w_in=w_in[i], cmp_pe=cmp_pe[i], cmp_w1=cmp_w1[i], cmp_b1=cmp_b1[i],
                  cmp_w2=cmp_w2[i], cmp_b2=cmp_b2[i], lam_re=ssm_lam_re[i], lam_im=ssm_lam_im[i],
                  log_dt=ssm_log_dt[i], b_re=ssm_b_re[i], b_im=ssm_b_im[i], c_re=ssm_c_re[i], c_im=ssm_c_im[i],
                  d_skip=ssm_d[i], w_glu=ssm_w_glu[i], b_glu=ssm_b_glu[i], norm_attn_out=norm_attn_out[i],
                  norm_ssm_out=norm_ssm_out[i], w_out=w_out[i], norm_ffn=norm_ffn[i], router_w=router_w[i],
                  router_b=router_b[i], w_gu=w_gu[i], b_gu=b_gu[i], w_down=w_down[i], b_down=b_down[i])
        h0 = jnp.zeros((xp.shape[0], SSM_GROUPS, SSM_STATE), xp.dtype)
        xp, kv1, w1, r1, m1 = mixer_layer(xp, 0, None, None, None, h0, h0, rel_bias, lw)
        pool = jnp.transpose(cache_nsa_kv[i].reshape(cache_nsa_kv.shape[1], cache_nsa_kv.shape[2], ROW_LANES), (0, 2, 1))
        xs, kv2, w2, r2, m2 = mixer_layer(xs, past_len, pool, page_table, cache_win_kv[i], state_ssm_re[i], state_ssm_im[i], rel_bias, lw)
        xp, xs = moe_residual([xp, xs], lw)
        kv_p.append(kv1); win_p.append(w1); sre_p.append(r1); sim_p.append(m1)
        kv_s.append(kv2); win_s.append(w2); sre_s.append(r2); sim_s.append(m2)
    y_prompt = rms_norm(xp, norm_final)
    y_sample = rms_norm(xs, norm_final)
    return (y_prompt, y_sample, jnp.stack(kv_p), jnp.stack(win_p), jnp.stack(sre_p), jnp.stack(sim_p),
            jnp.stack(kv_s), jnp.stack(win_s), jnp.stack(sre_s), jnp.stack(sim_s))
```

```python
import functools
import math
import jax, jax.numpy as jnp
from jax import lax
from jax.experimental import pallas as pl
from jax.experimental.pallas import tpu as pltpu

D_MODEL = 2048
DEPTH = 1
NSA_HEADS = 16
NSA_KV_HEADS = 2
HEAD_DIM = 64
NSA_WIDTH = NSA_HEADS * HEAD_DIM
KV_WIDTH = NSA_KV_HEADS * HEAD_DIM
N_PAGED = 4
N_KV = 6
SSM_WIDTH = D_MODEL - NSA_WIDTH
SSM_CH = 16
SSM_GROUPS = SSM_WIDTH // SSM_CH
SSM_STATE = 64
IN_WIDTH = NSA_WIDTH + N_KV * KV_WIDTH + 3 * NSA_HEADS + SSM_WIDTH
CMP_BLOCK = 32
CMP_STRIDE = 16
CMP_HIDDEN = 2 * HEAD_DIM
SEL_BLOCK = 64
SEL_TOP_N = 16
WINDOW = 512
Q_BLOCK = 64
FORCE_SCORE = 1e6
REL_BUCKETS = 32
REL_MAX_DIST = 4096
N_EXPERTS = 32
TOP_K = 4
D_FF = D_MODEL
SWIGLU_LIMIT = 7.0
SWIGLU_ALPHA = 1.702
RMS_EPS = 1e-5
NEG_INF = -1e30


def _mm_body(x_ref, w_ref, o_ref):
    o_ref[...] = jnp.dot(x_ref[...].astype(jnp.bfloat16), w_ref[...].astype(jnp.bfloat16),
                         preferred_element_type=jnp.float32)


def pallas_matmul(x, w, tm=256, tn=512):
    m, k = x.shape
    n = w.shape[1]
    tm = min(tm, m)
    tn = min(tn, n)
    return pl.pallas_call(
        _mm_body,
        grid=(m // tm, n // tn),
        in_specs=[pl.BlockSpec((tm, k), lambda i, j: (i, 0)),
                  pl.BlockSpec((k, tn), lambda i, j: (0, j))],
        out_specs=pl.BlockSpec((tm, tn), lambda i, j: (i, j)),
        out_shape=jax.ShapeDtypeStruct((m, n), jnp.float32),
        name="matmul",
    )(x, w)


GRP = NSA_HEADS // NSA_KV_HEADS
TQ = 128
TK = 128
LANES = 128
BUCKET_TABLE_LEN = 32768
BIAS_TILE_ELEMS = 16384
VMEM_LIMIT = 56 * 1024 * 1024


def rel_bucket(dist):
    n = jnp.maximum(dist, 0)
    exact = REL_BUCKETS // 2
    nf = jnp.maximum(n, 1).astype(jnp.float32)
    large = exact + (jnp.log(nf / exact) / math.log(REL_MAX_DIST / exact) * (REL_BUCKETS - exact)).astype(jnp.int32)
    return jnp.where(n < exact, n, jnp.minimum(large, REL_BUCKETS - 1))


def bucket_thresholds():
    tab = rel_bucket(jnp.arange(BUCKET_TABLE_LEN, dtype=jnp.int32))
    return jnp.sum(tab[None, :] < jnp.arange(REL_BUCKETS, dtype=jnp.int32)[:, None], axis=1).astype(jnp.int32)


def _bias_table_body(thr_ref, rb_ref, d_ref, o_ref):
    n = jnp.maximum(d_ref[0], 0)
    for h in range(NSA_HEADS):
        val = jnp.full(n.shape, rb_ref[h], jnp.float32)
        for k in range(1, REL_BUCKETS):
            val = jnp.where(n >= thr_ref[k], rb_ref[k * NSA_HEADS + h], val)
        o_ref[h // GRP, 0, h % GRP] = val


def bias_tables(dist, thr, rel_bias):
    n, r, c = dist.shape
    ct = min(c, BIAS_TILE_ELEMS // r)
    assert c % ct == 0
    return pl.pallas_call(
        _bias_table_body,
        grid_spec=pltpu.PrefetchScalarGridSpec(
            num_scalar_prefetch=2,
            grid=(n, c // ct),
            in_specs=[pl.BlockSpec((1, r, ct), lambda i, j, *_: (i, 0, j))],
            out_specs=pl.BlockSpec((NSA_KV_HEADS, 1, GRP, r, ct), lambda i, j, *_: (0, i, 0, 0, j)),
        ),
        out_shape=jax.ShapeDtypeStruct((NSA_KV_HEADS, n, GRP, r, c), jnp.float32),
        name="bias_tables",
    )(thr, rel_bias.reshape(-1), dist)


def _softmax_tile(s3, msk, m, l, acc, v):
    s3 = jnp.where(msk[None], s3, NEG_INF)
    m_new = jnp.maximum(m, jnp.max(s3, axis=-1, keepdims=True))
    p = jnp.where(msk[None], jnp.exp(s3 - m_new), 0.0)
    alpha = jnp.exp(m - m_new)
    l = alpha * l + jnp.sum(p, axis=-1, keepdims=True)
    pv = jnp.dot(p.reshape(GRP * TQ, p.shape[-1]).astype(jnp.bfloat16), v, preferred_element_type=jnp.float32)
    acc = alpha * acc + pv.reshape(GRP, TQ, HEAD_DIM)
    return m_new, l, acc


def _nsa_prompt_body(q_ref, g_ref, kc_ref, vc_ref, ks_ref, vs_ref, kw_ref, vw_ref, bt_ref, bc_ref, cov_ref, exp_ref,
                     o_ref, mask_ref, *, n_cmp, n_sel):
    f32, bf16 = jnp.float32, jnp.bfloat16
    qi = pl.program_id(2)
    nt_dims = (((1,), (1,)), ((), ()))
    q = (q_ref[0] * (HEAD_DIM ** -0.5)).reshape(GRP * TQ, HEAD_DIM).astype(bf16)
    qpos = qi * TQ + lax.broadcasted_iota(jnp.int32, (TQ, LANES), 0)
    lane = lax.broadcasted_iota(jnp.int32, (TQ, LANES), 1)

    ok_c = (qpos >= CMP_STRIDE * lane + (CMP_BLOCK - 1)) & (lane < n_cmp)
    s_c = lax.dot_general(q, kc_ref[0, 0].astype(bf16), nt_dims, preferred_element_type=f32) + bc_ref[0, 0].reshape(GRP * TQ, LANES)
    s_c = jnp.where(ok_c[None], s_c.reshape(GRP, TQ, LANES), NEG_INF)
    m_c = jnp.max(s_c, axis=-1, keepdims=True)
    e_c = jnp.where(ok_c[None], jnp.exp(s_c - m_c), 0.0)
    p_c = e_c / jnp.maximum(jnp.sum(e_c, axis=-1, keepdims=True), 1e-30)
    o_c = jnp.dot(p_c.reshape(GRP * TQ, LANES).astype(bf16), vc_ref[0, 0].astype(bf16),
                  preferred_element_type=f32).reshape(GRP, TQ, HEAD_DIM)

    imp = jnp.dot(jnp.sum(p_c, axis=0).astype(bf16), cov_ref[...], preferred_element_type=f32)
    cur = qpos // SEL_BLOCK
    forced = (lane == 0) | (lane == cur) | (lane == cur - 1)
    score = jnp.where(forced, FORCE_SCORE, jnp.where(lane * SEL_BLOCK <= qpos, imp, -1.0))
    score = jnp.where(lane < n_sel, score, -jnp.inf)
    sel = jnp.zeros((TQ, LANES), f32)
    for _ in range(min(SEL_TOP_N, n_sel)):
        best = jnp.max(score, axis=-1, keepdims=True)
        idx = jnp.min(jnp.where(score == best, lane, LANES), axis=-1, keepdims=True)
        hit = lane == idx
        sel = jnp.where(hit, 1.0, sel)
        score = jnp.where(hit, -jnp.inf, score)
    mask_ref[...] = jnp.dot(sel.astype(bf16), exp_ref[...], preferred_element_type=f32)

    row = lax.broadcasted_iota(jnp.int32, (TQ, TK), 0)
    col = lax.broadcasted_iota(jnp.int32, (TQ, TK), 1)
    init = (jnp.full((GRP, TQ, 1), NEG_INF, f32), jnp.zeros((GRP, TQ, 1), f32), jnp.zeros((GRP, TQ, HEAD_DIM), f32))

    def scores(k_ref, ki):
        k = k_ref[0, 0, pl.ds(pl.multiple_of(ki * TK, TK), TK), :].astype(bf16)
        s = lax.dot_general(q, k, nt_dims, preferred_element_type=f32)
        return s.reshape(GRP, TQ, TK) + bt_ref[0, qi - ki]

    def sel_step(ki, carry):
        dist = (qi - ki) * TQ + row - col
        msk = (mask_ref[:, pl.ds(pl.multiple_of(ki * TK, TK), TK)] > 0.5) & (dist >= 0)
        v = vs_ref[0, 0, pl.ds(pl.multiple_of(ki * TK, TK), TK), :].astype(bf16)
        return _softmax_tile(scores(ks_ref, ki), msk, *carry, v)

    def win_step(ki, carry):
        dist = (qi - ki) * TQ + row - col
        msk = (dist >= 0) & (dist < WINDOW)
        v = vw_ref[0, 0, pl.ds(pl.multiple_of(ki * TK, TK), TK), :].astype(bf16)
        return _softmax_tile(scores(kw_ref, ki), msk, *carry, v)

    _, l_s, acc_s = lax.fori_loop(0, qi + 1, sel_step, init)
    _, l_w, acc_w = lax.fori_loop(jnp.maximum(qi - WINDOW // TK, 0), qi + 1, win_step, init)
    o_s = acc_s / jnp.maximum(l_s, 1e-30)
    o_w = acc_w / jnp.maximum(l_w, 1e-30)
    g = jax.nn.sigmoid(g_ref[0])
    o_ref[0] = g[:, :, 0:1] * o_c + g[:, :, 1:2] * o_s + g[:, :, 2:3] * o_w


def nsa_prompt(q4, g4, kc, vc, ks, vs, kw, vw, thr, rel_bias, n_cmp):
    b, _, t, _ = q4.shape
    nq = t // TQ
    n_sel = t // SEL_BLOCK
    assert t % TQ == 0 and n_sel <= LANES and n_cmp <= LANES and TQ == TK
    i = jnp.arange(TQ, dtype=jnp.int32)
    d_toep = jnp.arange(nq, dtype=jnp.int32)[:, None, None] * TQ + i[:, None] - i[None, :]
    d_cmp = (jnp.arange(t, dtype=jnp.int32)[:, None] - (CMP_STRIDE * jnp.arange(LANES, dtype=jnp.int32)[None, :] + CMP_BLOCK - 1)).reshape(nq, TQ, LANES)
    tabs = bias_tables(jnp.concatenate([d_toep, d_cmp], axis=0), thr, rel_bias)
    bt, bc = tabs[:, :nq], tabs[:, nq:]
    c_start = CMP_STRIDE * jnp.arange(LANES)[:, None]
    s_start = SEL_BLOCK * jnp.arange(LANES)[None, :]
    cover = ((c_start < s_start + SEL_BLOCK) & (c_start + CMP_BLOCK > s_start)
             & (jnp.arange(LANES)[:, None] < n_cmp) & (jnp.arange(LANES)[None, :] < n_sel)).astype(jnp.bfloat16)
    expand = (jnp.arange(t)[None, :] // SEL_BLOCK == jnp.arange(LANES)[:, None]).astype(jnp.bfloat16)
    kv_spec = lambda rows: pl.BlockSpec((1, 1, rows, HEAD_DIM), lambda h, bb, qq: (bb, h, 0, 0))
    return pl.pallas_call(
        functools.partial(_nsa_prompt_body, n_cmp=n_cmp, n_sel=n_sel),
        grid=(NSA_KV_HEADS, b, nq),
        in_specs=[pl.BlockSpec((1, GRP, TQ, HEAD_DIM), lambda h, bb, qq: (bb, h, qq, 0)),
                  pl.BlockSpec((1, GRP, TQ, 3), lambda h, bb, qq: (bb, h, qq, 0)),
                  kv_spec(LANES), kv_spec(LANES), kv_spec(t), kv_spec(t), kv_spec(t), kv_spec(t),
                  pl.BlockSpec((1, nq, GRP, TQ, TK), lambda h, bb, qq: (h, 0, 0, 0, 0)),
                  pl.BlockSpec((1, 1, GRP, TQ, LANES), lambda h, bb, qq: (h, qq, 0, 0, 0)),
                  pl.BlockSpec((LANES, LANES), lambda h, bb, qq: (0, 0)),
                  pl.BlockSpec((LANES, t), lambda h, bb, qq: (0, 0))],
        out_specs=pl.BlockSpec((1, GRP, TQ, HEAD_DIM), lambda h, bb, qq: (bb, h, qq, 0)),
        out_shape=jax.ShapeDtypeStruct(q4.shape, jnp.float32),
        scratch_shapes=[pltpu.VMEM((TQ, t), jnp.float32)],
        compiler_params=pltpu.CompilerParams(vmem_limit_bytes=VMEM_LIMIT),
        name="nsa_prompt",
    )(q4, g4, kc, vc, ks, vs, kw, vw, bt, bc, cover, expand)


def _nsa_prefill_body(q_ref, g_ref, kc_ref, vct_ref, ks_ref, vst_ref, kw_ref, vwt_ref, bt_ref, bc_ref, cov_ref, exp_ref,
                      o_ref, mask_ref, *, n_cmp, n_sel):
    f32, bf16 = jnp.float32, jnp.bfloat16
    qi = pl.program_id(2)
    nt_dims = (((1,), (1,)), ((), ()))
    q = (q_ref[0] * (HEAD_DIM ** -0.5)).reshape(GRP * TQ, HEAD_DIM).astype(bf16)
    lanes = [slice(g * TQ, (g + 1) * TQ) for g in range(GRP)]
    row = lax.broadcasted_iota(jnp.int32, (LANES, TQ), 0)
    qpos = qi * TQ + lax.broadcasted_iota(jnp.int32, (LANES, TQ), 1)

    ok_c = (qpos >= CMP_STRIDE * row + (CMP_BLOCK - 1)) & (row < n_cmp)
    s_c = lax.dot_general(kc_ref[0, 0].astype(bf16), q, nt_dims, preferred_element_type=f32)
    p_sum = jnp.zeros((LANES, TQ), f32)
    p_parts = []
    for g in range(GRP):
        p = _masked_softmax_rows(s_c[:, lanes[g]] + bc_ref[0, 0, g], ok_c)
        p_sum = p_sum + p
        p_parts.append(p.astype(bf16))
    o_c = jnp.dot(vct_ref[0, 0].astype(bf16), jnp.concatenate(p_parts, axis=1), preferred_element_type=f32)

    imp = jnp.dot(cov_ref[...], p_sum.astype(bf16), preferred_element_type=f32)
    rs = -(-n_sel // 8) * 8
    blk = lax.broadcasted_iota(jnp.int32, (rs, TQ), 0)
    qp = qi * TQ + lax.broadcasted_iota(jnp.int32, (rs, TQ), 1)
    cur = qp // SEL_BLOCK
    forced = (blk == 0) | (blk == cur) | (blk == cur - 1)
    score = jnp.where(forced, FORCE_SCORE, jnp.where(blk * SEL_BLOCK <= qp, imp[:rs], -1.0))
    score = jnp.where(blk < n_sel, score, -jnp.inf)
    sel = jnp.zeros((rs, TQ), f32)
    for s in range(n_sel):
        r = jnp.max(jnp.where(blk == s, score, -jnp.inf), axis=0, keepdims=True)
        beats = (score > r) | ((score == r) & (blk < s))
        rank = jnp.sum(jnp.where(beats, 1.0, 0.0), axis=0, keepdims=True)
        sel = jnp.where((blk == s) & (rank < min(SEL_TOP_N, n_sel)), 1.0, sel)
    sel = jnp.concatenate([sel, jnp.zeros((LANES - rs, TQ), f32)], axis=0)
    mask_ref[...] = jnp.dot(exp_ref[...], sel.astype(bf16), preferred_element_type=f32)

    key_i = lax.broadcasted_iota(jnp.int32, (TK, TQ), 0)
    qry_i = lax.broadcasted_iota(jnp.int32, (TK, TQ), 1)
    init = (jnp.full((1, GRP * TQ), NEG_INF, f32), jnp.zeros((1, GRP * TQ), f32), jnp.zeros((HEAD_DIM, GRP * TQ), f32))

    def tile(ki, carry, k_ref, vt_ref, msk):
        m, l, acc = carry
        keys = pl.ds(pl.multiple_of(ki * TK, TK), TK)
        s_all = lax.dot_general(k_ref[0, 0, keys, :].astype(bf16), q, nt_dims, preferred_element_type=f32)
        m_parts, l_parts, a_parts, p_parts = [], [], [], []
        for g in range(GRP):
            s = jnp.where(msk, s_all[:, lanes[g]] + bt_ref[0, qi - ki, g], NEG_INF)
            m_new = jnp.maximum(m[:, lanes[g]], jnp.max(s, axis=0, keepdims=True))
            p = jnp.where(msk, jnp.exp(s - m_new), 0.0)
            alpha = jnp.exp(m[:, lanes[g]] - m_new)
            m_parts.append(m_new)
            a_parts.append(alpha)
            l_parts.append(alpha * l[:, lanes[g]] + jnp.sum(p, axis=0, keepdims=True))
            p_parts.append(p.astype(bf16))
        pv = jnp.dot(vt_ref[0, 0, :, keys].astype(bf16), jnp.concatenate(p_parts, axis=1), preferred_element_type=f32)
        return (jnp.concatenate(m_parts, axis=1), jnp.concatenate(l_parts, axis=1),
                jnp.concatenate(a_parts, axis=1) * acc + pv)

    def sel_step(ki, carry):
        dist = (qi - ki) * TQ + qry_i - key_i
        msk = (mask_ref[pl.ds(pl.multiple_of(ki * TK, TK), TK), :] > 0.5) & (dist >= 0)
        return tile(ki, carry, ks_ref, vst_ref, msk)

    def win_step(ki, carry):
        dist = (qi - ki) * TQ + qry_i - key_i
        return tile(ki, carry, kw_ref, vwt_ref, (dist >= 0) & (dist < WINDOW))

    _, l_s, acc_s = lax.fori_loop(0, qi + 1, sel_step, init)
    _, l_w, acc_w = lax.fori_loop(jnp.maximum(qi - WINDOW // TK, 0), qi + 1, win_step, init)
    o_s = acc_s / jnp.maximum(l_s, 1e-30)
    o_w = acc_w / jnp.maximum(l_w, 1e-30)
    for g in range(GRP):
        gate = jax.nn.sigmoid(g_ref[0, 0, :, g, :])
        o_ref[0, g] = gate[0:1] * o_c[:, lanes[g]] + gate[1:2] * o_s[:, lanes[g]] + gate[2:3] * o_w[:, lanes[g]]


def _masked_softmax_rows(s, ok):
    s = jnp.where(ok, s, NEG_INF)
    e = jnp.where(ok, jnp.exp(s - jnp.max(s, axis=0, keepdims=True)), 0.0)
    return e / jnp.maximum(jnp.sum(e, axis=0, keepdims=True), 1e-30)


def nsa_prefill(q, glog, kcvc, kv_new, thr, rel_bias, n_cmp):
    f32, bf16 = jnp.float32, jnp.bfloat16
    b, t = q.shape[:2]
    nq = t // TQ
    n_sel = t // SEL_BLOCK
    assert t % TQ == 0 and n_sel <= LANES and n_cmp < LANES and kcvc.shape[1] == LANES and TQ == TK == LANES
    q4 = jnp.transpose(q, (0, 2, 1, 3))
    g5 = jnp.transpose(glog.reshape(b, t, NSA_KV_HEADS, GRP, 3), (0, 2, 4, 3, 1))
    cmp_tok = jnp.pad(kcvc[:, 1:], ((0, 0), (0, 1), (0, 0))).reshape(b, LANES, 2, NSA_KV_HEADS, HEAD_DIM)
    kc = jnp.transpose(cmp_tok[:, :, 0], (0, 2, 1, 3))
    vct = jnp.transpose(cmp_tok[:, :, 1], (0, 2, 3, 1))
    rows = lambda n: jnp.transpose(kv_new[:, :, n], (0, 2, 1, 3))
    cols = lambda n: jnp.transpose(kv_new[:, :, n], (0, 2, 3, 1))
    i = jnp.arange(TQ, dtype=jnp.int32)
    d_toep = jnp.arange(nq, dtype=jnp.int32)[:, None, None] * TQ + i[None, None, :] - i[None, :, None]
    c_end = CMP_STRIDE * jnp.arange(LANES, dtype=jnp.int32) + CMP_BLOCK - 1
    d_cmp = jnp.arange(t, dtype=jnp.int32).reshape(nq, 1, TQ) - c_end[None, :, None]
    tabs = bias_tables(jnp.concatenate([d_toep, d_cmp], axis=0), thr, rel_bias)
    bt, bc = tabs[:, :nq], tabs[:, nq:]
    s_start = SEL_BLOCK * jnp.arange(LANES)[:, None]
    c_start = CMP_STRIDE * jnp.arange(LANES)[None, :]
    cover = ((c_start < s_start + SEL_BLOCK) & (c_start + CMP_BLOCK > s_start)
             & (jnp.arange(LANES)[None, :] < n_cmp) & (jnp.arange(LANES)[:, None] < n_sel)).astype(bf16)
    expand = (jnp.arange(t)[:, None] // SEL_BLOCK == jnp.arange(LANES)[None, :]).astype(bf16)
    spec = lambda *blk: pl.BlockSpec((1, 1) + blk, lambda h, bb, qq: (bb, h) + (0,) * len(blk))
    out = pl.pallas_call(
        functools.partial(_nsa_prefill_body, n_cmp=n_cmp, n_sel=n_sel),
        grid=(NSA_KV_HEADS, b, nq),
        in_specs=[pl.BlockSpec((1, GRP, TQ, HEAD_DIM), lambda h, bb, qq: (bb, h, qq, 0)),
                  pl.BlockSpec((1, 1, 3, GRP, TQ), lambda h, bb, qq: (bb, h, 0, 0, qq)),
                  spec(LANES, HEAD_DIM), spec(HEAD_DIM, LANES),
                  spec(t, HEAD_DIM), spec(HEAD_DIM, t), spec(t, HEAD_DIM), spec(HEAD_DIM, t),
                  pl.BlockSpec((1, nq, GRP, TK, TQ), lambda h, bb, qq: (h, 0, 0, 0, 0)),
                  pl.BlockSpec((1, 1, GRP, LANES, TQ), lambda h, bb, qq: (h, qq, 0, 0, 0)),
                  pl.BlockSpec((LANES, LANES), lambda h, bb, qq: (0, 0)),
                  pl.BlockSpec((t, LANES), lambda h, bb, qq: (0, 0))],
        out_specs=pl.BlockSpec((1, GRP, HEAD_DIM, TQ), lambda h, bb, qq: (bb, h, 0, qq)),
        out_shape=jax.ShapeDtypeStruct((b, NSA_HEADS, HEAD_DIM, t), f32),
        scratch_shapes=[pltpu.VMEM((t, TQ), f32)],
        compiler_params=pltpu.CompilerParams(vmem_limit_bytes=VMEM_LIMIT),
        name="nsa_prefill",
    )(q4, g5, kc, vct, rows(2), cols(3), rows(4), cols(5), bt, bc, cover, expand)
    return jnp.transpose(out, (0, 3, 1, 2)).reshape(b, t, NSA_WIDTH)


S5_GB = 8
S5_CH = S5_GB * SSM_CH
S5_ST = S5_GB * SSM_STATE


def _cmul(a_re, a_im, b_re, b_im):
    return a_re * b_re - a_im * b_im, a_re * b_im + a_im * b_re


def _s5_body(u_ref, h0re_ref, h0im_ref, a_ref, bcat_ref, ccat_ref, d_ref, y_ref, hre_ref, him_ref, xs_ref, hin_ref,
             *, chained):
    f32, bf16 = jnp.float32, jnp.bfloat16
    n_l, n_r = u_ref.shape[1], u_ref.shape[2]
    u2 = u_ref[0].reshape(n_l * n_r, S5_CH)
    xs_ref[...] = jnp.dot(u2.astype(bf16), bcat_ref[0].astype(bf16),
                          preferred_element_type=f32).reshape(n_l, n_r, 2 * S5_ST)
    a_re, a_im = a_ref[0:1, :], a_ref[1:2, :]

    def scan_step(j, carry):
        h_re, h_im, p_re, p_im = carry
        x = xs_ref[j]
        t_re, t_im = _cmul(a_re, a_im, h_re, h_im)
        h_re, h_im = t_re + x[:, :S5_ST], t_im + x[:, S5_ST:]
        xs_ref[j] = jnp.concatenate([h_re, h_im], axis=1)
        return (h_re, h_im) + _cmul(a_re, a_im, p_re, p_im)

    if chained:
        start = (jnp.zeros((n_r, S5_ST), f32), jnp.zeros((n_r, S5_ST), f32))
    else:
        start = (h0re_ref[0], h0im_ref[0])
    ones = (jnp.ones((1, S5_ST), f32), jnp.zeros((1, S5_ST), f32))
    h_re, h_im, al_re, al_im = lax.fori_loop(0, n_l, scan_step, start + ones)

    if chained:
        hin_ref[0:1, :] = jnp.concatenate([h0re_ref[0], h0im_ref[0]], axis=1)

        def chain_step(c, carry):
            z = xs_ref[n_l - 1, pl.ds(c - 1, 1), :]
            t_re, t_im = _cmul(al_re, al_im, *carry)
            n_re, n_im = t_re + z[:, :S5_ST], t_im + z[:, S5_ST:]
            hin_ref[pl.ds(c, 1), :] = jnp.concatenate([n_re, n_im], axis=1)
            return n_re, n_im

        h_re, h_im = lax.fori_loop(1, n_r + 1, chain_step, (h0re_ref[0], h0im_ref[0]))

        def fix_step(j, carry):
            p_re, p_im = carry
            hin = hin_ref[0:n_r, :]
            t_re, t_im = _cmul(p_re, p_im, hin[:, :S5_ST], hin[:, S5_ST:])
            xs_ref[j] = xs_ref[j] + jnp.concatenate([t_re, t_im], axis=1)
            return _cmul(a_re, a_im, p_re, p_im)

        lax.fori_loop(0, n_l, fix_step, (a_re, a_im))

    hre_ref[0] = h_re
    him_ref[0] = h_im
    hs = xs_ref[...].reshape(n_l * n_r, 2 * S5_ST).astype(bf16)
    y = jnp.dot(hs, ccat_ref[0].astype(bf16), preferred_element_type=f32) + d_ref[...] * u2
    y_ref[0] = jax.nn.gelu(y).reshape(n_l, n_r, S5_CH)


def s5_discretize(lam_re, lam_im, log_dt, b_re, b_im, c_re, c_im):
    f32 = jnp.float32
    dt = jnp.exp(log_dt.astype(f32))[:, None]
    lr, li = lam_re.astype(f32), lam_im.astype(f32)
    mag = jnp.exp(lr * dt)
    a_re, a_im = mag * jnp.cos(li * dt), mag * jnp.sin(li * dt)
    den = lr * lr + li * li
    z_re = ((a_re - 1.0) * lr + a_im * li) / den
    z_im = (a_im * lr - (a_re - 1.0) * li) / den
    br, bim = b_re.astype(f32), b_im.astype(f32)
    bb_re = z_re[..., None] * br - z_im[..., None] * bim
    bb_im = z_re[..., None] * bim + z_im[..., None] * br
    ngb = SSM_GROUPS // S5_GB
    eye = jnp.eye(S5_GB, dtype=f32)

    def block_diag(w):
        wd = w[:, :, :, None, :] * eye[None, :, None, :, None]
        return wd.reshape(ngb, S5_GB * w.shape[2], S5_GB * w.shape[3])

    def pack_b(bb):
        return block_diag(jnp.swapaxes(bb.reshape(ngb, S5_GB, SSM_STATE, SSM_CH), 2, 3))

    def pack_c(cc):
        return block_diag(jnp.swapaxes(cc.reshape(ngb, S5_GB, SSM_CH, SSM_STATE), 2, 3))

    a = jnp.stack([a_re.reshape(-1), a_im.reshape(-1)])
    bcat = jnp.concatenate([pack_b(bb_re), pack_b(bb_im)], axis=2)
    ccat = jnp.concatenate([pack_c(c_re.astype(f32)), -pack_c(c_im.astype(f32))], axis=1)
    return a, bcat, ccat


def s5_scan(u4, h0_re, h0_im, a, bcat, ccat, d_skip, chained):
    nb, n_l, n_r, _ = u4.shape
    rh = h0_re.shape[1]
    ngb = SSM_GROUPS // S5_GB
    st_spec = pl.BlockSpec((1, rh, S5_ST), lambda i, j: (i, 0, j))
    return pl.pallas_call(
        functools.partial(_s5_body, chained=chained),
        grid=(nb, ngb),
        in_specs=[pl.BlockSpec((1, n_l, n_r, S5_CH), lambda i, j: (i, 0, 0, j)), st_spec, st_spec,
                  pl.BlockSpec((2, S5_ST), lambda i, j: (0, j)),
                  pl.BlockSpec((1, S5_CH, 2 * S5_ST), lambda i, j: (j, 0, 0)),
                  pl.BlockSpec((1, 2 * S5_ST, S5_CH), lambda i, j: (j, 0, 0)),
                  pl.BlockSpec((1, S5_CH), lambda i, j: (0, j))],
        out_specs=[pl.BlockSpec((1, n_l, n_r, S5_CH), lambda i, j: (i, 0, 0, j)), st_spec, st_spec],
        out_shape=[jax.ShapeDtypeStruct(u4.shape, jnp.float32),
                   jax.ShapeDtypeStruct(h0_re.shape, jnp.float32), jax.ShapeDtypeStruct(h0_re.shape, jnp.float32)],
        scratch_shapes=[pltpu.VMEM((n_l, n_r, 2 * S5_ST), jnp.float32), pltpu.VMEM((n_r + 8, 2 * S5_ST), jnp.float32)],
        compiler_params=pltpu.CompilerParams(vmem_limit_bytes=VMEM_LIMIT),
        name="s5_scan",
    )(u4, h0_re, h0_im, a, bcat, ccat, d_skip.reshape(1, SSM_WIDTH))


S5_CHUNK = 64


def s5_mixer_pallas(u, h0_re, h0_im, lam_re, lam_im, log_dt, b_re, b_im, c_re, c_im, d_skip, w_glu, b_glu, chained):
    bsz, t, _ = u.shape
    f32 = jnp.float32
    a, bcat, ccat = s5_discretize(lam_re, lam_im, log_dt, b_re, b_im, c_re, c_im)
    flat = lambda h: h.astype(f32).reshape(bsz, SSM_GROUPS * SSM_STATE)
    if chained:
        n_r = t // S5_CHUNK
        u4 = jnp.transpose(u.astype(f32).reshape(bsz, n_r, S5_CHUNK, SSM_WIDTH), (0, 2, 1, 3))
        y4, h_re, h_im = s5_scan(u4, flat(h0_re)[:, None], flat(h0_im)[:, None], a, bcat, ccat, d_skip.astype(f32), True)
        y = jnp.transpose(y4, (0, 2, 1, 3)).reshape(bsz, t, SSM_WIDTH)
        h_re, h_im = h_re[:, 0], h_im[:, 0]
    else:
        u4 = jnp.transpose(u.astype(f32), (1, 0, 2))[None]
        y4, h_re, h_im = s5_scan(u4, flat(h0_re)[None], flat(h0_im)[None], a, bcat, ccat, d_skip.astype(f32), False)
        y = jnp.transpose(y4[0], (1, 0, 2))
        h_re, h_im = h_re[0], h_im[0]
    out = y * jax.nn.sigmoid(y @ w_glu.astype(f32) + b_glu.astype(f32))
    st = lambda h: h.reshape(bsz, SSM_GROUPS, SSM_STATE).astype(u.dtype)
    return out.astype(u.dtype), st(h_re), st(h_im)


ROW_LANES = N_PAGED * KV_WIDTH
CHUNKS_PER_PAGE = 128 // CMP_STRIDE
CMP_HID2 = NSA_KV_HEADS * CMP_HIDDEN


def _compress_body(pt_ref, *refs, n_g, transposed):
    f32, bf16 = jnp.float32, jnp.bfloat16
    page_refs = (refs[:n_g], refs[n_g:2 * n_g])
    w1_ref, c1_ref, w2_ref, b2_ref, o_ref, carry_ref, rows_ref = refs[2 * n_g:]
    m_rows = n_g * CHUNKS_PER_PAGE
    first = pl.program_id(1) == 0
    row_id = lax.broadcasted_iota(jnp.int32, (m_rows, CMP_HID2), 0)
    outs = []
    for t in range(2):
        for k, r in enumerate(page_refs[t]):
            rows_ref[k * 128:(k + 1) * 128, :] = r[0].T if transposed else r[0]
        cols = [rows_ref[pl.ds(j, m_rows, stride=CMP_STRIDE), :] for j in range(CMP_STRIDE)]
        x = jnp.concatenate(cols, axis=1).astype(bf16)
        part = jnp.dot(x, w1_ref[t], preferred_element_type=f32)
        p0, p1 = part[:, :CMP_HID2], part[:, CMP_HID2:]
        prev = jnp.where(first, 0.0, carry_ref[t, 0:1, :])
        shifted = jnp.where(row_id == 0, prev, pltpu.roll(p0, 1, axis=0))
        carry_ref[t, 0:1, :] = p0[m_rows - 1:m_rows, :]
        h1 = (c1_ref[t] + shifted) + p1
        outs.append(jnp.dot(jax.nn.gelu(h1).astype(bf16), w2_ref[t], preferred_element_type=f32) + b2_ref[t])
    o_ref[0] = jnp.concatenate(outs, axis=1)


def compress_pages(pool, page_table, cmp_pe, cmp_w1, cmp_b1, cmp_w2, cmp_b2, n_g, transposed):
    n_seq, n_pages = page_table.shape
    assert n_pages % n_g == 0 and pool.shape[1:] == ((ROW_LANES, 128) if transposed else (128, ROW_LANES))
    f32, bf16 = jnp.float32, jnp.bfloat16
    r = CMP_BLOCK // CMP_STRIDE
    eye = jnp.eye(NSA_KV_HEADS, dtype=f32)
    w1 = cmp_w1.astype(f32).reshape(2, r, CMP_STRIDE, HEAD_DIM, CMP_HIDDEN)
    w1 = jnp.transpose(w1, (0, 2, 3, 1, 4))[:, :, None, :, :, None, :] * eye[None, None, :, None, None, :, None]
    w1 = w1.reshape(2, CMP_STRIDE * KV_WIDTH, r * CMP_HID2).astype(bf16)
    c1 = jnp.stack([jnp.einsum('ld,ldf->f', cmp_pe[t], cmp_w1[t]) + cmp_b1[t] for t in range(2)])
    c1 = jnp.tile(c1[:, None, :], (1, 1, NSA_KV_HEADS))
    w2 = (cmp_w2.astype(f32)[:, None, :, None, :] * eye[None, :, None, :, None]).reshape(2, CMP_HID2, KV_WIDTH).astype(bf16)
    b2 = jnp.tile(cmp_b2.astype(f32)[:, None, :], (1, 1, NSA_KV_HEADS))
    m_rows = n_g * CHUNKS_PER_PAGE
    page_spec = lambda k, t: pl.BlockSpec((1, 128, KV_WIDTH), lambda b, g, pt: (
        (pt[b * n_pages + g * n_g + k], t, 0) if transposed else (pt[b * n_pages + g * n_g + k], 0, t)))
    full = lambda a: pl.BlockSpec(a.shape, lambda b, g, pt: (0,) * a.ndim)
    return pl.pallas_call(
        functools.partial(_compress_body, n_g=n_g, transposed=transposed),
        grid_spec=pltpu.PrefetchScalarGridSpec(
            num_scalar_prefetch=1,
            grid=(n_seq, n_pages // n_g),
            in_specs=[page_spec(k, t) for t in range(2) for k in range(n_g)] + [full(w1), full(c1), full(w2), full(b2)],
            out_specs=pl.BlockSpec((1, m_rows, 2 * KV_WIDTH), lambda b, g, pt: (b, g, 0)),
            scratch_shapes=[pltpu.VMEM((2, 8, CMP_HID2), f32), pltpu.VMEM((n_g * 128, KV_WIDTH), f32)],
        ),
        out_shape=jax.ShapeDtypeStruct((n_seq, n_pages * CHUNKS_PER_PAGE, 2 * KV_WIDTH), f32),
        compiler_params=pltpu.CompilerParams(vmem_limit_bytes=VMEM_LIMIT),
        name="compress_pages",
    )(page_table.reshape(-1).astype(jnp.int32), *([pool] * (2 * n_g)), w1, c1, w2, b2)


SD_PAGES = 64
SD_KEYS = SD_PAGES * 128
SD_SUB = 16
SD_ROWS = NSA_HEADS * 8
WIN_PAD = 640


def _masked_softmax(s, ok):
    s = jnp.where(ok, s, NEG_INF)
    e = jnp.where(ok, jnp.exp(s - jnp.max(s, axis=-1, keepdims=True)), 0.0)
    return e / jnp.maximum(jnp.sum(e, axis=-1, keepdims=True), 1e-30)


def _rows_from_group(a):
    t = a.shape[0] // NSA_KV_HEADS
    a4 = jnp.broadcast_to(a.reshape(NSA_KV_HEADS, 1, t, a.shape[1]), (NSA_KV_HEADS, GRP, t, a.shape[1]))
    return a4.reshape(NSA_KV_HEADS * GRP * t, a.shape[1])


def _nsa_decode_body(pt_ref, *refs, pos0, n_cmp, n_sel, n_new, win_len):
    f32, bf16 = jnp.float32, jnp.bfloat16
    kt_refs, vt_refs = refs[:SD_PAGES], refs[SD_PAGES:2 * SD_PAGES]
    (q_ref, g_ref, kcvc_ref, knew_ref, wh_ref, bc_ref, bs_ref, bn_ref, bw_ref, cov_ref, exp_ref,
     o_ref, sel_ref, oc_ref, m_ref, l_ref, acc_ref, mask_ref) = refs[2 * SD_PAGES:]
    tile, n_tiles = pl.program_id(1), pl.num_programs(1)
    nt_dims = (((1,), (1,)), ((), ()))
    nn_dims = (((1,), (0,)), ((), ()))
    n_tok = SD_ROWS // NSA_HEADS
    q = (q_ref[0] * (HEAD_DIM ** -0.5)).astype(bf16)
    qpos = pos0 + lax.broadcasted_iota(jnp.int32, (SD_ROWS, 1), 0) % n_tok

    @pl.when(tile == 0)
    def _():
        n_c = kcvc_ref.shape[1]
        m_idx = lax.broadcasted_iota(jnp.int32, (SD_ROWS, n_c), 1)
        ok_c = (m_idx >= 1) & (m_idx <= n_cmp) & (qpos >= CMP_STRIDE * m_idx + (CMP_BLOCK - 1 - CMP_STRIDE))
        s_c = lax.dot_general(q, kcvc_ref[0, :, :KV_WIDTH].astype(bf16), nt_dims, preferred_element_type=f32) + bc_ref[...]
        p_c = _masked_softmax(s_c, ok_c)
        oc_ref[...] = jnp.dot(p_c.astype(bf16), kcvc_ref[0, :, KV_WIDTH:].astype(bf16), preferred_element_type=f32)
        p_sum = jnp.sum(p_c.reshape(NSA_KV_HEADS, GRP, n_tok, n_c), axis=1).reshape(NSA_KV_HEADS * n_tok, n_c)
        imp = jnp.dot(p_sum.astype(bf16), cov_ref[...], preferred_element_type=f32)
        n_l = imp.shape[1]
        lane = lax.broadcasted_iota(jnp.int32, (NSA_KV_HEADS * n_tok, n_l), 1)
        qp = pos0 + lax.broadcasted_iota(jnp.int32, (NSA_KV_HEADS * n_tok, n_l), 0) % n_tok
        cur = qp // SEL_BLOCK
        forced = (lane == 0) | (lane == cur) | (lane == cur - 1)
        score = jnp.where(forced, FORCE_SCORE, jnp.where(lane * SEL_BLOCK <= qp, imp, -1.0))
        score = jnp.where(lane < n_sel, score, -jnp.inf)
        sel = jnp.zeros(score.shape, f32)
        for _ in range(min(SEL_TOP_N, n_sel)):
            best = jnp.max(score, axis=-1, keepdims=True)
            idx = jnp.min(jnp.where(score == best, lane, n_l), axis=-1, keepdims=True)
            hit = lane == idx
            sel = jnp.where(hit, 1.0, sel)
            score = jnp.where(hit, -jnp.inf, score)
        sel_ref[...] = sel
        m_ref[...] = jnp.full(m_ref.shape, NEG_INF, f32)
        l_ref[...] = jnp.zeros(l_ref.shape, f32)
        acc_ref[...] = jnp.zeros(acc_ref.shape, f32)

    def online(s, ok, v, v_dims):
        m_old = m_ref[...]
        s = jnp.where(ok, s, NEG_INF)
        m_new = jnp.maximum(m_old, jnp.max(s, axis=-1, keepdims=True))
        p = jnp.where(ok, jnp.exp(s - m_new), 0.0)
        alpha = jnp.exp(m_old - m_new)
        l_ref[...] = alpha * l_ref[...] + jnp.sum(p, axis=-1, keepdims=True)
        acc_ref[...] = alpha * acc_ref[...] + lax.dot_general(p.astype(bf16), v, v_dims, preferred_element_type=f32)
        m_ref[...] = m_new

    sel_tile = sel_ref[:, pl.ds(pl.multiple_of(tile * LANES, LANES), LANES)]
    mask_ref[...] = jnp.dot(sel_tile.astype(bf16), exp_ref[...], preferred_element_type=f32)
    sub_keys = SD_SUB * 128
    for sub in range(SD_PAGES // SD_SUB):
        pages = slice(sub * SD_SUB, (sub + 1) * SD_SUB)
        kt = jnp.concatenate([r[0] for r in kt_refs[pages]], axis=1).astype(bf16)
        vt = jnp.concatenate([r[0] for r in vt_refs[pages]], axis=1).astype(bf16)
        s = jnp.dot(q, kt, preferred_element_type=f32) + bs_ref[:, sub * sub_keys:(sub + 1) * sub_keys]
        ok = _rows_from_group(mask_ref[:, sub * sub_keys:(sub + 1) * sub_keys]) > 0.5
        online(s, ok, vt, nt_dims)

    @pl.when(tile == n_tiles - 1)
    def _():
        lane = lax.broadcasted_iota(jnp.int32, (SD_ROWS, LANES), 1)
        new_blk = pos0 // SEL_BLOCK
        sel_new = _rows_from_group(sel_ref[:, new_blk:new_blk + 1]) > 0.5
        ok_n = sel_new & (lane < n_new) & (pos0 + lane <= qpos)
        s_n = lax.dot_general(q, knew_ref[0, :, :KV_WIDTH].astype(bf16), nt_dims, preferred_element_type=f32) + bn_ref[...]
        online(s_n, ok_n, knew_ref[0, :, KV_WIDTH:].astype(bf16), nn_dims)
        o_s = acc_ref[...] / jnp.maximum(l_ref[...], 1e-30)
        j = lax.broadcasted_iota(jnp.int32, (SD_ROWS, WIN_PAD), 1)
        kwpos = pos0 - win_len + j
        dist = qpos - kwpos
        ok_w = (dist >= 0) & (dist < WINDOW) & (kwpos >= 0) & (j < win_len + n_new)
        s_w = lax.dot_general(q, wh_ref[0, :, :KV_WIDTH].astype(bf16), nt_dims, preferred_element_type=f32) + bw_ref[...]
        o_w = jnp.dot(_masked_softmax(s_w, ok_w).astype(bf16), wh_ref[0, :, KV_WIDTH:].astype(bf16), preferred_element_type=f32)
        g = jax.nn.sigmoid(g_ref[0])
        o = g[:, 0:1] * oc_ref[...] + g[:, 1:2] * o_s + g[:, 2:3] * o_w
        row = lax.broadcasted_iota(jnp.int32, (SD_ROWS, HEAD_DIM), 0)
        o_ref[0] = jnp.where(row < SD_ROWS // NSA_KV_HEADS, o[:, :HEAD_DIM], o[:, HEAD_DIM:])


def nsa_decode(q, glog, kv_new, kcvc, pool_t, page_table, cache_win, thr, rel_bias, pos0):
    f32, bf16 = jnp.float32, jnp.bfloat16
    b, t = q.shape[:2]
    n_pages = page_table.shape[1]
    win_len = cache_win.shape[1]
    assert t * NSA_HEADS == SD_ROWS and pos0 == n_pages * 128 and n_pages % SD_PAGES == 0 and pos0 % SEL_BLOCK == 0
    assert win_len + t <= WIN_PAD and t <= SEL_BLOCK and SD_KEYS == LANES * SEL_BLOCK
    n_rows = pos0 + t
    n_cmp = (n_rows - CMP_BLOCK) // CMP_STRIDE + 1
    n_sel = -(-n_rows // SEL_BLOCK)
    n_c = kcvc.shape[1]
    sel_lanes = -(-n_sel // LANES) * LANES
    q5 = jnp.transpose(q.reshape(b, t, NSA_KV_HEADS, GRP, HEAD_DIM), (0, 2, 3, 1, 4))
    qz = (q5[:, :, :, :, None, :] * jnp.eye(NSA_KV_HEADS, dtype=f32)[None, :, None, None, :, None]).reshape(b, SD_ROWS, KV_WIDTH)
    g3 = jnp.transpose(glog.reshape(b, t, NSA_HEADS, 3), (0, 2, 1, 3)).reshape(b, SD_ROWS, 3)
    knew = jnp.pad(kv_new[:, :, 2:4].reshape(b, t, 2 * KV_WIDTH), ((0, 0), (0, LANES - t), (0, 0)))
    whist = jnp.concatenate([cache_win.reshape(b, win_len, 2 * KV_WIDTH), kv_new[:, :, 4:6].reshape(b, t, 2 * KV_WIDTH),
                             jnp.zeros((b, WIN_PAD - win_len - t, 2 * KV_WIDTH), f32)], axis=1)
    qp = pos0 + jnp.arange(t, dtype=jnp.int32)[:, None]
    tab = lambda dist: bias_tables(dist[None], thr, rel_bias).reshape(SD_ROWS, dist.shape[1])
    bias_c = tab(qp - (CMP_STRIDE * jnp.arange(n_c, dtype=jnp.int32)[None, :] + CMP_BLOCK - 1 - CMP_STRIDE))
    bias_s = tab(qp - jnp.arange(pos0, dtype=jnp.int32)[None, :])
    bias_n = tab(qp - (pos0 + jnp.arange(LANES, dtype=jnp.int32)[None, :]))
    bias_w = tab(qp - (pos0 - win_len + jnp.arange(WIN_PAD, dtype=jnp.int32)[None, :]))
    m_idx = jnp.arange(n_c)[:, None]
    c_start = CMP_STRIDE * (m_idx - 1)
    s_start = SEL_BLOCK * jnp.arange(sel_lanes)[None, :]
    cover = ((c_start < s_start + SEL_BLOCK) & (c_start + CMP_BLOCK > s_start) & (m_idx >= 1) & (m_idx <= n_cmp)
             & (jnp.arange(sel_lanes)[None, :] < n_sel)).astype(bf16)
    expand = (jnp.arange(SD_KEYS)[None, :] // SEL_BLOCK == jnp.arange(LANES)[:, None]).astype(bf16)
    page_spec = lambda k, blk: pl.BlockSpec((1, KV_WIDTH, 128), lambda bb, g, pt: (pt[bb * n_pages + g * SD_PAGES + k], blk, 0))
    per_seq = lambda a: pl.BlockSpec((1,) + a.shape[1:], lambda bb, g, pt: (bb,) + (0,) * (a.ndim - 1))
    full = lambda a: pl.BlockSpec(a.shape, lambda bb, g, pt: (0,) * a.ndim)
    kv_rows = NSA_KV_HEADS * t
    out = pl.pallas_call(
        functools.partial(_nsa_decode_body, pos0=pos0, n_cmp=n_cmp, n_sel=n_sel, n_new=t, win_len=win_len),
        grid_spec=pltpu.PrefetchScalarGridSpec(
            num_scalar_prefetch=1,
            grid=(b, n_pages // SD_PAGES),
            in_specs=[page_spec(k, blk) for blk in (2, 3) for k in range(SD_PAGES)]
            + [per_seq(qz), per_seq(g3), per_seq(kcvc), per_seq(knew), per_seq(whist), full(bias_c),
               pl.BlockSpec((SD_ROWS, SD_KEYS), lambda bb, g, pt: (0, g)), full(bias_n), full(bias_w), full(cover), full(expand)],
            out_specs=pl.BlockSpec((1, SD_ROWS, HEAD_DIM), lambda bb, g, pt: (bb, 0, 0)),
            scratch_shapes=[pltpu.VMEM((kv_rows, sel_lanes), f32), pltpu.VMEM((SD_ROWS, KV_WIDTH), f32),
                            pltpu.VMEM((SD_ROWS, 1), f32), pltpu.VMEM((SD_ROWS, 1), f32), pltpu.VMEM((SD_ROWS, KV_WIDTH), f32),
                            pltpu.VMEM((kv_rows, SD_KEYS), f32)],
        ),
        out_shape=jax.ShapeDtypeStruct((b, SD_ROWS, HEAD_DIM), f32),
        compiler_params=pltpu.CompilerParams(vmem_limit_bytes=VMEM_LIMIT),
        name="nsa_decode",
    )(page_table.reshape(-1).astype(jnp.int32), *([pool_t] * (2 * SD_PAGES)), qz, g3, kcvc, knew, whist,
      bias_c, bias_s, bias_n, bias_w, cover, expand)
    out = jnp.transpose(out.reshape(b, NSA_KV_HEADS, GRP, t, HEAD_DIM), (0, 3, 1, 2, 4))
    return out.reshape(b, t, NSA_WIDTH)


def rms_norm(x, g):
    xf = x.astype(jnp.float32)
    y = xf * lax.rsqrt(jnp.mean(xf * xf, axis=-1, keepdims=True) + RMS_EPS)
    return (y * g.astype(jnp.float32)).astype(x.dtype)


def masked_softmax(s, mask):
    s = jnp.where(mask, s.astype(jnp.float32), NEG_INF)
    m = jnp.max(s, axis=-1, keepdims=True)
    e = jnp.where(mask, jnp.exp(s - m), 0.0)
    return e / jnp.maximum(jnp.sum(e, axis=-1, keepdims=True), 1e-30)


def compress_rows(x, pe, w1, b1, w2, b2, n_rows):
    b = x.shape[0]
    nc = (n_rows - CMP_BLOCK) // CMP_STRIDE + 1
    r = CMP_BLOCK // CMP_STRIDE
    nch = nc + r - 1
    chunks = x[:, :nch * CMP_STRIDE].reshape(b, nch, CMP_STRIDE, NSA_KV_HEADS, HEAD_DIM)
    part = jnp.einsum('bcjhd,rjdf->bcrhf', chunks, w1.reshape(r, CMP_STRIDE, HEAD_DIM, CMP_HIDDEN))
    h1 = jnp.einsum('ld,ldf->f', pe, w1) + b1
    for o in range(r):
        h1 = h1 + part[:, o:o + nc, o]
    return jnp.einsum('bchf,fd->bchd', jax.nn.gelu(h1), w2) + b2


def nsa_query_block(q, qpos, gates, kw, vw, kwpos, kc, vc, c_end, ks_b, vs_b, cover, rel_bias):
    b, nq = q.shape[:2]
    grp = NSA_HEADS // NSA_KV_HEADS
    qg = (q * HEAD_DIM ** -0.5).reshape(b, nq, NSA_KV_HEADS, grp, HEAD_DIM)
    rb = rel_bias.reshape(REL_BUCKETS, NSA_KV_HEADS, grp)
    hi = jnp.arange(NSA_KV_HEADS)[None, None, :, None]
    bi = jnp.arange(b)[:, None, None, None]
    dist_c = qpos[:, None] - c_end[None, :]
    bias_c = jnp.moveaxis(rb[rel_bucket(dist_c)], 1, -1)
    s_c = jnp.einsum('bqhgd,bchd->bqhgc', qg, kc) + bias_c
    p_c = masked_softmax(s_c, (dist_c >= 0)[:, None, None, :])
    o_c = jnp.einsum('bqhgc,bchd->bqhgd', p_c.astype(vc.dtype), vc)
    imp = jnp.einsum('bqhgc,cs->bqhs', p_c, cover)
    ns = ks_b.shape[1]
    blk = jnp.arange(ns)[None, :]
    cur = (qpos // SEL_BLOCK)[:, None]
    forced = ((blk == 0) | (blk == cur) | (blk == cur - 1))[:, None, :]
    valid = (blk * SEL_BLOCK <= qpos[:, None])[:, None, :]
    score = jnp.where(forced, FORCE_SCORE, jnp.where(valid, imp, -1.0))
    _, idx = lax.top_k(score, min(SEL_TOP_N, ns))
    k_s = ks_b[bi, idx, :, hi].reshape(b, nq, NSA_KV_HEADS, -1, HEAD_DIM)
    v_s = vs_b[bi, idx, :, hi].reshape(b, nq, NSA_KV_HEADS, -1, HEAD_DIM)
    s_pos = (idx[..., None] * SEL_BLOCK + jnp.arange(SEL_BLOCK)).reshape(b, nq, NSA_KV_HEADS, -1)
    dist_s = qpos[None, :, None, None] - s_pos
    bias_s = jnp.moveaxis(rb[rel_bucket(dist_s), hi], -1, 3)
    s_s = jnp.einsum('bqhgd,bqhkd->bqhgk', qg, k_s) + bias_s
    p_s = masked_softmax(s_s, (dist_s >= 0)[:, :, :, None, :])
    o_s = jnp.einsum('bqhgk,bqhkd->bqhgd', p_s.astype(v_s.dtype), v_s)
    dist_w = qpos[:, None] - kwpos[None, :]
    mask_w = (dist_w >= 0) & (dist_w < WINDOW) & (kwpos >= 0)[None, :]
    bias_w = jnp.moveaxis(rb[rel_bucket(dist_w)], 1, -1)
    s_w = jnp.einsum('bqhgd,bkhd->bqhgk', qg, kw) + bias_w
    p_w = masked_softmax(s_w, mask_w[:, None, None, :])
    o_w = jnp.einsum('bqhgk,bkhd->bqhgd', p_w.astype(vw.dtype), vw)
    g = gates.reshape(b, nq, NSA_KV_HEADS, grp, 3)
    o = g[..., 0:1] * o_c + g[..., 1:2] * o_s + g[..., 2:3] * o_w
    return o.reshape(b, nq, NSA_WIDTH)


def s5_mixer(u, h0_re, h0_im, lam_re, lam_im, log_dt, b_re, b_im, c_re, c_im, d_skip, w_glu, b_glu):
    bsz, t, _ = u.shape
    f32 = jnp.float32
    uf = u.astype(f32).reshape(bsz, t, SSM_GROUPS, SSM_CH)
    dt = jnp.exp(log_dt.astype(f32))[:, None]
    lr, li = lam_re.astype(f32), lam_im.astype(f32)
    mag = jnp.exp(lr * dt)
    a_re, a_im = mag * jnp.cos(li * dt), mag * jnp.sin(li * dt)
    den = lr * lr + li * li
    z_re = ((a_re - 1.0) * lr + a_im * li) / den
    z_im = (a_im * lr - (a_re - 1.0) * li) / den
    br, bim = b_re.astype(f32), b_im.astype(f32)
    bb_re = z_re[..., None] * br - z_im[..., None] * bim
    bb_im = z_re[..., None] * bim + z_im[..., None] * br
    x_re = jnp.einsum('btgc,gpc->btgp', uf, bb_re)
    x_im = jnp.einsum('btgc,gpc->btgp', uf, bb_im)
    ar = jnp.broadcast_to(a_re, x_re.shape)
    ai = jnp.broadcast_to(a_im, x_re.shape)

    def combine(e1, e2):
        a1r, a1i, b1r, b1i = e1
        a2r, a2i, b2r, b2i = e2
        return (a2r * a1r - a2i * a1i, a2r * a1i + a2i * a1r,
                a2r * b1r - a2i * b1i + b2r, a2r * b1i + a2i * b1r + b2i)

    pr, pim, sr, si = lax.associative_scan(combine, (ar, ai, x_re, x_im), axis=1)
    h0r = h0_re.astype(f32)[:, None]
    h0i = h0_im.astype(f32)[:, None]
    h_re = pr * h0r - pim * h0i + sr
    h_im = pr * h0i + pim * h0r + si
    y = jnp.einsum('btgp,gcp->btgc', h_re, c_re.astype(f32)) - jnp.einsum('btgp,gcp->btgc', h_im, c_im.astype(f32))
    y = jax.nn.gelu(y.reshape(bsz, t, SSM_WIDTH) + d_skip.astype(f32) * u.astype(f32))
    out = y * jax.nn.sigmoid(y @ w_glu.astype(f32) + b_glu.astype(f32))
    return out.astype(u.dtype), h_re[:, -1].astype(u.dtype), h_im[:, -1].astype(u.dtype)


def moe_ffn(x, router_w, router_b, w_gu, b_gu, w_down, b_down):
    n = x.shape[0]
    router_pad = jnp.pad(router_w, ((0, 0), (0, LANES - N_EXPERTS)))
    logits = pallas_matmul(x, router_pad)[:, :N_EXPERTS] + router_b.astype(jnp.float32)
    top_val, top_idx = lax.top_k(logits, TOP_K)
    gate = jax.nn.softmax(top_val, axis=-1)
    nk = n * TOP_K
    n_items = N_EXPERTS + nk // MOE_ROWS
    flat_e = top_idx.reshape(nk)
    onehot = (flat_e[:, None] == jnp.arange(N_EXPERTS, dtype=flat_e.dtype)[None, :]).astype(jnp.int32)
    running = jnp.cumsum(onehot, axis=0)
    counts = running[-1]
    pos_in_e = jnp.sum(onehot * running, axis=1) - 1
    items_e = (counts + MOE_ROWS - 1) // MOE_ROWS
    item_end = jnp.cumsum(items_e)
    item_start = item_end - items_e
    total = item_end[-1]
    dest = (item_start[flat_e] * MOE_ROWS + pos_in_e).astype(jnp.int32)
    row_tok = jnp.zeros(n_items * MOE_ROWS, jnp.int32).at[dest].set(jnp.arange(nk, dtype=jnp.int32) // TOP_K)
    item = jnp.minimum(jnp.arange(n_items, dtype=jnp.int32), total - 1)
    item_e = jnp.minimum(jnp.searchsorted(item_end, item, side='right'), N_EXPERTS - 1).astype(jnp.int32)
    rows = jnp.clip(counts[item_e] - (item - item_start[item_e]) * MOE_ROWS, 0, MOE_ROWS)
    item_sub = jnp.where(jnp.arange(n_items) < total, (rows + MOE_SUB - 1) // MOE_SUB, 0).astype(jnp.int32)
    half = x.shape[1] // 2
    bits = lax.bitcast_convert_type(x.astype(jnp.bfloat16), jnp.uint16).astype(jnp.uint32)
    xb = ((bits[:, :half] << 16) | bits[:, half:])[row_tok]
    item_ok = (jnp.arange(n_items) < total).astype(jnp.int32)
    yb = moe_experts(item_e, item_sub, item.astype(jnp.int32), item_ok, xb, w_gu, b_gu, w_down, b_down)
    dest_k = dest.reshape(n, TOP_K)
    out = gate[:, 0:1] * yb[dest_k[:, 0]]
    for k in range(1, TOP_K):
        out = out + gate[:, k:k + 1] * yb[dest_k[:, k]]
    return out


MOE_ROWS = 1280
MOE_SUB = 256
MOE_TF = 256


def _moe_body(e_ref, sub_ref, blk_ref, ok_ref, x_ref, wg_ref, wl_ref, bg_ref, bl_ref, wd_ref, bd_ref, o_ref, xs_ref):
    bf16 = jnp.bfloat16
    i, f = pl.program_id(0), pl.program_id(1)
    n_sub = sub_ref[i]
    wg = wg_ref[0].astype(bf16)
    wl = wl_ref[0].astype(bf16)
    wd = wd_ref[0].astype(bf16)

    @pl.when(f == 0)
    def _():
        def unpack_block(j, _):
            rows = pl.ds(pl.multiple_of(j * MOE_SUB, MOE_SUB), MOE_SUB)
            w = x_ref[rows, :]
            hi = lax.bitcast_convert_type(w & jnp.uint32(0xFFFF0000), jnp.float32)
            lo = lax.bitcast_convert_type(w << 16, jnp.float32)
            xs_ref[rows, :] = jnp.concatenate([hi, lo], axis=1).astype(bf16)
            return 0

        lax.fori_loop(0, n_sub, unpack_block, 0)

    def sub_block(j, _):
        rows = pl.ds(pl.multiple_of(j * MOE_SUB, MOE_SUB), MOE_SUB)
        x = xs_ref[rows, :]
        hg = jnp.dot(x, wg, preferred_element_type=jnp.float32) + bg_ref[0]
        hl = jnp.dot(x, wl, preferred_element_type=jnp.float32) + bl_ref[0]
        hg = jnp.minimum(hg, SWIGLU_LIMIT)
        hl = jnp.clip(hl, -SWIGLU_LIMIT, SWIGLU_LIMIT)
        act = hg * jax.nn.sigmoid(SWIGLU_ALPHA * hg) * (hl + 1.0)
        y = jnp.dot(act.astype(bf16), wd, preferred_element_type=jnp.float32)

        @pl.when(f == 0)
        def _():
            o_ref[rows, :] = y + bd_ref[0]

        @pl.when(f > 0)
        def _():
            o_ref[rows, :] += y

        return 0

    lax.fori_loop(0, n_sub, sub_block, 0)

    @pl.when(f == 0)
    def _():
        def zero_block(j, _):
            o_ref[pl.ds(pl.multiple_of(j * MOE_SUB, MOE_SUB), MOE_SUB), :] = jnp.zeros((MOE_SUB, o_ref.shape[1]), jnp.float32)
            return 0

        lax.fori_loop(n_sub, MOE_ROWS // MOE_SUB, zero_block, 0)


def moe_experts(item_e, item_sub, item_blk, item_ok, xb, w_gu, b_gu, w_down, b_down):
    n_items = item_e.shape[0]
    n_f = D_FF // MOE_TF
    d = 2 * xb.shape[1]

    def col(f, ok, i):
        return f * ok[i] + (n_f - 1) * (1 - ok[i])

    return pl.pallas_call(
        _moe_body,
        grid_spec=pltpu.PrefetchScalarGridSpec(
            num_scalar_prefetch=4,
            grid=(n_items, n_f),
            in_specs=[pl.BlockSpec((MOE_ROWS, d // 2), lambda i, f, e, s, b, ok: (b[i], 0)),
                      pl.BlockSpec((1, d, MOE_TF), lambda i, f, e, s, b, ok: (e[i], 0, col(f, ok, i))),
                      pl.BlockSpec((1, d, MOE_TF), lambda i, f, e, s, b, ok: (e[i], 0, n_f + col(f, ok, i))),
                      pl.BlockSpec((1, 1, MOE_TF), lambda i, f, e, s, b, ok: (e[i], 0, col(f, ok, i))),
                      pl.BlockSpec((1, 1, MOE_TF), lambda i, f, e, s, b, ok: (e[i], 0, n_f + col(f, ok, i))),
                      pl.BlockSpec((1, MOE_TF, d), lambda i, f, e, s, b, ok: (e[i], col(f, ok, i), 0)),
                      pl.BlockSpec((1, 1, d), lambda i, f, e, s, b, ok: (e[i], 0, 0))],
            out_specs=pl.BlockSpec((MOE_ROWS, d), lambda i, f, e, s, b, ok: (i, 0)),
            scratch_shapes=[pltpu.VMEM((MOE_ROWS, d), jnp.bfloat16)],
        ),
        out_shape=jax.ShapeDtypeStruct((n_items * MOE_ROWS, d), jnp.float32),
        compiler_params=pltpu.CompilerParams(vmem_limit_bytes=VMEM_LIMIT),
        name="moe_experts",
    )(item_e, item_sub, item_blk, item_ok, xb, w_gu, w_gu, b_gu[:, None, :], b_gu[:, None, :], w_down, b_down[:, None, :])


def layer_forward(x, pos0, past_kv, win_buf, h0_re, h0_im, rel_bias, lw):
    b, t, _ = x.shape
    h = rms_norm(x, lw['norm_mix'])
    proj = h @ lw['w_in']
    o1 = NSA_WIDTH
    o2 = o1 + N_KV * KV_WIDTH
    o3 = o2 + 3 * NSA_HEADS
    q = proj[..., :o1].reshape(b, t, NSA_HEADS, HEAD_DIM)
    kv_new = proj[..., o1:o2].reshape(b, t, N_KV, NSA_KV_HEADS, HEAD_DIM)
    gates = jax.nn.sigmoid(proj[..., o2:o3].astype(jnp.float32)).reshape(b, t, NSA_HEADS, 3)
    u = proj[..., o3:]
    paged_new = kv_new[:, :, :N_PAGED]
    win_new = kv_new[:, :, N_PAGED:]
    qpos = pos0 + jnp.arange(t, dtype=jnp.int32)

    n_rows = pos0 + t
    ns = -(-n_rows // SEL_BLOCK)
    pad = jnp.zeros((b, ns * SEL_BLOCK - n_rows) + paged_new.shape[2:], paged_new.dtype)
    rows = jnp.concatenate(([] if past_kv is None else [past_kv]) + [paged_new, pad], axis=1)
    kc = compress_rows(rows[:, :, 0], lw['cmp_pe'][0], lw['cmp_w1'][0], lw['cmp_b1'][0], lw['cmp_w2'][0], lw['cmp_b2'][0], n_rows)
    vc = compress_rows(rows[:, :, 1], lw['cmp_pe'][1], lw['cmp_w1'][1], lw['cmp_b1'][1], lw['cmp_w2'][1], lw['cmp_b2'][1], n_rows)
    nc = kc.shape[1]
    c_start = jnp.arange(nc) * CMP_STRIDE
    c_end = c_start + CMP_BLOCK - 1
    s_start = jnp.arange(ns) * SEL_BLOCK
    cover = ((c_start[:, None] < s_start[None, :] + SEL_BLOCK) & (c_start[:, None] + CMP_BLOCK > s_start[None, :])).astype(jnp.float32)
    ks_b = rows[:, :, 2].reshape(b, ns, SEL_BLOCK, NSA_KV_HEADS, HEAD_DIM)
    vs_b = rows[:, :, 3].reshape(b, ns, SEL_BLOCK, NSA_KV_HEADS, HEAD_DIM)

    if win_buf is None:
        new_win = win_new[:, t - min(WINDOW, n_rows):]
        tr = lambda a: jnp.transpose(a, (0, 2, 1, 3))
        pad_c = lambda a: tr(jnp.pad(a, ((0, 0), (0, LANES - nc), (0, 0), (0, 0))))
        g4 = tr(proj[..., o2:o3].reshape(b, t, NSA_HEADS, 3))
        attn4 = nsa_prompt(tr(q), g4, pad_c(kc), pad_c(vc), tr(kv_new[:, :, 2]), tr(kv_new[:, :, 3]),
                           tr(kv_new[:, :, 4]), tr(kv_new[:, :, 5]), bucket_thresholds(), rel_bias, nc)
        attn = tr(attn4).reshape(b, t, NSA_WIDTH).astype(x.dtype)
        return _layer_tail(x, attn, u, h0_re, h0_im, lw, paged_new, new_win, True)
    if win_buf is None:
        qb = min(Q_BLOCK, t)
        nb = t // qb
        wpad = jnp.concatenate([jnp.zeros((b, WINDOW) + win_new.shape[2:], win_new.dtype), win_new], axis=1)
        widx = jnp.arange(nb)[:, None] * qb + jnp.arange(qb + WINDOW)[None, :]
        w_blocks = jnp.moveaxis(wpad[:, widx], 1, 0)
        w_pos = widx - WINDOW
        w_hist = win_new
    else:
        qb, nb = t, 1
        w_hist = jnp.concatenate([win_buf, win_new], axis=1)
        w_blocks = w_hist[None]
        w_pos = (pos0 - win_buf.shape[1] + jnp.arange(w_hist.shape[1], dtype=jnp.int32))[None]
    new_win = w_hist[:, w_hist.shape[1] - min(WINDOW, n_rows):]

    def run(args):
        q_b, qpos_b, g_b, w_b, wpos_b = args
        return nsa_query_block(q_b, qpos_b, g_b, w_b[:, :, 0], w_b[:, :, 1], wpos_b,
                               kc, vc, c_end, ks_b, vs_b, cover, rel_bias)

    q_blocks = jnp.moveaxis(q.reshape(b, nb, qb, NSA_HEADS, HEAD_DIM), 1, 0)
    g_blocks = jnp.moveaxis(gates.reshape(b, nb, qb, NSA_HEADS, 3), 1, 0)
    attn = lax.map(run, (q_blocks, qpos.reshape(nb, qb), g_blocks, w_blocks, w_pos))
    attn = jnp.moveaxis(attn, 0, 1).reshape(b, t, NSA_WIDTH).astype(x.dtype)
    return _layer_tail(x, attn, u, h0_re, h0_im, lw, paged_new, new_win, False)


ROW_TILE = 256
PROJ_TILE = 512


def _rms(v, g):
    return v * lax.rsqrt(jnp.mean(v * v, axis=-1, keepdims=True) + RMS_EPS) * g


def _norm_proj_body(x_ref, g_ref, w_ref, o_ref, h_ref):
    @pl.when(pl.program_id(1) == 0)
    def _():
        h_ref[...] = _rms(x_ref[...], g_ref[...]).astype(jnp.bfloat16)

    o_ref[...] = jnp.dot(h_ref[...], w_ref[...], preferred_element_type=jnp.float32)


def norm_proj(x, gain, w):
    n, d = x.shape
    cols = w.shape[1]
    return pl.pallas_call(
        _norm_proj_body,
        grid=(n // ROW_TILE, cols // PROJ_TILE),
        in_specs=[pl.BlockSpec((ROW_TILE, d), lambda i, j: (i, 0)),
                  pl.BlockSpec((1, d), lambda i, j: (0, 0)),
                  pl.BlockSpec((d, PROJ_TILE), lambda i, j: (0, j))],
        out_specs=pl.BlockSpec((ROW_TILE, PROJ_TILE), lambda i, j: (i, j)),
        out_shape=jax.ShapeDtypeStruct((n, cols), jnp.float32),
        scratch_shapes=[pltpu.VMEM((ROW_TILE, d), jnp.bfloat16)],
        name="norm_proj",
    )(x, gain.reshape(1, d), w)


def _mix_out_body(attn_ref, y_ref, x_ref, wglu_ref, bglu_ref, ga_ref, gs_ref, wout_ref, gf_ref, wr_ref,
                  x2_ref, hp_ref, lg_ref):
    f32, bf16 = jnp.float32, jnp.bfloat16
    y = y_ref[...]
    ssm = y * jax.nn.sigmoid(jnp.dot(y.astype(bf16), wglu_ref[...], preferred_element_type=f32) + bglu_ref[...])
    merged = jnp.concatenate([_rms(attn_ref[...], ga_ref[...]), _rms(ssm, gs_ref[...])], axis=1).astype(bf16)
    x2 = x_ref[...] + jnp.dot(merged, wout_ref[...], preferred_element_type=f32)
    x2_ref[...] = x2
    hm = _rms(x2, gf_ref[...]).astype(bf16)
    lg_ref[...] = jnp.dot(hm, wr_ref[...], preferred_element_type=f32)
    bits = lax.bitcast_convert_type(hm.astype(f32), jnp.uint32)
    half = bits.shape[1] // 2
    hp_ref[...] = bits[:, :half] | (bits[:, half:] >> 16)


def mix_out(attn, y, x, lw):
    n = x.shape[0]
    f32, bf16 = jnp.float32, jnp.bfloat16
    row = lambda a: a.astype(f32).reshape(1, -1)
    wr = jnp.pad(lw['router_w'], ((0, 0), (0, LANES - N_EXPERTS))).astype(bf16)
    consts = [lw['w_glu'].astype(bf16), row(lw['b_glu']), row(lw['norm_attn_out']), row(lw['norm_ssm_out']),
              lw['w_out'].astype(bf16), row(lw['norm_ffn']), wr]
    tile = lambda c: pl.BlockSpec((ROW_TILE, c), lambda i: (i, 0))
    full = lambda a: pl.BlockSpec(a.shape, lambda i: (0, 0))
    return pl.pallas_call(
        _mix_out_body,
        grid=(n // ROW_TILE,),
        in_specs=[tile(NSA_WIDTH), tile(SSM_WIDTH), tile(D_MODEL)] + [full(c) for c in consts],
        out_specs=[tile(D_MODEL), tile(D_MODEL // 2), tile(LANES)],
        out_shape=[jax.ShapeDtypeStruct((n, D_MODEL), f32), jax.ShapeDtypeStruct((n, D_MODEL // 2), jnp.uint32),
                   jax.ShapeDtypeStruct((n, LANES), f32)],
        compiler_params=pltpu.CompilerParams(vmem_limit_bytes=VMEM_LIMIT),
        name="mix_out",
    )(attn, y, x, *consts)


def _combine_body(x_ref, y0_ref, y1_ref, y2_ref, y3_ref, gate_ref, g_ref, o_ref, *, normed):
    gate = gate_ref[...]
    moe = gate[:, 0:1] * y0_ref[...]
    for k, y_ref in enumerate((y1_ref, y2_ref, y3_ref), start=1):
        moe = moe + gate[:, k:k + 1] * y_ref[...]
    out = x_ref[...] + moe
    o_ref[...] = _rms(out, g_ref[...]) if normed else out


def combine_norm(x, ys, gate, gain):
    n, d = x.shape
    normed = gain is not None
    gain = jnp.ones((d,), jnp.float32) if gain is None else gain
    tile = lambda c: pl.BlockSpec((ROW_TILE, c), lambda i: (i, 0))
    return pl.pallas_call(
        functools.partial(_combine_body, normed=normed),
        grid=(n // ROW_TILE,),
        in_specs=[tile(d)] * (1 + TOP_K) + [tile(TOP_K), pl.BlockSpec((1, d), lambda i: (0, 0))],
        out_specs=tile(d),
        out_shape=jax.ShapeDtypeStruct((n, d), jnp.float32),
        name="combine_norm",
    )(x, *ys, gate, gain.astype(jnp.float32).reshape(1, d))


def moe_routed(hp, logits, router_b, w_gu, b_gu, w_down, b_down):
    n = hp.shape[0]
    top_val, top_idx = lax.top_k(logits[:, :N_EXPERTS] + router_b.astype(jnp.float32), TOP_K)
    gate = jax.nn.softmax(top_val, axis=-1)
    nk = n * TOP_K
    n_items = N_EXPERTS + nk // MOE_ROWS
    flat_e = top_idx.reshape(nk)
    onehot = (flat_e[:, None] == jnp.arange(N_EXPERTS, dtype=flat_e.dtype)[None, :]).astype(jnp.int32)
    running = jnp.cumsum(onehot, axis=0)
    counts = running[-1]
    pos_in_e = jnp.sum(onehot * running, axis=1) - 1
    items_e = (counts + MOE_ROWS - 1) // MOE_ROWS
    item_end = jnp.cumsum(items_e)
    item_start = item_end - items_e
    total = item_end[-1]
    dest = (item_start[flat_e] * MOE_ROWS + pos_in_e).astype(jnp.int32)
    row_tok = jnp.zeros(n_items * MOE_ROWS, jnp.int32).at[dest].set(jnp.arange(nk, dtype=jnp.int32) // TOP_K)
    item = jnp.minimum(jnp.arange(n_items, dtype=jnp.int32), total - 1)
    item_e = jnp.minimum(jnp.searchsorted(item_end, item, side='right'), N_EXPERTS - 1).astype(jnp.int32)
    rows = jnp.clip(counts[item_e] - (item - item_start[item_e]) * MOE_ROWS, 0, MOE_ROWS)
    item_sub = jnp.where(jnp.arange(n_items) < total, (rows + MOE_SUB - 1) // MOE_SUB, 0).astype(jnp.int32)
    item_ok = (jnp.arange(n_items) < total).astype(jnp.int32)
    yb = moe_experts(item_e, item_sub, item.astype(jnp.int32), item_ok, hp[row_tok], w_gu, b_gu, w_down, b_down)
    dest_k = dest.reshape(n, TOP_K)
    return [yb[dest_k[:, k]] for k in range(TOP_K)], gate


def mixer_layer(x, pos0, pool, page_table, win_buf, h0_re, h0_im, rel_bias, lw):
    b, t, _ = x.shape
    w_in = jnp.pad(lw['w_in'], ((0, 0), (0, -IN_WIDTH % PROJ_TILE))).astype(jnp.bfloat16)
    proj = norm_proj(x.reshape(b * t, D_MODEL), lw['norm_mix'].astype(jnp.float32), w_in).reshape(b, t, -1)
    o1 = NSA_WIDTH
    o2 = o1 + N_KV * KV_WIDTH
    o3 = o2 + 3 * NSA_HEADS
    q = proj[..., :o1].reshape(b, t, NSA_HEADS, HEAD_DIM)
    kv_new = proj[..., o1:o2].reshape(b, t, N_KV, NSA_KV_HEADS, HEAD_DIM)
    u = proj[..., o3:IN_WIDTH]
    paged_new = kv_new[:, :, :N_PAGED]
    win_new = kv_new[:, :, N_PAGED:]
    n_rows = pos0 + t
    n_cmp =(n_rows - CMP_BLOCK) // CMP_STRIDE + 1
    thr = bucket_thresholds()
    cmp_w = (lw['cmp_pe'], lw['cmp_w1'], lw['cmp_b1'], lw['cmp_w2'], lw['cmp_b2'])
    if pool is None:
        assert pos0 == 0 and t % 128 == 0
        pages = t // 128
        own_pool = paged_new.reshape(b * pages, 128, ROW_LANES)
        own_table = jnp.arange(b * pages, dtype=jnp.int32).reshape(b, pages)
        kcvc = compress_pages(own_pool, own_table, *cmp_w, n_g=pages, transposed=False)
        attn = nsa_prefill(q, proj[..., o2:o3], kcvc, kv_new, thr, rel_bias, n_cmp).astype(x.dtype)
        w_hist = win_new
    else:
        kcvc = compress_pages(pool, page_table, *cmp_w, n_g=32, transposed=True)
        attn = nsa_decode(q, proj[..., o2:o3], kv_new, kcvc, pool, page_table, win_buf, thr, rel_bias, pos0).astype(x.dtype)
        w_hist = jnp.concatenate([win_buf, win_new], axis=1)
    new_win = w_hist[:, w_hist.shape[1] - min(WINDOW, n_rows):]
    y, h_re, h_im = s5_branch(u, h0_re, h0_im, lw, pool is None)
    x2, hp, logits = mix_out(attn.reshape(b * t, NSA_WIDTH), y.reshape(b * t, SSM_WIDTH), x.reshape(b * t, D_MODEL), lw)
    return (x2, hp, logits), paged_new, new_win, h_re, h_im


def s5_branch(u, h0_re, h0_im, lw, chained):
    bsz, t, _ = u.shape
    f32 = jnp.float32
    a, bcat, ccat = s5_discretize(lw['lam_re'], lw['lam_im'], lw['log_dt'], lw['b_re'], lw['b_im'], lw['c_re'], lw['c_im'])
    d_skip = lw['d_skip'].astype(f32)
    flat = lambda h: h.astype(f32).reshape(bsz, SSM_GROUPS * SSM_STATE)
    if chained:
        n_r = t // S5_CHUNK
        u4 = jnp.transpose(u.astype(f32).reshape(bsz, n_r, S5_CHUNK, SSM_WIDTH), (0, 2, 1, 3))
        y4, h_re, h_im = s5_scan(u4, flat(h0_re)[:, None], flat(h0_im)[:, None], a, bcat, ccat, d_skip, True)
        y = jnp.transpose(y4, (0, 2, 1, 3)).reshape(bsz, t, SSM_WIDTH)
        h_re, h_im = h_re[:, 0], h_im[:, 0]
    else:
        u4 = jnp.transpose(u.astype(f32), (1, 0, 2))[None]
        y4, h_re, h_im = s5_scan(u4, flat(h0_re)[None], flat(h0_im)[None], a, bcat, ccat, d_skip, False)
        y = jnp.transpose(y4[0], (1, 0, 2))
        h_re, h_im = h_re[0], h_im[0]
    st = lambda h: h.reshape(bsz, SSM_GROUPS, SSM_STATE).astype(u.dtype)
    return y, st(h_re), st(h_im)


def moe_residual(groups, lw, final_gain):
    sizes = [g[0].shape[0] for g in groups]
    hp = jnp.concatenate([g[1] for g in groups], axis=0)
    logits = jnp.concatenate([g[2] for g in groups], axis=0)
    ys, gate = moe_routed(hp, logits, lw['router_b'], lw['w_gu'], lw['b_gu'], lw['w_down'], lw['b_down'])
    outs, start = [], 0
    for (x2, _, _), n in zip(groups, sizes):
        rows = slice(start, start + n)
        outs.append(combine_norm(x2, [y[rows] for y in ys], gate[rows], final_gain))
        start += n
    return outs


def kernel(x_prompt, x_sample, cache_nsa_kv, cache_win_kv, state_ssm_re, state_ssm_im, page_table, rel_bias,
           norm_mix, w_in, cmp_pe, cmp_w1, cmp_b1, cmp_w2, cmp_b2, ssm_lam_re, ssm_lam_im, ssm_log_dt,
           ssm_b_re, ssm_b_im, ssm_c_re, ssm_c_im, ssm_d, ssm_w_glu, ssm_b_glu, norm_attn_out, norm_ssm_out,
           w_out, norm_ffn, router_w, router_b, w_gu, b_gu, w_down, b_down, norm_final):
    n_seq, n_pages = page_table.shape
    past_len = n_pages * cache_nsa_kv.shape[2]
    xp, xs = x_prompt, x_sample
    kv_p, win_p, sre_p, sim_p = [], [], [], []
    kv_s, win_s, sre_s, sim_s = [], [], [], []
    for i in range(DEPTH):
        lw = dict(norm_mix=norm_mix[i], w_in=w_in[i], cmp_pe=cmp_pe[i], cmp_w1=cmp_w1[i], cmp_b1=cmp_b1[i],
                  cmp_w2=cmp_w2[i], cmp_b2=cmp_b2[i], lam_re=ssm_lam_re[i], lam_im=ssm_lam_im[i],
                  log_dt=ssm_log_dt[i], b_re=ssm_b_re[i], b_im=ssm_b_im[i], c_re=ssm_c_re[i], c_im=ssm_c_im[i],
                  d_skip=ssm_d[i], w_glu=ssm_w_glu[i], b_glu=ssm_b_glu[i], norm_attn_out=norm_attn_out[i],
                  norm_ssm_out=norm_ssm_out[i], w_out=w_out[i], norm_ffn=norm_ffn[i], router_w=router_w[i],
                  router_b=router_b[i], w_gu=w_gu[i], b_gu=b_gu[i], w_down=w_down[i], b_down=b_down[i])
        h0 = jnp.zeros((xp.shape[0], SSM_GROUPS, SSM_STATE), xp.dtype)
        gp, kv1, w1, r1, m1 = mixer_layer(xp, 0, None, None, None, h0, h0, rel_bias, lw)
        pool = jnp.transpose(cache_nsa_kv[i].reshape(cache_nsa_kv.shape[1], cache_nsa_kv.shape[2], ROW_LANES), (0, 2, 1))
        gs, kv2, w2, r2, m2 = mixer_layer(xs, past_len, pool, page_table, cache_win_kv[i], state_ssm_re[i], state_ssm_im[i], rel_bias, lw)
        op, os_ = moe_residual([gp, gs], lw, norm_final if i == DEPTH - 1 else None)
        xp, xs = op.reshape(xp.shape), os_.reshape(xs.shape)
        kv_p.append(kv1); win_p.append(w1); sre_p.append(r1); sim_p.append(m1)
        kv_s.append(kv2); win_s.append(w2); sre_s.append(r2); sim_s.append(m2)
    return (xp, xs, jnp.stack(kv_p), jnp.stack(win_p), jnp.stack(sre_p), jnp.stack(sim_p),
            jnp.stack(kv_s), jnp.stack(win_s), jnp.stack(sre_s), jnp.stack(sim_s))
```

```python
import functools
import math
import jax, jax.numpy as jnp
from jax import lax
from jax.experimental import pallas as pl
from jax.experimental.pallas import tpu as pltpu

D_MODEL = 2048
DEPTH = 1
NSA_HEADS = 16
NSA_KV_HEADS = 2
HEAD_DIM = 64
NSA_WIDTH = NSA_HEADS * HEAD_DIM
KV_WIDTH = NSA_KV_HEADS * HEAD_DIM
N_PAGED = 4
N_KV = 6
SSM_WIDTH = D_MODEL - NSA_WIDTH
SSM_CH = 16
SSM_GROUPS = SSM_WIDTH // SSM_CH
SSM_STATE = 64
IN_WIDTH = NSA_WIDTH + N_KV * KV_WIDTH + 3 * NSA_HEADS + SSM_WIDTH
CMP_BLOCK = 32
CMP_STRIDE = 16
CMP_HIDDEN = 2 * HEAD_DIM
SEL_BLOCK = 64
SEL_TOP_N = 16
WINDOW = 512
Q_BLOCK = 64
FORCE_SCORE = 1e6
REL_BUCKETS = 32
REL_MAX_DIST = 4096
N_EXPERTS = 32
TOP_K = 4
D_FF = D_MODEL
SWIGLU_LIMIT = 7.0
SWIGLU_ALPHA = 1.702
RMS_EPS = 1e-5
NEG_INF = -1e30


def _mm_body(x_ref, w_ref, o_ref):
    o_ref[...] = jnp.dot(x_ref[...].astype(jnp.bfloat16), w_ref[...].astype(jnp.bfloat16),
                         preferred_element_type=jnp.float32)


def pallas_matmul(x, w, tm=256, tn=512):
    m, k = x.shape
    n = w.shape[1]
    tm = min(tm, m)
    tn = min(tn, n)
    return pl.pallas_call(
        _mm_body,
        grid=(m // tm, n // tn),
        in_specs=[pl.BlockSpec((tm, k), lambda i, j: (i, 0)),
                  pl.BlockSpec((k, tn), lambda i, j: (0, j))],
        out_specs=pl.BlockSpec((tm, tn), lambda i, j: (i, j)),
        out_shape=jax.ShapeDtypeStruct((m, n), jnp.float32),
        name="matmul",
    )(x, w)


GRP = NSA_HEADS // NSA_KV_HEADS
TQ = 128
TK = 128
LANES = 128
BUCKET_TABLE_LEN = 32768
BIAS_TILE_ELEMS = 16384
VMEM_LIMIT = 56 * 1024 * 1024


def rel_bucket(dist):
    n = jnp.maximum(dist, 0)
    exact = REL_BUCKETS // 2
    nf = jnp.maximum(n, 1).astype(jnp.float32)
    large = exact + (jnp.log(nf / exact) / math.log(REL_MAX_DIST / exact) * (REL_BUCKETS - exact)).astype(jnp.int32)
    return jnp.where(n < exact, n, jnp.minimum(large, REL_BUCKETS - 1))


def bucket_thresholds():
    tab = rel_bucket(jnp.arange(BUCKET_TABLE_LEN, dtype=jnp.int32))
    return jnp.sum(tab[None, :] < jnp.arange(REL_BUCKETS, dtype=jnp.int32)[:, None], axis=1).astype(jnp.int32)


def _bias_table_body(thr_ref, rb_ref, d_ref, o_ref):
    n = jnp.maximum(d_ref[0], 0)
    for h in range(NSA_HEADS):
        val = jnp.full(n.shape, rb_ref[h], jnp.float32)
        for k in range(1, REL_BUCKETS):
            val = jnp.where(n >= thr_ref[k], rb_ref[k * NSA_HEADS + h], val)
        o_ref[h // GRP, 0, h % GRP] = val


def bias_tables(dist, thr, rel_bias):
    n, r, c = dist.shape
    ct = min(c, BIAS_TILE_ELEMS // r)
    assert c % ct == 0
    return pl.pallas_call(
        _bias_table_body,
        grid_spec=pltpu.PrefetchScalarGridSpec(
            num_scalar_prefetch=2,
            grid=(n, c // ct),
            in_specs=[pl.BlockSpec((1, r, ct), lambda i, j, *_: (i, 0, j))],
            out_specs=pl.BlockSpec((NSA_KV_HEADS, 1, GRP, r, ct), lambda i, j, *_: (0, i, 0, 0, j)),
        ),
        out_shape=jax.ShapeDtypeStruct((NSA_KV_HEADS, n, GRP, r, c), jnp.float32),
        name="bias_tables",
    )(thr, rel_bias.reshape(-1), dist)


def _softmax_tile(s3, msk, m, l, acc, v):
    s3 = jnp.where(msk[None], s3, NEG_INF)
    m_new = jnp.maximum(m, jnp.max(s3, axis=-1, keepdims=True))
    p = jnp.where(msk[None], jnp.exp(s3 - m_new), 0.0)
    alpha = jnp.exp(m - m_new)
    l = alpha * l + jnp.sum(p, axis=-1, keepdims=True)
    pv = jnp.dot(p.reshape(GRP * TQ, p.shape[-1]).astype(jnp.bfloat16), v, preferred_element_type=jnp.float32)
    acc = alpha * acc + pv.reshape(GRP, TQ, HEAD_DIM)
    return m_new, l, acc


def _nsa_prompt_body(q_ref, g_ref, kc_ref, vc_ref, ks_ref, vs_ref, kw_ref, vw_ref, bt_ref, bc_ref, cov_ref, exp_ref,
                     o_ref, mask_ref, *, n_cmp, n_sel):
    f32, bf16 = jnp.float32, jnp.bfloat16
    qi = pl.program_id(2)
    nt_dims = (((1,), (1,)), ((), ()))
    q = (q_ref[0] * (HEAD_DIM ** -0.5)).reshape(GRP * TQ, HEAD_DIM).astype(bf16)
    qpos = qi * TQ + lax.broadcasted_iota(jnp.int32, (TQ, LANES), 0)
    lane = lax.broadcasted_iota(jnp.int32, (TQ, LANES), 1)

    ok_c = (qpos >= CMP_STRIDE * lane + (CMP_BLOCK - 1)) & (lane < n_cmp)
    s_c = lax.dot_general(q, kc_ref[0, 0].astype(bf16), nt_dims, preferred_element_type=f32) + bc_ref[0, 0].reshape(GRP * TQ, LANES)
    s_c = jnp.where(ok_c[None], s_c.reshape(GRP, TQ, LANES), NEG_INF)
    m_c = jnp.max(s_c, axis=-1, keepdims=True)
    e_c = jnp.where(ok_c[None], jnp.exp(s_c - m_c), 0.0)
    p_c = e_c / jnp.maximum(jnp.sum(e_c, axis=-1, keepdims=True), 1e-30)
    o_c = jnp.dot(p_c.reshape(GRP * TQ, LANES).astype(bf16), vc_ref[0, 0].astype(bf16),
                  preferred_element_type=f32).reshape(GRP, TQ, HEAD_DIM)

    imp = jnp.dot(jnp.sum(p_c, axis=0).astype(bf16), cov_ref[...], preferred_element_type=f32)
    cur = qpos // SEL_BLOCK
    forced = (lane == 0) | (lane == cur) | (lane == cur - 1)
    score = jnp.where(forced, FORCE_SCORE, jnp.where(lane * SEL_BLOCK <= qpos, imp, -1.0))
    score = jnp.where(lane < n_sel, score, -jnp.inf)
    sel = jnp.zeros((TQ, LANES), f32)
    for _ in range(min(SEL_TOP_N, n_sel)):
        best = jnp.max(score, axis=-1, keepdims=True)
        idx = jnp.min(jnp.where(score == best, lane, LANES), axis=-1, keepdims=True)
        hit = lane == idx
        sel = jnp.where(hit, 1.0, sel)
        score = jnp.where(hit, -jnp.inf, score)
    mask_ref[...] = jnp.dot(sel.astype(bf16), exp_ref[...], preferred_element_type=f32)

    row = lax.broadcasted_iota(jnp.int32, (TQ, TK), 0)
    col = lax.broadcasted_iota(jnp.int32, (TQ, TK), 1)
    init = (jnp.full((GRP, TQ, 1), NEG_INF, f32), jnp.zeros((GRP, TQ, 1), f32), jnp.zeros((GRP, TQ, HEAD_DIM), f32))

    def scores(k_ref, ki):
        k = k_ref[0, 0, pl.ds(pl.multiple_of(ki * TK, TK), TK), :].astype(bf16)
        s = lax.dot_general(q, k, nt_dims, preferred_element_type=f32)
        return s.reshape(GRP, TQ, TK) + bt_ref[0, qi - ki]

    def sel_step(ki, carry):
        dist = (qi - ki) * TQ + row - col
        msk = (mask_ref[:, pl.ds(pl.multiple_of(ki * TK, TK), TK)] > 0.5) & (dist >= 0)
        v = vs_ref[0, 0, pl.ds(pl.multiple_of(ki * TK, TK), TK), :].astype(bf16)
        return _softmax_tile(scores(ks_ref, ki), msk, *carry, v)

    def win_step(ki, carry):
        dist = (qi - ki) * TQ + row - col
        msk = (dist >= 0) & (dist < WINDOW)
        v = vw_ref[0, 0, pl.ds(pl.multiple_of(ki * TK, TK), TK), :].astype(bf16)
        return _softmax_tile(scores(kw_ref, ki), msk, *carry, v)

    _, l_s, acc_s = lax.fori_loop(0, qi + 1, sel_step, init)
    _, l_w, acc_w = lax.fori_loop(jnp.maximum(qi - WINDOW // TK, 0), qi + 1, win_step, init)
    o_s = acc_s / jnp.maximum(l_s, 1e-30)
    o_w = acc_w / jnp.maximum(l_w, 1e-30)
    g = jax.nn.sigmoid(g_ref[0])
    o_ref[0] = g[:, :, 0:1] * o_c + g[:, :, 1:2] * o_s + g[:, :, 2:3] * o_w


def nsa_prompt(q4, g4, kc, vc, ks, vs, kw, vw, thr, rel_bias, n_cmp):
    b, _, t, _ = q4.shape
    nq = t // TQ
    n_sel = t // SEL_BLOCK
    assert t % TQ == 0 and n_sel <= LANES and n_cmp <= LANES and TQ == TK
    i = jnp.arange(TQ, dtype=jnp.int32)
    d_toep = jnp.arange(nq, dtype=jnp.int32)[:, None, None] * TQ + i[:, None] - i[None, :]
    d_cmp = (jnp.arange(t, dtype=jnp.int32)[:, None] - (CMP_STRIDE * jnp.arange(LANES, dtype=jnp.int32)[None, :] + CMP_BLOCK - 1)).reshape(nq, TQ, LANES)
    tabs = bias_tables(jnp.concatenate([d_toep, d_cmp], axis=0), thr, rel_bias)
    bt, bc = tabs[:, :nq], tabs[:, nq:]
    c_start = CMP_STRIDE * jnp.arange(LANES)[:, None]
    s_start = SEL_BLOCK * jnp.arange(LANES)[None, :]
    cover = ((c_start < s_start + SEL_BLOCK) & (c_start + CMP_BLOCK > s_start)
             & (jnp.arange(LANES)[:, None] < n_cmp) & (jnp.arange(LANES)[None, :] < n_sel)).astype(jnp.bfloat16)
    expand = (jnp.arange(t)[None, :] // SEL_BLOCK == jnp.arange(LANES)[:, None]).astype(jnp.bfloat16)
    kv_spec = lambda rows: pl.BlockSpec((1, 1, rows, HEAD_DIM), lambda h, bb, qq: (bb, h, 0, 0))
    return pl.pallas_call(
        functools.partial(_nsa_prompt_body, n_cmp=n_cmp, n_sel=n_sel),
        grid=(NSA_KV_HEADS, b, nq),
        in_specs=[pl.BlockSpec((1, GRP, TQ, HEAD_DIM), lambda h, bb, qq: (bb, h, qq, 0)),
                  pl.BlockSpec((1, GRP, TQ, 3), lambda h, bb, qq: (bb, h, qq, 0)),
                  kv_spec(LANES), kv_spec(LANES), kv_spec(t), kv_spec(t), kv_spec(t), kv_spec(t),
                  pl.BlockSpec((1, nq, GRP, TQ, TK), lambda h, bb, qq: (h, 0, 0, 0, 0)),
                  pl.BlockSpec((1, 1, GRP, TQ, LANES), lambda h, bb, qq: (h, qq, 0, 0, 0)),
                  pl.BlockSpec((LANES, LANES), lambda h, bb, qq: (0, 0)),
                  pl.BlockSpec((LANES, t), lambda h, bb, qq: (0, 0))],
        out_specs=pl.BlockSpec((1, GRP, TQ, HEAD_DIM), lambda h, bb, qq: (bb, h, qq, 0)),
        out_shape=jax.ShapeDtypeStruct(q4.shape, jnp.float32),
        scratch_shapes=[pltpu.VMEM((TQ, t), jnp.float32)],
        compiler_params=pltpu.CompilerParams(vmem_limit_bytes=VMEM_LIMIT),
        name="nsa_prompt",
    )(q4, g4, kc, vc, ks, vs, kw, vw, bt, bc, cover, expand)


def _nsa_prefill_body(q_ref, g_ref, kc_ref, vct_ref, ks_ref, vst_ref, kw_ref, vwt_ref, bt_ref, bc_ref, cov_ref, exp_ref,
                      o_ref, mask_ref, *, n_cmp, n_sel):
    f32, bf16 = jnp.float32, jnp.bfloat16
    qi = pl.program_id(2)
    nt_dims = (((1,), (1,)), ((), ()))
    q = (q_ref[0] * (HEAD_DIM ** -0.5)).reshape(GRP * TQ, HEAD_DIM).astype(bf16)
    lanes = [slice(g * TQ, (g + 1) * TQ) for g in range(GRP)]
    row = lax.broadcasted_iota(jnp.int32, (LANES, TQ), 0)
    qpos = qi * TQ + lax.broadcasted_iota(jnp.int32, (LANES, TQ), 1)

    ok_c = (qpos >= CMP_STRIDE * row + (CMP_BLOCK - 1)) & (row < n_cmp)
    s_c = lax.dot_general(kc_ref[0, 0].astype(bf16), q, nt_dims, preferred_element_type=f32)
    p_sum = jnp.zeros((LANES, TQ), f32)
    p_parts = []
    for g in range(GRP):
        p = _masked_softmax_rows(s_c[:, lanes[g]] + bc_ref[0, 0, g], ok_c)
        p_sum = p_sum + p
        p_parts.append(p.astype(bf16))
    o_c = jnp.dot(vct_ref[0, 0].astype(bf16), jnp.concatenate(p_parts, axis=1), preferred_element_type=f32)

    imp = jnp.dot(cov_ref[...], p_sum.astype(bf16), preferred_element_type=f32)
    rs = -(-n_sel // 8) * 8
    blk = lax.broadcasted_iota(jnp.int32, (rs, TQ), 0)
    qp = qi * TQ + lax.broadcasted_iota(jnp.int32, (rs, TQ), 1)
    cur = qp // SEL_BLOCK
    forced = (blk == 0) | (blk == cur) | (blk == cur - 1)
    score = jnp.where(forced, FORCE_SCORE, jnp.where(blk * SEL_BLOCK <= qp, imp[:rs], -1.0))
    score = jnp.where(blk < n_sel, score, -jnp.inf)
    sel = jnp.zeros((rs, TQ), f32)
    for s in range(n_sel):
        r = jnp.max(jnp.where(blk == s, score, -jnp.inf), axis=0, keepdims=True)
        beats = (score > r) | ((score == r) & (blk < s))
        rank = jnp.sum(jnp.where(beats, 1.0, 0.0), axis=0, keepdims=True)
        sel = jnp.where((blk == s) & (rank < min(SEL_TOP_N, n_sel)), 1.0, sel)
    sel = jnp.concatenate([sel, jnp.zeros((LANES - rs, TQ), f32)], axis=0)
    mask_ref[...] = jnp.dot(exp_ref[...], sel.astype(bf16), preferred_element_type=f32)

    key_i = lax.broadcasted_iota(jnp.int32, (TK, TQ), 0)
    qry_i = lax.broadcasted_iota(jnp.int32, (TK, TQ), 1)
    init = (jnp.full((1, GRP * TQ), NEG_INF, f32), jnp.zeros((1, GRP * TQ), f32), jnp.zeros((HEAD_DIM, GRP * TQ), f32))

    def tile(ki, carry, k_ref, vt_ref, msk):
        m, l, acc = carry
        keys = pl.ds(pl.multiple_of(ki * TK, TK), TK)
        s_all = lax.dot_general(k_ref[0, 0, keys, :].astype(bf16), q, nt_dims, preferred_element_type=f32)
        m_parts, l_parts, a_parts, p_parts = [], [], [], []
        for g in range(GRP):
            s = jnp.where(msk, s_all[:, lanes[g]] + bt_ref[0, qi - ki, g], NEG_INF)
            m_new = jnp.maximum(m[:, lanes[g]], jnp.max(s, axis=0, keepdims=True))
            p = jnp.where(msk, jnp.exp(s - m_new), 0.0)
            alpha = jnp.exp(m[:, lanes[g]] - m_new)
            m_parts.append(m_new)
            a_parts.append(alpha)
            l_parts.append(alpha * l[:, lanes[g]] + jnp.sum(p, axis=0, keepdims=True))
            p_parts.append(p.astype(bf16))
        pv = jnp.dot(vt_ref[0, 0, :, keys].astype(bf16), jnp.concatenate(p_parts, axis=1), preferred_element_type=f32)
        return (jnp.concatenate(m_parts, axis=1), jnp.concatenate(l_parts, axis=1),
                jnp.concatenate(a_parts, axis=1) * acc + pv)

    def sel_step(ki, carry):
        dist = (qi - ki) * TQ + qry_i - key_i
        msk = (mask_ref[pl.ds(pl.multiple_of(ki * TK, TK), TK), :] > 0.5) & (dist >= 0)
        return tile(ki, carry, ks_ref, vst_ref, msk)

    def win_step(ki, carry):
        dist = (qi - ki) * TQ + qry_i - key_i
        return tile(ki, carry, kw_ref, vwt_ref, (dist >= 0) & (dist < WINDOW))

    _, l_s, acc_s = lax.fori_loop(0, qi + 1, sel_step, init)
    _, l_w, acc_w = lax.fori_loop(jnp.maximum(qi - WINDOW // TK, 0), qi + 1, win_step, init)
    o_s = acc_s / jnp.maximum(l_s, 1e-30)
    o_w = acc_w / jnp.maximum(l_w, 1e-30)
    for g in range(GRP):
        gate = jax.nn.sigmoid(g_ref[0, 0, :, g, :])
        o_ref[0, g] = gate[0:1] * o_c[:, lanes[g]] + gate[1:2] * o_s[:, lanes[g]] + gate[2:3] * o_w[:, lanes[g]]


def _masked_softmax_rows(s, ok):
    s = jnp.where(ok, s, NEG_INF)
    e = jnp.where(ok, jnp.exp(s - jnp.max(s, axis=0, keepdims=True)), 0.0)
    return e / jnp.maximum(jnp.sum(e, axis=0, keepdims=True), 1e-30)


def nsa_prefill(q, glog, kcvc, kv_new, thr, rel_bias, n_cmp):
    f32, bf16 = jnp.float32, jnp.bfloat16
    b, t = q.shape[:2]
    nq = t // TQ
    n_sel = t // SEL_BLOCK
    assert t % TQ == 0 and n_sel <= LANES and n_cmp < LANES and kcvc.shape[1] == LANES and TQ == TK == LANES
    q4 = jnp.transpose(q, (0, 2, 1, 3))
    g5 = jnp.transpose(glog.reshape(b, t, NSA_KV_HEADS, GRP, 3), (0, 2, 4, 3, 1))
    cmp_tok = jnp.pad(kcvc[:, 1:], ((0, 0), (0, 1), (0, 0))).reshape(b, LANES, 2, NSA_KV_HEADS, HEAD_DIM)
    kc = jnp.transpose(cmp_tok[:, :, 0], (0, 2, 1, 3))
    vct = jnp.transpose(cmp_tok[:, :, 1], (0, 2, 3, 1))
    rows = lambda n: jnp.transpose(kv_new[:, :, n], (0, 2, 1, 3))
    cols = lambda n: jnp.transpose(kv_new[:, :, n], (0, 2, 3, 1))
    i = jnp.arange(TQ, dtype=jnp.int32)
    d_toep = jnp.arange(nq, dtype=jnp.int32)[:, None, None] * TQ + i[None, None, :] - i[None, :, None]
    c_end = CMP_STRIDE * jnp.arange(LANES, dtype=jnp.int32) + CMP_BLOCK - 1
    d_cmp = jnp.arange(t, dtype=jnp.int32).reshape(nq, 1, TQ) - c_end[None, :, None]
    tabs = bias_tables(jnp.concatenate([d_toep, d_cmp], axis=0), thr, rel_bias)
    bt, bc = tabs[:, :nq], tabs[:, nq:]
    s_start = SEL_BLOCK * jnp.arange(LANES)[:, None]
    c_start = CMP_STRIDE * jnp.arange(LANES)[None, :]
    cover = ((c_start < s_start + SEL_BLOCK) & (c_start + CMP_BLOCK > s_start)
             & (jnp.arange(LANES)[None, :] < n_cmp) & (jnp.arange(LANES)[:, None] < n_sel)).astype(bf16)
    expand = (jnp.arange(t)[:, None] // SEL_BLOCK == jnp.arange(LANES)[None, :]).astype(bf16)
    spec = lambda *blk: pl.BlockSpec((1, 1) + blk, lambda h, bb, qq: (bb, h) + (0,) * len(blk))
    out = pl.pallas_call(
        functools.partial(_nsa_prefill_body, n_cmp=n_cmp, n_sel=n_sel),
        grid=(NSA_KV_HEADS, b, nq),
        in_specs=[pl.BlockSpec((1, GRP, TQ, HEAD_DIM), lambda h, bb, qq: (bb, h, qq, 0)),
                  pl.BlockSpec((1, 1, 3, GRP, TQ), lambda h, bb, qq: (bb, h, 0, 0, qq)),
                  spec(LANES, HEAD_DIM), spec(HEAD_DIM, LANES),
                  spec(t, HEAD_DIM), spec(HEAD_DIM, t), spec(t, HEAD_DIM), spec(HEAD_DIM, t),
                  pl.BlockSpec((1, nq, GRP, TK, TQ), lambda h, bb, qq: (h, 0, 0, 0, 0)),
                  pl.BlockSpec((1, 1, GRP, LANES, TQ), lambda h, bb, qq: (h, qq, 0, 0, 0)),
                  pl.BlockSpec((LANES, LANES), lambda h, bb, qq: (0, 0)),
                  pl.BlockSpec((t, LANES), lambda h, bb, qq: (0, 0))],
        out_specs=pl.BlockSpec((1, GRP, HEAD_DIM, TQ), lambda h, bb, qq: (bb, h, 0, qq)),
        out_shape=jax.ShapeDtypeStruct((b, NSA_HEADS, HEAD_DIM, t), f32),
        scratch_shapes=[pltpu.VMEM((t, TQ), f32)],
        compiler_params=pltpu.CompilerParams(vmem_limit_bytes=VMEM_LIMIT),
        name="nsa_prefill",
    )(q4, g5, kc, vct, rows(2), cols(3), rows(4), cols(5), bt, bc, cover, expand)
    return jnp.transpose(out, (0, 3, 1, 2)).reshape(b, t, NSA_WIDTH)


S5_GB = 8
S5_CH = S5_GB * SSM_CH
S5_ST = S5_GB * SSM_STATE


def _cmul(a_re, a_im, b_re, b_im):
    return a_re * b_re - a_im * b_im, a_re * b_im + a_im * b_re


def _s5_body(u_ref, h0re_ref, h0im_ref, a_ref, bcat_ref, ccat_ref, d_ref, y_ref, hre_ref, him_ref, xs_ref, hin_ref,
             *, chained):
    f32, bf16 = jnp.float32, jnp.bfloat16
    n_l, n_r = u_ref.shape[1], u_ref.shape[2]
    u2 = u_ref[0].reshape(n_l * n_r, S5_CH)
    xs_ref[...] = jnp.dot(u2.astype(bf16), bcat_ref[0].astype(bf16),
                          preferred_element_type=f32).reshape(n_l, n_r, 2 * S5_ST)
    a_re, a_im = a_ref[0:1, :], a_ref[1:2, :]

    def scan_step(j, carry):
        h_re, h_im, p_re, p_im = carry
        x = xs_ref[j]
        t_re, t_im = _cmul(a_re, a_im, h_re, h_im)
        h_re, h_im = t_re + x[:, :S5_ST], t_im + x[:, S5_ST:]
        xs_ref[j] = jnp.concatenate([h_re, h_im], axis=1)
        return (h_re, h_im) + _cmul(a_re, a_im, p_re, p_im)

    if chained:
        start = (jnp.zeros((n_r, S5_ST), f32), jnp.zeros((n_r, S5_ST), f32))
    else:
        start = (h0re_ref[0], h0im_ref[0])
    ones = (jnp.ones((1, S5_ST), f32), jnp.zeros((1, S5_ST), f32))
    h_re, h_im, al_re, al_im = lax.fori_loop(0, n_l, scan_step, start + ones)

    if chained:
        hin_ref[0:1, :] = jnp.concatenate([h0re_ref[0], h0im_ref[0]], axis=1)

        def chain_step(c, carry):
            z = xs_ref[n_l - 1, pl.ds(c - 1, 1), :]
            t_re, t_im = _cmul(al_re, al_im, *carry)
            n_re, n_im = t_re + z[:, :S5_ST], t_im + z[:, S5_ST:]
            hin_ref[pl.ds(c, 1), :] = jnp.concatenate([n_re, n_im], axis=1)
            return n_re, n_im

        h_re, h_im = lax.fori_loop(1, n_r + 1, chain_step, (h0re_ref[0], h0im_ref[0]))

        def fix_step(j, carry):
            p_re, p_im = carry
            hin = hin_ref[0:n_r, :]
            t_re, t_im = _cmul(p_re, p_im, hin[:, :S5_ST], hin[:, S5_ST:])
            xs_ref[j] = xs_ref[j] + jnp.concatenate([t_re, t_im], axis=1)
            return _cmul(a_re, a_im, p_re, p_im)

        lax.fori_loop(0, n_l, fix_step, (a_re, a_im))

    hre_ref[0] = h_re
    him_ref[0] = h_im
    hs = xs_ref[...].reshape(n_l * n_r, 2 * S5_ST).astype(bf16)
    y = jnp.dot(hs, ccat_ref[0].astype(bf16), preferred_element_type=f32) + d_ref[...] * u2
    y_ref[0] = jax.nn.gelu(y).reshape(n_l, n_r, S5_CH)


def s5_discretize(lam_re, lam_im, log_dt, b_re, b_im, c_re, c_im):
    f32 = jnp.float32
    dt = jnp.exp(log_dt.astype(f32))[:, None]
    lr, li = lam_re.astype(f32), lam_im.astype(f32)
    mag = jnp.exp(lr * dt)
    a_re, a_im = mag * jnp.cos(li * dt), mag * jnp.sin(li * dt)
    den = lr * lr + li * li
    z_re = ((a_re - 1.0) * lr + a_im * li) / den
    z_im = (a_im * lr - (a_re - 1.0) * li) / den
    br, bim = b_re.astype(f32), b_im.astype(f32)
    bb_re = z_re[..., None] * br - z_im[..., None] * bim
    bb_im = z_re[..., None] * bim + z_im[..., None] * br
    ngb = SSM_GROUPS // S5_GB
    eye = jnp.eye(S5_GB, dtype=f32)

    def block_diag(w):
        wd = w[:, :, :, None, :] * eye[None, :, None, :, None]
        return wd.reshape(ngb, S5_GB * w.shape[2], S5_GB * w.shape[3])

    def pack_b(bb):
        return block_diag(jnp.swapaxes(bb.reshape(ngb, S5_GB, SSM_STATE, SSM_CH), 2, 3))

    def pack_c(cc):
        return block_diag(jnp.swapaxes(cc.reshape(ngb, S5_GB, SSM_CH, SSM_STATE), 2, 3))

    a = jnp.stack([a_re.reshape(-1), a_im.reshape(-1)])
    bcat = jnp.concatenate([pack_b(bb_re), pack_b(bb_im)], axis=2)
    ccat = jnp.concatenate([pack_c(c_re.astype(f32)), -pack_c(c_im.astype(f32))], axis=1)
    return a, bcat, ccat


def s5_scan(u4, h0_re, h0_im, a, bcat, ccat, d_skip, chained):
    nb, n_l, n_r, _ = u4.shape
    rh = h0_re.shape[1]
    ngb = SSM_GROUPS // S5_GB
    st_spec = pl.BlockSpec((1, rh, S5_ST), lambda i, j: (i, 0, j))
    return pl.pallas_call(
        functools.partial(_s5_body, chained=chained),
        grid=(nb, ngb),
        in_specs=[pl.BlockSpec((1, n_l, n_r, S5_CH), lambda i, j: (i, 0, 0, j)), st_spec, st_spec,
                  pl.BlockSpec((2, S5_ST), lambda i, j: (0, j)),
                  pl.BlockSpec((1, S5_CH, 2 * S5_ST), lambda i, j: (j, 0, 0)),
                  pl.BlockSpec((1, 2 * S5_ST, S5_CH), lambda i, j: (j, 0, 0)),
                  pl.BlockSpec((1, S5_CH), lambda i, j: (0, j))],
        out_specs=[pl.BlockSpec((1, n_l, n_r, S5_CH), lambda i, j: (i, 0, 0, j)), st_spec, st_spec],
        out_shape=[jax.ShapeDtypeStruct(u4.shape, jnp.float32),
                   jax.ShapeDtypeStruct(h0_re.shape, jnp.float32), jax.ShapeDtypeStruct(h0_re.shape, jnp.float32)],
        scratch_shapes=[pltpu.VMEM((n_l, n_r, 2 * S5_ST), jnp.float32), pltpu.VMEM((n_r + 8, 2 * S5_ST), jnp.float32)],
        compiler_params=pltpu.CompilerParams(vmem_limit_bytes=VMEM_LIMIT),
        name="s5_scan",
    )(u4, h0_re, h0_im, a, bcat, ccat, d_skip.reshape(1, SSM_WIDTH))


S5_CHUNK = 64


def s5_mixer_pallas(u, h0_re, h0_im, lam_re, lam_im, log_dt, b_re, b_im, c_re, c_im, d_skip, w_glu, b_glu, chained):
    bsz, t, _ = u.shape
    f32 = jnp.float32
    a, bcat, ccat = s5_discretize(lam_re, lam_im, log_dt, b_re, b_im, c_re, c_im)
    flat = lambda h: h.astype(f32).reshape(bsz, SSM_GROUPS * SSM_STATE)
    if chained:
        n_r = t // S5_CHUNK
        u4 = jnp.transpose(u.astype(f32).reshape(bsz, n_r, S5_CHUNK, SSM_WIDTH), (0, 2, 1, 3))
        y4, h_re, h_im = s5_scan(u4, flat(h0_re)[:, None], flat(h0_im)[:, None], a, bcat, ccat, d_skip.astype(f32), True)
        y = jnp.transpose(y4, (0, 2, 1, 3)).reshape(bsz, t, SSM_WIDTH)
        h_re, h_im = h_re[:, 0], h_im[:, 0]
    else:
        u4 = jnp.transpose(u.astype(f32), (1, 0, 2))[None]
        y4, h_re, h_im = s5_scan(u4, flat(h0_re)[None], flat(h0_im)[None], a, bcat, ccat, d_skip.astype(f32), False)
        y = jnp.transpose(y4[0], (1, 0, 2))
        h_re, h_im = h_re[0], h_im[0]
    out = y * jax.nn.sigmoid(y @ w_glu.astype(f32) + b_glu.astype(f32))
    st = lambda h: h.reshape(bsz, SSM_GROUPS, SSM_STATE).astype(u.dtype)
    return out.astype(u.dtype), st(h_re), st(h_im)


ROW_LANES = N_PAGED * KV_WIDTH
CHUNKS_PER_PAGE = 128 // CMP_STRIDE
CMP_HID2 = NSA_KV_HEADS * CMP_HIDDEN
CMP_PAGES = 64


def _compress_body(pt_ref, *refs, n_g, transposed):
    f32, bf16 = jnp.float32, jnp.bfloat16
    page_refs = (refs[:n_g], refs[n_g:2 * n_g])
    w1_ref, c1_ref, w2_ref, b2_ref, o_ref, carry_ref, rows_ref = refs[2 * n_g:]
    m_rows = n_g * CHUNKS_PER_PAGE
    first = pl.program_id(1) == 0
    row_id = lax.broadcasted_iota(jnp.int32, (m_rows, CMP_HID2), 0)
    outs = []
    for t in range(2):
        for k, r in enumerate(page_refs[t]):
            rows_ref[k * 128:(k + 1) * 128, :] = r[0].T if transposed else r[0]
        cols = [rows_ref[pl.ds(j, m_rows, stride=CMP_STRIDE), :] for j in range(CMP_STRIDE)]
        x = jnp.concatenate(cols, axis=1).astype(bf16)
        part = jnp.dot(x, w1_ref[t], preferred_element_type=f32)
        p0, p1 = part[:, :CMP_HID2], part[:, CMP_HID2:]
        prev = jnp.where(first, 0.0, carry_ref[t, 0:1, :])
        shifted = jnp.where(row_id == 0, prev, pltpu.roll(p0, 1, axis=0))
        carry_ref[t, 0:1, :] = p0[m_rows - 1:m_rows, :]
        h1 = (c1_ref[t] + shifted) + p1
        outs.append(jnp.dot(jax.nn.gelu(h1).astype(bf16), w2_ref[t], preferred_element_type=f32) + b2_ref[t])
    o_ref[0] = jnp.concatenate(outs, axis=1)


def compress_pages(pool, page_table, cmp_pe, cmp_w1, cmp_b1, cmp_w2, cmp_b2, n_g, transposed):
    n_seq, n_pages = page_table.shape
    assert n_pages % n_g == 0 and pool.shape[1:] == ((ROW_LANES, 128) if transposed else (128, ROW_LANES))
    f32, bf16 = jnp.float32, jnp.bfloat16
    r = CMP_BLOCK // CMP_STRIDE
    eye = jnp.eye(NSA_KV_HEADS, dtype=f32)
    w1 = cmp_w1.astype(f32).reshape(2, r, CMP_STRIDE, HEAD_DIM, CMP_HIDDEN)
    w1 = jnp.transpose(w1, (0, 2, 3, 1, 4))[:, :, None, :, :, None, :] * eye[None, None, :, None, None, :, None]
    w1 = w1.reshape(2, CMP_STRIDE * KV_WIDTH, r * CMP_HID2).astype(bf16)
    c1 = jnp.stack([jnp.einsum('ld,ldf->f', cmp_pe[t], cmp_w1[t]) + cmp_b1[t] for t in range(2)])
    c1 = jnp.tile(c1[:, None, :], (1, 1, NSA_KV_HEADS))
    w2 = (cmp_w2.astype(f32)[:, None, :, None, :] * eye[None, :, None, :, None]).reshape(2, CMP_HID2, KV_WIDTH).astype(bf16)
    b2 = jnp.tile(cmp_b2.astype(f32)[:, None, :], (1, 1, NSA_KV_HEADS))
    m_rows = n_g * CHUNKS_PER_PAGE
    page_spec = lambda k, t: pl.BlockSpec((1, 128, KV_WIDTH), lambda b, g, pt: (
        (pt[b * n_pages + g * n_g + k], t, 0) if transposed else (pt[b * n_pages + g * n_g + k], 0, t)))
    full = lambda a: pl.BlockSpec(a.shape, lambda b, g, pt: (0,) * a.ndim)
    return pl.pallas_call(
        functools.partial(_compress_body, n_g=n_g, transposed=transposed),
        grid_spec=pltpu.PrefetchScalarGridSpec(
            num_scalar_prefetch=1,
            grid=(n_seq, n_pages // n_g),
            in_specs=[page_spec(k, t) for t in range(2) for k in range(n_g)] + [full(w1), full(c1), full(w2), full(b2)],
            out_specs=pl.BlockSpec((1, m_rows, 2 * KV_WIDTH), lambda b, g, pt: (b, g, 0)),
            scratch_shapes=[pltpu.VMEM((2, 8, CMP_HID2), f32), pltpu.VMEM((n_g * 128, KV_WIDTH), f32)],
        ),
        out_shape=jax.ShapeDtypeStruct((n_seq, n_pages * CHUNKS_PER_PAGE, 2 * KV_WIDTH), f32),
        compiler_params=pltpu.CompilerParams(vmem_limit_bytes=VMEM_LIMIT),
        name="compress_pages",
    )(page_table.reshape(-1).astype(jnp.int32), *([pool] * (2 * n_g)), w1, c1, w2, b2)


SD_PAGES = 64
SD_KEYS = SD_PAGES * 128
SD_SUB = 16
SD_ROWS = NSA_HEADS * 8
WIN_PAD = 640


def _masked_softmax(s, ok):
    s = jnp.where(ok, s, NEG_INF)
    e = jnp.where(ok, jnp.exp(s - jnp.max(s, axis=-1, keepdims=True)), 0.0)
    return e / jnp.maximum(jnp.sum(e, axis=-1, keepdims=True), 1e-30)


def _rows_from_group(a):
    t = a.shape[0] // NSA_KV_HEADS
    a4 = jnp.broadcast_to(a.reshape(NSA_KV_HEADS, 1, t, a.shape[1]), (NSA_KV_HEADS, GRP, t, a.shape[1]))
    return a4.reshape(NSA_KV_HEADS * GRP * t, a.shape[1])


def _nsa_decode_body(pt_ref, *refs, pos0, n_cmp, n_sel, n_new, win_len):
    f32, bf16 = jnp.float32, jnp.bfloat16
    kt_refs, vt_refs = refs[:SD_PAGES], refs[SD_PAGES:2 * SD_PAGES]
    (q_ref, g_ref, kcvc_ref, knew_ref, wh_ref, bc_ref, bs_ref, bn_ref, bw_ref, cov_ref, exp_ref,
     o_ref, sel_ref, oc_ref, m_ref, l_ref, acc_ref, mask_ref) = refs[2 * SD_PAGES:]
    tile, n_tiles = pl.program_id(1), pl.num_programs(1)
    nt_dims = (((1,), (1,)), ((), ()))
    nn_dims = (((1,), (0,)), ((), ()))
    n_tok = SD_ROWS // NSA_HEADS
    q = (q_ref[0] * (HEAD_DIM ** -0.5)).astype(bf16)
    qpos = pos0 + lax.broadcasted_iota(jnp.int32, (SD_ROWS, 1), 0) % n_tok

    @pl.when(tile == 0)
    def _():
        n_c = kcvc_ref.shape[1]
        m_idx = lax.broadcasted_iota(jnp.int32, (SD_ROWS, n_c), 1)
        ok_c = (m_idx >= 1) & (m_idx <= n_cmp) & (qpos >= CMP_STRIDE * m_idx + (CMP_BLOCK - 1 - CMP_STRIDE))
        s_c = lax.dot_general(q, kcvc_ref[0, :, :KV_WIDTH].astype(bf16), nt_dims, preferred_element_type=f32) + bc_ref[...]
        p_c = _masked_softmax(s_c, ok_c)
        oc_ref[...] = jnp.dot(p_c.astype(bf16), kcvc_ref[0, :, KV_WIDTH:].astype(bf16), preferred_element_type=f32)
        p_sum = jnp.sum(p_c.reshape(NSA_KV_HEADS, GRP, n_tok, n_c), axis=1).reshape(NSA_KV_HEADS * n_tok, n_c)
        imp = jnp.dot(p_sum.astype(bf16), cov_ref[...], preferred_element_type=f32)
        n_l = imp.shape[1]
        lane = lax.broadcasted_iota(jnp.int32, (NSA_KV_HEADS * n_tok, n_l), 1)
        qp = pos0 + lax.broadcasted_iota(jnp.int32, (NSA_KV_HEADS * n_tok, n_l), 0) % n_tok
        cur = qp // SEL_BLOCK
        forced = (lane == 0) | (lane == cur) | (lane == cur - 1)
        score = jnp.where(forced, FORCE_SCORE, jnp.where(lane * SEL_BLOCK <= qp, imp, -1.0))
        score = jnp.where(lane < n_sel, score, -jnp.inf)
        sel = jnp.zeros(score.shape, f32)
        for _ in range(min(SEL_TOP_N, n_sel)):
            best = jnp.max(score, axis=-1, keepdims=True)
            lane_f = lane.astype(f32)
            hit = lane_f == jnp.min(jnp.where(score == best, lane_f, float(n_l)), axis=-1, keepdims=True)
            sel = jnp.where(hit, 1.0, sel)
            score = jnp.where(hit, -jnp.inf, score)
        sel_ref[...] = sel
        m_ref[...] = jnp.full(m_ref.shape, NEG_INF, f32)
        l_ref[...] = jnp.zeros(l_ref.shape, f32)
        acc_ref[...] = jnp.zeros(acc_ref.shape, f32)

    def online(s, ok, v, v_dims):
        m_old = m_ref[...]
        s = jnp.where(ok, s, NEG_INF)
        m_new = jnp.maximum(m_old, jnp.max(s, axis=-1, keepdims=True))
        p = jnp.where(ok, jnp.exp(s - m_new), 0.0)
        alpha = jnp.exp(m_old - m_new)
        l_ref[...] = alpha * l_ref[...] + jnp.sum(p, axis=-1, keepdims=True)
        acc_ref[...] = alpha * acc_ref[...] + lax.dot_general(p.astype(bf16), v, v_dims, preferred_element_type=f32)
        m_ref[...] = m_new

    sel_tile = sel_ref[:, pl.ds(pl.multiple_of(tile * LANES, LANES), LANES)]
    mask_ref[...] = jnp.dot(sel_tile.astype(bf16), exp_ref[...], preferred_element_type=f32)
    sub_keys = SD_SUB * 128
    for sub in range(SD_PAGES // SD_SUB):
        pages = slice(sub * SD_SUB, (sub + 1) * SD_SUB)
        kt = jnp.concatenate([r[0] for r in kt_refs[pages]], axis=1).astype(bf16)
        vt = jnp.concatenate([r[0] for r in vt_refs[pages]], axis=1).astype(bf16)
        s = jnp.dot(q, kt, preferred_element_type=f32) + bs_ref[:, sub * sub_keys:(sub + 1) * sub_keys]
        ok = _rows_from_group(mask_ref[:, sub * sub_keys:(sub + 1) * sub_keys]) > 0.5
        online(s, ok, vt, nt_dims)

    @pl.when(tile == n_tiles - 1)
    def _():
        lane = lax.broadcasted_iota(jnp.int32, (SD_ROWS, LANES), 1)
        new_blk = pos0 // SEL_BLOCK
        sel_new = _rows_from_group(sel_ref[:, new_blk:new_blk + 1]) > 0.5
        ok_n = sel_new & (lane < n_new) & (pos0 + lane <= qpos)
        s_n = lax.dot_general(q, knew_ref[0, :, :KV_WIDTH].astype(bf16), nt_dims, preferred_element_type=f32) + bn_ref[...]
        online(s_n, ok_n, knew_ref[0, :, KV_WIDTH:].astype(bf16), nn_dims)
        o_s = acc_ref[...] / jnp.maximum(l_ref[...], 1e-30)
        j = lax.broadcasted_iota(jnp.int32, (SD_ROWS, WIN_PAD), 1)
        kwpos = pos0 - win_len + j
        dist = qpos - kwpos
        ok_w = (dist >= 0) & (dist < WINDOW) & (kwpos >= 0) & (j < win_len + n_new)
        s_w = lax.dot_general(q, wh_ref[0, :, :KV_WIDTH].astype(bf16), nt_dims, preferred_element_type=f32) + bw_ref[...]
        o_w = jnp.dot(_masked_softmax(s_w, ok_w).astype(bf16), wh_ref[0, :, KV_WIDTH:].astype(bf16), preferred_element_type=f32)
        g = jax.nn.sigmoid(g_ref[0])
        o = g[:, 0:1] * oc_ref[...] + g[:, 1:2] * o_s + g[:, 2:3] * o_w
        row = lax.broadcasted_iota(jnp.int32, (SD_ROWS, HEAD_DIM), 0)
        o_ref[0] = jnp.where(row < SD_ROWS // NSA_KV_HEADS, o[:, :HEAD_DIM], o[:, HEAD_DIM:])


def nsa_decode(q, glog, kv_new, kcvc, pool_t, page_table, cache_win, thr, rel_bias, pos0):
    f32, bf16 = jnp.float32, jnp.bfloat16
    b, t = q.shape[:2]
    n_pages = page_table.shape[1]
    win_len = cache_win.shape[1]
    assert t * NSA_HEADS == SD_ROWS and pos0 == n_pages * 128 and n_pages % SD_PAGES == 0 and pos0 % SEL_BLOCK == 0
    assert win_len + t <= WIN_PAD and t <= SEL_BLOCK and SD_KEYS == LANES * SEL_BLOCK
    n_rows = pos0 + t
    n_cmp = (n_rows - CMP_BLOCK) // CMP_STRIDE + 1
    n_sel = -(-n_rows // SEL_BLOCK)
    n_c = kcvc.shape[1]
    sel_lanes = -(-n_sel // LANES) * LANES
    q5 = jnp.transpose(q.reshape(b, t, NSA_KV_HEADS, GRP, HEAD_DIM), (0, 2, 3, 1, 4))
    qz = (q5[:, :, :, :, None, :] * jnp.eye(NSA_KV_HEADS, dtype=f32)[None, :, None, None, :, None]).reshape(b, SD_ROWS, KV_WIDTH)
    g3 = jnp.transpose(glog.reshape(b, t, NSA_HEADS, 3), (0, 2, 1, 3)).reshape(b, SD_ROWS, 3)
    knew = jnp.pad(kv_new[:, :, 2:4].reshape(b, t, 2 * KV_WIDTH), ((0, 0), (0, LANES - t), (0, 0)))
    whist = jnp.concatenate([cache_win.reshape(b, win_len, 2 * KV_WIDTH), kv_new[:, :, 4:6].reshape(b, t, 2 * KV_WIDTH),
                             jnp.zeros((b, WIN_PAD - win_len - t, 2 * KV_WIDTH), f32)], axis=1)
    qp = pos0 + jnp.arange(t, dtype=jnp.int32)[:, None]
    tab = lambda dist: bias_tables(dist[None], thr, rel_bias).reshape(SD_ROWS, dist.shape[1])
    bias_c = tab(qp - (CMP_STRIDE * jnp.arange(n_c, dtype=jnp.int32)[None, :] + CMP_BLOCK - 1 - CMP_STRIDE))
    bias_s = tab(qp - jnp.arange(pos0, dtype=jnp.int32)[None, :])
    bias_n = tab(qp - (pos0 + jnp.arange(LANES, dtype=jnp.int32)[None, :]))
    bias_w = tab(qp - (pos0 - win_len + jnp.arange(WIN_PAD, dtype=jnp.int32)[None, :]))
    m_idx = jnp.arange(n_c)[:, None]
    c_start = CMP_STRIDE * (m_idx - 1)
    s_start = SEL_BLOCK * jnp.arange(sel_lanes)[None, :]
    cover = ((c_start < s_start + SEL_BLOCK) & (c_start + CMP_BLOCK > s_start) & (m_idx >= 1) & (m_idx <= n_cmp)
             & (jnp.arange(sel_lanes)[None, :] < n_sel)).astype(bf16)
    expand = (jnp.arange(SD_KEYS)[None, :] // SEL_BLOCK == jnp.arange(LANES)[:, None]).astype(bf16)
    page_spec = lambda k, blk: pl.BlockSpec((1, KV_WIDTH, 128), lambda bb, g, pt: (pt[bb * n_pages + g * SD_PAGES + k], blk, 0))
    per_seq = lambda a: pl.BlockSpec((1,) + a.shape[1:], lambda bb, g, pt: (bb,) + (0,) * (a.ndim - 1))
    full = lambda a: pl.BlockSpec(a.shape, lambda bb, g, pt: (0,) * a.ndim)
    kv_rows = NSA_KV_HEADS * t
    out = pl.pallas_call(
        functools.partial(_nsa_decode_body, pos0=pos0, n_cmp=n_cmp, n_sel=n_sel, n_new=t, win_len=win_len),
        grid_spec=pltpu.PrefetchScalarGridSpec(
            num_scalar_prefetch=1,
            grid=(b, n_pages // SD_PAGES),
            in_specs=[page_spec(k, blk) for blk in (2, 3) for k in range(SD_PAGES)]
            + [per_seq(qz), per_seq(g3), per_seq(kcvc), per_seq(knew), per_seq(whist), full(bias_c),
               pl.BlockSpec((SD_ROWS, SD_KEYS), lambda bb, g, pt: (0, g)), full(bias_n), full(bias_w), full(cover), full(expand)],
            out_specs=pl.BlockSpec((1, SD_ROWS, HEAD_DIM), lambda bb, g, pt: (bb, 0, 0)),
            scratch_shapes=[pltpu.VMEM((kv_rows, sel_lanes), f32), pltpu.VMEM((SD_ROWS, KV_WIDTH), f32),
                            pltpu.VMEM((SD_ROWS, 1), f32), pltpu.VMEM((SD_ROWS, 1), f32), pltpu.VMEM((SD_ROWS, KV_WIDTH), f32),
                            pltpu.VMEM((kv_rows, SD_KEYS), f32)],
        ),
        out_shape=jax.ShapeDtypeStruct((b, SD_ROWS, HEAD_DIM), f32),
        compiler_params=pltpu.CompilerParams(vmem_limit_bytes=VMEM_LIMIT),
        name="nsa_decode",
    )(page_table.reshape(-1).astype(jnp.int32), *([pool_t] * (2 * SD_PAGES)), qz, g3, kcvc, knew, whist,
      bias_c, bias_s, bias_n, bias_w, cover, expand)
    out = jnp.transpose(out.reshape(b, NSA_KV_HEADS, GRP, t, HEAD_DIM), (0, 3, 1, 2, 4))
    return out.reshape(b, t, NSA_WIDTH)


def rms_norm(x, g):
    xf = x.astype(jnp.float32)
    y = xf * lax.rsqrt(jnp.mean(xf * xf, axis=-1, keepdims=True) + RMS_EPS)
    return (y * g.astype(jnp.float32)).astype(x.dtype)


def masked_softmax(s, mask):
    s = jnp.where(mask, s.astype(jnp.float32), NEG_INF)
    m = jnp.max(s, axis=-1, keepdims=True)
    e = jnp.where(mask, jnp.exp(s - m), 0.0)
    return e / jnp.maximum(jnp.sum(e, axis=-1, keepdims=True), 1e-30)


def compress_rows(x, pe, w1, b1, w2, b2, n_rows):
    b = x.shape[0]
    nc = (n_rows - CMP_BLOCK) // CMP_STRIDE + 1
    r = CMP_BLOCK // CMP_STRIDE
    nch = nc + r - 1
    chunks = x[:, :nch * CMP_STRIDE].reshape(b, nch, CMP_STRIDE, NSA_KV_HEADS, HEAD_DIM)
    part = jnp.einsum('bcjhd,rjdf->bcrhf', chunks, w1.reshape(r, CMP_STRIDE, HEAD_DIM, CMP_HIDDEN))
    h1 = jnp.einsum('ld,ldf->f', pe, w1) + b1
    for o in range(r):
        h1 = h1 + part[:, o:o + nc, o]
    return jnp.einsum('bchf,fd->bchd', jax.nn.gelu(h1), w2) + b2


def nsa_query_block(q, qpos, gates, kw, vw, kwpos, kc, vc, c_end, ks_b, vs_b, cover, rel_bias):
    b, nq = q.shape[:2]
    grp = NSA_HEADS // NSA_KV_HEADS
    qg = (q * HEAD_DIM ** -0.5).reshape(b, nq, NSA_KV_HEADS, grp, HEAD_DIM)
    rb = rel_bias.reshape(REL_BUCKETS, NSA_KV_HEADS, grp)
    hi = jnp.arange(NSA_KV_HEADS)[None, None, :, None]
    bi = jnp.arange(b)[:, None, None, None]
    dist_c = qpos[:, None] - c_end[None, :]
    bias_c = jnp.moveaxis(rb[rel_bucket(dist_c)], 1, -1)
    s_c = jnp.einsum('bqhgd,bchd->bqhgc', qg, kc) + bias_c
    p_c = masked_softmax(s_c, (dist_c >= 0)[:, None, None, :])
    o_c = jnp.einsum('bqhgc,bchd->bqhgd', p_c.astype(vc.dtype), vc)
    imp = jnp.einsum('bqhgc,cs->bqhs', p_c, cover)
    ns = ks_b.shape[1]
    blk = jnp.arange(ns)[None, :]
    cur = (qpos // SEL_BLOCK)[:, None]
    forced = ((blk == 0) | (blk == cur) | (blk == cur - 1))[:, None, :]
    valid = (blk * SEL_BLOCK <= qpos[:, None])[:, None, :]
    score = jnp.where(forced, FORCE_SCORE, jnp.where(valid, imp, -1.0))
    _, idx = lax.top_k(score, min(SEL_TOP_N, ns))
    k_s = ks_b[bi, idx, :, hi].reshape(b, nq, NSA_KV_HEADS, -1, HEAD_DIM)
    v_s = vs_b[bi, idx, :, hi].reshape(b, nq, NSA_KV_HEADS, -1, HEAD_DIM)
    s_pos = (idx[..., None] * SEL_BLOCK + jnp.arange(SEL_BLOCK)).reshape(b, nq, NSA_KV_HEADS, -1)
    dist_s = qpos[None, :, None, None] - s_pos
    bias_s = jnp.moveaxis(rb[rel_bucket(dist_s), hi], -1, 3)
    s_s = jnp.einsum('bqhgd,bqhkd->bqhgk', qg, k_s) + bias_s
    p_s = masked_softmax(s_s, (dist_s >= 0)[:, :, :, None, :])
    o_s = jnp.einsum('bqhgk,bqhkd->bqhgd', p_s.astype(v_s.dtype), v_s)
    dist_w = qpos[:, None] - kwpos[None, :]
    mask_w = (dist_w >= 0) & (dist_w < WINDOW) & (kwpos >= 0)[None, :]
    bias_w = jnp.moveaxis(rb[rel_bucket(dist_w)], 1, -1)
    s_w = jnp.einsum('bqhgd,bkhd->bqhgk', qg, kw) + bias_w
    p_w = masked_softmax(s_w, mask_w[:, None, None, :])
    o_w = jnp.einsum('bqhgk,bkhd->bqhgd', p_w.astype(vw.dtype), vw)
    g = gates.reshape(b, nq, NSA_KV_HEADS, grp, 3)
    o = g[..., 0:1] * o_c + g[..., 1:2] * o_s + g[..., 2:3] * o_w
    return o.reshape(b, nq, NSA_WIDTH)


def s5_mixer(u, h0_re, h0_im, lam_re, lam_im, log_dt, b_re, b_im, c_re, c_im, d_skip, w_glu, b_glu):
    bsz, t, _ = u.shape
    f32 = jnp.float32
    uf = u.astype(f32).reshape(bsz, t, SSM_GROUPS, SSM_CH)
    dt = jnp.exp(log_dt.astype(f32))[:, None]
    lr, li = lam_re.astype(f32), lam_im.astype(f32)
    mag = jnp.exp(lr * dt)
    a_re, a_im = mag * jnp.cos(li * dt), mag * jnp.sin(li * dt)
    den = lr * lr + li * li
    z_re = ((a_re - 1.0) * lr + a_im * li) / den
    z_im = (a_im * lr - (a_re - 1.0) * li) / den
    br, bim = b_re.astype(f32), b_im.astype(f32)
    bb_re = z_re[..., None] * br - z_im[..., None] * bim
    bb_im = z_re[..., None] * bim + z_im[..., None] * br
    x_re = jnp.einsum('btgc,gpc->btgp', uf, bb_re)
    x_im = jnp.einsum('btgc,gpc->btgp', uf, bb_im)
    ar = jnp.broadcast_to(a_re, x_re.shape)
    ai = jnp.broadcast_to(a_im, x_re.shape)

    def combine(e1, e2):
        a1r, a1i, b1r, b1i = e1
        a2r, a2i, b2r, b2i = e2
        return (a2r * a1r - a2i * a1i, a2r * a1i + a2i * a1r,
                a2r * b1r - a2i * b1i + b2r, a2r * b1i + a2i * b1r + b2i)

    pr, pim, sr, si = lax.associative_scan(combine, (ar, ai, x_re, x_im), axis=1)
    h0r = h0_re.astype(f32)[:, None]
    h0i = h0_im.astype(f32)[:, None]
    h_re = pr * h0r - pim * h0i + sr
    h_im = pr * h0i + pim * h0r + si
    y = jnp.einsum('btgp,gcp->btgc', h_re, c_re.astype(f32)) - jnp.einsum('btgp,gcp->btgc', h_im, c_im.astype(f32))
    y = jax.nn.gelu(y.reshape(bsz, t, SSM_WIDTH) + d_skip.astype(f32) * u.astype(f32))
    out = y * jax.nn.sigmoid(y @ w_glu.astype(f32) + b_glu.astype(f32))
    return out.astype(u.dtype), h_re[:, -1].astype(u.dtype), h_im[:, -1].astype(u.dtype)


def moe_ffn(x, router_w, router_b, w_gu, b_gu, w_down, b_down):
    n = x.shape[0]
    router_pad = jnp.pad(router_w, ((0, 0), (0, LANES - N_EXPERTS)))
    logits = pallas_matmul(x, router_pad)[:, :N_EXPERTS] + router_b.astype(jnp.float32)
    top_val, top_idx = lax.top_k(logits, TOP_K)
    gate = jax.nn.softmax(top_val, axis=-1)
    nk = n * TOP_K
    n_items = N_EXPERTS + nk // MOE_ROWS
    flat_e = top_idx.reshape(nk)
    onehot = (flat_e[:, None] == jnp.arange(N_EXPERTS, dtype=flat_e.dtype)[None, :]).astype(jnp.int32)
    running = jnp.cumsum(onehot, axis=0)
    counts = running[-1]
    pos_in_e = jnp.sum(onehot * running, axis=1) - 1
    items_e = (counts + MOE_ROWS - 1) // MOE_ROWS
    item_end = jnp.cumsum(items_e)
    item_start = item_end - items_e
    total = item_end[-1]
    dest = (item_start[flat_e] * MOE_ROWS + pos_in_e).astype(jnp.int32)
    row_tok = jnp.zeros(n_items * MOE_ROWS, jnp.int32).at[dest].set(jnp.arange(nk, dtype=jnp.int32) // TOP_K)
    item = jnp.minimum(jnp.arange(n_items, dtype=jnp.int32), total - 1)
    item_e = jnp.minimum(jnp.searchsorted(item_end, item, side='right'), N_EXPERTS - 1).astype(jnp.int32)
    rows = jnp.clip(counts[item_e] - (item - item_start[item_e]) * MOE_ROWS, 0, MOE_ROWS)
    item_sub = jnp.where(jnp.arange(n_items) < total, (rows + MOE_SUB - 1) // MOE_SUB, 0).astype(jnp.int32)
    half = x.shape[1] // 2
    bits = lax.bitcast_convert_type(x.astype(jnp.bfloat16), jnp.uint16).astype(jnp.uint32)
    xb = ((bits[:, :half] << 16) | bits[:, half:])[row_tok]
    item_ok = (jnp.arange(n_items) < total).astype(jnp.int32)
    yb = moe_experts(item_e, item_sub, item.astype(jnp.int32), item_ok, xb, w_gu, b_gu, w_down, b_down)
    dest_k = dest.reshape(n, TOP_K)
    out = gate[:, 0:1] * yb[dest_k[:, 0]]
    for k in range(1, TOP_K):
        out = out + gate[:, k:k + 1] * yb[dest_k[:, k]]
    return out


MOE_ROWS = 1280
MOE_SUB = 256
MOE_TF = 256


def _moe_body(e_ref, sub_ref, blk_ref, ok_ref, x_ref, wg_ref, wl_ref, bg_ref, bl_ref, wd_ref, bd_ref, o_ref, xs_ref):
    bf16 = jnp.bfloat16
    i, f = pl.program_id(0), pl.program_id(1)
    n_sub = sub_ref[i]
    d = o_ref.shape[1]

    @pl.when(f == 0)
    def _():
        o_ref[...] = jnp.broadcast_to(bd_ref[0], o_ref.shape)

    @pl.when((f == 0) & (n_sub > 0))
    def _():
        for j in range(MOE_ROWS // MOE_SUB):
            rows = slice(j * MOE_SUB, (j + 1) * MOE_SUB)
            w = lax.bitcast_convert_type(x_ref[rows, :], jnp.uint32)
            hi = lax.bitcast_convert_type(w & jnp.uint32(0xFFFF0000), jnp.float32)
            lo = lax.bitcast_convert_type(w << 16, jnp.float32)
            xs_ref[rows, :] = jnp.concatenate([hi, lo], axis=1).astype(bf16)

    def expert_rows(n_rows):
        x = xs_ref[0:n_rows, :]
        hg = jnp.dot(x, wg_ref[0].astype(bf16), preferred_element_type=jnp.float32) + bg_ref[0]
        hl = jnp.dot(x, wl_ref[0].astype(bf16), preferred_element_type=jnp.float32) + bl_ref[0]
        hg = jnp.minimum(hg, SWIGLU_LIMIT)
        hl = jnp.clip(hl, -SWIGLU_LIMIT, SWIGLU_LIMIT)
        act = (hg * jax.nn.sigmoid(SWIGLU_ALPHA * hg) * (hl + 1.0)).astype(bf16)
        for c in range(d // MOE_TF):
            cols = slice(c * MOE_TF, (c + 1) * MOE_TF)
            o_ref[0:n_rows, cols] += jnp.dot(act, wd_ref[0, :, cols].astype(bf16), preferred_element_type=jnp.float32)

    @pl.when(n_sub == 1)
    def _():
        expert_rows(MOE_SUB)

    @pl.when(n_sub > 1)
    def _():
        expert_rows(MOE_ROWS)


def moe_experts(item_e, item_sub, item_blk, item_ok, xb, w_gu, b_gu, w_down, b_down):
    n_items = item_e.shape[0]
    n_f = D_FF // MOE_TF
    d = 2 * xb.shape[1]

    def col(f, ok, i):
        return f * ok[i] + (n_f - 1) * (1 - ok[i])

    return pl.pallas_call(
        _moe_body,
        grid_spec=pltpu.PrefetchScalarGridSpec(
            num_scalar_prefetch=4,
            grid=(n_items, n_f),
            in_specs=[pl.BlockSpec((MOE_ROWS, d // 2), lambda i, f, e, s, b, ok: (b[i], 0)),
                      pl.BlockSpec((1, d, MOE_TF), lambda i, f, e, s, b, ok: (e[i], 0, col(f, ok, i))),
                      pl.BlockSpec((1, d, MOE_TF), lambda i, f, e, s, b, ok: (e[i], 0, n_f + col(f, ok, i))),
                      pl.BlockSpec((1, 1, MOE_TF), lambda i, f, e, s, b, ok: (e[i], 0, col(f, ok, i))),
                      pl.BlockSpec((1, 1, MOE_TF), lambda i, f, e, s, b, ok: (e[i], 0, n_f + col(f, ok, i))),
                      pl.BlockSpec((1, MOE_TF, d), lambda i, f, e, s, b, ok: (e[i], col(f, ok, i), 0)),
                      pl.BlockSpec((1, 1, d), lambda i, f, e, s, b, ok: (e[i], 0, 0))],
            out_specs=pl.BlockSpec((MOE_ROWS, d), lambda i, f, e, s, b, ok: (i, 0)),
            scratch_shapes=[pltpu.VMEM((MOE_ROWS, d), jnp.bfloat16)],
        ),
        out_shape=jax.ShapeDtypeStruct((n_items * MOE_ROWS, d), jnp.float32),
        compiler_params=pltpu.CompilerParams(vmem_limit_bytes=VMEM_LIMIT),
        name="moe_experts",
    )(item_e, item_sub, item_blk, item_ok, xb, w_gu, w_gu, b_gu[:, None, :], b_gu[:, None, :], w_down, b_down[:, None, :])


def layer_forward(x, pos0, past_kv, win_buf, h0_re, h0_im, rel_bias, lw):
    b, t, _ = x.shape
    h = rms_norm(x, lw['norm_mix'])
    proj = h @ lw['w_in']
    o1 = NSA_WIDTH
    o2 = o1 + N_KV * KV_WIDTH
    o3 = o2 + 3 * NSA_HEADS
    q = proj[..., :o1].reshape(b, t, NSA_HEADS, HEAD_DIM)
    kv_new = proj[..., o1:o2].reshape(b, t, N_KV, NSA_KV_HEADS, HEAD_DIM)
    gates = jax.nn.sigmoid(proj[..., o2:o3].astype(jnp.float32)).reshape(b, t, NSA_HEADS, 3)
    u = proj[..., o3:]
    paged_new = kv_new[:, :, :N_PAGED]
    win_new = kv_new[:, :, N_PAGED:]
    qpos = pos0 + jnp.arange(t, dtype=jnp.int32)

    n_rows = pos0 + t
    ns = -(-n_rows // SEL_BLOCK)
    pad = jnp.zeros((b, ns * SEL_BLOCK - n_rows) + paged_new.shape[2:], paged_new.dtype)
    rows = jnp.concatenate(([] if past_kv is None else [past_kv]) + [paged_new, pad], axis=1)
    kc = compress_rows(rows[:, :, 0], lw['cmp_pe'][0], lw['cmp_w1'][0], lw['cmp_b1'][0], lw['cmp_w2'][0], lw['cmp_b2'][0], n_rows)
    vc = compress_rows(rows[:, :, 1], lw['cmp_pe'][1], lw['cmp_w1'][1], lw['cmp_b1'][1], lw['cmp_w2'][1], lw['cmp_b2'][1], n_rows)
    nc = kc.shape[1]
    c_start = jnp.arange(nc) * CMP_STRIDE
    c_end = c_start + CMP_BLOCK - 1
    s_start = jnp.arange(ns) * SEL_BLOCK
    cover = ((c_start[:, None] < s_start[None, :] + SEL_BLOCK) & (c_start[:, None] + CMP_BLOCK > s_start[None, :])).astype(jnp.float32)
    ks_b = rows[:, :, 2].reshape(b, ns, SEL_BLOCK, NSA_KV_HEADS, HEAD_DIM)
    vs_b = rows[:, :, 3].reshape(b, ns, SEL_BLOCK, NSA_KV_HEADS, HEAD_DIM)

    if win_buf is None:
        new_win = win_new[:, t - min(WINDOW, n_rows):]
        tr = lambda a: jnp.transpose(a, (0, 2, 1, 3))
        pad_c = lambda a: tr(jnp.pad(a, ((0, 0), (0, LANES - nc), (0, 0), (0, 0))))
        g4 = tr(proj[..., o2:o3].reshape(b, t, NSA_HEADS, 3))
        attn4 = nsa_prompt(tr(q), g4, pad_c(kc), pad_c(vc), tr(kv_new[:, :, 2]), tr(kv_new[:, :, 3]),
                           tr(kv_new[:, :, 4]), tr(kv_new[:, :, 5]), bucket_thresholds(), rel_bias, nc)
        attn = tr(attn4).reshape(b, t, NSA_WIDTH).astype(x.dtype)
        return _layer_tail(x, attn, u, h0_re, h0_im, lw, paged_new, new_win, True)
    if win_buf is None:
        qb = min(Q_BLOCK, t)
        nb = t // qb
        wpad = jnp.concatenate([jnp.zeros((b, WINDOW) + win_new.shape[2:], win_new.dtype), win_new], axis=1)
        widx = jnp.arange(nb)[:, None] * qb + jnp.arange(qb + WINDOW)[None, :]
        w_blocks = jnp.moveaxis(wpad[:, widx], 1, 0)
        w_pos = widx - WINDOW
        w_hist = win_new
    else:
        qb, nb = t, 1
        w_hist = jnp.concatenate([win_buf, win_new], axis=1)
        w_blocks = w_hist[None]
        w_pos = (pos0 - win_buf.shape[1] + jnp.arange(w_hist.shape[1], dtype=jnp.int32))[None]
    new_win = w_hist[:, w_hist.shape[1] - min(WINDOW, n_rows):]

    def run(args):
        q_b, qpos_b, g_b, w_b, wpos_b = args
        return nsa_query_block(q_b, qpos_b, g_b, w_b[:, :, 0], w_b[:, :, 1], wpos_b,
                               kc, vc, c_end, ks_b, vs_b, cover, rel_bias)

    q_blocks = jnp.moveaxis(q.reshape(b, nb, qb, NSA_HEADS, HEAD_DIM), 1, 0)
    g_blocks = jnp.moveaxis(gates.reshape(b, nb, qb, NSA_HEADS, 3), 1, 0)
    attn = lax.map(run, (q_blocks, qpos.reshape(nb, qb), g_blocks, w_blocks, w_pos))
    attn = jnp.moveaxis(attn, 0, 1).reshape(b, t, NSA_WIDTH).astype(x.dtype)
    return _layer_tail(x, attn, u, h0_re, h0_im, lw, paged_new, new_win, False)


ROW_TILE = 256
PROJ_TILE = 512
PROJ_ROWS = 512


def _rms(v, g):
    return v * lax.rsqrt(jnp.mean(v * v, axis=-1, keepdims=True) + RMS_EPS) * g


def _norm_proj_body(x_ref, g_ref, w_ref, o_ref, h_ref):
    @pl.when(pl.program_id(1) == 0)
    def _():
        h_ref[...] = _rms(x_ref[...], g_ref[...]).astype(jnp.bfloat16)

    o_ref[...] = jnp.dot(h_ref[...], w_ref[...], preferred_element_type=jnp.float32)


def norm_proj(x, gain, w):
    n, d = x.shape
    cols = w.shape[1]
    tm = min(n, PROJ_ROWS)
    return pl.pallas_call(
        _norm_proj_body,
        grid=(n // tm, cols // PROJ_TILE),
        in_specs=[pl.BlockSpec((tm, d), lambda i, j: (i, 0)),
                  pl.BlockSpec((1, d), lambda i, j: (0, 0)),
                  pl.BlockSpec((d, PROJ_TILE), lambda i, j: (0, j))],
        out_specs=pl.BlockSpec((tm, PROJ_TILE), lambda i, j: (i, j)),
        out_shape=jax.ShapeDtypeStruct((n, cols), jnp.float32),
        scratch_shapes=[pltpu.VMEM((tm, d), jnp.bfloat16)],
        compiler_params=pltpu.CompilerParams(vmem_limit_bytes=VMEM_LIMIT),
        name="norm_proj",
    )(x, gain.reshape(1, d), w)


def _mix_out_body(attn_ref, y_ref, x_ref, wglu_ref, bglu_ref, ga_ref, gs_ref, wout_ref, gf_ref, wr_ref,
                  x2_ref, hp_ref, lg_ref):
    f32, bf16 = jnp.float32, jnp.bfloat16
    y = y_ref[...]
    ssm = y * jax.nn.sigmoid(jnp.dot(y.astype(bf16), wglu_ref[...], preferred_element_type=f32) + bglu_ref[...])
    merged = jnp.concatenate([_rms(attn_ref[...], ga_ref[...]), _rms(ssm, gs_ref[...])], axis=1).astype(bf16)
    x2 = x_ref[...] + jnp.dot(merged, wout_ref[...], preferred_element_type=f32)
    x2_ref[...] = x2
    hm = _rms(x2, gf_ref[...]).astype(bf16)
    lg_ref[...] = jnp.dot(hm, wr_ref[...], preferred_element_type=f32)
    bits = lax.bitcast_convert_type(hm.astype(f32), jnp.uint32)
    half = bits.shape[1] // 2
    hp_ref[...] = lax.bitcast_convert_type(bits[:, :half] | (bits[:, half:] >> 16), f32)


def mix_out(attn, y, x, lw):
    n = x.shape[0]
    f32, bf16 = jnp.float32, jnp.bfloat16
    row = lambda a: a.astype(f32).reshape(1, -1)
    wr = jnp.pad(lw['router_w'], ((0, 0), (0, LANES - N_EXPERTS))).astype(bf16)
    consts = [lw['w_glu'].astype(bf16), row(lw['b_glu']), row(lw['norm_attn_out']), row(lw['norm_ssm_out']),
              lw['w_out'].astype(bf16), row(lw['norm_ffn']), wr]
    tile = lambda c: pl.BlockSpec((ROW_TILE, c), lambda i: (i, 0))
    full = lambda a: pl.BlockSpec(a.shape, lambda i: (0, 0))
    return pl.pallas_call(
        _mix_out_body,
        grid=(n // ROW_TILE,),
        in_specs=[tile(NSA_WIDTH), tile(SSM_WIDTH), tile(D_MODEL)] + [full(c) for c in consts],
        out_specs=[tile(D_MODEL), tile(D_MODEL // 2), tile(LANES)],
        out_shape=[jax.ShapeDtypeStruct((n, D_MODEL), f32), jax.ShapeDtypeStruct((n, D_MODEL // 2), f32),
                   jax.ShapeDtypeStruct((n, LANES), f32)],
        compiler_params=pltpu.CompilerParams(vmem_limit_bytes=VMEM_LIMIT),
        name="mix_out",
    )(attn, y, x, *consts)


def _combine_body(x_ref, y0_ref, y1_ref, y2_ref, y3_ref, gate_ref, g_ref, o_ref, *, normed):
    gate = gate_ref[...]
    moe = gate[:, 0:1] * y0_ref[...]
    for k, y_ref in enumerate((y1_ref, y2_ref, y3_ref), start=1):
        moe = moe + gate[:, k:k + 1] * y_ref[...]
    out = x_ref[...] + moe
    o_ref[...] = _rms(out, g_ref[...]) if normed else out


def combine_norm(x, ys, gate, gain):
    n, d = x.shape
    normed = gain is not None
    gain = jnp.ones((d,), jnp.float32) if gain is None else gain
    tile = lambda c: pl.BlockSpec((ROW_TILE, c), lambda i: (i, 0))
    return pl.pallas_call(
        functools.partial(_combine_body, normed=normed),
        grid=(n // ROW_TILE,),
        in_specs=[tile(d)] * (1 + TOP_K) + [tile(TOP_K), pl.BlockSpec((1, d), lambda i: (0, 0))],
        out_specs=tile(d),
        out_shape=jax.ShapeDtypeStruct((n, d), jnp.float32),
        name="combine_norm",
    )(x, *ys, gate, gain.astype(jnp.float32).reshape(1, d))


def moe_routed(hp, logits, router_b, w_gu, b_gu, w_down, b_down):
    n = hp.shape[0]
    top_val, top_idx = lax.top_k(logits[:, :N_EXPERTS] + router_b.astype(jnp.float32), TOP_K)
    gate = jax.nn.softmax(top_val, axis=-1)
    nk = n * TOP_K
    n_items = N_EXPERTS + nk // MOE_ROWS
    flat_e = top_idx.reshape(nk)
    onehot = (flat_e[:, None] == jnp.arange(N_EXPERTS, dtype=flat_e.dtype)[None, :]).astype(jnp.int32)
    running = jnp.cumsum(onehot, axis=0)
    counts = running[-1]
    pos_in_e = jnp.sum(onehot * running, axis=1) - 1
    items_e = (counts + MOE_ROWS - 1) // MOE_ROWS
    item_end = jnp.cumsum(items_e)
    item_start = item_end - items_e
    total = item_end[-1]
    dest = (item_start[flat_e] * MOE_ROWS + pos_in_e).astype(jnp.int32)
    row_tok = jnp.zeros(n_items * MOE_ROWS, jnp.int32).at[dest].set(jnp.arange(nk, dtype=jnp.int32) // TOP_K)
    item = jnp.minimum(jnp.arange(n_items, dtype=jnp.int32), total - 1)
    item_e = jnp.minimum(jnp.searchsorted(item_end, item, side='right'), N_EXPERTS - 1).astype(jnp.int32)
    rows = jnp.clip(counts[item_e] - (item - item_start[item_e]) * MOE_ROWS, 0, MOE_ROWS)
    item_sub = jnp.where(jnp.arange(n_items) < total, (rows + MOE_SUB - 1) // MOE_SUB, 0).astype(jnp.int32)
    item_ok = (jnp.arange(n_items) < total).astype(jnp.int32)
    yb = moe_experts(item_e, item_sub, item.astype(jnp.int32), item_ok, hp[row_tok], w_gu, b_gu, w_down, b_down)
    return yb, dest.reshape(n, TOP_K), gate


def mixer_layer(x, pos0, pool, page_table, win_buf, h0_re, h0_im, rel_bias, lw):
    b, t, _ = x.shape
    w_in = jnp.pad(lw['w_in'], ((0, 0), (0, -IN_WIDTH % PROJ_TILE))).astype(jnp.bfloat16)
    proj = norm_proj(x.reshape(b * t, D_MODEL), lw['norm_mix'].astype(jnp.float32), w_in).reshape(b, t, -1)
    o1 = NSA_WIDTH
    o2 = o1 + N_KV * KV_WIDTH
    o3 = o2 + 3 * NSA_HEADS
    q = proj[..., :o1].reshape(b, t, NSA_HEADS, HEAD_DIM)
    kv_new = proj[..., o1:o2].reshape(b, t, N_KV, NSA_KV_HEADS, HEAD_DIM)
    u = proj[..., o3:IN_WIDTH]
    paged_new = kv_new[:, :, :N_PAGED]
    win_new = kv_new[:, :, N_PAGED:]
    n_rows = pos0 + t
    n_cmp = (n_rows - CMP_BLOCK) // CMP_STRIDE + 1
    thr = bucket_thresholds()
    cmp_w = (lw['cmp_pe'], lw['cmp_w1'], lw['cmp_b1'], lw['cmp_w2'], lw['cmp_b2'])
    if pool is None:
        assert pos0 == 0 and t % 128 == 0
        pages = t // 128
        own_pool = paged_new.reshape(b * pages, 128, ROW_LANES)
        own_table = jnp.arange(b * pages, dtype=jnp.int32).reshape(b, pages)
        kcvc = compress_pages(own_pool, own_table, *cmp_w, n_g=pages, transposed=False)
        attn = nsa_prefill(q, proj[..., o2:o3], kcvc, kv_new, thr, rel_bias, n_cmp).astype(x.dtype)
        w_hist = win_new
    else:
        kcvc = compress_pages(pool, page_table, *cmp_w, n_g=CMP_PAGES, transposed=True)
        attn = nsa_decode(q, proj[..., o2:o3], kv_new, kcvc, pool, page_table, win_buf, thr, rel_bias, pos0).astype(x.dtype)
        w_hist = jnp.concatenate([win_buf, win_new], axis=1)
    new_win = w_hist[:, w_hist.shape[1] - min(WINDOW, n_rows):]
    y, h_re, h_im = s5_branch(u, h0_re, h0_im, lw, pool is None)
    x2, hp, logits = mix_out(attn.reshape(b * t, NSA_WIDTH), y.reshape(b * t, SSM_WIDTH), x.reshape(b * t, D_MODEL), lw)
    return (x2, hp, logits), paged_new, new_win, h_re, h_im


def s5_branch(u, h0_re, h0_im, lw, chained):
    bsz, t, _ = u.shape
    f32 = jnp.float32
    a, bcat, ccat = s5_discretize(lw['lam_re'], lw['lam_im'], lw['log_dt'], lw['b_re'], lw['b_im'], lw['c_re'], lw['c_im'])
    d_skip = lw['d_skip'].astype(f32)
    flat = lambda h: h.astype(f32).reshape(bsz, SSM_GROUPS * SSM_STATE)
    if chained:
        n_r = t // S5_CHUNK
        u4 = jnp.transpose(u.astype(f32).reshape(bsz, n_r, S5_CHUNK, SSM_WIDTH), (0, 2, 1, 3))
        y4, h_re, h_im = s5_scan(u4, flat(h0_re)[:, None], flat(h0_im)[:, None], a, bcat, ccat, d_skip, True)
        y = jnp.transpose(y4, (0, 2, 1, 3)).reshape(bsz, t, SSM_WIDTH)
        h_re, h_im = h_re[:, 0], h_im[:, 0]
    else:
        u4 = jnp.transpose(u.astype(f32), (1, 0, 2))[None]
        y4, h_re, h_im = s5_scan(u4, flat(h0_re)[None], flat(h0_im)[None], a, bcat, ccat, d_skip, False)
        y = jnp.transpose(y4[0], (1, 0, 2))
        h_re, h_im = h_re[0], h_im[0]
    st = lambda h: h.reshape(bsz, SSM_GROUPS, SSM_STATE).astype(u.dtype)
    return y, st(h_re), st(h_im)


def moe_residual(groups, lw, final_gain):
    sizes = [g[0].shape[0] for g in groups]
    hp = jnp.concatenate([g[1] for g in groups], axis=0)
    logits = jnp.concatenate([g[2] for g in groups], axis=0)
    yb, dest, gate = moe_routed(hp, logits, lw['router_b'], lw['w_gu'], lw['b_gu'], lw['w_down'], lw['b_down'])
    outs, start = [], 0
    for (x2, _, _), n in zip(groups, sizes):
        rows = slice(start, start + n)
        outs.append(combine_norm(x2, [yb[dest[rows, k]] for k in range(TOP_K)], gate[rows], final_gain))
        start += n
    return outs


def kernel(x_prompt, x_sample, cache_nsa_kv, cache_win_kv, state_ssm_re, state_ssm_im, page_table, rel_bias,
           norm_mix, w_in, cmp_pe, cmp_w1, cmp_b1, cmp_w2, cmp_b2, ssm_lam_re, ssm_lam_im, ssm_log_dt,
           ssm_b_re, ssm_b_im, ssm_c_re, ssm_c_im, ssm_d, ssm_w_glu, ssm_b_glu, norm_attn_out, norm_ssm_out,
           w_out, norm_ffn, router_w, router_b, w_gu, b_gu, w_down, b_down, norm_final):
    n_seq, n_pages = page_table.shape
    past_len = n_pages * cache_nsa_kv.shape[2]
    xp, xs = x_prompt, x_sample
    kv_p, win_p, sre_p, sim_p = [], [], [], []
    kv_s, win_s, sre_s, sim_s = [], [], [], []
    for i in range(DEPTH):
        lw = dict(norm_mix=norm_mix[i], w_in=w_in[i], cmp_pe=cmp_pe[i], cmp_w1=cmp_w1[i], cmp_b1=cmp_b1[i],
                  cmp_w2=cmp_w2[i], cmp_b2=cmp_b2[i], lam_re=ssm_lam_re[i], lam_im=ssm_lam_im[i],
                  log_dt=ssm_log_dt[i], b_re=ssm_b_re[i], b_im=ssm_b_im[i], c_re=ssm_c_re[i], c_im=ssm_c_im[i],
                  d_skip=ssm_d[i], w_glu=ssm_w_glu[i], b_glu=ssm_b_glu[i], norm_attn_out=norm_attn_out[i],
                  norm_ssm_out=norm_ssm_out[i], w_out=w_out[i], norm_ffn=norm_ffn[i], router_w=router_w[i],
                  router_b=router_b[i], w_gu=w_gu[i], b_gu=b_gu[i], w_down=w_down[i], b_down=b_down[i])
        h0 = jnp.zeros((xp.shape[0], SSM_GROUPS, SSM_STATE), xp.dtype)
        gp, kv1, w1, r1, m1 = mixer_layer(xp, 0, None, None, None, h0, h0, rel_bias, lw)
        pool = jnp.transpose(cache_nsa_kv[i].reshape(cache_nsa_kv.shape[1], cache_nsa_kv.shape[2], ROW_LANES), (0, 2, 1))
        gs, kv2, w2, r2, m2 = mixer_layer(xs, past_len, pool, page_table, cache_win_kv[i], state_ssm_re[i], state_ssm_im[i], rel_bias, lw)
        op, os_ = moe_residual([gp, gs], lw, norm_final if i == DEPTH - 1 else None)
        xp, xs = op.reshape(xp.shape), os_.reshape(xs.shape)
        kv_p.append(kv1); win_p.append(w1); sre_p.append(r1); sim_p.append(m1)
        kv_s.append(kv2); win_s.append(w2); sre_s.append(r2); sim_s.append(m2)
    return (xp, xs, jnp.stack(kv_p), jnp.stack(win_p), jnp.stack(sre_p), jnp.stack(sim_p),
            jnp.stack(kv_s), jnp.stack(win_s), jnp.stack(sre_s), jnp.stack(sim_s))
```

```python
import functools
import math
import jax, jax.numpy as jnp
from jax import lax
from jax.experimental import pallas as pl
from jax.experimental.pallas import tpu as pltpu

D_MODEL = 2048
DEPTH = 1
NSA_HEADS = 16
NSA_KV_HEADS = 2
HEAD_DIM = 64
NSA_WIDTH = NSA_HEADS * HEAD_DIM
KV_WIDTH = NSA_KV_HEADS * HEAD_DIM
N_PAGED = 4
N_KV = 6
SSM_WIDTH = D_MODEL - NSA_WIDTH
SSM_CH = 16
SSM_GROUPS = SSM_WIDTH // SSM_CH
SSM_STATE = 64
IN_WIDTH = NSA_WIDTH + N_KV * KV_WIDTH + 3 * NSA_HEADS + SSM_WIDTH
CMP_BLOCK = 32
CMP_STRIDE = 16
CMP_HIDDEN = 2 * HEAD_DIM
SEL_BLOCK = 64
SEL_TOP_N = 16
WINDOW = 512
Q_BLOCK = 64
FORCE_SCORE = 1e6
REL_BUCKETS = 32
REL_MAX_DIST = 4096
N_EXPERTS = 32
TOP_K = 4
D_FF = D_MODEL
SWIGLU_LIMIT = 7.0
SWIGLU_ALPHA = 1.702
RMS_EPS = 1e-5
NEG_INF = -1e30


def _mm_body(x_ref, w_ref, o_ref):
    o_ref[...] = jnp.dot(x_ref[...].astype(jnp.bfloat16), w_ref[...].astype(jnp.bfloat16),
                         preferred_element_type=jnp.float32)


def pallas_matmul(x, w, tm=256, tn=512):
    m, k = x.shape
    n = w.shape[1]
    tm = min(tm, m)
    tn = min(tn, n)
    return pl.pallas_call(
        _mm_body,
        grid=(m // tm, n // tn),
        in_specs=[pl.BlockSpec((tm, k), lambda i, j: (i, 0)),
                  pl.BlockSpec((k, tn), lambda i, j: (0, j))],
        out_specs=pl.BlockSpec((tm, tn), lambda i, j: (i, j)),
        out_shape=jax.ShapeDtypeStruct((m, n), jnp.float32),
        name="matmul",
    )(x, w)


GRP = NSA_HEADS // NSA_KV_HEADS
TQ = 128
TK = 128
KEY_TILES = 2
LANES = 128
BUCKET_TABLE_LEN = 32768
BIAS_TILE_ELEMS = 16384
VMEM_LIMIT = 56 * 1024 * 1024


def rel_bucket(dist):
    n = jnp.maximum(dist, 0)
    exact = REL_BUCKETS // 2
    nf = jnp.maximum(n, 1).astype(jnp.float32)
    large = exact + (jnp.log(nf / exact) / math.log(REL_MAX_DIST / exact) * (REL_BUCKETS - exact)).astype(jnp.int32)
    return jnp.where(n < exact, n, jnp.minimum(large, REL_BUCKETS - 1))


def bucket_thresholds():
    tab = rel_bucket(jnp.arange(BUCKET_TABLE_LEN, dtype=jnp.int32))
    return jnp.sum(tab[None, :] < jnp.arange(REL_BUCKETS, dtype=jnp.int32)[:, None], axis=1).astype(jnp.int32)


def _bias_table_body(thr_ref, rb_ref, d_ref, o_ref):
    n = jnp.maximum(d_ref[0], 0)
    for h in range(NSA_HEADS):
        val = jnp.full(n.shape, rb_ref[h], jnp.float32)
        for k in range(1, REL_BUCKETS):
            val = jnp.where(n >= thr_ref[k], rb_ref[k * NSA_HEADS + h], val)
        o_ref[h // GRP, 0, h % GRP] = val


def bias_tables(dist, thr, rel_bias):
    n, r, c = dist.shape
    ct = min(c, BIAS_TILE_ELEMS // r)
    assert c % ct == 0
    return pl.pallas_call(
        _bias_table_body,
        grid_spec=pltpu.PrefetchScalarGridSpec(
            num_scalar_prefetch=2,
            grid=(n, c // ct),
            in_specs=[pl.BlockSpec((1, r, ct), lambda i, j, *_: (i, 0, j))],
            out_specs=pl.BlockSpec((NSA_KV_HEADS, 1, GRP, r, ct), lambda i, j, *_: (0, i, 0, 0, j)),
        ),
        out_shape=jax.ShapeDtypeStruct((NSA_KV_HEADS, n, GRP, r, c), jnp.float32),
        name="bias_tables",
    )(thr, rel_bias.reshape(-1), dist)


def _nsa_prefill_body(q_ref, g_ref, kc_ref, vct_ref, ks_ref, vst_ref, kw_ref, vwt_ref, bt_ref, bc_ref, cov_ref, exp_ref,
                      o_ref, mask_ref, *, n_cmp, n_sel):
    f32, bf16 = jnp.float32, jnp.bfloat16
    qi = pl.program_id(2)
    nt_dims = (((1,), (1,)), ((), ()))
    q = (q_ref[0] * (HEAD_DIM ** -0.5)).reshape(GRP * TQ, HEAD_DIM).astype(bf16)
    lanes = [slice(g * TQ, (g + 1) * TQ) for g in range(GRP)]
    row = lax.broadcasted_iota(jnp.int32, (LANES, TQ), 0)
    qpos = qi * TQ + lax.broadcasted_iota(jnp.int32, (LANES, TQ), 1)

    ok_c = (qpos >= CMP_STRIDE * row + (CMP_BLOCK - 1)) & (row < n_cmp)
    s_c = lax.dot_general(kc_ref[0, 0].astype(bf16), q, nt_dims, preferred_element_type=f32)
    p_sum = jnp.zeros((LANES, TQ), f32)
    p_parts = []
    for g in range(GRP):
        p = _masked_softmax_rows(s_c[:, lanes[g]] + bc_ref[0, 0, g], ok_c)
        p_sum = p_sum + p
        p_parts.append(p.astype(bf16))
    o_c = jnp.dot(vct_ref[0, 0].astype(bf16), jnp.concatenate(p_parts, axis=1), preferred_element_type=f32)

    imp = jnp.dot(cov_ref[...], p_sum.astype(bf16), preferred_element_type=f32)
    rs = -(-n_sel // 8) * 8
    blk = lax.broadcasted_iota(jnp.int32, (rs, TQ), 0)
    qp = qi * TQ + lax.broadcasted_iota(jnp.int32, (rs, TQ), 1)
    cur = qp // SEL_BLOCK
    forced = (blk == 0) | (blk == cur) | (blk == cur - 1)
    score = jnp.where(forced, FORCE_SCORE, jnp.where(blk * SEL_BLOCK <= qp, imp[:rs], -1.0))
    score = jnp.where(blk < n_sel, score, -jnp.inf)
    sel = jnp.zeros((rs, TQ), f32)
    for s in range(n_sel):
        r = jnp.max(jnp.where(blk == s, score, -jnp.inf), axis=0, keepdims=True)
        beats = (score > r) | ((score == r) & (blk < s))
        rank = jnp.sum(jnp.where(beats, 1.0, 0.0), axis=0, keepdims=True)
        sel = jnp.where((blk == s) & (rank < min(SEL_TOP_N, n_sel)), 1.0, sel)
    sel = jnp.concatenate([sel, jnp.zeros((LANES - rs, TQ), f32)], axis=0)
    mask_ref[...] = jnp.dot(exp_ref[...], sel.astype(bf16), preferred_element_type=f32)

    key_i = lax.broadcasted_iota(jnp.int32, (KEY_TILES * TK, TQ), 0)
    qry_i = lax.broadcasted_iota(jnp.int32, (KEY_TILES * TK, TQ), 1)
    init = (jnp.full((1, GRP * TQ), NEG_INF, f32), jnp.zeros((1, GRP * TQ), f32), jnp.zeros((HEAD_DIM, GRP * TQ), f32))
    n_diag = bt_ref.shape[1]

    def tile(kk, carry, k_ref, vt_ref, msk):
        m, l, acc = carry
        keys = pl.ds(pl.multiple_of(kk * (KEY_TILES * TK), KEY_TILES * TK), KEY_TILES * TK)
        s_all = lax.dot_general(k_ref[0, 0, keys, :].astype(bf16), q, nt_dims, preferred_element_type=f32)
        diag = [jnp.clip(qi - (KEY_TILES * kk + j), 0, n_diag - 1) for j in range(KEY_TILES)]
        m_parts, l_parts, a_parts, p_parts = [], [], [], []
        for g in range(GRP):
            bias = jnp.concatenate([bt_ref[0, dg, g] for dg in diag], axis=0)
            s = jnp.where(msk, s_all[:, lanes[g]] + bias, NEG_INF)
            m_new = jnp.maximum(m[:, lanes[g]], jnp.max(s, axis=0, keepdims=True))
            p = jnp.where(msk, jnp.exp(s - m_new), 0.0)
            alpha = jnp.exp(m[:, lanes[g]] - m_new)
            m_parts.append(m_new)
            a_parts.append(alpha)
            l_parts.append(alpha * l[:, lanes[g]] + jnp.sum(p, axis=0, keepdims=True))
            p_parts.append(p.astype(bf16))
        pv = jnp.dot(vt_ref[0, 0, :, keys].astype(bf16), jnp.concatenate(p_parts, axis=1), preferred_element_type=f32)
        return (jnp.concatenate(m_parts, axis=1), jnp.concatenate(l_parts, axis=1),
                jnp.concatenate(a_parts, axis=1) * acc + pv)

    def sel_step(kk, carry):
        dist = qi * TQ + qry_i - (kk * (KEY_TILES * TK) + key_i)
        rows = pl.ds(pl.multiple_of(kk * (KEY_TILES * TK), KEY_TILES * TK), KEY_TILES * TK)
        return tile(kk, carry, ks_ref, vst_ref, (mask_ref[rows, :] > 0.5) & (dist >= 0))

    def win_step(kk, carry):
        dist = qi * TQ + qry_i - (kk * (KEY_TILES * TK) + key_i)
        return tile(kk, carry, kw_ref, vwt_ref, (dist >= 0) & (dist < WINDOW))

    last = qi // KEY_TILES + 1
    _, l_s, acc_s = lax.fori_loop(0, last, sel_step, init)
    _, l_w, acc_w = lax.fori_loop(jnp.maximum(qi - WINDOW // TK, 0) // KEY_TILES, last, win_step, init)
    o_s = acc_s / jnp.maximum(l_s, 1e-30)
    o_w = acc_w / jnp.maximum(l_w, 1e-30)
    for g in range(GRP):
        gate = jax.nn.sigmoid(g_ref[0, 0, :, g, :])
        o_ref[0, g] = gate[0:1] * o_c[:, lanes[g]] + gate[1:2] * o_s[:, lanes[g]] + gate[2:3] * o_w[:, lanes[g]]


def _masked_softmax_rows(s, ok):
    s = jnp.where(ok, s, NEG_INF)
    e = jnp.where(ok, jnp.exp(s - jnp.max(s, axis=0, keepdims=True)), 0.0)
    return e / jnp.maximum(jnp.sum(e, axis=0, keepdims=True), 1e-30)


def nsa_prefill(q, glog, kcvc, kv_new, thr, rel_bias, n_cmp):
    f32, bf16 = jnp.float32, jnp.bfloat16
    b, t = q.shape[:2]
    nq = t // TQ
    n_sel = t // SEL_BLOCK
    assert t % (KEY_TILES * TK) == 0 and n_sel <= LANES and n_cmp < LANES and kcvc.shape[1] == LANES and TQ == TK == LANES
    q4 = jnp.transpose(q, (0, 2, 1, 3))
    g5 = jnp.transpose(glog.reshape(b, t, NSA_KV_HEADS, GRP, 3), (0, 2, 4, 3, 1))
    cmp_tok = jnp.pad(kcvc[:, 1:], ((0, 0), (0, 1), (0, 0))).reshape(b, LANES, 2, NSA_KV_HEADS, HEAD_DIM)
    kc = jnp.transpose(cmp_tok[:, :, 0], (0, 2, 1, 3))
    vct = jnp.transpose(cmp_tok[:, :, 1], (0, 2, 3, 1))
    rows = lambda n: jnp.transpose(kv_new[:, :, n], (0, 2, 1, 3))
    cols = lambda n: jnp.transpose(kv_new[:, :, n], (0, 2, 3, 1))
    i = jnp.arange(TQ, dtype=jnp.int32)
    d_toep = jnp.arange(nq, dtype=jnp.int32)[:, None, None] * TQ + i[None, None, :] - i[None, :, None]
    c_end = CMP_STRIDE * jnp.arange(LANES, dtype=jnp.int32) + CMP_BLOCK - 1
    d_cmp = jnp.arange(t, dtype=jnp.int32).reshape(nq, 1, TQ) - c_end[None, :, None]
    tabs = bias_tables(jnp.concatenate([d_toep, d_cmp], axis=0), thr, rel_bias)
    bt, bc = tabs[:, :nq], tabs[:, nq:]
    s_start = SEL_BLOCK * jnp.arange(LANES)[:, None]
    c_start = CMP_STRIDE * jnp.arange(LANES)[None, :]
    cover = ((c_start < s_start + SEL_BLOCK) & (c_start + CMP_BLOCK > s_start)
             & (jnp.arange(LANES)[None, :] < n_cmp) & (jnp.arange(LANES)[:, None] < n_sel)).astype(bf16)
    expand = (jnp.arange(t)[:, None] // SEL_BLOCK == jnp.arange(LANES)[None, :]).astype(bf16)
    spec = lambda *blk: pl.BlockSpec((1, 1) + blk, lambda h, bb, qq: (bb, h) + (0,) * len(blk))
    out = pl.pallas_call(
        functools.partial(_nsa_prefill_body, n_cmp=n_cmp, n_sel=n_sel),
        grid=(NSA_KV_HEADS, b, nq),
        in_specs=[pl.BlockSpec((1, GRP, TQ, HEAD_DIM), lambda h, bb, qq: (bb, h, qq, 0)),
                  pl.BlockSpec((1, 1, 3, GRP, TQ), lambda h, bb, qq: (bb, h, 0, 0, qq)),
                  spec(LANES, HEAD_DIM), spec(HEAD_DIM, LANES),
                  spec(t, HEAD_DIM), spec(HEAD_DIM, t), spec(t, HEAD_DIM), spec(HEAD_DIM, t),
                  pl.BlockSpec((1, nq, GRP, TK, TQ), lambda h, bb, qq: (h, 0, 0, 0, 0)),
                  pl.BlockSpec((1, 1, GRP, LANES, TQ), lambda h, bb, qq: (h, qq, 0, 0, 0)),
                  pl.BlockSpec((LANES, LANES), lambda h, bb, qq: (0, 0)),
                  pl.BlockSpec((t, LANES), lambda h, bb, qq: (0, 0))],
        out_specs=pl.BlockSpec((1, GRP, HEAD_DIM, TQ), lambda h, bb, qq: (bb, h, 0, qq)),
        out_shape=jax.ShapeDtypeStruct((b, NSA_HEADS, HEAD_DIM, t), f32),
        scratch_shapes=[pltpu.VMEM((t, TQ), f32)],
        compiler_params=pltpu.CompilerParams(vmem_limit_bytes=VMEM_LIMIT),
        name="nsa_prefill",
    )(q4, g5, kc, vct, rows(2), cols(3), rows(4), cols(5), bt, bc, cover, expand)
    return jnp.transpose(out, (0, 3, 1, 2)).reshape(b, t, NSA_WIDTH)


S5_GB = 8
S5_CH = S5_GB * SSM_CH
S5_ST = S5_GB * SSM_STATE


def _cmul(a_re, a_im, b_re, b_im):
    return a_re * b_re - a_im * b_im, a_re * b_im + a_im * b_re


def _s5_body(u_ref, h0re_ref, h0im_ref, a_ref, bcat_ref, ccat_ref, d_ref, y_ref, hre_ref, him_ref, xs_ref, hin_ref,
             *, chained):
    f32, bf16 = jnp.float32, jnp.bfloat16
    n_l, n_r = u_ref.shape[1], u_ref.shape[2]
    u2 = u_ref[0].reshape(n_l * n_r, S5_CH)
    xs_ref[...] = jnp.dot(u2.astype(bf16), bcat_ref[0].astype(bf16),
                          preferred_element_type=f32).reshape(n_l, n_r, 2 * S5_ST)
    a_re, a_im = a_ref[0:1, :], a_ref[1:2, :]

    def scan_step(j, carry):
        h_re, h_im, p_re, p_im = carry
        x = xs_ref[j]
        t_re, t_im = _cmul(a_re, a_im, h_re, h_im)
        h_re, h_im = t_re + x[:, :S5_ST], t_im + x[:, S5_ST:]
        xs_ref[j] = jnp.concatenate([h_re, h_im], axis=1)
        return (h_re, h_im) + _cmul(a_re, a_im, p_re, p_im)

    if chained:
        start = (jnp.zeros((n_r, S5_ST), f32), jnp.zeros((n_r, S5_ST), f32))
    else:
        start = (h0re_ref[0], h0im_ref[0])
    ones = (jnp.ones((1, S5_ST), f32), jnp.zeros((1, S5_ST), f32))
    h_re, h_im, al_re, al_im = lax.fori_loop(0, n_l, scan_step, start + ones)

    if chained:
        hin_ref[0:1, :] = jnp.concatenate([h0re_ref[0], h0im_ref[0]], axis=1)

        def chain_step(c, carry):
            z = xs_ref[n_l - 1, pl.ds(c - 1, 1), :]
            t_re, t_im = _cmul(al_re, al_im, *carry)
            n_re, n_im = t_re + z[:, :S5_ST], t_im + z[:, S5_ST:]
            hin_ref[pl.ds(c, 1), :] = jnp.concatenate([n_re, n_im], axis=1)
            return n_re, n_im

        h_re, h_im = lax.fori_loop(1, n_r + 1, chain_step, (h0re_ref[0], h0im_ref[0]))

        def fix_step(j, carry):
            p_re, p_im = carry
            hin = hin_ref[0:n_r, :]
            t_re, t_im = _cmul(p_re, p_im, hin[:, :S5_ST], hin[:, S5_ST:])
            xs_ref[j] = xs_ref[j] + jnp.concatenate([t_re, t_im], axis=1)
            return _cmul(a_re, a_im, p_re, p_im)

        lax.fori_loop(0, n_l, fix_step, (a_re, a_im))

    hre_ref[0] = h_re
    him_ref[0] = h_im
    hs = xs_ref[...].reshape(n_l * n_r, 2 * S5_ST).astype(bf16)
    y = jnp.dot(hs, ccat_ref[0].astype(bf16), preferred_element_type=f32) + d_ref[...] * u2
    y_ref[0] = jax.nn.gelu(y).reshape(n_l, n_r, S5_CH)


def s5_discretize(lam_re, lam_im, log_dt, b_re, b_im, c_re, c_im):
    f32 = jnp.float32
    dt = jnp.exp(log_dt.astype(f32))[:, None]
    lr, li = lam_re.astype(f32), lam_im.astype(f32)
    mag = jnp.exp(lr * dt)
    a_re, a_im = mag * jnp.cos(li * dt), mag * jnp.sin(li * dt)
    den = lr * lr + li * li
    z_re = ((a_re - 1.0) * lr + a_im * li) / den
    z_im = (a_im * lr - (a_re - 1.0) * li) / den
    br, bim = b_re.astype(f32), b_im.astype(f32)
    bb_re = z_re[..., None] * br - z_im[..., None] * bim
    bb_im = z_re[..., None] * bim + z_im[..., None] * br
    ngb = SSM_GROUPS // S5_GB
    eye = jnp.eye(S5_GB, dtype=f32)

    def block_diag(w):
        wd = w[:, :, :, None, :] * eye[None, :, None, :, None]
        return wd.reshape(ngb, S5_GB * w.shape[2], S5_GB * w.shape[3])

    def pack_b(bb):
        return block_diag(jnp.swapaxes(bb.reshape(ngb, S5_GB, SSM_STATE, SSM_CH), 2, 3))

    def pack_c(cc):
        return block_diag(jnp.swapaxes(cc.reshape(ngb, S5_GB, SSM_CH, SSM_STATE), 2, 3))

    a = jnp.stack([a_re.reshape(-1), a_im.reshape(-1)])
    bcat = jnp.concatenate([pack_b(bb_re), pack_b(bb_im)], axis=2)
    ccat = jnp.concatenate([pack_c(c_re.astype(f32)), -pack_c(c_im.astype(f32))], axis=1)
    return a, bcat, ccat


def s5_scan(u4, h0_re, h0_im, a, bcat, ccat, d_skip, chained):
    nb, n_l, n_r, _ = u4.shape
    rh = h0_re.shape[1]
    ngb = SSM_GROUPS // S5_GB
    st_spec = pl.BlockSpec((1, rh, S5_ST), lambda i, j: (i, 0, j))
    return pl.pallas_call(
        functools.partial(_s5_body, chained=chained),
        grid=(nb, ngb),
        in_specs=[pl.BlockSpec((1, n_l, n_r, S5_CH), lambda i, j: (i, 0, 0, j)), st_spec, st_spec,
                  pl.BlockSpec((2, S5_ST), lambda i, j: (0, j)),
                  pl.BlockSpec((1, S5_CH, 2 * S5_ST), lambda i, j: (j, 0, 0)),
                  pl.BlockSpec((1, 2 * S5_ST, S5_CH), lambda i, j: (j, 0, 0)),
                  pl.BlockSpec((1, S5_CH), lambda i, j: (0, j))],
        out_specs=[pl.BlockSpec((1, n_l, n_r, S5_CH), lambda i, j: (i, 0, 0, j)), st_spec, st_spec],
        out_shape=[jax.ShapeDtypeStruct(u4.shape, jnp.float32),
                   jax.ShapeDtypeStruct(h0_re.shape, jnp.float32), jax.ShapeDtypeStruct(h0_re.shape, jnp.float32)],
        scratch_shapes=[pltpu.VMEM((n_l, n_r, 2 * S5_ST), jnp.float32), pltpu.VMEM((n_r + 8, 2 * S5_ST), jnp.float32)],
        compiler_params=pltpu.CompilerParams(vmem_limit_bytes=VMEM_LIMIT),
        name="s5_scan",
    )(u4, h0_re, h0_im, a, bcat, ccat, d_skip.reshape(1, SSM_WIDTH))


S5_CHUNK = 64


def s5_mixer_pallas(u, h0_re, h0_im, lam_re, lam_im, log_dt, b_re, b_im, c_re, c_im, d_skip, w_glu, b_glu, chained):
    bsz, t, _ = u.shape
    f32 = jnp.float32
    a, bcat, ccat = s5_discretize(lam_re, lam_im, log_dt, b_re, b_im, c_re, c_im)
    flat = lambda h: h.astype(f32).reshape(bsz, SSM_GROUPS * SSM_STATE)
    if chained:
        n_r = t // S5_CHUNK
        u4 = jnp.transpose(u.astype(f32).reshape(bsz, n_r, S5_CHUNK, SSM_WIDTH), (0, 2, 1, 3))
        y4, h_re, h_im = s5_scan(u4, flat(h0_re)[:, None], flat(h0_im)[:, None], a, bcat, ccat, d_skip.astype(f32), True)
        y = jnp.transpose(y4, (0, 2, 1, 3)).reshape(bsz, t, SSM_WIDTH)
        h_re, h_im = h_re[:, 0], h_im[:, 0]
    else:
        u4 = jnp.transpose(u.astype(f32), (1, 0, 2))[None]
        y4, h_re, h_im = s5_scan(u4, flat(h0_re)[None], flat(h0_im)[None], a, bcat, ccat, d_skip.astype(f32), False)
        y = jnp.transpose(y4[0], (1, 0, 2))
        h_re, h_im = h_re[0], h_im[0]
    out = y * jax.nn.sigmoid(y @ w_glu.astype(f32) + b_glu.astype(f32))
    st = lambda h: h.reshape(bsz, SSM_GROUPS, SSM_STATE).astype(u.dtype)
    return out.astype(u.dtype), st(h_re), st(h_im)


ROW_LANES = N_PAGED * KV_WIDTH
CHUNKS_PER_PAGE = 128 // CMP_STRIDE
CMP_HID2 = NSA_KV_HEADS * CMP_HIDDEN
CMP_PAGES = 64


def _compress_body(pt_ref, *refs, n_g, transposed):
    f32, bf16 = jnp.float32, jnp.bfloat16
    page_refs = (refs[:n_g], refs[n_g:2 * n_g])
    w1_ref, c1_ref, w2_ref, b2_ref, o_ref, carry_ref, rows_ref = refs[2 * n_g:]
    m_rows = n_g * CHUNKS_PER_PAGE
    first = pl.program_id(1) == 0
    row_id = lax.broadcasted_iota(jnp.int32, (m_rows, CMP_HID2), 0)
    outs = []
    for t in range(2):
        for k, r in enumerate(page_refs[t]):
            rows_ref[k * 128:(k + 1) * 128, :] = r[0].T if transposed else r[0]
        cols = [rows_ref[pl.ds(j, m_rows, stride=CMP_STRIDE), :] for j in range(CMP_STRIDE)]
        x = jnp.concatenate(cols, axis=1).astype(bf16)
        part = jnp.dot(x, w1_ref[t], preferred_element_type=f32)
        p0, p1 = part[:, :CMP_HID2], part[:, CMP_HID2:]
        prev = jnp.where(first, 0.0, carry_ref[t, 0:1, :])
        shifted = jnp.where(row_id == 0, prev, pltpu.roll(p0, 1, axis=0))
        carry_ref[t, 0:1, :] = p0[m_rows - 1:m_rows, :]
        h1 = (c1_ref[t] + shifted) + p1
        outs.append(jnp.dot(jax.nn.gelu(h1).astype(bf16), w2_ref[t], preferred_element_type=f32) + b2_ref[t])
    o_ref[0] = jnp.concatenate(outs, axis=1)


def compress_pages(pool, page_table, cmp_pe, cmp_w1, cmp_b1, cmp_w2, cmp_b2, n_g, transposed):
    n_seq, n_pages = page_table.shape
    assert n_pages % n_g == 0 and pool.shape[1:] == ((ROW_LANES, 128) if transposed else (128, ROW_LANES))
    f32, bf16 = jnp.float32, jnp.bfloat16
    r = CMP_BLOCK // CMP_STRIDE
    eye = jnp.eye(NSA_KV_HEADS, dtype=f32)
    w1 = cmp_w1.astype(f32).reshape(2, r, CMP_STRIDE, HEAD_DIM, CMP_HIDDEN)
    w1 = jnp.transpose(w1, (0, 2, 3, 1, 4))[:, :, None, :, :, None, :] * eye[None, None, :, None, None, :, None]
    w1 = w1.reshape(2, CMP_STRIDE * KV_WIDTH, r * CMP_HID2).astype(bf16)
    c1 = jnp.stack([jnp.einsum('ld,ldf->f', cmp_pe[t], cmp_w1[t]) + cmp_b1[t] for t in range(2)])
    c1 = jnp.tile(c1[:, None, :], (1, 1, NSA_KV_HEADS))
    w2 = (cmp_w2.astype(f32)[:, None, :, None, :] * eye[None, :, None, :, None]).reshape(2, CMP_HID2, KV_WIDTH).astype(bf16)
    b2 = jnp.tile(cmp_b2.astype(f32)[:, None, :], (1, 1, NSA_KV_HEADS))
    m_rows = n_g * CHUNKS_PER_PAGE
    page_spec = lambda k, t: pl.BlockSpec((1, 128, KV_WIDTH), lambda b, g, pt: (
        (pt[b * n_pages + g * n_g + k], t, 0) if transposed else (pt[b * n_pages + g * n_g + k], 0, t)))
    full = lambda a: pl.BlockSpec(a.shape, lambda b, g, pt: (0,) * a.ndim)
    return pl.pallas_call(
        functools.partial(_compress_body, n_g=n_g, transposed=transposed),
        grid_spec=pltpu.PrefetchScalarGridSpec(
            num_scalar_prefetch=1,
            grid=(n_seq, n_pages // n_g),
            in_specs=[page_spec(k, t) for t in range(2) for k in range(n_g)] + [full(w1), full(c1), full(w2), full(b2)],
            out_specs=pl.BlockSpec((1, m_rows, 2 * KV_WIDTH), lambda b, g, pt: (b, g, 0)),
            scratch_shapes=[pltpu.VMEM((2, 8, CMP_HID2), f32), pltpu.VMEM((n_g * 128, KV_WIDTH), f32)],
        ),
        out_shape=jax.ShapeDtypeStruct((n_seq, n_pages * CHUNKS_PER_PAGE, 2 * KV_WIDTH), f32),
        compiler_params=pltpu.CompilerParams(vmem_limit_bytes=VMEM_LIMIT),
        name="compress_pages",
    )(page_table.reshape(-1).astype(jnp.int32), *([pool] * (2 * n_g)), w1, c1, w2, b2)


SD_PAGES = 64
SD_KEYS = SD_PAGES * 128
SD_SUB = 16
SD_ROWS = NSA_HEADS * 8
WIN_PAD = 640


def _masked_softmax(s, ok):
    s = jnp.where(ok, s, NEG_INF)
    e = jnp.where(ok, jnp.exp(s - jnp.max(s, axis=-1, keepdims=True)), 0.0)
    return e / jnp.maximum(jnp.sum(e, axis=-1, keepdims=True), 1e-30)


def _rows_from_group(a):
    t = a.shape[0] // NSA_KV_HEADS
    a4 = jnp.broadcast_to(a.reshape(NSA_KV_HEADS, 1, t, a.shape[1]), (NSA_KV_HEADS, GRP, t, a.shape[1]))
    return a4.reshape(NSA_KV_HEADS * GRP * t, a.shape[1])


def _nsa_decode_body(pt_ref, *refs, pos0, n_cmp, n_sel, n_new, win_len):
    f32, bf16 = jnp.float32, jnp.bfloat16
    kt_refs, vt_refs = refs[:SD_PAGES], refs[SD_PAGES:2 * SD_PAGES]
    (q_ref, g_ref, kcvc_ref, knew_ref, wh_ref, bc_ref, bs_ref, bn_ref, bw_ref, cov_ref, exp_ref,
     o_ref, sel_ref, oc_ref, m_ref, l_ref, acc_ref, mask_ref) = refs[2 * SD_PAGES:]
    tile, n_tiles = pl.program_id(1), pl.num_programs(1)
    nt_dims = (((1,), (1,)), ((), ()))
    nn_dims = (((1,), (0,)), ((), ()))
    n_tok = SD_ROWS // NSA_HEADS
    q = (q_ref[0] * (HEAD_DIM ** -0.5)).astype(bf16)
    qpos = pos0 + lax.broadcasted_iota(jnp.int32, (SD_ROWS, 1), 0) % n_tok

    @pl.when(tile == 0)
    def _():
        n_c = kcvc_ref.shape[1]
        m_idx = lax.broadcasted_iota(jnp.int32, (SD_ROWS, n_c), 1)
        ok_c = (m_idx >= 1) & (m_idx <= n_cmp) & (qpos >= CMP_STRIDE * m_idx + (CMP_BLOCK - 1 - CMP_STRIDE))
        s_c = lax.dot_general(q, kcvc_ref[0, :, :KV_WIDTH].astype(bf16), nt_dims, preferred_element_type=f32) + bc_ref[...]
        p_c = _masked_softmax(s_c, ok_c)
        oc_ref[...] = jnp.dot(p_c.astype(bf16), kcvc_ref[0, :, KV_WIDTH:].astype(bf16), preferred_element_type=f32)
        p_sum = jnp.sum(p_c.reshape(NSA_KV_HEADS, GRP, n_tok, n_c), axis=1).reshape(NSA_KV_HEADS * n_tok, n_c)
        imp = jnp.dot(p_sum.astype(bf16), cov_ref[...], preferred_element_type=f32)
        n_l = imp.shape[1]
        lane = lax.broadcasted_iota(jnp.int32, (NSA_KV_HEADS * n_tok, n_l), 1)
        qp = pos0 + lax.broadcasted_iota(jnp.int32, (NSA_KV_HEADS * n_tok, n_l), 0) % n_tok
        cur = qp // SEL_BLOCK
        forced = (lane == 0) | (lane == cur) | (lane == cur - 1)
        score = jnp.where(forced, FORCE_SCORE, jnp.where(lane * SEL_BLOCK <= qp, imp, -1.0))
        score = jnp.where(lane < n_sel, score, -jnp.inf)
        sel = jnp.zeros(score.shape, f32)
        for _ in range(min(SEL_TOP_N, n_sel)):
            best = jnp.max(score, axis=-1, keepdims=True)
            lane_f = lane.astype(f32)
            hit = lane_f == jnp.min(jnp.where(score == best, lane_f, float(n_l)), axis=-1, keepdims=True)
            sel = jnp.where(hit, 1.0, sel)
            score = jnp.where(hit, -jnp.inf, score)
        sel_ref[...] = sel
        m_ref[...] = jnp.full(m_ref.shape, NEG_INF, f32)
        l_ref[...] = jnp.zeros(l_ref.shape, f32)
        acc_ref[...] = jnp.zeros(acc_ref.shape, f32)

    def online(s, ok, v, v_dims):
        m_old = m_ref[...]
        s = jnp.where(ok, s, NEG_INF)
        m_new = jnp.maximum(m_old, jnp.max(s, axis=-1, keepdims=True))
        p = jnp.where(ok, jnp.exp(s - m_new), 0.0)
        alpha = jnp.exp(m_old - m_new)
        l_ref[...] = alpha * l_ref[...] + jnp.sum(p, axis=-1, keepdims=True)
        acc_ref[...] = alpha * acc_ref[...] + lax.dot_general(p.astype(bf16), v, v_dims, preferred_element_type=f32)
        m_ref[...] = m_new

    sel_tile = sel_ref[:, pl.ds(pl.multiple_of(tile * LANES, LANES), LANES)]
    mask_ref[...] = jnp.dot(sel_tile.astype(bf16), exp_ref[...], preferred_element_type=f32)
    sub_keys = SD_SUB * 128
    for sub in range(SD_PAGES // SD_SUB):
        pages = slice(sub * SD_SUB, (sub + 1) * SD_SUB)
        kt = jnp.concatenate([r[0] for r in kt_refs[pages]], axis=1).astype(bf16)
        vt = jnp.concatenate([r[0] for r in vt_refs[pages]], axis=1).astype(bf16)
        s = jnp.dot(q, kt, preferred_element_type=f32) + bs_ref[:, sub * sub_keys:(sub + 1) * sub_keys]
        ok = _rows_from_group(mask_ref[:, sub * sub_keys:(sub + 1) * sub_keys]) > 0.5
        online(s, ok, vt, nt_dims)

    @pl.when(tile == n_tiles - 1)
    def _():
        lane = lax.broadcasted_iota(jnp.int32, (SD_ROWS, LANES), 1)
        new_blk = pos0 // SEL_BLOCK
        sel_new = _rows_from_group(sel_ref[:, new_blk:new_blk + 1]) > 0.5
        ok_n = sel_new & (lane < n_new) & (pos0 + lane <= qpos)
        s_n = lax.dot_general(q, knew_ref[0, :, :KV_WIDTH].astype(bf16), nt_dims, preferred_element_type=f32) + bn_ref[...]
        online(s_n, ok_n, knew_ref[0, :, KV_WIDTH:].astype(bf16), nn_dims)
        o_s = acc_ref[...] / jnp.maximum(l_ref[...], 1e-30)
        j = lax.broadcasted_iota(jnp.int32, (SD_ROWS, WIN_PAD), 1)
        kwpos = pos0 - win_len + j
        dist = qpos - kwpos
        ok_w = (dist >= 0) & (dist < WINDOW) & (kwpos >= 0) & (j < win_len + n_new)
        s_w = lax.dot_general(q, wh_ref[0, :, :KV_WIDTH].astype(bf16), nt_dims, preferred_element_type=f32) + bw_ref[...]
        o_w = jnp.dot(_masked_softmax(s_w, ok_w).astype(bf16), wh_ref[0, :, KV_WIDTH:].astype(bf16), preferred_element_type=f32)
        g = jax.nn.sigmoid(g_ref[0])
        o = g[:, 0:1] * oc_ref[...] + g[:, 1:2] * o_s + g[:, 2:3] * o_w
        row = lax.broadcasted_iota(jnp.int32, (SD_ROWS, HEAD_DIM), 0)
        o_ref[0] = jnp.where(row < SD_ROWS // NSA_KV_HEADS, o[:, :HEAD_DIM], o[:, HEAD_DIM:])


def nsa_decode(q, glog, kv_new, kcvc, pool_t, page_table, cache_win, thr, rel_bias, pos0):
    f32, bf16 = jnp.float32, jnp.bfloat16
    b, t = q.shape[:2]
    n_pages = page_table.shape[1]
    win_len = cache_win.shape[1]
    assert t * NSA_HEADS == SD_ROWS and pos0 == n_pages * 128 and n_pages % SD_PAGES == 0 and pos0 % SEL_BLOCK == 0
    assert win_len + t <= WIN_PAD and t <= SEL_BLOCK and SD_KEYS == LANES * SEL_BLOCK
    n_rows = pos0 + t
    n_cmp = (n_rows - CMP_BLOCK) // CMP_STRIDE + 1
    n_sel = -(-n_rows // SEL_BLOCK)
    n_c = kcvc.shape[1]
    sel_lanes = -(-n_sel // LANES) * LANES
    q5 = jnp.transpose(q.reshape(b, t, NSA_KV_HEADS, GRP, HEAD_DIM), (0, 2, 3, 1, 4))
    qz = (q5[:, :, :, :, None, :] * jnp.eye(NSA_KV_HEADS, dtype=f32)[None, :, None, None, :, None]).reshape(b, SD_ROWS, KV_WIDTH)
    g3 = jnp.transpose(glog.reshape(b, t, NSA_HEADS, 3), (0, 2, 1, 3)).reshape(b, SD_ROWS, 3)
    knew = jnp.pad(kv_new[:, :, 2:4].reshape(b, t, 2 * KV_WIDTH), ((0, 0), (0, LANES - t), (0, 0)))
    whist = jnp.concatenate([cache_win.reshape(b, win_len, 2 * KV_WIDTH), kv_new[:, :, 4:6].reshape(b, t, 2 * KV_WIDTH),
                             jnp.zeros((b, WIN_PAD - win_len - t, 2 * KV_WIDTH), f32)], axis=1)
    qp = pos0 + jnp.arange(t, dtype=jnp.int32)[:, None]
    tab = lambda dist: bias_tables(dist[None], thr, rel_bias).reshape(SD_ROWS, dist.shape[1])
    bias_c = tab(qp - (CMP_STRIDE * jnp.arange(n_c, dtype=jnp.int32)[None, :] + CMP_BLOCK - 1 - CMP_STRIDE))
    bias_s = tab(qp - jnp.arange(pos0, dtype=jnp.int32)[None, :])
    bias_n = tab(qp - (pos0 + jnp.arange(LANES, dtype=jnp.int32)[None, :]))
    bias_w = tab(qp - (pos0 - win_len + jnp.arange(WIN_PAD, dtype=jnp.int32)[None, :]))
    m_idx = jnp.arange(n_c)[:, None]
    c_start = CMP_STRIDE * (m_idx - 1)
    s_start = SEL_BLOCK * jnp.arange(sel_lanes)[None, :]
    cover = ((c_start < s_start + SEL_BLOCK) & (c_start + CMP_BLOCK > s_start) & (m_idx >= 1) & (m_idx <= n_cmp)
             & (jnp.arange(sel_lanes)[None, :] < n_sel)).astype(bf16)
    expand = (jnp.arange(SD_KEYS)[None, :] // SEL_BLOCK == jnp.arange(LANES)[:, None]).astype(bf16)
    page_spec = lambda k, blk: pl.BlockSpec((1, KV_WIDTH, 128), lambda bb, g, pt: (pt[bb * n_pages + g * SD_PAGES + k], blk, 0))
    per_seq = lambda a: pl.BlockSpec((1,) + a.shape[1:], lambda bb, g, pt: (bb,) + (0,) * (a.ndim - 1))
    full = lambda a: pl.BlockSpec(a.shape, lambda bb, g, pt: (0,) * a.ndim)
    kv_rows = NSA_KV_HEADS * t
    out = pl.pallas_call(
        functools.partial(_nsa_decode_body, pos0=pos0, n_cmp=n_cmp, n_sel=n_sel, n_new=t, win_len=win_len),
        grid_spec=pltpu.PrefetchScalarGridSpec(
            num_scalar_prefetch=1,
            grid=(b, n_pages // SD_PAGES),
            in_specs=[page_spec(k, blk) for blk in (2, 3) for k in range(SD_PAGES)]
            + [per_seq(qz), per_seq(g3), per_seq(kcvc), per_seq(knew), per_seq(whist), full(bias_c),
               pl.BlockSpec((SD_ROWS, SD_KEYS), lambda bb, g, pt: (0, g)), full(bias_n), full(bias_w), full(cover), full(expand)],
            out_specs=pl.BlockSpec((1, SD_ROWS, HEAD_DIM), lambda bb, g, pt: (bb, 0, 0)),
            scratch_shapes=[pltpu.VMEM((kv_rows, sel_lanes), f32), pltpu.VMEM((SD_ROWS, KV_WIDTH), f32),
                            pltpu.VMEM((SD_ROWS, 1), f32), pltpu.VMEM((SD_ROWS, 1), f32), pltpu.VMEM((SD_ROWS, KV_WIDTH), f32),
                            pltpu.VMEM((kv_rows, SD_KEYS), f32)],
        ),
        out_shape=jax.ShapeDtypeStruct((b, SD_ROWS, HEAD_DIM), f32),
        compiler_params=pltpu.CompilerParams(vmem_limit_bytes=VMEM_LIMIT),
        name="nsa_decode",
    )(page_table.reshape(-1).astype(jnp.int32), *([pool_t] * (2 * SD_PAGES)), qz, g3, kcvc, knew, whist,
      bias_c, bias_s, bias_n, bias_w, cover, expand)
    out = jnp.transpose(out.reshape(b, NSA_KV_HEADS, GRP, t, HEAD_DIM), (0, 3, 1, 2, 4))
    return out.reshape(b, t, NSA_WIDTH)


def rms_norm(x, g):
    xf = x.astype(jnp.float32)
    y = xf * lax.rsqrt(jnp.mean(xf * xf, axis=-1, keepdims=True) + RMS_EPS)
    return (y * g.astype(jnp.float32)).astype(x.dtype)


def masked_softmax(s, mask):
    s = jnp.where(mask, s.astype(jnp.float32), NEG_INF)
    m = jnp.max(s, axis=-1, keepdims=True)
    e = jnp.where(mask, jnp.exp(s - m), 0.0)
    return e / jnp.maximum(jnp.sum(e, axis=-1, keepdims=True), 1e-30)


def compress_rows(x, pe, w1, b1, w2, b2, n_rows):
    b = x.shape[0]
    nc = (n_rows - CMP_BLOCK) // CMP_STRIDE + 1
    r = CMP_BLOCK // CMP_STRIDE
    nch = nc + r - 1
    chunks = x[:, :nch * CMP_STRIDE].reshape(b, nch, CMP_STRIDE, NSA_KV_HEADS, HEAD_DIM)
    part = jnp.einsum('bcjhd,rjdf->bcrhf', chunks, w1.reshape(r, CMP_STRIDE, HEAD_DIM, CMP_HIDDEN))
    h1 = jnp.einsum('ld,ldf->f', pe, w1) + b1
    for o in range(r):
        h1 = h1 + part[:, o:o + nc, o]
    return jnp.einsum('bchf,fd->bchd', jax.nn.gelu(h1), w2) + b2


def nsa_query_block(q, qpos, gates, kw, vw, kwpos, kc, vc, c_end, ks_b, vs_b, cover, rel_bias):
    b, nq = q.shape[:2]
    grp = NSA_HEADS // NSA_KV_HEADS
    qg = (q * HEAD_DIM ** -0.5).reshape(b, nq, NSA_KV_HEADS, grp, HEAD_DIM)
    rb = rel_bias.reshape(REL_BUCKETS, NSA_KV_HEADS, grp)
    hi = jnp.arange(NSA_KV_HEADS)[None, None, :, None]
    bi = jnp.arange(b)[:, None, None, None]
    dist_c = qpos[:, None] - c_end[None, :]
    bias_c = jnp.moveaxis(rb[rel_bucket(dist_c)], 1, -1)
    s_c = jnp.einsum('bqhgd,bchd->bqhgc', qg, kc) + bias_c
    p_c = masked_softmax(s_c, (dist_c >= 0)[:, None, None, :])
    o_c = jnp.einsum('bqhgc,bchd->bqhgd', p_c.astype(vc.dtype), vc)
    imp = jnp.einsum('bqhgc,cs->bqhs', p_c, cover)
    ns = ks_b.shape[1]
    blk = jnp.arange(ns)[None, :]
    cur = (qpos // SEL_BLOCK)[:, None]
    forced = ((blk == 0) | (blk == cur) | (blk == cur - 1))[:, None, :]
    valid = (blk * SEL_BLOCK <= qpos[:, None])[:, None, :]
    score = jnp.where(forced, FORCE_SCORE, jnp.where(valid, imp, -1.0))
    _, idx = lax.top_k(score, min(SEL_TOP_N, ns))
    k_s = ks_b[bi, idx, :, hi].reshape(b, nq, NSA_KV_HEADS, -1, HEAD_DIM)
    v_s = vs_b[bi, idx, :, hi].reshape(b, nq, NSA_KV_HEADS, -1, HEAD_DIM)
    s_pos = (idx[..., None] * SEL_BLOCK + jnp.arange(SEL_BLOCK)).reshape(b, nq, NSA_KV_HEADS, -1)
    dist_s = qpos[None, :, None, None] - s_pos
    bias_s = jnp.moveaxis(rb[rel_bucket(dist_s), hi], -1, 3)
    s_s = jnp.einsum('bqhgd,bqhkd->bqhgk', qg, k_s) + bias_s
    p_s = masked_softmax(s_s, (dist_s >= 0)[:, :, :, None, :])
    o_s = jnp.einsum('bqhgk,bqhkd->bqhgd', p_s.astype(v_s.dtype), v_s)
    dist_w = qpos[:, None] - kwpos[None, :]
    mask_w = (dist_w >= 0) & (dist_w < WINDOW) & (kwpos >= 0)[None, :]
    bias_w = jnp.moveaxis(rb[rel_bucket(dist_w)], 1, -1)
    s_w = jnp.einsum('bqhgd,bkhd->bqhgk', qg, kw) + bias_w
    p_w = masked_softmax(s_w, mask_w[:, None, None, :])
    o_w = jnp.einsum('bqhgk,bkhd->bqhgd', p_w.astype(vw.dtype), vw)
    g = gates.reshape(b, nq, NSA_KV_HEADS, grp, 3)
    o = g[..., 0:1] * o_c + g[..., 1:2] * o_s + g[..., 2:3] * o_w
    return o.reshape(b, nq, NSA_WIDTH)


def s5_mixer(u, h0_re, h0_im, lam_re, lam_im, log_dt, b_re, b_im, c_re, c_im, d_skip, w_glu, b_glu):
    bsz, t, _ = u.shape
    f32 = jnp.float32
    uf = u.astype(f32).reshape(bsz, t, SSM_GROUPS, SSM_CH)
    dt = jnp.exp(log_dt.astype(f32))[:, None]
    lr, li = lam_re.astype(f32), lam_im.astype(f32)
    mag = jnp.exp(lr * dt)
    a_re, a_im = mag * jnp.cos(li * dt), mag * jnp.sin(li * dt)
    den = lr * lr + li * li
    z_re = ((a_re - 1.0) * lr + a_im * li) / den
    z_im = (a_im * lr - (a_re - 1.0) * li) / den
    br, bim = b_re.astype(f32), b_im.astype(f32)
    bb_re = z_re[..., None] * br - z_im[..., None] * bim
    bb_im = z_re[..., None] * bim + z_im[..., None] * br
    x_re = jnp.einsum('btgc,gpc->btgp', uf, bb_re)
    x_im = jnp.einsum('btgc,gpc->btgp', uf, bb_im)
    ar = jnp.broadcast_to(a_re, x_re.shape)
    ai = jnp.broadcast_to(a_im, x_re.shape)

    def combine(e1, e2):
        a1r, a1i, b1r, b1i = e1
        a2r, a2i, b2r, b2i = e2
        return (a2r * a1r - a2i * a1i, a2r * a1i + a2i * a1r,
                a2r * b1r - a2i * b1i + b2r, a2r * b1i + a2i * b1r + b2i)

    pr, pim, sr, si = lax.associative_scan(combine, (ar, ai, x_re, x_im), axis=1)
    h0r = h0_re.astype(f32)[:, None]
    h0i = h0_im.astype(f32)[:, None]
    h_re = pr * h0r - pim * h0i + sr
    h_im = pr * h0i + pim * h0r + si
    y = jnp.einsum('btgp,gcp->btgc', h_re, c_re.astype(f32)) - jnp.einsum('btgp,gcp->btgc', h_im, c_im.astype(f32))
    y = jax.nn.gelu(y.reshape(bsz, t, SSM_WIDTH) + d_skip.astype(f32) * u.astype(f32))
    out = y * jax.nn.sigmoid(y @ w_glu.astype(f32) + b_glu.astype(f32))
    return out.astype(u.dtype), h_re[:, -1].astype(u.dtype), h_im[:, -1].astype(u.dtype)


def moe_ffn(x, router_w, router_b, w_gu, b_gu, w_down, b_down):
    n = x.shape[0]
    router_pad = jnp.pad(router_w, ((0, 0), (0, LANES - N_EXPERTS)))
    logits = pallas_matmul(x, router_pad)[:, :N_EXPERTS] + router_b.astype(jnp.float32)
    top_val, top_idx = lax.top_k(logits, TOP_K)
    gate = jax.nn.softmax(top_val, axis=-1)
    nk = n * TOP_K
    n_items = N_EXPERTS + nk // MOE_ROWS
    flat_e = top_idx.reshape(nk)
    onehot = (flat_e[:, None] == jnp.arange(N_EXPERTS, dtype=flat_e.dtype)[None, :]).astype(jnp.int32)
    running = jnp.cumsum(onehot, axis=0)
    counts = running[-1]
    pos_in_e = jnp.sum(onehot * running, axis=1) - 1
    items_e = (counts + MOE_ROWS - 1) // MOE_ROWS
    item_end = jnp.cumsum(items_e)
    item_start = item_end - items_e
    total = item_end[-1]
    dest = (item_start[flat_e] * MOE_ROWS + pos_in_e).astype(jnp.int32)
    row_tok = jnp.zeros(n_items * MOE_ROWS, jnp.int32).at[dest].set(jnp.arange(nk, dtype=jnp.int32) // TOP_K)
    item = jnp.minimum(jnp.arange(n_items, dtype=jnp.int32), total - 1)
    item_e = jnp.minimum(jnp.searchsorted(item_end, item, side='right'), N_EXPERTS - 1).astype(jnp.int32)
    rows = jnp.clip(counts[item_e] - (item - item_start[item_e]) * MOE_ROWS, 0, MOE_ROWS)
    item_sub = jnp.where(jnp.arange(n_items) < total, (rows + MOE_SUB - 1) // MOE_SUB, 0).astype(jnp.int32)
    half = x.shape[1] // 2
    bits = lax.bitcast_convert_type(x.astype(jnp.bfloat16), jnp.uint16).astype(jnp.uint32)
    xb = ((bits[:, :half] << 16) | bits[:, half:])[row_tok]
    item_ok = (jnp.arange(n_items) < total).astype(jnp.int32)
    yb = moe_experts(item_e, item_sub, item.astype(jnp.int32), item_ok, xb, w_gu, b_gu, w_down, b_down)
    dest_k = dest.reshape(n, TOP_K)
    out = gate[:, 0:1] * yb[dest_k[:, 0]]
    for k in range(1, TOP_K):
        out = out + gate[:, k:k + 1] * yb[dest_k[:, k]]
    return out


MOE_ROWS = 1280
MOE_SUB = 256
MOE_TF = 512
MOE_COLS = 256


def _moe_body(e_ref, sub_ref, blk_ref, ok_ref, x_ref, wg_ref, wl_ref, bg_ref, bl_ref, wd_ref, bd_ref, o_ref, xs_ref):
    bf16 = jnp.bfloat16
    i, f = pl.program_id(0), pl.program_id(1)
    n_sub = sub_ref[i]
    d = o_ref.shape[1]

    @pl.when(f == 0)
    def _():
        o_ref[...] = jnp.broadcast_to(bd_ref[0], o_ref.shape)

    @pl.when((f == 0) & (n_sub > 0))
    def _():
        for j in range(MOE_ROWS // MOE_SUB):
            rows = slice(j * MOE_SUB, (j + 1) * MOE_SUB)
            w = lax.bitcast_convert_type(x_ref[rows, :], jnp.uint32)
            hi = lax.bitcast_convert_type(w & jnp.uint32(0xFFFF0000), jnp.float32)
            lo = lax.bitcast_convert_type(w << 16, jnp.float32)
            xs_ref[rows, :] = jnp.concatenate([hi, lo], axis=1).astype(bf16)

    def expert_rows(n_rows):
        x = xs_ref[0:n_rows, :]
        hg = jnp.dot(x, wg_ref[0].astype(bf16), preferred_element_type=jnp.float32) + bg_ref[0]
        hl = jnp.dot(x, wl_ref[0].astype(bf16), preferred_element_type=jnp.float32) + bl_ref[0]
        hg = jnp.minimum(hg, SWIGLU_LIMIT)
        hl = jnp.clip(hl, -SWIGLU_LIMIT, SWIGLU_LIMIT)
        act = (hg * jax.nn.sigmoid(SWIGLU_ALPHA * hg) * (hl + 1.0)).astype(bf16)
        for c in range(d // MOE_COLS):
            cols = slice(c * MOE_COLS, (c + 1) * MOE_COLS)
            o_ref[0:n_rows, cols] += jnp.dot(act, wd_ref[0, :, cols].astype(bf16), preferred_element_type=jnp.float32)

    @pl.when(n_sub == 1)
    def _():
        expert_rows(MOE_SUB)

    @pl.when(n_sub > 1)
    def _():
        expert_rows(MOE_ROWS)


def moe_experts(item_e, item_sub, item_blk, item_ok, xb, w_gu, b_gu, w_down, b_down):
    n_items = item_e.shape[0]
    n_f = D_FF // MOE_TF
    d = 2 * xb.shape[1]

    def col(f, ok, i):
        return f * ok[i] + (n_f - 1) * (1 - ok[i])

    return pl.pallas_call(
        _moe_body,
        grid_spec=pltpu.PrefetchScalarGridSpec(
            num_scalar_prefetch=4,
            grid=(n_items, n_f),
            in_specs=[pl.BlockSpec((MOE_ROWS, d // 2), lambda i, f, e, s, b, ok: (b[i], 0), pipeline_mode=pl.Buffered(1)),
                      pl.BlockSpec((1, d, MOE_TF), lambda i, f, e, s, b, ok: (e[i], 0, col(f, ok, i))),
                      pl.BlockSpec((1, d, MOE_TF), lambda i, f, e, s, b, ok: (e[i], 0, n_f + col(f, ok, i))),
                      pl.BlockSpec((1, 1, MOE_TF), lambda i, f, e, s, b, ok: (e[i], 0, col(f, ok, i))),
                      pl.BlockSpec((1, 1, MOE_TF), lambda i, f, e, s, b, ok: (e[i], 0, n_f + col(f, ok, i))),
                      pl.BlockSpec((1, MOE_TF, d), lambda i, f, e, s, b, ok: (e[i], col(f, ok, i), 0)),
                      pl.BlockSpec((1, 1, d), lambda i, f, e, s, b, ok: (e[i], 0, 0))],
            out_specs=pl.BlockSpec((MOE_ROWS, d), lambda i, f, e, s, b, ok: (i, 0), pipeline_mode=pl.Buffered(1)),
            scratch_shapes=[pltpu.VMEM((MOE_ROWS, d), jnp.bfloat16)],
        ),
        out_shape=jax.ShapeDtypeStruct((n_items * MOE_ROWS, d), jnp.float32),
        compiler_params=pltpu.CompilerParams(vmem_limit_bytes=VMEM_LIMIT),
        name="moe_experts",
    )(item_e, item_sub, item_blk, item_ok, xb, w_gu, w_gu, b_gu[:, None, :], b_gu[:, None, :], w_down, b_down[:, None, :])


def layer_forward(x, pos0, past_kv, win_buf, h0_re, h0_im, rel_bias, lw):
    b, t, _ = x.shape
    h = rms_norm(x, lw['norm_mix'])
    proj = h @ lw['w_in']
    o1 = NSA_WIDTH
    o2 = o1 + N_KV * KV_WIDTH
    o3 = o2 + 3 * NSA_HEADS
    q = proj[..., :o1].reshape(b, t, NSA_HEADS, HEAD_DIM)
    kv_new = proj[..., o1:o2].reshape(b, t, N_KV, NSA_KV_HEADS, HEAD_DIM)
    gates = jax.nn.sigmoid(proj[..., o2:o3].astype(jnp.float32)).reshape(b, t, NSA_HEADS, 3)
    u = proj[..., o3:]
    paged_new = kv_new[:, :, :N_PAGED]
    win_new = kv_new[:, :, N_PAGED:]
    qpos = pos0 + jnp.arange(t, dtype=jnp.int32)

    n_rows = pos0 + t
    ns = -(-n_rows // SEL_BLOCK)
    pad = jnp.zeros((b, ns * SEL_BLOCK - n_rows) + paged_new.shape[2:], paged_new.dtype)
    rows = jnp.concatenate(([] if past_kv is None else [past_kv]) + [paged_new, pad], axis=1)
    kc = compress_rows(rows[:, :, 0], lw['cmp_pe'][0], lw['cmp_w1'][0], lw['cmp_b1'][0], lw['cmp_w2'][0], lw['cmp_b2'][0], n_rows)
    vc = compress_rows(rows[:, :, 1], lw['cmp_pe'][1], lw['cmp_w1'][1], lw['cmp_b1'][1], lw['cmp_w2'][1], lw['cmp_b2'][1], n_rows)
    nc = kc.shape[1]
    c_start = jnp.arange(nc) * CMP_STRIDE
    c_end = c_start + CMP_BLOCK - 1
    s_start = jnp.arange(ns) * SEL_BLOCK
    cover = ((c_start[:, None] < s_start[None, :] + SEL_BLOCK) & (c_start[:, None] + CMP_BLOCK > s_start[None, :])).astype(jnp.float32)
    ks_b = rows[:, :, 2].reshape(b, ns, SEL_BLOCK, NSA_KV_HEADS, HEAD_DIM)
    vs_b = rows[:, :, 3].reshape(b, ns, SEL_BLOCK, NSA_KV_HEADS, HEAD_DIM)

    if win_buf is None:
        new_win = win_new[:, t - min(WINDOW, n_rows):]
        tr = lambda a: jnp.transpose(a, (0, 2, 1, 3))
        pad_c = lambda a: tr(jnp.pad(a, ((0, 0), (0, LANES - nc), (0, 0), (0, 0))))
        g4 = tr(proj[..., o2:o3].reshape(b, t, NSA_HEADS, 3))
        attn4 = nsa_prompt(tr(q), g4, pad_c(kc), pad_c(vc), tr(kv_new[:, :, 2]), tr(kv_new[:, :, 3]),
                           tr(kv_new[:, :, 4]), tr(kv_new[:, :, 5]), bucket_thresholds(), rel_bias, nc)
        attn = tr(attn4).reshape(b, t, NSA_WIDTH).astype(x.dtype)
        return _layer_tail(x, attn, u, h0_re, h0_im, lw, paged_new, new_win, True)
    if win_buf is None:
        qb = min(Q_BLOCK, t)
        nb = t // qb
        wpad = jnp.concatenate([jnp.zeros((b, WINDOW) + win_new.shape[2:], win_new.dtype), win_new], axis=1)
        widx = jnp.arange(nb)[:, None] * qb + jnp.arange(qb + WINDOW)[None, :]
        w_blocks = jnp.moveaxis(wpad[:, widx], 1, 0)
        w_pos = widx - WINDOW
        w_hist = win_new
    else:
        qb, nb = t, 1
        w_hist = jnp.concatenate([win_buf, win_new], axis=1)
        w_blocks = w_hist[None]
        w_pos = (pos0 - win_buf.shape[1] + jnp.arange(w_hist.shape[1], dtype=jnp.int32))[None]
    new_win = w_hist[:, w_hist.shape[1] - min(WINDOW, n_rows):]

    def run(args):
        q_b, qpos_b, g_b, w_b, wpos_b = args
        return nsa_query_block(q_b, qpos_b, g_b, w_b[:, :, 0], w_b[:, :, 1], wpos_b,
                               kc, vc, c_end, ks_b, vs_b, cover, rel_bias)

    q_blocks = jnp.moveaxis(q.reshape(b, nb, qb, NSA_HEADS, HEAD_DIM), 1, 0)
    g_blocks = jnp.moveaxis(gates.reshape(b, nb, qb, NSA_HEADS, 3), 1, 0)
    attn = lax.map(run, (q_blocks, qpos.reshape(nb, qb), g_blocks, w_blocks, w_pos))
    attn = jnp.moveaxis(attn, 0, 1).reshape(b, t, NSA_WIDTH).astype(x.dtype)
    return _layer_tail(x, attn, u, h0_re, h0_im, lw, paged_new, new_win, False)


ROW_TILE = 256
PROJ_TILE = 512
PROJ_ROWS = 512


def _rms(v, g):
    return v * lax.rsqrt(jnp.mean(v * v, axis=-1, keepdims=True) + RMS_EPS) * g


def _norm_proj_body(x_ref, g_ref, w_ref, o_ref, h_ref):
    @pl.when(pl.program_id(1) == 0)
    def _():
        h_ref[...] = _rms(x_ref[...], g_ref[...]).astype(jnp.bfloat16)

    o_ref[...] = jnp.dot(h_ref[...], w_ref[...], preferred_element_type=jnp.float32)


def norm_proj(x, gain, w):
    n, d = x.shape
    cols = w.shape[1]
    tm = min(n, PROJ_ROWS)
    return pl.pallas_call(
        _norm_proj_body,
        grid=(n // tm, cols // PROJ_TILE),
        in_specs=[pl.BlockSpec((tm, d), lambda i, j: (i, 0)),
                  pl.BlockSpec((1, d), lambda i, j: (0, 0)),
                  pl.BlockSpec((d, PROJ_TILE), lambda i, j: (0, j))],
        out_specs=pl.BlockSpec((tm, PROJ_TILE), lambda i, j: (i, j)),
        out_shape=jax.ShapeDtypeStruct((n, cols), jnp.float32),
        scratch_shapes=[pltpu.VMEM((tm, d), jnp.bfloat16)],
        compiler_params=pltpu.CompilerParams(vmem_limit_bytes=VMEM_LIMIT),
        name="norm_proj",
    )(x, gain.reshape(1, d), w)


def _mix_out_body(attn_ref, y_ref, x_ref, wglu_ref, bglu_ref, ga_ref, gs_ref, wout_ref, gf_ref, wr_ref,
                  x2_ref, hp_ref, lg_ref):
    f32, bf16 = jnp.float32, jnp.bfloat16
    y = y_ref[...]
    ssm = y * jax.nn.sigmoid(jnp.dot(y.astype(bf16), wglu_ref[...], preferred_element_type=f32) + bglu_ref[...])
    merged = jnp.concatenate([_rms(attn_ref[...], ga_ref[...]), _rms(ssm, gs_ref[...])], axis=1).astype(bf16)
    x2 = x_ref[...] + jnp.dot(merged, wout_ref[...], preferred_element_type=f32)
    x2_ref[...] = x2
    hm = _rms(x2, gf_ref[...]).astype(bf16)
    lg_ref[...] = jnp.dot(hm, wr_ref[...], preferred_element_type=f32)
    bits = lax.bitcast_convert_type(hm.astype(f32), jnp.uint32)
    half = bits.shape[1] // 2
    hp_ref[...] = lax.bitcast_convert_type(bits[:, :half] | (bits[:, half:] >> 16), f32)


def mix_out(attn, y, x, lw):
    n = x.shape[0]
    f32, bf16 = jnp.float32, jnp.bfloat16
    row = lambda a: a.astype(f32).reshape(1, -1)
    wr = jnp.pad(lw['router_w'], ((0, 0), (0, LANES - N_EXPERTS))).astype(bf16)
    consts = [lw['w_glu'].astype(bf16), row(lw['b_glu']), row(lw['norm_attn_out']), row(lw['norm_ssm_out']),
              lw['w_out'].astype(bf16), row(lw['norm_ffn']), wr]
    tile = lambda c: pl.BlockSpec((ROW_TILE, c), lambda i: (i, 0))
    full = lambda a: pl.BlockSpec(a.shape, lambda i: (0, 0))
    return pl.pallas_call(
        _mix_out_body,
        grid=(n // ROW_TILE,),
        in_specs=[tile(NSA_WIDTH), tile(SSM_WIDTH), tile(D_MODEL)] + [full(c) for c in consts],
        out_specs=[tile(D_MODEL), tile(D_MODEL // 2), tile(LANES)],
        out_shape=[jax.ShapeDtypeStruct((n, D_MODEL), f32), jax.ShapeDtypeStruct((n, D_MODEL // 2), f32),
                   jax.ShapeDtypeStruct((n, LANES), f32)],
        compiler_params=pltpu.CompilerParams(vmem_limit_bytes=VMEM_LIMIT),
        name="mix_out",
    )(attn, y, x, *consts)


def _combine_body(x_ref, y0_ref, y1_ref, y2_ref, y3_ref, gate_ref, g_ref, o_ref, *, normed):
    gate = gate_ref[...]
    moe = gate[:, 0:1] * y0_ref[...]
    for k, y_ref in enumerate((y1_ref, y2_ref, y3_ref), start=1):
        moe = moe + gate[:, k:k + 1] * y_ref[...]
    out = x_ref[...] + moe
    o_ref[...] = _rms(out, g_ref[...]) if normed else out


def combine_norm(x, ys, gate, gain):
    n, d = x.shape
    normed = gain is not None
    gain = jnp.ones((d,), jnp.float32) if gain is None else gain
    tile = lambda c: pl.BlockSpec((ROW_TILE, c), lambda i: (i, 0))
    return pl.pallas_call(
        functools.partial(_combine_body, normed=normed),
        grid=(n // ROW_TILE,),
        in_specs=[tile(d)] * (1 + TOP_K) + [tile(TOP_K), pl.BlockSpec((1, d), lambda i: (0, 0))],
        out_specs=tile(d),
        out_shape=jax.ShapeDtypeStruct((n, d), jnp.float32),
        name="combine_norm",
    )(x, *ys, gate, gain.astype(jnp.float32).reshape(1, d))


def moe_routed(hp, logits, router_b, w_gu, b_gu, w_down, b_down):
    n = hp.shape[0]
    top_val, top_idx = lax.top_k(logits[:, :N_EXPERTS] + router_b.astype(jnp.float32), TOP_K)
    gate = jax.nn.softmax(top_val, axis=-1)
    nk = n * TOP_K
    n_items = N_EXPERTS + nk // MOE_ROWS
    flat_e = top_idx.reshape(nk)
    onehot = (flat_e[:, None] == jnp.arange(N_EXPERTS, dtype=flat_e.dtype)[None, :]).astype(jnp.int32)
    running = jnp.cumsum(onehot, axis=0)
    counts = running[-1]
    pos_in_e = jnp.sum(onehot * running, axis=1) - 1
    items_e = (counts + MOE_ROWS - 1) // MOE_ROWS
    item_end = jnp.cumsum(items_e)
    item_start = item_end - items_e
    total = item_end[-1]
    dest = (item_start[flat_e] * MOE_ROWS + pos_in_e).astype(jnp.int32)
    row_tok = jnp.zeros(n_items * MOE_ROWS, jnp.int32).at[dest].set(jnp.arange(nk, dtype=jnp.int32) // TOP_K)
    item = jnp.minimum(jnp.arange(n_items, dtype=jnp.int32), total - 1)
    item_e = jnp.minimum(jnp.searchsorted(item_end, item, side='right'), N_EXPERTS - 1).astype(jnp.int32)
    rows = jnp.clip(counts[item_e] - (item - item_start[item_e]) * MOE_ROWS, 0, MOE_ROWS)
    item_sub = jnp.where(jnp.arange(n_items) < total, (rows + MOE_SUB - 1) // MOE_SUB, 0).astype(jnp.int32)
    item_ok = (jnp.arange(n_items) < total).astype(jnp.int32)
    yb = moe_experts(item_e, item_sub, item.astype(jnp.int32), item_ok, hp[row_tok], w_gu, b_gu, w_down, b_down)
    return yb, dest.reshape(n, TOP_K), gate


def mixer_layer(x, pos0, pool, page_table, win_buf, h0_re, h0_im, rel_bias, lw):
    b, t, _ = x.shape
    w_in = jnp.pad(lw['w_in'], ((0, 0), (0, -IN_WIDTH % PROJ_TILE))).astype(jnp.bfloat16)
    proj = norm_proj(x.reshape(b * t, D_MODEL), lw['norm_mix'].astype(jnp.float32), w_in).reshape(b, t, -1)
    o1 = NSA_WIDTH
    o2 = o1 + N_KV * KV_WIDTH
    o3 = o2 + 3 * NSA_HEADS
    q = proj[..., :o1].reshape(b, t, NSA_HEADS, HEAD_DIM)
    kv_new = proj[..., o1:o2].reshape(b, t, N_KV, NSA_KV_HEADS, HEAD_DIM)
    u = proj[..., o3:IN_WIDTH]
    paged_new = kv_new[:, :, :N_PAGED]
    win_new = kv_new[:, :, N_PAGED:]
    n_rows = pos0 + t
    n_cmp = (n_rows - CMP_BLOCK) // CMP_STRIDE + 1
    thr = bucket_thresholds()
    cmp_w = (lw['cmp_pe'], lw['cmp_w1'], lw['cmp_b1'], lw['cmp_w2'], lw['cmp_b2'])
    if pool is None:
        assert pos0 == 0 and t % 128 == 0
        pages = t // 128
        own_pool = paged_new.reshape(b * pages, 128, ROW_LANES)
        own_table = jnp.arange(b * pages, dtype=jnp.int32).reshape(b, pages)
        kcvc = compress_pages(own_pool, own_table, *cmp_w, n_g=pages, transposed=False)
        attn = nsa_prefill(q, proj[..., o2:o3], kcvc, kv_new, thr, rel_bias, n_cmp).astype(x.dtype)
        w_hist = win_new
    else:
        kcvc = compress_pages(pool, page_table, *cmp_w, n_g=CMP_PAGES, transposed=True)
        attn = nsa_decode(q, proj[..., o2:o3], kv_new, kcvc, pool, page_table, win_buf, thr, rel_bias, pos0).astype(x.dtype)
        w_hist = jnp.concatenate([win_buf, win_new], axis=1)
    new_win = w_hist[:, w_hist.shape[1] - min(WINDOW, n_rows):]
    y, h_re, h_im = s5_branch(u, h0_re, h0_im, lw, pool is None)
    x2, hp, logits = mix_out(attn.reshape(b * t, NSA_WIDTH), y.reshape(b * t, SSM_WIDTH), x.reshape(b * t, D_MODEL), lw)
    return (x2, hp, logits), paged_new, new_win, h_re, h_im


def s5_branch(u, h0_re, h0_im, lw, chained):
    bsz, t, _ = u.shape
    f32 = jnp.float32
    a, bcat, ccat = s5_discretize(lw['lam_re'], lw['lam_im'], lw['log_dt'], lw['b_re'], lw['b_im'], lw['c_re'], lw['c_im'])
    d_skip = lw['d_skip'].astype(f32)
    flat = lambda h: h.astype(f32).reshape(bsz, SSM_GROUPS * SSM_STATE)
    if chained:
        n_r = t // S5_CHUNK
        u4 = jnp.transpose(u.astype(f32).reshape(bsz, n_r, S5_CHUNK, SSM_WIDTH), (0, 2, 1, 3))
        y4, h_re, h_im = s5_scan(u4, flat(h0_re)[:, None], flat(h0_im)[:, None], a, bcat, ccat, d_skip, True)
        y = jnp.transpose(y4, (0, 2, 1, 3)).reshape(bsz, t, SSM_WIDTH)
        h_re, h_im = h_re[:, 0], h_im[:, 0]
    else:
        u4 = jnp.transpose(u.astype(f32), (1, 0, 2))[None]
        y4, h_re, h_im = s5_scan(u4, flat(h0_re)[None], flat(h0_im)[None], a, bcat, ccat, d_skip, False)
        y = jnp.transpose(y4[0], (1, 0, 2))
        h_re, h_im = h_re[0], h_im[0]
    st = lambda h: h.reshape(bsz, SSM_GROUPS, SSM_STATE).astype(u.dtype)
    return y, st(h_re), st(h_im)


def moe_residual(groups, lw, final_gain):
    sizes = [g[0].shape[0] for g in groups]
    hp = jnp.concatenate([g[1] for g in groups], axis=0)
    logits = jnp.concatenate([g[2] for g in groups], axis=0)
    yb, dest, gate = moe_routed(hp, logits, lw['router_b'], lw['w_gu'], lw['b_gu'], lw['w_down'], lw['b_down'])
    outs, start = [], 0
    for (x2, _, _), n in zip(groups, sizes):
        rows = slice(start, start + n)
        outs.append(combine_norm(x2, [yb[dest[rows, k]] for k in range(TOP_K)], gate[rows], final_gain))
        start += n
    return outs


def kernel(x_prompt, x_sample, cache_nsa_kv, cache_win_kv, state_ssm_re, state_ssm_im, page_table, rel_bias,
           norm_mix, w_in, cmp_pe, cmp_w1, cmp_b1, cmp_w2, cmp_b2, ssm_lam_re, ssm_lam_im, ssm_log_dt,
           ssm_b_re, ssm_b_im, ssm_c_re, ssm_c_im, ssm_d, ssm_w_glu, ssm_b_glu, norm_attn_out, norm_ssm_out,
           w_out, norm_ffn, router_w, router_b, w_gu, b_gu, w_down, b_down, norm_final):
    n_seq, n_pages = page_table.shape
    past_len = n_pages * cache_nsa_kv.shape[2]
    xp, xs = x_prompt, x_sample
    kv_p, win_p, sre_p, sim_p = [], [], [], []
    kv_s, win_s, sre_s, sim_s = [], [], [], []
    for i in range(DEPTH):
        lw = dict(norm_mix=norm_mix[i], w_in=w_in[i], cmp_pe=cmp_pe[i], cmp_w1=cmp_w1[i], cmp_b1=cmp_b1[i],
                  cmp_w2=cmp_w2[i], cmp_b2=cmp_b2[i], lam_re=ssm_lam_re[i], lam_im=ssm_lam_im[i],
                  log_dt=ssm_log_dt[i], b_re=ssm_b_re[i], b_im=ssm_b_im[i], c_re=ssm_c_re[i], c_im=ssm_c_im[i],
                  d_skip=ssm_d[i], w_glu=ssm_w_glu[i], b_glu=ssm_b_glu[i], norm_attn_out=norm_attn_out[i],
                  norm_ssm_out=norm_ssm_out[i], w_out=w_out[i], norm_ffn=norm_ffn[i], router_w=router_w[i],
                  router_b=router_b[i], w_gu=w_gu[i], b_gu=b_gu[i], w_down=w_down[i], b_down=b_down[i])
        h0 = jnp.zeros((xp.shape[0], SSM_GROUPS, SSM_STATE), xp.dtype)
        gp, kv1, w1, r1, m1 = mixer_layer(xp, 0, None, None, None, h0, h0, rel_bias, lw)
        pool = jnp.transpose(cache_nsa_kv[i].reshape(cache_nsa_kv.shape[1], cache_nsa_kv.shape[2], ROW_LANES), (0, 2, 1))
        gs, kv2, w2, r2, m2 = mixer_layer(xs, past_len, pool, page_table, cache_win_kv[i], state_ssm_re[i], state_ssm_im[i], rel_bias, lw)
        op, os_ = moe_residual([gp, gs], lw, norm_final if i == DEPTH - 1 else None)
        xp, xs = op.reshape(xp.shape), os_.reshape(xs.shape)
        kv_p.append(kv1); win_p.append(w1); sre_p.append(r1); sim_p.append(m1)
        kv_s.append(kv2); win_s.append(w2); sre_s.append(r2); sim_s.append(m2)
    return (xp, xs, jnp.stack(kv_p), jnp.stack(win_p), jnp.stack(sre_p), jnp.stack(sim_p),
            jnp.stack(kv_s), jnp.stack(win_s), jnp.stack(sre_s), jnp.stack(sim_s))
```

```python
import functools
import math
import jax, jax.numpy as jnp
from jax import lax
from jax.experimental import pallas as pl
from jax.experimental.pallas import tpu as pltpu

D_MODEL = 2048
DEPTH = 1
NSA_HEADS = 16
NSA_KV_HEADS = 2
HEAD_DIM = 64
NSA_WIDTH = NSA_HEADS * HEAD_DIM
KV_WIDTH = NSA_KV_HEADS * HEAD_DIM
N_PAGED = 4
N_KV = 6
SSM_WIDTH = D_MODEL - NSA_WIDTH
SSM_CH = 16
SSM_GROUPS = SSM_WIDTH // SSM_CH
SSM_STATE = 64
IN_WIDTH = NSA_WIDTH + N_KV * KV_WIDTH + 3 * NSA_HEADS + SSM_WIDTH
CMP_BLOCK = 32
CMP_STRIDE = 16
CMP_HIDDEN = 2 * HEAD_DIM
SEL_BLOCK = 64
SEL_TOP_N = 16
WINDOW = 512
FORCE_SCORE = 1e6
REL_BUCKETS = 32
REL_MAX_DIST = 4096
N_EXPERTS = 32
TOP_K = 4
D_FF = D_MODEL
SWIGLU_LIMIT = 7.0
SWIGLU_ALPHA = 1.702
RMS_EPS = 1e-5
NEG_INF = -1e30


GRP = NSA_HEADS // NSA_KV_HEADS
TQ = 128
TK = 128
KEY_TILES = 2
LANES = 128
BUCKET_TABLE_LEN = 32768
BIAS_TILE_ELEMS = 16384
VMEM_LIMIT = 56 * 1024 * 1024


def rel_bucket(dist):
    n = jnp.maximum(dist, 0)
    exact = REL_BUCKETS // 2
    nf = jnp.maximum(n, 1).astype(jnp.float32)
    large = exact + (jnp.log(nf / exact) / math.log(REL_MAX_DIST / exact) * (REL_BUCKETS - exact)).astype(jnp.int32)
    return jnp.where(n < exact, n, jnp.minimum(large, REL_BUCKETS - 1))


def bucket_thresholds():
    tab = rel_bucket(jnp.arange(BUCKET_TABLE_LEN, dtype=jnp.int32))
    return jnp.sum(tab[None, :] < jnp.arange(REL_BUCKETS, dtype=jnp.int32)[:, None], axis=1).astype(jnp.int32)


def _bias_table_body(thr_ref, rb_ref, d_ref, o_ref):
    n = jnp.maximum(d_ref[0], 0)
    for h in range(NSA_HEADS):
        val = jnp.full(n.shape, rb_ref[h], jnp.float32)
        for k in range(1, REL_BUCKETS):
            val = jnp.where(n >= thr_ref[k], rb_ref[k * NSA_HEADS + h], val)
        o_ref[h // GRP, 0, h % GRP] = val


def bias_tables(dist, thr, rel_bias):
    n, r, c = dist.shape
    ct = min(c, BIAS_TILE_ELEMS // r)
    assert c % ct == 0
    return pl.pallas_call(
        _bias_table_body,
        grid_spec=pltpu.PrefetchScalarGridSpec(
            num_scalar_prefetch=2,
            grid=(n, c // ct),
            in_specs=[pl.BlockSpec((1, r, ct), lambda i, j, *_: (i, 0, j))],
            out_specs=pl.BlockSpec((NSA_KV_HEADS, 1, GRP, r, ct), lambda i, j, *_: (0, i, 0, 0, j)),
        ),
        out_shape=jax.ShapeDtypeStruct((NSA_KV_HEADS, n, GRP, r, c), jnp.float32),
        name="bias_tables",
    )(thr, rel_bias.reshape(-1), dist)


def _nsa_prefill_body(q_ref, g_ref, kc_ref, vct_ref, ks_ref, vst_ref, kw_ref, vwt_ref, bt_ref, bc_ref, cov_ref, exp_ref,
                      o_ref, mask_ref, *, n_cmp, n_sel):
    f32, bf16 = jnp.float32, jnp.bfloat16
    qi = pl.program_id(2)
    nt_dims = (((1,), (1,)), ((), ()))
    q = (q_ref[0] * (HEAD_DIM ** -0.5)).reshape(GRP * TQ, HEAD_DIM).astype(bf16)
    lanes = [slice(g * TQ, (g + 1) * TQ) for g in range(GRP)]
    row = lax.broadcasted_iota(jnp.int32, (LANES, TQ), 0)
    qpos = qi * TQ + lax.broadcasted_iota(jnp.int32, (LANES, TQ), 1)

    ok_c = (qpos >= CMP_STRIDE * row + (CMP_BLOCK - 1)) & (row < n_cmp)
    s_c = lax.dot_general(kc_ref[0, 0].astype(bf16), q, nt_dims, preferred_element_type=f32)
    p_sum = jnp.zeros((LANES, TQ), f32)
    p_parts = []
    for g in range(GRP):
        p = _masked_softmax_rows(s_c[:, lanes[g]] + bc_ref[0, 0, g], ok_c)
        p_sum = p_sum + p
        p_parts.append(p.astype(bf16))
    o_c = jnp.dot(vct_ref[0, 0].astype(bf16), jnp.concatenate(p_parts, axis=1), preferred_element_type=f32)

    imp = jnp.dot(cov_ref[...], p_sum.astype(bf16), preferred_element_type=f32)
    rs = -(-n_sel // 8) * 8
    blk = lax.broadcasted_iota(jnp.int32, (rs, TQ), 0)
    qp = qi * TQ + lax.broadcasted_iota(jnp.int32, (rs, TQ), 1)
    cur = qp // SEL_BLOCK
    forced = (blk == 0) | (blk == cur) | (blk == cur - 1)
    score = jnp.where(forced, FORCE_SCORE, jnp.where(blk * SEL_BLOCK <= qp, imp[:rs], -1.0))
    score = jnp.where(blk < n_sel, score, -jnp.inf)
    sel = jnp.zeros((rs, TQ), f32)
    for s in range(n_sel):
        r = jnp.max(jnp.where(blk == s, score, -jnp.inf), axis=0, keepdims=True)
        beats = (score > r) | ((score == r) & (blk < s))
        rank = jnp.sum(jnp.where(beats, 1.0, 0.0), axis=0, keepdims=True)
        sel = jnp.where((blk == s) & (rank < min(SEL_TOP_N, n_sel)), 1.0, sel)
    sel = jnp.concatenate([sel, jnp.zeros((LANES - rs, TQ), f32)], axis=0)
    mask_ref[...] = jnp.dot(exp_ref[...], sel.astype(bf16), preferred_element_type=f32)

    key_i = lax.broadcasted_iota(jnp.int32, (KEY_TILES * TK, TQ), 0)
    qry_i = lax.broadcasted_iota(jnp.int32, (KEY_TILES * TK, TQ), 1)
    init = (jnp.full((1, GRP * TQ), NEG_INF, f32), jnp.zeros((1, GRP * TQ), f32), jnp.zeros((HEAD_DIM, GRP * TQ), f32))
    n_diag = bt_ref.shape[1]

    def tile(kk, carry, k_ref, vt_ref, msk):
        m, l, acc = carry
        keys = pl.ds(pl.multiple_of(kk * (KEY_TILES * TK), KEY_TILES * TK), KEY_TILES * TK)
        s_all = lax.dot_general(k_ref[0, 0, keys, :].astype(bf16), q, nt_dims, preferred_element_type=f32)
        diag = [jnp.clip(qi - (KEY_TILES * kk + j), 0, n_diag - 1) for j in range(KEY_TILES)]
        m_parts, l_parts, a_parts, p_parts = [], [], [], []
        for g in range(GRP):
            bias = jnp.concatenate([bt_ref[0, dg, g] for dg in diag], axis=0)
            s = jnp.where(msk, s_all[:, lanes[g]] + bias, NEG_INF)
            m_new = jnp.maximum(m[:, lanes[g]], jnp.max(s, axis=0, keepdims=True))
            p = jnp.where(msk, jnp.exp(s - m_new), 0.0)
            alpha = jnp.exp(m[:, lanes[g]] - m_new)
            m_parts.append(m_new)
            a_parts.append(alpha)
            l_parts.append(alpha * l[:, lanes[g]] + jnp.sum(p, axis=0, keepdims=True))
            p_parts.append(p.astype(bf16))
        pv = jnp.dot(vt_ref[0, 0, :, keys].astype(bf16), jnp.concatenate(p_parts, axis=1), preferred_element_type=f32)
        return (jnp.concatenate(m_parts, axis=1), jnp.concatenate(l_parts, axis=1),
                jnp.concatenate(a_parts, axis=1) * acc + pv)

    def sel_step(kk, carry):
        dist = qi * TQ + qry_i - (kk * (KEY_TILES * TK) + key_i)
        rows = pl.ds(pl.multiple_of(kk * (KEY_TILES * TK), KEY_TILES * TK), KEY_TILES * TK)
        return tile(kk, carry, ks_ref, vst_ref, (mask_ref[rows, :] > 0.5) & (dist >= 0))

    def win_step(kk, carry):
        dist = qi * TQ + qry_i - (kk * (KEY_TILES * TK) + key_i)
        return tile(kk, carry, kw_ref, vwt_ref, (dist >= 0) & (dist < WINDOW))

    last = qi // KEY_TILES + 1
    _, l_s, acc_s = lax.fori_loop(0, last, sel_step, init)
    _, l_w, acc_w = lax.fori_loop(jnp.maximum(qi - WINDOW // TK, 0) // KEY_TILES, last, win_step, init)
    o_s = acc_s / jnp.maximum(l_s, 1e-30)
    o_w = acc_w / jnp.maximum(l_w, 1e-30)
    for g in range(GRP):
        gate = jax.nn.sigmoid(g_ref[0, 0, :, g, :])
        o_ref[0, g] = gate[0:1] * o_c[:, lanes[g]] + gate[1:2] * o_s[:, lanes[g]] + gate[2:3] * o_w[:, lanes[g]]


def _masked_softmax_rows(s, ok):
    s = jnp.where(ok, s, NEG_INF)
    e = jnp.where(ok, jnp.exp(s - jnp.max(s, axis=0, keepdims=True)), 0.0)
    return e / jnp.maximum(jnp.sum(e, axis=0, keepdims=True), 1e-30)


def nsa_prefill(q, glog, kcvc, kv_new, thr, rel_bias, n_cmp):
    f32, bf16 = jnp.float32, jnp.bfloat16
    b, t = q.shape[:2]
    nq = t // TQ
    n_sel = t // SEL_BLOCK
    assert t % (KEY_TILES * TK) == 0 and n_sel <= LANES and n_cmp < LANES and kcvc.shape[1] == LANES and TQ == TK == LANES
    q4 = jnp.transpose(q, (0, 2, 1, 3))
    g5 = jnp.transpose(glog.reshape(b, t, NSA_KV_HEADS, GRP, 3), (0, 2, 4, 3, 1))
    cmp_tok = jnp.pad(kcvc[:, 1:], ((0, 0), (0, 1), (0, 0))).reshape(b, LANES, 2, NSA_KV_HEADS, HEAD_DIM)
    kc = jnp.transpose(cmp_tok[:, :, 0], (0, 2, 1, 3))
    vct = jnp.transpose(cmp_tok[:, :, 1], (0, 2, 3, 1))
    rows = lambda n: jnp.transpose(kv_new[:, :, n], (0, 2, 1, 3))
    cols = lambda n: jnp.transpose(kv_new[:, :, n], (0, 2, 3, 1))
    i = jnp.arange(TQ, dtype=jnp.int32)
    d_toep = jnp.arange(nq, dtype=jnp.int32)[:, None, None] * TQ + i[None, None, :] - i[None, :, None]
    c_end = CMP_STRIDE * jnp.arange(LANES, dtype=jnp.int32) + CMP_BLOCK - 1
    d_cmp = jnp.arange(t, dtype=jnp.int32).reshape(nq, 1, TQ) - c_end[None, :, None]
    tabs = bias_tables(jnp.concatenate([d_toep, d_cmp], axis=0), thr, rel_bias)
    bt, bc = tabs[:, :nq], tabs[:, nq:]
    s_start = SEL_BLOCK * jnp.arange(LANES)[:, None]
    c_start = CMP_STRIDE * jnp.arange(LANES)[None, :]
    cover = ((c_start < s_start + SEL_BLOCK) & (c_start + CMP_BLOCK > s_start)
             & (jnp.arange(LANES)[None, :] < n_cmp) & (jnp.arange(LANES)[:, None] < n_sel)).astype(bf16)
    expand = (jnp.arange(t)[:, None] // SEL_BLOCK == jnp.arange(LANES)[None, :]).astype(bf16)
    spec = lambda *blk: pl.BlockSpec((1, 1) + blk, lambda h, bb, qq: (bb, h) + (0,) * len(blk))
    out = pl.pallas_call(
        functools.partial(_nsa_prefill_body, n_cmp=n_cmp, n_sel=n_sel),
        grid=(NSA_KV_HEADS, b, nq),
        in_specs=[pl.BlockSpec((1, GRP, TQ, HEAD_DIM), lambda h, bb, qq: (bb, h, qq, 0)),
                  pl.BlockSpec((1, 1, 3, GRP, TQ), lambda h, bb, qq: (bb, h, 0, 0, qq)),
                  spec(LANES, HEAD_DIM), spec(HEAD_DIM, LANES),
                  spec(t, HEAD_DIM), spec(HEAD_DIM, t), spec(t, HEAD_DIM), spec(HEAD_DIM, t),
                  pl.BlockSpec((1, nq, GRP, TK, TQ), lambda h, bb, qq: (h, 0, 0, 0, 0)),
                  pl.BlockSpec((1, 1, GRP, LANES, TQ), lambda h, bb, qq: (h, qq, 0, 0, 0)),
                  pl.BlockSpec((LANES, LANES), lambda h, bb, qq: (0, 0)),
                  pl.BlockSpec((t, LANES), lambda h, bb, qq: (0, 0))],
        out_specs=pl.BlockSpec((1, GRP, HEAD_DIM, TQ), lambda h, bb, qq: (bb, h, 0, qq)),
        out_shape=jax.ShapeDtypeStruct((b, NSA_HEADS, HEAD_DIM, t), f32),
        scratch_shapes=[pltpu.VMEM((t, TQ), f32)],
        compiler_params=pltpu.CompilerParams(vmem_limit_bytes=VMEM_LIMIT),
        name="nsa_prefill",
    )(q4, g5, kc, vct, rows(2), cols(3), rows(4), cols(5), bt, bc, cover, expand)
    return jnp.transpose(out, (0, 3, 1, 2)).reshape(b, t, NSA_WIDTH)


S5_GB = 8
S5_CH = S5_GB * SSM_CH
S5_ST = S5_GB * SSM_STATE


def _cmul(a_re, a_im, b_re, b_im):
    return a_re * b_re - a_im * b_im, a_re * b_im + a_im * b_re


def _s5_body(u_ref, h0re_ref, h0im_ref, a_ref, bcat_ref, ccat_ref, d_ref, y_ref, hre_ref, him_ref, xs_ref, hin_ref,
             *, chained):
    f32, bf16 = jnp.float32, jnp.bfloat16
    n_l, n_r = u_ref.shape[1], u_ref.shape[2]
    u2 = u_ref[0].reshape(n_l * n_r, S5_CH)
    xs_ref[...] = jnp.dot(u2.astype(bf16), bcat_ref[0].astype(bf16),
                          preferred_element_type=f32).reshape(n_l, n_r, 2 * S5_ST)
    a_re, a_im = a_ref[0:1, :], a_ref[1:2, :]

    def scan_step(j, carry):
        h_re, h_im, p_re, p_im = carry
        x = xs_ref[j]
        t_re, t_im = _cmul(a_re, a_im, h_re, h_im)
        h_re, h_im = t_re + x[:, :S5_ST], t_im + x[:, S5_ST:]
        xs_ref[j] = jnp.concatenate([h_re, h_im], axis=1)
        return (h_re, h_im) + _cmul(a_re, a_im, p_re, p_im)

    if chained:
        start = (jnp.zeros((n_r, S5_ST), f32), jnp.zeros((n_r, S5_ST), f32))
    else:
        start = (h0re_ref[0], h0im_ref[0])
    ones = (jnp.ones((1, S5_ST), f32), jnp.zeros((1, S5_ST), f32))
    h_re, h_im, al_re, al_im = lax.fori_loop(0, n_l, scan_step, start + ones)

    if chained:
        hin_ref[0:1, :] = jnp.concatenate([h0re_ref[0], h0im_ref[0]], axis=1)

        def chain_step(c, carry):
            z = xs_ref[n_l - 1, pl.ds(c - 1, 1), :]
            t_re, t_im = _cmul(al_re, al_im, *carry)
            n_re, n_im = t_re + z[:, :S5_ST], t_im + z[:, S5_ST:]
            hin_ref[pl.ds(c, 1), :] = jnp.concatenate([n_re, n_im], axis=1)
            return n_re, n_im

        h_re, h_im = lax.fori_loop(1, n_r + 1, chain_step, (h0re_ref[0], h0im_ref[0]))

        def fix_step(j, carry):
            p_re, p_im = carry
            hin = hin_ref[0:n_r, :]
            t_re, t_im = _cmul(p_re, p_im, hin[:, :S5_ST], hin[:, S5_ST:])
            xs_ref[j] = xs_ref[j] + jnp.concatenate([t_re, t_im], axis=1)
            return _cmul(a_re, a_im, p_re, p_im)

        lax.fori_loop(0, n_l, fix_step, (a_re, a_im))

    hre_ref[0] = h_re
    him_ref[0] = h_im
    hs = xs_ref[...].reshape(n_l * n_r, 2 * S5_ST).astype(bf16)
    y = jnp.dot(hs, ccat_ref[0].astype(bf16), preferred_element_type=f32) + d_ref[...] * u2
    y_ref[0] = jax.nn.gelu(y).reshape(n_l, n_r, S5_CH)


def s5_discretize(lam_re, lam_im, log_dt, b_re, b_im, c_re, c_im):
    f32 = jnp.float32
    dt = jnp.exp(log_dt.astype(f32))[:, None]
    lr, li = lam_re.astype(f32), lam_im.astype(f32)
    mag = jnp.exp(lr * dt)
    a_re, a_im = mag * jnp.cos(li * dt), mag * jnp.sin(li * dt)
    den = lr * lr + li * li
    z_re = ((a_re - 1.0) * lr + a_im * li) / den
    z_im = (a_im * lr - (a_re - 1.0) * li) / den
    br, bim = b_re.astype(f32), b_im.astype(f32)
    bb_re = z_re[..., None] * br - z_im[..., None] * bim
    bb_im = z_re[..., None] * bim + z_im[..., None] * br
    ngb = SSM_GROUPS // S5_GB
    eye = jnp.eye(S5_GB, dtype=f32)

    def block_diag(w):
        wd = w[:, :, :, None, :] * eye[None, :, None, :, None]
        return wd.reshape(ngb, S5_GB * w.shape[2], S5_GB * w.shape[3])

    def pack_b(bb):
        return block_diag(jnp.swapaxes(bb.reshape(ngb, S5_GB, SSM_STATE, SSM_CH), 2, 3))

    def pack_c(cc):
        return block_diag(jnp.swapaxes(cc.reshape(ngb, S5_GB, SSM_CH, SSM_STATE), 2, 3))

    a = jnp.stack([a_re.reshape(-1), a_im.reshape(-1)])
    bcat = jnp.concatenate([pack_b(bb_re), pack_b(bb_im)], axis=2)
    ccat = jnp.concatenate([pack_c(c_re.astype(f32)), -pack_c(c_im.astype(f32))], axis=1)
    return a, bcat, ccat


def s5_scan(u4, h0_re, h0_im, a, bcat, ccat, d_skip, chained):
    nb, n_l, n_r, _ = u4.shape
    rh = h0_re.shape[1]
    ngb = SSM_GROUPS // S5_GB
    st_spec = pl.BlockSpec((1, rh, S5_ST), lambda i, j: (i, 0, j))
    return pl.pallas_call(
        functools.partial(_s5_body, chained=chained),
        grid=(nb, ngb),
        in_specs=[pl.BlockSpec((1, n_l, n_r, S5_CH), lambda i, j: (i, 0, 0, j)), st_spec, st_spec,
                  pl.BlockSpec((2, S5_ST), lambda i, j: (0, j)),
                  pl.BlockSpec((1, S5_CH, 2 * S5_ST), lambda i, j: (j, 0, 0)),
                  pl.BlockSpec((1, 2 * S5_ST, S5_CH), lambda i, j: (j, 0, 0)),
                  pl.BlockSpec((1, S5_CH), lambda i, j: (0, j))],
        out_specs=[pl.BlockSpec((1, n_l, n_r, S5_CH), lambda i, j: (i, 0, 0, j)), st_spec, st_spec],
        out_shape=[jax.ShapeDtypeStruct(u4.shape, jnp.float32),
                   jax.ShapeDtypeStruct(h0_re.shape, jnp.float32), jax.ShapeDtypeStruct(h0_re.shape, jnp.float32)],
        scratch_shapes=[pltpu.VMEM((n_l, n_r, 2 * S5_ST), jnp.float32), pltpu.VMEM((n_r + 8, 2 * S5_ST), jnp.float32)],
        compiler_params=pltpu.CompilerParams(vmem_limit_bytes=VMEM_LIMIT),
        name="s5_scan",
    )(u4, h0_re, h0_im, a, bcat, ccat, d_skip.reshape(1, SSM_WIDTH))


S5_CHUNK = 64


ROW_LANES = N_PAGED * KV_WIDTH
CHUNKS_PER_PAGE = 128 // CMP_STRIDE
CMP_HID2 = NSA_KV_HEADS * CMP_HIDDEN
CMP_PAGES = 64


def _compress_body(pt_ref, *refs, n_g, transposed):
    f32, bf16 = jnp.float32, jnp.bfloat16
    page_refs = (refs[:n_g], refs[n_g:2 * n_g])
    w1_ref, c1_ref, w2_ref, b2_ref, o_ref, carry_ref, rows_ref = refs[2 * n_g:]
    m_rows = n_g * CHUNKS_PER_PAGE
    first = pl.program_id(1) == 0
    row_id = lax.broadcasted_iota(jnp.int32, (m_rows, CMP_HID2), 0)
    outs = []
    for t in range(2):
        for k, r in enumerate(page_refs[t]):
            rows_ref[k * 128:(k + 1) * 128, :] = r[0].T if transposed else r[0]
        cols = [rows_ref[pl.ds(j, m_rows, stride=CMP_STRIDE), :] for j in range(CMP_STRIDE)]
        x = jnp.concatenate(cols, axis=1).astype(bf16)
        part = jnp.dot(x, w1_ref[t], preferred_element_type=f32)
        p0, p1 = part[:, :CMP_HID2], part[:, CMP_HID2:]
        prev = jnp.where(first, 0.0, carry_ref[t, 0:1, :])
        shifted = jnp.where(row_id == 0, prev, pltpu.roll(p0, 1, axis=0))
        carry_ref[t, 0:1, :] = p0[m_rows - 1:m_rows, :]
        h1 = (c1_ref[t] + shifted) + p1
        outs.append(jnp.dot(jax.nn.gelu(h1).astype(bf16), w2_ref[t], preferred_element_type=f32) + b2_ref[t])
    o_ref[0] = jnp.concatenate(outs, axis=1)


def compress_pages(pool, page_table, cmp_pe, cmp_w1, cmp_b1, cmp_w2, cmp_b2, n_g, transposed):
    n_seq, n_pages = page_table.shape
    assert n_pages % n_g == 0 and pool.shape[1:] == ((ROW_LANES, 128) if transposed else (128, ROW_LANES))
    f32, bf16 = jnp.float32, jnp.bfloat16
    r = CMP_BLOCK // CMP_STRIDE
    eye = jnp.eye(NSA_KV_HEADS, dtype=f32)
    w1 = cmp_w1.astype(f32).reshape(2, r, CMP_STRIDE, HEAD_DIM, CMP_HIDDEN)
    w1 = jnp.transpose(w1, (0, 2, 3, 1, 4))[:, :, None, :, :, None, :] * eye[None, None, :, None, None, :, None]
    w1 = w1.reshape(2, CMP_STRIDE * KV_WIDTH, r * CMP_HID2).astype(bf16)
    c1 = jnp.stack([jnp.einsum('ld,ldf->f', cmp_pe[t], cmp_w1[t]) + cmp_b1[t] for t in range(2)])
    c1 = jnp.tile(c1[:, None, :], (1, 1, NSA_KV_HEADS))
    w2 = (cmp_w2.astype(f32)[:, None, :, None, :] * eye[None, :, None, :, None]).reshape(2, CMP_HID2, KV_WIDTH).astype(bf16)
    b2 = jnp.tile(cmp_b2.astype(f32)[:, None, :], (1, 1, NSA_KV_HEADS))
    m_rows = n_g * CHUNKS_PER_PAGE
    page_spec = lambda k, t: pl.BlockSpec((1, 128, KV_WIDTH), lambda b, g, pt: (
        (pt[b * n_pages + g * n_g + k], t, 0) if transposed else (pt[b * n_pages + g * n_g + k], 0, t)))
    full = lambda a: pl.BlockSpec(a.shape, lambda b, g, pt: (0,) * a.ndim)
    return pl.pallas_call(
        functools.partial(_compress_body, n_g=n_g, transposed=transposed),
        grid_spec=pltpu.PrefetchScalarGridSpec(
            num_scalar_prefetch=1,
            grid=(n_seq, n_pages // n_g),
            in_specs=[page_spec(k, t) for t in range(2) for k in range(n_g)] + [full(w1), full(c1), full(w2), full(b2)],
            out_specs=pl.BlockSpec((1, m_rows, 2 * KV_WIDTH), lambda b, g, pt: (b, g, 0)),
            scratch_shapes=[pltpu.VMEM((2, 8, CMP_HID2), f32), pltpu.VMEM((n_g * 128, KV_WIDTH), f32)],
        ),
        out_shape=jax.ShapeDtypeStruct((n_seq, n_pages * CHUNKS_PER_PAGE, 2 * KV_WIDTH), f32),
        compiler_params=pltpu.CompilerParams(vmem_limit_bytes=VMEM_LIMIT),
        name="compress_pages",
    )(page_table.reshape(-1).astype(jnp.int32), *([pool] * (2 * n_g)), w1, c1, w2, b2)


SD_PAGES = 64
SD_KEYS = SD_PAGES * 128
SD_SUB = 16
SD_ROWS = NSA_HEADS * 8
WIN_PAD = 640


def _masked_softmax(s, ok):
    s = jnp.where(ok, s, NEG_INF)
    e = jnp.where(ok, jnp.exp(s - jnp.max(s, axis=-1, keepdims=True)), 0.0)
    return e / jnp.maximum(jnp.sum(e, axis=-1, keepdims=True), 1e-30)


def _rows_from_group(a):
    t = a.shape[0] // NSA_KV_HEADS
    a4 = jnp.broadcast_to(a.reshape(NSA_KV_HEADS, 1, t, a.shape[1]), (NSA_KV_HEADS, GRP, t, a.shape[1]))
    return a4.reshape(NSA_KV_HEADS * GRP * t, a.shape[1])


def _nsa_decode_body(pt_ref, *refs, pos0, n_cmp, n_sel, n_new, win_len):
    f32, bf16 = jnp.float32, jnp.bfloat16
    kt_refs, vt_refs = refs[:SD_PAGES], refs[SD_PAGES:2 * SD_PAGES]
    (q_ref, g_ref, kcvc_ref, knew_ref, wh_ref, bc_ref, bs_ref, bn_ref, bw_ref, cov_ref, exp_ref,
     o_ref, sel_ref, oc_ref, m_ref, l_ref, acc_ref, mask_ref) = refs[2 * SD_PAGES:]
    tile, n_tiles = pl.program_id(1), pl.num_programs(1)
    nt_dims = (((1,), (1,)), ((), ()))
    nn_dims = (((1,), (0,)), ((), ()))
    n_tok = SD_ROWS // NSA_HEADS
    q = (q_ref[0] * (HEAD_DIM ** -0.5)).astype(bf16)
    qpos = pos0 + lax.broadcasted_iota(jnp.int32, (SD_ROWS, 1), 0) % n_tok

    @pl.when(tile == 0)
    def _():
        n_c = kcvc_ref.shape[1]
        m_idx = lax.broadcasted_iota(jnp.int32, (SD_ROWS, n_c), 1)
        ok_c = (m_idx >= 1) & (m_idx <= n_cmp) & (qpos >= CMP_STRIDE * m_idx + (CMP_BLOCK - 1 - CMP_STRIDE))
        s_c = lax.dot_general(q, kcvc_ref[0, :, :KV_WIDTH].astype(bf16), nt_dims, preferred_element_type=f32) + bc_ref[...]
        p_c = _masked_softmax(s_c, ok_c)
        oc_ref[...] = jnp.dot(p_c.astype(bf16), kcvc_ref[0, :, KV_WIDTH:].astype(bf16), preferred_element_type=f32)
        p_sum = jnp.sum(p_c.reshape(NSA_KV_HEADS, GRP, n_tok, n_c), axis=1).reshape(NSA_KV_HEADS * n_tok, n_c)
        imp = jnp.dot(p_sum.astype(bf16), cov_ref[...], preferred_element_type=f32)
        n_l = imp.shape[1]
        lane = lax.broadcasted_iota(jnp.int32, (NSA_KV_HEADS * n_tok, n_l), 1)
        qp = pos0 + lax.broadcasted_iota(jnp.int32, (NSA_KV_HEADS * n_tok, n_l), 0) % n_tok
        cur = qp // SEL_BLOCK
        forced = (lane == 0) | (lane == cur) | (lane == cur - 1)
        score = jnp.where(forced, FORCE_SCORE, jnp.where(lane * SEL_BLOCK <= qp, imp, -1.0))
        score = jnp.where(lane < n_sel, score, -jnp.inf)
        sel = jnp.zeros(score.shape, f32)
        for _ in range(min(SEL_TOP_N, n_sel)):
            best = jnp.max(score, axis=-1, keepdims=True)
            lane_f = lane.astype(f32)
            hit = lane_f == jnp.min(jnp.where(score == best, lane_f, float(n_l)), axis=-1, keepdims=True)
            sel = jnp.where(hit, 1.0, sel)
            score = jnp.where(hit, -jnp.inf, score)
        sel_ref[...] = sel
        m_ref[...] = jnp.full(m_ref.shape, NEG_INF, f32)
        l_ref[...] = jnp.zeros(l_ref.shape, f32)
        acc_ref[...] = jnp.zeros(acc_ref.shape, f32)

    def online(s, ok, v, v_dims):
        m_old = m_ref[...]
        s = jnp.where(ok, s, NEG_INF)
        m_new = jnp.maximum(m_old, jnp.max(s, axis=-1, keepdims=True))
        p = jnp.where(ok, jnp.exp(s - m_new), 0.0)
        alpha = jnp.exp(m_old - m_new)
        l_ref[...] = alpha * l_ref[...] + jnp.sum(p, axis=-1, keepdims=True)
        acc_ref[...] = alpha * acc_ref[...] + lax.dot_general(p.astype(bf16), v, v_dims, preferred_element_type=f32)
        m_ref[...] = m_new

    sel_tile = sel_ref[:, pl.ds(pl.multiple_of(tile * LANES, LANES), LANES)]
    mask_ref[...] = jnp.dot(sel_tile.astype(bf16), exp_ref[...], preferred_element_type=f32)
    sub_keys = SD_SUB * 128
    for sub in range(SD_PAGES // SD_SUB):
        pages = slice(sub * SD_SUB, (sub + 1) * SD_SUB)
        kt = jnp.concatenate([r[0] for r in kt_refs[pages]], axis=1).astype(bf16)
        vt = jnp.concatenate([r[0] for r in vt_refs[pages]], axis=1).astype(bf16)
        s = jnp.dot(q, kt, preferred_element_type=f32) + bs_ref[:, sub * sub_keys:(sub + 1) * sub_keys]
        ok = _rows_from_group(mask_ref[:, sub * sub_keys:(sub + 1) * sub_keys]) > 0.5
        online(s, ok, vt, nt_dims)

    @pl.when(tile == n_tiles - 1)
    def _():
        lane = lax.broadcasted_iota(jnp.int32, (SD_ROWS, LANES), 1)
        new_blk = pos0 // SEL_BLOCK
        sel_new = _rows_from_group(sel_ref[:, new_blk:new_blk + 1]) > 0.5
        ok_n = sel_new & (lane < n_new) & (pos0 + lane <= qpos)
        s_n = lax.dot_general(q, knew_ref[0, :, :KV_WIDTH].astype(bf16), nt_dims, preferred_element_type=f32) + bn_ref[...]
        online(s_n, ok_n, knew_ref[0, :, KV_WIDTH:].astype(bf16), nn_dims)
        o_s = acc_ref[...] / jnp.maximum(l_ref[...], 1e-30)
        j = lax.broadcasted_iota(jnp.int32, (SD_ROWS, WIN_PAD), 1)
        kwpos = pos0 - win_len + j
        dist = qpos - kwpos
        ok_w = (dist >= 0) & (dist < WINDOW) & (kwpos >= 0) & (j < win_len + n_new)
        s_w = lax.dot_general(q, wh_ref[0, :, :KV_WIDTH].astype(bf16), nt_dims, preferred_element_type=f32) + bw_ref[...]
        o_w = jnp.dot(_masked_softmax(s_w, ok_w).astype(bf16), wh_ref[0, :, KV_WIDTH:].astype(bf16), preferred_element_type=f32)
        g = jax.nn.sigmoid(g_ref[0])
        o = g[:, 0:1] * oc_ref[...] + g[:, 1:2] * o_s + g[:, 2:3] * o_w
        row = lax.broadcasted_iota(jnp.int32, (SD_ROWS, HEAD_DIM), 0)
        o_ref[0] = jnp.where(row < SD_ROWS // NSA_KV_HEADS, o[:, :HEAD_DIM], o[:, HEAD_DIM:])


def nsa_decode(q, glog, kv_new, kcvc, pool_t, page_table, cache_win, thr, rel_bias, pos0):
    f32, bf16 = jnp.float32, jnp.bfloat16
    b, t = q.shape[:2]
    n_pages = page_table.shape[1]
    win_len = cache_win.shape[1]
    assert t * NSA_HEADS == SD_ROWS and pos0 == n_pages * 128 and n_pages % SD_PAGES == 0 and pos0 % SEL_BLOCK == 0
    assert win_len + t <= WIN_PAD and t <= SEL_BLOCK and SD_KEYS == LANES * SEL_BLOCK
    n_rows = pos0 + t
    n_cmp = (n_rows - CMP_BLOCK) // CMP_STRIDE + 1
    n_sel = -(-n_rows // SEL_BLOCK)
    n_c = kcvc.shape[1]
    sel_lanes = -(-n_sel // LANES) * LANES
    q5 = jnp.transpose(q.reshape(b, t, NSA_KV_HEADS, GRP, HEAD_DIM), (0, 2, 3, 1, 4))
    qz = (q5[:, :, :, :, None, :] * jnp.eye(NSA_KV_HEADS, dtype=f32)[None, :, None, None, :, None]).reshape(b, SD_ROWS, KV_WIDTH)
    g3 = jnp.transpose(glog.reshape(b, t, NSA_HEADS, 3), (0, 2, 1, 3)).reshape(b, SD_ROWS, 3)
    knew = jnp.pad(kv_new[:, :, 2:4].reshape(b, t, 2 * KV_WIDTH), ((0, 0), (0, LANES - t), (0, 0)))
    whist = jnp.concatenate([cache_win.reshape(b, win_len, 2 * KV_WIDTH), kv_new[:, :, 4:6].reshape(b, t, 2 * KV_WIDTH),
                             jnp.zeros((b, WIN_PAD - win_len - t, 2 * KV_WIDTH), f32)], axis=1)
    qp = pos0 + jnp.arange(t, dtype=jnp.int32)[:, None]
    tab = lambda dist: bias_tables(dist[None], thr, rel_bias).reshape(SD_ROWS, dist.shape[1])
    bias_c = tab(qp - (CMP_STRIDE * jnp.arange(n_c, dtype=jnp.int32)[None, :] + CMP_BLOCK - 1 - CMP_STRIDE))
    bias_s = tab(qp - jnp.arange(pos0, dtype=jnp.int32)[None, :])
    bias_n = tab(qp - (pos0 + jnp.arange(LANES, dtype=jnp.int32)[None, :]))
    bias_w = tab(qp - (pos0 - win_len + jnp.arange(WIN_PAD, dtype=jnp.int32)[None, :]))
    m_idx = jnp.arange(n_c)[:, None]
    c_start = CMP_STRIDE * (m_idx - 1)
    s_start = SEL_BLOCK * jnp.arange(sel_lanes)[None, :]
    cover = ((c_start < s_start + SEL_BLOCK) & (c_start + CMP_BLOCK > s_start) & (m_idx >= 1) & (m_idx <= n_cmp)
             & (jnp.arange(sel_lanes)[None, :] < n_sel)).astype(bf16)
    expand = (jnp.arange(SD_KEYS)[None, :] // SEL_BLOCK == jnp.arange(LANES)[:, None]).astype(bf16)
    page_spec = lambda k, blk: pl.BlockSpec((1, KV_WIDTH, 128), lambda bb, g, pt: (pt[bb * n_pages + g * SD_PAGES + k], blk, 0))
    per_seq = lambda a: pl.BlockSpec((1,) + a.shape[1:], lambda bb, g, pt: (bb,) + (0,) * (a.ndim - 1))
    full = lambda a: pl.BlockSpec(a.shape, lambda bb, g, pt: (0,) * a.ndim)
    kv_rows = NSA_KV_HEADS * t
    out = pl.pallas_call(
        functools.partial(_nsa_decode_body, pos0=pos0, n_cmp=n_cmp, n_sel=n_sel, n_new=t, win_len=win_len),
        grid_spec=pltpu.PrefetchScalarGridSpec(
            num_scalar_prefetch=1,
            grid=(b, n_pages // SD_PAGES),
            in_specs=[page_spec(k, blk) for blk in (2, 3) for k in range(SD_PAGES)]
            + [per_seq(qz), per_seq(g3), per_seq(kcvc), per_seq(knew), per_seq(whist), full(bias_c),
               pl.BlockSpec((SD_ROWS, SD_KEYS), lambda bb, g, pt: (0, g)), full(bias_n), full(bias_w), full(cover), full(expand)],
            out_specs=pl.BlockSpec((1, SD_ROWS, HEAD_DIM), lambda bb, g, pt: (bb, 0, 0)),
            scratch_shapes=[pltpu.VMEM((kv_rows, sel_lanes), f32), pltpu.VMEM((SD_ROWS, KV_WIDTH), f32),
                            pltpu.VMEM((SD_ROWS, 1), f32), pltpu.VMEM((SD_ROWS, 1), f32), pltpu.VMEM((SD_ROWS, KV_WIDTH), f32),
                            pltpu.VMEM((kv_rows, SD_KEYS), f32)],
        ),
        out_shape=jax.ShapeDtypeStruct((b, SD_ROWS, HEAD_DIM), f32),
        compiler_params=pltpu.CompilerParams(vmem_limit_bytes=VMEM_LIMIT),
        name="nsa_decode",
    )(page_table.reshape(-1).astype(jnp.int32), *([pool_t] * (2 * SD_PAGES)), qz, g3, kcvc, knew, whist,
      bias_c, bias_s, bias_n, bias_w, cover, expand)
    out = jnp.transpose(out.reshape(b, NSA_KV_HEADS, GRP, t, HEAD_DIM), (0, 3, 1, 2, 4))
    return out.reshape(b, t, NSA_WIDTH)


MOE_ROWS = 1280
MOE_SUB = 128
MOE_PATHS = (256, 1024, 1152, MOE_ROWS)
MOE_CHUNK = 256
MOE_TF = 256
MOE_COLS = 256


def _moe_body(e_ref, sub_ref, blk_ref, ok_ref, x_ref, wg_ref, wl_ref, bg_ref, bl_ref, wd_ref, bd_ref, o_ref, xs_ref):
    bf16 = jnp.bfloat16
    i, f = pl.program_id(0), pl.program_id(1)
    n_sub = sub_ref[i]
    d = o_ref.shape[1]

    @pl.when(f == 0)
    def _():
        o_ref[...] = jnp.broadcast_to(bd_ref[0], o_ref.shape)

    @pl.when((f == 0) & (n_sub > 0))
    def _():
        for j in range(MOE_ROWS // MOE_CHUNK):
            rows = slice(j * MOE_CHUNK, (j + 1) * MOE_CHUNK)
            w = lax.bitcast_convert_type(x_ref[rows, :], jnp.uint32)
            hi = lax.bitcast_convert_type(w & jnp.uint32(0xFFFF0000), jnp.float32)
            lo = lax.bitcast_convert_type(w << 16, jnp.float32)
            xs_ref[rows, :] = jnp.concatenate([hi, lo], axis=1).astype(bf16)

    def expert_rows(n_rows):
        x = xs_ref[0:n_rows, :]
        hg = jnp.dot(x, wg_ref[0].astype(bf16), preferred_element_type=jnp.float32) + bg_ref[0]
        hl = jnp.dot(x, wl_ref[0].astype(bf16), preferred_element_type=jnp.float32) + bl_ref[0]
        hg = jnp.minimum(hg, SWIGLU_LIMIT)
        hl = jnp.clip(hl, -SWIGLU_LIMIT, SWIGLU_LIMIT)
        act = (hg * jax.nn.sigmoid(SWIGLU_ALPHA * hg) * (hl + 1.0)).astype(bf16)
        for c in range(d // MOE_COLS):
            cols = slice(c * MOE_COLS, (c + 1) * MOE_COLS)
            o_ref[0:n_rows, cols] += jnp.dot(act, wd_ref[0, :, cols].astype(bf16), preferred_element_type=jnp.float32)

    below = 0
    for path_rows in MOE_PATHS:
        @pl.when((n_sub * MOE_SUB > below) & (n_sub * MOE_SUB <= path_rows))
        def _(path_rows=path_rows):
            expert_rows(path_rows)

        below = path_rows


def moe_experts(item_e, item_sub, item_blk, item_ok, xb, w_gu, b_gu, w_down, b_down):
    n_items = item_e.shape[0]
    n_f = D_FF // MOE_TF
    d = 2 * xb.shape[1]

    def col(f, ok, i):
        return f * ok[i] + (n_f - 1) * (1 - ok[i])

    return pl.pallas_call(
        _moe_body,
        grid_spec=pltpu.PrefetchScalarGridSpec(
            num_scalar_prefetch=4,
            grid=(n_items, n_f),
            in_specs=[pl.BlockSpec((MOE_ROWS, d // 2), lambda i, f, e, s, b, ok: (b[i], 0)),
                      pl.BlockSpec((1, d, MOE_TF), lambda i, f, e, s, b, ok: (e[i], 0, col(f, ok, i))),
                      pl.BlockSpec((1, d, MOE_TF), lambda i, f, e, s, b, ok: (e[i], 0, n_f + col(f, ok, i))),
                      pl.BlockSpec((1, 1, MOE_TF), lambda i, f, e, s, b, ok: (e[i], 0, col(f, ok, i))),
                      pl.BlockSpec((1, 1, MOE_TF), lambda i, f, e, s, b, ok: (e[i], 0, n_f + col(f, ok, i))),
                      pl.BlockSpec((1, MOE_TF, d), lambda i, f, e, s, b, ok: (e[i], col(f, ok, i), 0)),
                      pl.BlockSpec((1, 1, d), lambda i, f, e, s, b, ok: (e[i], 0, 0))],
            out_specs=pl.BlockSpec((MOE_ROWS, d), lambda i, f, e, s, b, ok: (i, 0)),
            scratch_shapes=[pltpu.VMEM((MOE_ROWS, d), jnp.bfloat16)],
        ),
        out_shape=jax.ShapeDtypeStruct((n_items * MOE_ROWS, d), jnp.float32),
        compiler_params=pltpu.CompilerParams(vmem_limit_bytes=VMEM_LIMIT),
        name="moe_experts",
    )(item_e, item_sub, item_blk, item_ok, xb, w_gu, w_gu, b_gu[:, None, :], b_gu[:, None, :], w_down, b_down[:, None, :])


ROW_TILE = 256
PROJ_TILE = 512
PROJ_ROWS = 512


def _rms(v, g):
    return v * lax.rsqrt(jnp.mean(v * v, axis=-1, keepdims=True) + RMS_EPS) * g


def _norm_proj_body(x_ref, g_ref, w_ref, o_ref, h_ref):
    @pl.when(pl.program_id(1) == 0)
    def _():
        h_ref[...] = _rms(x_ref[...], g_ref[...]).astype(jnp.bfloat16)

    o_ref[...] = jnp.dot(h_ref[...], w_ref[...], preferred_element_type=jnp.float32)


def norm_proj(x, gain, w):
    n, d = x.shape
    cols = w.shape[1]
    tm = min(n, PROJ_ROWS)
    return pl.pallas_call(
        _norm_proj_body,
        grid=(n // tm, cols // PROJ_TILE),
        in_specs=[pl.BlockSpec((tm, d), lambda i, j: (i, 0)),
                  pl.BlockSpec((1, d), lambda i, j: (0, 0)),
                  pl.BlockSpec((d, PROJ_TILE), lambda i, j: (0, j))],
        out_specs=pl.BlockSpec((tm, PROJ_TILE), lambda i, j: (i, j)),
        out_shape=jax.ShapeDtypeStruct((n, cols), jnp.float32),
        scratch_shapes=[pltpu.VMEM((tm, d), jnp.bfloat16)],
        compiler_params=pltpu.CompilerParams(vmem_limit_bytes=VMEM_LIMIT),
        name="norm_proj",
    )(x, gain.reshape(1, d), w)


def _mix_out_body(attn_ref, y_ref, x_ref, wglu_ref, bglu_ref, ga_ref, gs_ref, wout_ref, gf_ref, wr_ref,
                  x2_ref, hp_ref, lg_ref):
    f32, bf16 = jnp.float32, jnp.bfloat16
    y = y_ref[...]
    ssm = y * jax.nn.sigmoid(jnp.dot(y.astype(bf16), wglu_ref[...], preferred_element_type=f32) + bglu_ref[...])
    merged = jnp.concatenate([_rms(attn_ref[...], ga_ref[...]), _rms(ssm, gs_ref[...])], axis=1).astype(bf16)
    x2 = x_ref[...] + jnp.dot(merged, wout_ref[...], preferred_element_type=f32)
    x2_ref[...] = x2
    hm = _rms(x2, gf_ref[...]).astype(bf16)
    lg_ref[...] = jnp.dot(hm, wr_ref[...], preferred_element_type=f32)
    bits = lax.bitcast_convert_type(hm.astype(f32), jnp.uint32)
    half = bits.shape[1] // 2
    hp_ref[...] = lax.bitcast_convert_type(bits[:, :half] | (bits[:, half:] >> 16), f32)


def mix_out(attn, y, x, lw):
    n = x.shape[0]
    f32, bf16 = jnp.float32, jnp.bfloat16
    row = lambda a: a.astype(f32).reshape(1, -1)
    wr = jnp.pad(lw['router_w'], ((0, 0), (0, LANES - N_EXPERTS))).astype(bf16)
    consts = [lw['w_glu'].astype(bf16), row(lw['b_glu']), row(lw['norm_attn_out']), row(lw['norm_ssm_out']),
              lw['w_out'].astype(bf16), row(lw['norm_ffn']), wr]
    tile = lambda c: pl.BlockSpec((ROW_TILE, c), lambda i: (i, 0))
    full = lambda a: pl.BlockSpec(a.shape, lambda i: (0, 0))
    return pl.pallas_call(
        _mix_out_body,
        grid=(n // ROW_TILE,),
        in_specs=[tile(NSA_WIDTH), tile(SSM_WIDTH), tile(D_MODEL)] + [full(c) for c in consts],
        out_specs=[tile(D_MODEL), tile(D_MODEL // 2), tile(LANES)],
        out_shape=[jax.ShapeDtypeStruct((n, D_MODEL), f32), jax.ShapeDtypeStruct((n, D_MODEL // 2), f32),
                   jax.ShapeDtypeStruct((n, LANES), f32)],
        compiler_params=pltpu.CompilerParams(vmem_limit_bytes=VMEM_LIMIT),
        name="mix_out",
    )(attn, y, x, *consts)


def _combine_body(x_ref, y0_ref, y1_ref, y2_ref, y3_ref, gate_ref, g_ref, o_ref, *, normed):
    gate = gate_ref[...]
    moe = gate[:, 0:1] * y0_ref[...]
    for k, y_ref in enumerate((y1_ref, y2_ref, y3_ref), start=1):
        moe = moe + gate[:, k:k + 1] * y_ref[...]
    out = x_ref[...] + moe
    o_ref[...] = _rms(out, g_ref[...]) if normed else out


def combine_norm(x, ys, gate, gain):
    n, d = x.shape
    normed = gain is not None
    gain = jnp.ones((d,), jnp.float32) if gain is None else gain
    tile = lambda c: pl.BlockSpec((ROW_TILE, c), lambda i: (i, 0))
    return pl.pallas_call(
        functools.partial(_combine_body, normed=normed),
        grid=(n // ROW_TILE,),
        in_specs=[tile(d)] * (1 + TOP_K) + [tile(TOP_K), pl.BlockSpec((1, d), lambda i: (0, 0))],
        out_specs=tile(d),
        out_shape=jax.ShapeDtypeStruct((n, d), jnp.float32),
        name="combine_norm",
    )(x, *ys, gate, gain.astype(jnp.float32).reshape(1, d))


def moe_routed(hp, logits, router_b, w_gu, b_gu, w_down, b_down):
    n = hp.shape[0]
    top_val, top_idx = lax.top_k(logits[:, :N_EXPERTS] + router_b.astype(jnp.float32), TOP_K)
    gate = jax.nn.softmax(top_val, axis=-1)
    nk = n * TOP_K
    n_items = N_EXPERTS + nk // MOE_ROWS
    flat_e = top_idx.reshape(nk)
    onehot = (flat_e[:, None] == jnp.arange(N_EXPERTS, dtype=flat_e.dtype)[None, :]).astype(jnp.int32)
    running = jnp.cumsum(onehot, axis=0)
    counts = running[-1]
    pos_in_e = jnp.sum(onehot * running, axis=1) - 1
    items_e = (counts + MOE_ROWS - 1) // MOE_ROWS
    item_end = jnp.cumsum(items_e)
    item_start = item_end - items_e
    total = item_end[-1]
    dest = (item_start[flat_e] * MOE_ROWS + pos_in_e).astype(jnp.int32)
    row_tok = jnp.zeros(n_items * MOE_ROWS, jnp.int32).at[dest].set(jnp.arange(nk, dtype=jnp.int32) // TOP_K)
    item = jnp.minimum(jnp.arange(n_items, dtype=jnp.int32), total - 1)
    item_e = jnp.minimum(jnp.searchsorted(item_end, item, side='right'), N_EXPERTS - 1).astype(jnp.int32)
    rows = jnp.clip(counts[item_e] - (item - item_start[item_e]) * MOE_ROWS, 0, MOE_ROWS)
    item_sub = jnp.where(jnp.arange(n_items) < total, (rows + MOE_SUB - 1) // MOE_SUB, 0).astype(jnp.int32)
    item_ok = (jnp.arange(n_items) < total).astype(jnp.int32)
    yb = moe_experts(item_e, item_sub, item.astype(jnp.int32), item_ok, hp[row_tok], w_gu, b_gu, w_down, b_down)
    return yb, dest.reshape(n, TOP_K), gate


def mixer_layer(x, pos0, pool, page_table, win_buf, h0_re, h0_im, rel_bias, lw):
    b, t, _ = x.shape
    w_in = jnp.pad(lw['w_in'], ((0, 0), (0, -IN_WIDTH % PROJ_TILE))).astype(jnp.bfloat16)
    proj = norm_proj(x.reshape(b * t, D_MODEL), lw['norm_mix'].astype(jnp.float32), w_in).reshape(b, t, -1)
    o1 = NSA_WIDTH
    o2 = o1 + N_KV * KV_WIDTH
    o3 = o2 + 3 * NSA_HEADS
    q = proj[..., :o1].reshape(b, t, NSA_HEADS, HEAD_DIM)
    kv_new = proj[..., o1:o2].reshape(b, t, N_KV, NSA_KV_HEADS, HEAD_DIM)
    u = proj[..., o3:IN_WIDTH]
    paged_new = kv_new[:, :, :N_PAGED]
    win_new = kv_new[:, :, N_PAGED:]
    n_rows = pos0 + t
    n_cmp = (n_rows - CMP_BLOCK) // CMP_STRIDE + 1
    thr = bucket_thresholds()
    cmp_w = (lw['cmp_pe'], lw['cmp_w1'], lw['cmp_b1'], lw['cmp_w2'], lw['cmp_b2'])
    if pool is None:
        assert pos0 == 0 and t % 128 == 0
        pages = t // 128
        own_pool = paged_new.reshape(b * pages, 128, ROW_LANES)
        own_table = jnp.arange(b * pages, dtype=jnp.int32).reshape(b, pages)
        kcvc = compress_pages(own_pool, own_table, *cmp_w, n_g=pages, transposed=False)
        attn = nsa_prefill(q, proj[..., o2:o3], kcvc, kv_new, thr, rel_bias, n_cmp).astype(x.dtype)
        w_hist = win_new
    else:
        kcvc = compress_pages(pool, page_table, *cmp_w, n_g=CMP_PAGES, transposed=True)
        attn = nsa_decode(q, proj[..., o2:o3], kv_new, kcvc, pool, page_table, win_buf, thr, rel_bias, pos0).astype(x.dtype)
        w_hist = jnp.concatenate([win_buf, win_new], axis=1)
    new_win = w_hist[:, w_hist.shape[1] - min(WINDOW, n_rows):]
    y, h_re, h_im = s5_branch(u, h0_re, h0_im, lw, pool is None)
    x2, hp, logits = mix_out(attn.reshape(b * t, NSA_WIDTH), y.reshape(b * t, SSM_WIDTH), x.reshape(b * t, D_MODEL), lw)
    return (x2, hp, logits), paged_new, new_win, h_re, h_im


def s5_branch(u, h0_re, h0_im, lw, chained):
    bsz, t, _ = u.shape
    f32 = jnp.float32
    a, bcat, ccat = s5_discretize(lw['lam_re'], lw['lam_im'], lw['log_dt'], lw['b_re'], lw['b_im'], lw['c_re'], lw['c_im'])
    d_skip = lw['d_skip'].astype(f32)
    flat = lambda h: h.astype(f32).reshape(bsz, SSM_GROUPS * SSM_STATE)
    if chained:
        n_r = t // S5_CHUNK
        u4 = jnp.transpose(u.astype(f32).reshape(bsz, n_r, S5_CHUNK, SSM_WIDTH), (0, 2, 1, 3))
        y4, h_re, h_im = s5_scan(u4, flat(h0_re)[:, None], flat(h0_im)[:, None], a, bcat, ccat, d_skip, True)
        y = jnp.transpose(y4, (0, 2, 1, 3)).reshape(bsz, t, SSM_WIDTH)
        h_re, h_im = h_re[:, 0], h_im[:, 0]
    else:
        u4 = jnp.transpose(u.astype(f32), (1, 0, 2))[None]
        y4, h_re, h_im = s5_scan(u4, flat(h0_re)[None], flat(h0_im)[None], a, bcat, ccat, d_skip, False)
        y = jnp.transpose(y4[0], (1, 0, 2))
        h_re, h_im = h_re[0], h_im[0]
    st = lambda h: h.reshape(bsz, SSM_GROUPS, SSM_STATE).astype(u.dtype)
    return y, st(h_re), st(h_im)


def moe_residual(groups, lw, final_gain):
    sizes = [g[0].shape[0] for g in groups]
    hp = lax.optimization_barrier(jnp.concatenate([g[1] for g in groups], axis=0))
    logits = jnp.concatenate([g[2] for g in groups], axis=0)
    yb, dest, gate = moe_routed(hp, logits, lw['router_b'], lw['w_gu'], lw['b_gu'], lw['w_down'], lw['b_down'])
    outs, start = [], 0
    for (x2, _, _), n in zip(groups, sizes):
        rows = slice(start, start + n)
        outs.append(combine_norm(x2, [yb[dest[rows, k]] for k in range(TOP_K)], gate[rows], final_gain))
        start += n
    return outs


def kernel(x_prompt, x_sample, cache_nsa_kv, cache_win_kv, state_ssm_re, state_ssm_im, page_table, rel_bias,
           norm_mix, w_in, cmp_pe, cmp_w1, cmp_b1, cmp_w2, cmp_b2, ssm_lam_re, ssm_lam_im, ssm_log_dt,
           ssm_b_re, ssm_b_im, ssm_c_re, ssm_c_im, ssm_d, ssm_w_glu, ssm_b_glu, norm_attn_out, norm_ssm_out,
           w_out, norm_ffn, router_w, router_b, w_gu, b_gu, w_down, b_down, norm_final):
    n_seq, n_pages = page_table.shape
    past_len = n_pages * cache_nsa_kv.shape[2]
    xp, xs = x_prompt, x_sample
    kv_p, win_p, sre_p, sim_p = [], [], [], []
    kv_s, win_s, sre_s, sim_s = [], [], [], []
    for i in range(DEPTH):
        lw = dict(norm_mix=norm_mix[i], w_in=w_in[i], cmp_pe=cmp_pe[i], cmp_w1=cmp_w1[i], cmp_b1=cmp_b1[i],
                  cmp_w2=cmp_w2[i], cmp_b2=cmp_b2[i], lam_re=ssm_lam_re[i], lam_im=ssm_lam_im[i],
                  log_dt=ssm_log_dt[i], b_re=ssm_b_re[i], b_im=ssm_b_im[i], c_re=ssm_c_re[i], c_im=ssm_c_im[i],
                  d_skip=ssm_d[i], w_glu=ssm_w_glu[i], b_glu=ssm_b_glu[i], norm_attn_out=norm_attn_out[i],
                  norm_ssm_out=norm_ssm_out[i], w_out=w_out[i], norm_ffn=norm_ffn[i], router_w=router_w[i],
                  router_b=router_b[i], w_gu=w_gu[i], b_gu=b_gu[i], w_down=w_down[i], b_down=b_down[i])
        h0 = jnp.zeros((xp.shape[0], SSM_GROUPS, SSM_STATE), xp.dtype)
        gp, kv1, w1, r1, m1 = mixer_layer(xp, 0, None, None, None, h0, h0, rel_bias, lw)
        pool = jnp.transpose(cache_nsa_kv[i].reshape(cache_nsa_kv.shape[1], cache_nsa_kv.shape[2], ROW_LANES), (0, 2, 1))
        gs, kv2, w2, r2, m2 = mixer_layer(xs, past_len, pool, page_table, cache_win_kv[i], state_ssm_re[i], state_ssm_im[i], rel_bias, lw)
        op, os_ = moe_residual([gp, gs], lw, norm_final if i == DEPTH - 1 else None)
        xp, xs = op.reshape(xp.shape), os_.reshape(xs.shape)
        kv_p.append(kv1); win_p.append(w1); sre_p.append(r1); sim_p.append(m1)
        kv_s.append(kv2); win_s.append(w2); sre_s.append(r2); sim_s.append(m2)
    return (xp, xs, jnp.stack(kv_p), jnp.stack(win_p), jnp.stack(sre_p), jnp.stack(sim_p),
            jnp.stack(kv_s), jnp.stack(win_s), jnp.stack(sre_s), jnp.stack(sim_s))
```

```python
import functools
import math
import jax, jax.numpy as jnp
from jax import lax
from jax.experimental import pallas as pl
from jax.experimental.pallas import tpu as pltpu

D_MODEL = 2048
DEPTH = 1
NSA_HEADS = 16
NSA_KV_HEADS = 2
HEAD_DIM = 64
NSA_WIDTH = NSA_HEADS * HEAD_DIM
KV_WIDTH = NSA_KV_HEADS * HEAD_DIM
N_PAGED = 4
N_KV = 6
SSM_WIDTH = D_MODEL - NSA_WIDTH
SSM_CH = 16
SSM_GROUPS = SSM_WIDTH // SSM_CH
SSM_STATE = 64
IN_WIDTH = NSA_WIDTH + N_KV * KV_WIDTH + 3 * NSA_HEADS + SSM_WIDTH
CMP_BLOCK = 32
CMP_STRIDE = 16
CMP_HIDDEN = 2 * HEAD_DIM
SEL_BLOCK = 64
SEL_TOP_N = 16
WINDOW = 512
FORCE_SCORE = 1e6
REL_BUCKETS = 32
REL_MAX_DIST = 4096
N_EXPERTS = 32
TOP_K = 4
D_FF = D_MODEL
SWIGLU_LIMIT = 7.0
SWIGLU_ALPHA = 1.702
RMS_EPS = 1e-5
NEG_INF = -1e30


GRP = NSA_HEADS // NSA_KV_HEADS
TQ = 128
TK = 128
SEL_TILES = 4
WIN_TILES = 2
LANES = 128
BUCKET_TABLE_LEN = 32768
BIAS_TILE_ELEMS = 16384
VMEM_LIMIT = 56 * 1024 * 1024


def rel_bucket(dist):
    n = jnp.maximum(dist, 0)
    exact = REL_BUCKETS // 2
    nf = jnp.maximum(n, 1).astype(jnp.float32)
    large = exact + (jnp.log(nf / exact) / math.log(REL_MAX_DIST / exact) * (REL_BUCKETS - exact)).astype(jnp.int32)
    return jnp.where(n < exact, n, jnp.minimum(large, REL_BUCKETS - 1))


def bucket_thresholds():
    tab = rel_bucket(jnp.arange(BUCKET_TABLE_LEN, dtype=jnp.int32))
    return jnp.sum(tab[None, :] < jnp.arange(REL_BUCKETS, dtype=jnp.int32)[:, None], axis=1).astype(jnp.int32)


def _bias_table_body(thr_ref, rb_ref, d_ref, o_ref):
    n = jnp.maximum(d_ref[0], 0)
    for h in range(NSA_HEADS):
        val = jnp.full(n.shape, rb_ref[h], jnp.float32)
        for k in range(1, REL_BUCKETS):
            val = jnp.where(n >= thr_ref[k], rb_ref[k * NSA_HEADS + h], val)
        o_ref[h // GRP, 0, h % GRP] = val


def bias_tables(dist, thr, rel_bias):
    n, r, c = dist.shape
    ct = min(c, BIAS_TILE_ELEMS // r)
    assert c % ct == 0
    return pl.pallas_call(
        _bias_table_body,
        grid_spec=pltpu.PrefetchScalarGridSpec(
            num_scalar_prefetch=2,
            grid=(n, c // ct),
            in_specs=[pl.BlockSpec((1, r, ct), lambda i, j, *_: (i, 0, j))],
            out_specs=pl.BlockSpec((NSA_KV_HEADS, 1, GRP, r, ct), lambda i, j, *_: (0, i, 0, 0, j)),
        ),
        out_shape=jax.ShapeDtypeStruct((NSA_KV_HEADS, n, GRP, r, c), jnp.float32),
        name="bias_tables",
    )(thr, rel_bias.reshape(-1), dist)


def _nsa_prefill_body(q_ref, g_ref, kc_ref, vct_ref, ks_ref, vst_ref, kw_ref, vwt_ref, bt_ref, bc_ref, cov_ref, exp_ref,
                      o_ref, mask_ref, *, n_cmp, n_sel):
    f32, bf16 = jnp.float32, jnp.bfloat16
    qi = pl.program_id(2)
    nt_dims = (((1,), (1,)), ((), ()))
    q = (q_ref[0] * (HEAD_DIM ** -0.5)).reshape(GRP * TQ, HEAD_DIM).astype(bf16)
    lanes = [slice(g * TQ, (g + 1) * TQ) for g in range(GRP)]
    row = lax.broadcasted_iota(jnp.int32, (LANES, TQ), 0)
    qpos = qi * TQ + lax.broadcasted_iota(jnp.int32, (LANES, TQ), 1)

    ok_c = (qpos >= CMP_STRIDE * row + (CMP_BLOCK - 1)) & (row < n_cmp)
    s_c = lax.dot_general(kc_ref[0, 0].astype(bf16), q, nt_dims, preferred_element_type=f32)
    p_sum = jnp.zeros((LANES, TQ), f32)
    p_parts = []
    for g in range(GRP):
        p = _masked_softmax_rows(s_c[:, lanes[g]] + bc_ref[0, 0, g], ok_c)
        p_sum = p_sum + p
        p_parts.append(p.astype(bf16))
    o_c = jnp.dot(vct_ref[0, 0].astype(bf16), jnp.concatenate(p_parts, axis=1), preferred_element_type=f32)

    imp = jnp.dot(cov_ref[...], p_sum.astype(bf16), preferred_element_type=f32)
    rs = -(-n_sel // 8) * 8
    blk = lax.broadcasted_iota(jnp.int32, (rs, TQ), 0)
    qp = qi * TQ + lax.broadcasted_iota(jnp.int32, (rs, TQ), 1)
    cur = qp // SEL_BLOCK
    forced = (blk == 0) | (blk == cur) | (blk == cur - 1)
    score = jnp.where(forced, FORCE_SCORE, jnp.where(blk * SEL_BLOCK <= qp, imp[:rs], -1.0))
    score = jnp.where(blk < n_sel, score, -jnp.inf)
    sel = jnp.zeros((rs, TQ), f32)
    for s in range(n_sel):
        r = jnp.max(jnp.where(blk == s, score, -jnp.inf), axis=0, keepdims=True)
        beats = (score > r) | ((score == r) & (blk < s))
        rank = jnp.sum(jnp.where(beats, 1.0, 0.0), axis=0, keepdims=True)
        sel = jnp.where((blk == s) & (rank < min(SEL_TOP_N, n_sel)), 1.0, sel)
    sel = jnp.concatenate([sel, jnp.zeros((LANES - rs, TQ), f32)], axis=0)
    mask_ref[...] = jnp.dot(exp_ref[...], sel.astype(bf16), preferred_element_type=f32)

    init = (jnp.full((1, GRP * TQ), NEG_INF, f32), jnp.zeros((1, GRP * TQ), f32), jnp.zeros((HEAD_DIM, GRP * TQ), f32))
    n_diag = bt_ref.shape[1]

    def branch(k_ref, vt_ref, n_tiles, first_tile, keep):
        n_keys = n_tiles * TK
        key_i = lax.broadcasted_iota(jnp.int32, (n_keys, TQ), 0)
        qry_i = lax.broadcasted_iota(jnp.int32, (n_keys, TQ), 1)

        def step(kk, carry):
            m, l, acc = carry
            keys = pl.ds(pl.multiple_of(kk * n_keys, n_keys), n_keys)
            msk = keep(qi * TQ + qry_i - (kk * n_keys + key_i), keys)
            s_all = lax.dot_general(k_ref[0, 0, keys, :].astype(bf16), q, nt_dims, preferred_element_type=f32)
            diag = [jnp.clip(qi - (n_tiles * kk + j), 0, n_diag - 1) for j in range(n_tiles)]
            m_parts, l_parts, a_parts, p_parts = [], [], [], []
            for g in range(GRP):
                bias = jnp.concatenate([bt_ref[0, dg, g] for dg in diag], axis=0)
                s = jnp.where(msk, s_all[:, lanes[g]] + bias, NEG_INF)
                m_new = jnp.maximum(m[:, lanes[g]], jnp.max(s, axis=0, keepdims=True))
                p = jnp.where(msk, jnp.exp(s - m_new), 0.0)
                alpha = jnp.exp(m[:, lanes[g]] - m_new)
                m_parts.append(m_new)
                a_parts.append(alpha)
                l_parts.append(alpha * l[:, lanes[g]] + jnp.sum(p, axis=0, keepdims=True))
                p_parts.append(p.astype(bf16))
            pv = jnp.dot(vt_ref[0, 0, :, keys].astype(bf16), jnp.concatenate(p_parts, axis=1), preferred_element_type=f32)
            return (jnp.concatenate(m_parts, axis=1), jnp.concatenate(l_parts, axis=1),
                    jnp.concatenate(a_parts, axis=1) * acc + pv)

        _, l, acc = lax.fori_loop(first_tile // n_tiles, qi // n_tiles + 1, step, init)
        return acc / jnp.maximum(l, 1e-30)

    o_s = branch(ks_ref, vst_ref, SEL_TILES, 0, lambda dist, keys: (mask_ref[keys, :] > 0.5) & (dist >= 0))
    o_w = branch(kw_ref, vwt_ref, WIN_TILES, jnp.maximum(qi - WINDOW // TK, 0),
                 lambda dist, keys: (dist >= 0) & (dist < WINDOW))
    for g in range(GRP):
        gate = jax.nn.sigmoid(g_ref[0, 0, :, g, :])
        o_ref[0, g] = gate[0:1] * o_c[:, lanes[g]] + gate[1:2] * o_s[:, lanes[g]] + gate[2:3] * o_w[:, lanes[g]]


def _masked_softmax_rows(s, ok):
    s = jnp.where(ok, s, NEG_INF)
    e = jnp.where(ok, jnp.exp(s - jnp.max(s, axis=0, keepdims=True)), 0.0)
    return e / jnp.maximum(jnp.sum(e, axis=0, keepdims=True), 1e-30)


def nsa_prefill(q, glog, kcvc, kv_new, thr, rel_bias, n_cmp):
    f32, bf16 = jnp.float32, jnp.bfloat16
    b, t = q.shape[:2]
    nq = t // TQ
    n_sel = t // SEL_BLOCK
    assert t % (SEL_TILES * TK) == 0 and t % (WIN_TILES * TK) == 0 and n_sel <= LANES and n_cmp < LANES and kcvc.shape[1] == LANES and TQ == TK == LANES
    q4 = jnp.transpose(q, (0, 2, 1, 3))
    g5 = jnp.transpose(glog.reshape(b, t, NSA_KV_HEADS, GRP, 3), (0, 2, 4, 3, 1))
    cmp_tok = jnp.pad(kcvc[:, 1:], ((0, 0), (0, 1), (0, 0))).reshape(b, LANES, 2, NSA_KV_HEADS, HEAD_DIM)
    kc = jnp.transpose(cmp_tok[:, :, 0], (0, 2, 1, 3))
    vct = jnp.transpose(cmp_tok[:, :, 1], (0, 2, 3, 1))
    rows = lambda n: jnp.transpose(kv_new[:, :, n], (0, 2, 1, 3))
    cols = lambda n: jnp.transpose(kv_new[:, :, n], (0, 2, 3, 1))
    i = jnp.arange(TQ, dtype=jnp.int32)
    d_toep = jnp.arange(nq, dtype=jnp.int32)[:, None, None] * TQ + i[None, None, :] - i[None, :, None]
    c_end = CMP_STRIDE * jnp.arange(LANES, dtype=jnp.int32) + CMP_BLOCK - 1
    d_cmp = jnp.arange(t, dtype=jnp.int32).reshape(nq, 1, TQ) - c_end[None, :, None]
    tabs = bias_tables(jnp.concatenate([d_toep, d_cmp], axis=0), thr, rel_bias)
    bt, bc = tabs[:, :nq], tabs[:, nq:]
    s_start = SEL_BLOCK * jnp.arange(LANES)[:, None]
    c_start = CMP_STRIDE * jnp.arange(LANES)[None, :]
    cover = ((c_start < s_start + SEL_BLOCK) & (c_start + CMP_BLOCK > s_start)
             & (jnp.arange(LANES)[None, :] < n_cmp) & (jnp.arange(LANES)[:, None] < n_sel)).astype(bf16)
    expand = (jnp.arange(t)[:, None] // SEL_BLOCK == jnp.arange(LANES)[None, :]).astype(bf16)
    spec = lambda *blk: pl.BlockSpec((1, 1) + blk, lambda h, bb, qq: (bb, h) + (0,) * len(blk))
    out = pl.pallas_call(
        functools.partial(_nsa_prefill_body, n_cmp=n_cmp, n_sel=n_sel),
        grid=(NSA_KV_HEADS, b, nq),
        in_specs=[pl.BlockSpec((1, GRP, TQ, HEAD_DIM), lambda h, bb, qq: (bb, h, qq, 0)),
                  pl.BlockSpec((1, 1, 3, GRP, TQ), lambda h, bb, qq: (bb, h, 0, 0, qq)),
                  spec(LANES, HEAD_DIM), spec(HEAD_DIM, LANES),
                  spec(t, HEAD_DIM), spec(HEAD_DIM, t), spec(t, HEAD_DIM), spec(HEAD_DIM, t),
                  pl.BlockSpec((1, nq, GRP, TK, TQ), lambda h, bb, qq: (h, 0, 0, 0, 0)),
                  pl.BlockSpec((1, 1, GRP, LANES, TQ), lambda h, bb, qq: (h, qq, 0, 0, 0)),
                  pl.BlockSpec((LANES, LANES), lambda h, bb, qq: (0, 0)),
                  pl.BlockSpec((t, LANES), lambda h, bb, qq: (0, 0))],
        out_specs=pl.BlockSpec((1, GRP, HEAD_DIM, TQ), lambda h, bb, qq: (bb, h, 0, qq)),
        out_shape=jax.ShapeDtypeStruct((b, NSA_HEADS, HEAD_DIM, t), f32),
        scratch_shapes=[pltpu.VMEM((t, TQ), f32)],
        compiler_params=pltpu.CompilerParams(vmem_limit_bytes=VMEM_LIMIT),
        name="nsa_prefill",
    )(q4, g5, kc, vct, rows(2), cols(3), rows(4), cols(5), bt, bc, cover, expand)
    return jnp.transpose(out, (0, 3, 1, 2)).reshape(b, t, NSA_WIDTH)


S5_GB = 8
S5_CH = S5_GB * SSM_CH
S5_ST = S5_GB * SSM_STATE


def _cmul(a_re, a_im, b_re, b_im):
    return a_re * b_re - a_im * b_im, a_re * b_im + a_im * b_re


def _s5_body(u_ref, h0re_ref, h0im_ref, a_ref, bcat_ref, ccat_ref, d_ref, y_ref, hre_ref, him_ref, xs_ref, hin_ref,
             *, chained):
    f32, bf16 = jnp.float32, jnp.bfloat16
    n_l, n_r = u_ref.shape[1], u_ref.shape[2]
    u2 = u_ref[0].reshape(n_l * n_r, S5_CH)
    xs_ref[...] = jnp.dot(u2.astype(bf16), bcat_ref[0].astype(bf16),
                          preferred_element_type=f32).reshape(n_l, n_r, 2 * S5_ST)
    a_re, a_im = a_ref[0:1, :], a_ref[1:2, :]

    def scan_step(j, carry):
        h_re, h_im, p_re, p_im = carry
        x = xs_ref[j]
        t_re, t_im = _cmul(a_re, a_im, h_re, h_im)
        h_re, h_im = t_re + x[:, :S5_ST], t_im + x[:, S5_ST:]
        xs_ref[j] = jnp.concatenate([h_re, h_im], axis=1)
        return (h_re, h_im) + _cmul(a_re, a_im, p_re, p_im)

    if chained:
        start = (jnp.zeros((n_r, S5_ST), f32), jnp.zeros((n_r, S5_ST), f32))
    else:
        start = (h0re_ref[0], h0im_ref[0])
    ones = (jnp.ones((1, S5_ST), f32), jnp.zeros((1, S5_ST), f32))
    h_re, h_im, al_re, al_im = lax.fori_loop(0, n_l, scan_step, start + ones)

    if chained:
        hin_ref[0:1, :] = jnp.concatenate([h0re_ref[0], h0im_ref[0]], axis=1)

        def chain_step(c, carry):
            z = xs_ref[n_l - 1, pl.ds(c - 1, 1), :]
            t_re, t_im = _cmul(al_re, al_im, *carry)
            n_re, n_im = t_re + z[:, :S5_ST], t_im + z[:, S5_ST:]
            hin_ref[pl.ds(c, 1), :] = jnp.concatenate([n_re, n_im], axis=1)
            return n_re, n_im

        h_re, h_im = lax.fori_loop(1, n_r + 1, chain_step, (h0re_ref[0], h0im_ref[0]))

        def fix_step(j, carry):
            p_re, p_im = carry
            hin = hin_ref[0:n_r, :]
            t_re, t_im = _cmul(p_re, p_im, hin[:, :S5_ST], hin[:, S5_ST:])
            xs_ref[j] = xs_ref[j] + jnp.concatenate([t_re, t_im], axis=1)
            return _cmul(a_re, a_im, p_re, p_im)

        lax.fori_loop(0, n_l, fix_step, (a_re, a_im))

    hre_ref[0] = h_re
    him_ref[0] = h_im
    hs = xs_ref[...].reshape(n_l * n_r, 2 * S5_ST).astype(bf16)
    y = jnp.dot(hs, ccat_ref[0].astype(bf16), preferred_element_type=f32) + d_ref[...] * u2
    y_ref[0] = jax.nn.gelu(y).reshape(n_l, n_r, S5_CH)


def s5_discretize(lam_re, lam_im, log_dt, b_re, b_im, c_re, c_im):
    f32 = jnp.float32
    dt = jnp.exp(log_dt.astype(f32))[:, None]
    lr, li = lam_re.astype(f32), lam_im.astype(f32)
    mag = jnp.exp(lr * dt)
    a_re, a_im = mag * jnp.cos(li * dt), mag * jnp.sin(li * dt)
    den = lr * lr + li * li
    z_re = ((a_re - 1.0) * lr + a_im * li) / den
    z_im = (a_im * lr - (a_re - 1.0) * li) / den
    br, bim = b_re.astype(f32), b_im.astype(f32)
    bb_re = z_re[..., None] * br - z_im[..., None] * bim
    bb_im = z_re[..., None] * bim + z_im[..., None] * br
    ngb = SSM_GROUPS // S5_GB
    eye = jnp.eye(S5_GB, dtype=f32)

    def block_diag(w):
        wd = w[:, :, :, None, :] * eye[None, :, None, :, None]
        return wd.reshape(ngb, S5_GB * w.shape[2], S5_GB * w.shape[3])

    def pack_b(bb):
        return block_diag(jnp.swapaxes(bb.reshape(ngb, S5_GB, SSM_STATE, SSM_CH), 2, 3))

    def pack_c(cc):
        return block_diag(jnp.swapaxes(cc.reshape(ngb, S5_GB, SSM_CH, SSM_STATE), 2, 3))

    a = jnp.stack([a_re.reshape(-1), a_im.reshape(-1)])
    bcat = jnp.concatenate([pack_b(bb_re), pack_b(bb_im)], axis=2)
    ccat = jnp.concatenate([pack_c(c_re.astype(f32)), -pack_c(c_im.astype(f32))], axis=1)
    return a, bcat, ccat


def s5_scan(u4, h0_re, h0_im, a, bcat, ccat, d_skip, chained):
    nb, n_l, n_r, _ = u4.shape
    rh = h0_re.shape[1]
    ngb = SSM_GROUPS // S5_GB
    st_spec = pl.BlockSpec((1, rh, S5_ST), lambda i, j: (i, 0, j))
    return pl.pallas_call(
        functools.partial(_s5_body, chained=chained),
        grid=(nb, ngb),
        in_specs=[pl.BlockSpec((1, n_l, n_r, S5_CH), lambda i, j: (i, 0, 0, j)), st_spec, st_spec,
                  pl.BlockSpec((2, S5_ST), lambda i, j: (0, j)),
                  pl.BlockSpec((1, S5_CH, 2 * S5_ST), lambda i, j: (j, 0, 0)),
                  pl.BlockSpec((1, 2 * S5_ST, S5_CH), lambda i, j: (j, 0, 0)),
                  pl.BlockSpec((1, S5_CH), lambda i, j: (0, j))],
        out_specs=[pl.BlockSpec((1, n_l, n_r, S5_CH), lambda i, j: (i, 0, 0, j)), st_spec, st_spec],
        out_shape=[jax.ShapeDtypeStruct(u4.shape, jnp.float32),
                   jax.ShapeDtypeStruct(h0_re.shape, jnp.float32), jax.ShapeDtypeStruct(h0_re.shape, jnp.float32)],
        scratch_shapes=[pltpu.VMEM((n_l, n_r, 2 * S5_ST), jnp.float32), pltpu.VMEM((n_r + 8, 2 * S5_ST), jnp.float32)],
        compiler_params=pltpu.CompilerParams(vmem_limit_bytes=VMEM_LIMIT),
        name="s5_scan",
    )(u4, h0_re, h0_im, a, bcat, ccat, d_skip.reshape(1, SSM_WIDTH))


S5_CHUNK = 64


ROW_LANES = N_PAGED * KV_WIDTH
CHUNKS_PER_PAGE = 128 // CMP_STRIDE
CMP_HID2 = NSA_KV_HEADS * CMP_HIDDEN
CMP_PAGES = 64


def _compress_body(pt_ref, *refs, n_g, transposed):
    f32, bf16 = jnp.float32, jnp.bfloat16
    page_refs = (refs[:n_g], refs[n_g:2 * n_g])
    w1_ref, c1_ref, w2_ref, b2_ref, o_ref, carry_ref, rows_ref = refs[2 * n_g:]
    m_rows = n_g * CHUNKS_PER_PAGE
    first = pl.program_id(1) == 0
    row_id = lax.broadcasted_iota(jnp.int32, (m_rows, CMP_HID2), 0)
    outs = []
    for t in range(2):
        for k, r in enumerate(page_refs[t]):
            rows_ref[t, k * 128:(k + 1) * 128, :] = r[0].T if transposed else r[0]
        cols = [rows_ref[t, pl.ds(j, m_rows, stride=CMP_STRIDE), :] for j in range(CMP_STRIDE)]
        x = jnp.concatenate(cols, axis=1).astype(bf16)
        part = jnp.dot(x, w1_ref[t], preferred_element_type=f32)
        p0, p1 = part[:, :CMP_HID2], part[:, CMP_HID2:]
        prev = jnp.where(first, 0.0, carry_ref[t, 0:1, :])
        shifted = jnp.where(row_id == 0, prev, pltpu.roll(p0, 1, axis=0))
        carry_ref[t, 0:1, :] = p0[m_rows - 1:m_rows, :]
        h1 = (c1_ref[t] + shifted) + p1
        outs.append(jnp.dot(jax.nn.gelu(h1).astype(bf16), w2_ref[t], preferred_element_type=f32) + b2_ref[t])
    o_ref[0] = jnp.concatenate(outs, axis=1)


def compress_pages(pool, page_table, cmp_pe, cmp_w1, cmp_b1, cmp_w2, cmp_b2, n_g, transposed):
    n_seq, n_pages = page_table.shape
    assert n_pages % n_g == 0 and pool.shape[1:] == ((ROW_LANES, 128) if transposed else (128, ROW_LANES))
    f32, bf16 = jnp.float32, jnp.bfloat16
    r = CMP_BLOCK // CMP_STRIDE
    eye = jnp.eye(NSA_KV_HEADS, dtype=f32)
    w1 = cmp_w1.astype(f32).reshape(2, r, CMP_STRIDE, HEAD_DIM, CMP_HIDDEN)
    w1 = jnp.transpose(w1, (0, 2, 3, 1, 4))[:, :, None, :, :, None, :] * eye[None, None, :, None, None, :, None]
    w1 = w1.reshape(2, CMP_STRIDE * KV_WIDTH, r * CMP_HID2).astype(bf16)
    c1 = jnp.stack([jnp.einsum('ld,ldf->f', cmp_pe[t], cmp_w1[t]) + cmp_b1[t] for t in range(2)])
    c1 = jnp.tile(c1[:, None, :], (1, 1, NSA_KV_HEADS))
    w2 = (cmp_w2.astype(f32)[:, None, :, None, :] * eye[None, :, None, :, None]).reshape(2, CMP_HID2, KV_WIDTH).astype(bf16)
    b2 = jnp.tile(cmp_b2.astype(f32)[:, None, :], (1, 1, NSA_KV_HEADS))
    m_rows = n_g * CHUNKS_PER_PAGE
    page_spec = lambda k, t: pl.BlockSpec((1, 128, KV_WIDTH), lambda b, g, pt: (
        (pt[b * n_pages + g * n_g + k], t, 0) if transposed else (pt[b * n_pages + g * n_g + k], 0, t)))
    full = lambda a: pl.BlockSpec(a.shape, lambda b, g, pt: (0,) * a.ndim)
    return pl.pallas_call(
        functools.partial(_compress_body, n_g=n_g, transposed=transposed),
        grid_spec=pltpu.PrefetchScalarGridSpec(
            num_scalar_prefetch=1,
            grid=(n_seq, n_pages // n_g),
            in_specs=[page_spec(k, t) for t in range(2) for k in range(n_g)] + [full(w1), full(c1), full(w2), full(b2)],
            out_specs=pl.BlockSpec((1, m_rows, 2 * KV_WIDTH), lambda b, g, pt: (b, g, 0)),
            scratch_shapes=[pltpu.VMEM((2, 8, CMP_HID2), f32), pltpu.VMEM((2, n_g * 128, KV_WIDTH), f32)],
        ),
        out_shape=jax.ShapeDtypeStruct((n_seq, n_pages * CHUNKS_PER_PAGE, 2 * KV_WIDTH), f32),
        compiler_params=pltpu.CompilerParams(vmem_limit_bytes=VMEM_LIMIT),
        name="compress_pages",
    )(page_table.reshape(-1).astype(jnp.int32), *([pool] * (2 * n_g)), w1, c1, w2, b2)


SD_PAGES = 64
SD_KEYS = SD_PAGES * 128
SD_SUB = 16
SD_ROWS = NSA_HEADS * 8
WIN_PAD = 640


def _masked_softmax(s, ok):
    s = jnp.where(ok, s, NEG_INF)
    e = jnp.where(ok, jnp.exp(s - jnp.max(s, axis=-1, keepdims=True)), 0.0)
    return e / jnp.maximum(jnp.sum(e, axis=-1, keepdims=True), 1e-30)


def _rows_from_group(a):
    t = a.shape[0] // NSA_KV_HEADS
    a4 = jnp.broadcast_to(a.reshape(NSA_KV_HEADS, 1, t, a.shape[1]), (NSA_KV_HEADS, GRP, t, a.shape[1]))
    return a4.reshape(NSA_KV_HEADS * GRP * t, a.shape[1])


def _nsa_decode_body(pt_ref, *refs, pos0, n_cmp, n_sel, n_new, win_len):
    f32, bf16 = jnp.float32, jnp.bfloat16
    kt_refs, vt_refs = refs[:SD_PAGES], refs[SD_PAGES:2 * SD_PAGES]
    (q_ref, g_ref, kcvc_ref, knew_ref, wh_ref, bc_ref, bs_ref, bn_ref, bw_ref, cov_ref, exp_ref,
     o_ref, sel_ref, oc_ref, m_ref, l_ref, acc_ref, mask_ref) = refs[2 * SD_PAGES:]
    tile, n_tiles = pl.program_id(1), pl.num_programs(1)
    nt_dims = (((1,), (1,)), ((), ()))
    nn_dims = (((1,), (0,)), ((), ()))
    n_tok = SD_ROWS // NSA_HEADS
    q = (q_ref[0] * (HEAD_DIM ** -0.5)).astype(bf16)
    qpos = pos0 + lax.broadcasted_iota(jnp.int32, (SD_ROWS, 1), 0) % n_tok

    @pl.when(tile == 0)
    def _():
        n_c = kcvc_ref.shape[1]
        m_idx = lax.broadcasted_iota(jnp.int32, (SD_ROWS, n_c), 1)
        ok_c = (m_idx >= 1) & (m_idx <= n_cmp) & (qpos >= CMP_STRIDE * m_idx + (CMP_BLOCK - 1 - CMP_STRIDE))
        s_c = lax.dot_general(q, kcvc_ref[0, :, :KV_WIDTH].astype(bf16), nt_dims, preferred_element_type=f32) + bc_ref[...]
        p_c = _masked_softmax(s_c, ok_c)
        oc_ref[...] = jnp.dot(p_c.astype(bf16), kcvc_ref[0, :, KV_WIDTH:].astype(bf16), preferred_element_type=f32)
        p_sum = jnp.sum(p_c.reshape(NSA_KV_HEADS, GRP, n_tok, n_c), axis=1).reshape(NSA_KV_HEADS * n_tok, n_c)
        imp = jnp.dot(p_sum.astype(bf16), cov_ref[...], preferred_element_type=f32)
        n_l = imp.shape[1]
        lane = lax.broadcasted_iota(jnp.int32, (NSA_KV_HEADS * n_tok, n_l), 1)
        qp = pos0 + lax.broadcasted_iota(jnp.int32, (NSA_KV_HEADS * n_tok, n_l), 0) % n_tok
        cur = qp // SEL_BLOCK
        forced = (lane == 0) | (lane == cur) | (lane == cur - 1)
        score = jnp.where(forced, FORCE_SCORE, jnp.where(lane * SEL_BLOCK <= qp, imp, -1.0))
        score = jnp.where(lane < n_sel, score, -jnp.inf)
        sel = jnp.zeros(score.shape, f32)
        for _ in range(min(SEL_TOP_N, n_sel)):
            best = jnp.max(score, axis=-1, keepdims=True)
            lane_f = lane.astype(f32)
            hit = lane_f == jnp.min(jnp.where(score == best, lane_f, float(n_l)), axis=-1, keepdims=True)
            sel = jnp.where(hit, 1.0, sel)
            score = jnp.where(hit, -jnp.inf, score)
        sel_ref[...] = sel
        m_ref[...] = jnp.full(m_ref.shape, NEG_INF, f32)
        l_ref[...] = jnp.zeros(l_ref.shape, f32)
        acc_ref[...] = jnp.zeros(acc_ref.shape, f32)

    def online(s, ok, v, v_dims):
        m_old = m_ref[...]
        s = jnp.where(ok, s, NEG_INF)
        m_new = jnp.maximum(m_old, jnp.max(s, axis=-1, keepdims=True))
        p = jnp.where(ok, jnp.exp(s - m_new), 0.0)
        alpha = jnp.exp(m_old - m_new)
        l_ref[...] = alpha * l_ref[...] + jnp.sum(p, axis=-1, keepdims=True)
        acc_ref[...] = alpha * acc_ref[...] + lax.dot_general(p.astype(bf16), v, v_dims, preferred_element_type=f32)
        m_ref[...] = m_new

    sel_tile = sel_ref[:, pl.ds(pl.multiple_of(tile * LANES, LANES), LANES)]
    mask_ref[...] = jnp.dot(sel_tile.astype(bf16), exp_ref[...], preferred_element_type=f32)
    sub_keys = SD_SUB * 128
    for sub in range(SD_PAGES // SD_SUB):
        pages = slice(sub * SD_SUB, (sub + 1) * SD_SUB)
        kt = jnp.concatenate([r[0] for r in kt_refs[pages]], axis=1).astype(bf16)
        vt = jnp.concatenate([r[0] for r in vt_refs[pages]], axis=1).astype(bf16)
        s = jnp.dot(q, kt, preferred_element_type=f32) + bs_ref[:, sub * sub_keys:(sub + 1) * sub_keys]
        ok = _rows_from_group(mask_ref[:, sub * sub_keys:(sub + 1) * sub_keys]) > 0.5
        online(s, ok, vt, nt_dims)

    @pl.when(tile == n_tiles - 1)
    def _():
        lane = lax.broadcasted_iota(jnp.int32, (SD_ROWS, LANES), 1)
        new_blk = pos0 // SEL_BLOCK
        sel_new = _rows_from_group(sel_ref[:, new_blk:new_blk + 1]) > 0.5
        ok_n = sel_new & (lane < n_new) & (pos0 + lane <= qpos)
        s_n = lax.dot_general(q, knew_ref[0, :, :KV_WIDTH].astype(bf16), nt_dims, preferred_element_type=f32) + bn_ref[...]
        online(s_n, ok_n, knew_ref[0, :, KV_WIDTH:].astype(bf16), nn_dims)
        o_s = acc_ref[...] / jnp.maximum(l_ref[...], 1e-30)
        j = lax.broadcasted_iota(jnp.int32, (SD_ROWS, WIN_PAD), 1)
        kwpos = pos0 - win_len + j
        dist = qpos - kwpos
        ok_w = (dist >= 0) & (dist < WINDOW) & (kwpos >= 0) & (j < win_len + n_new)
        s_w = lax.dot_general(q, wh_ref[0, :, :KV_WIDTH].astype(bf16), nt_dims, preferred_element_type=f32) + bw_ref[...]
        o_w = jnp.dot(_masked_softmax(s_w, ok_w).astype(bf16), wh_ref[0, :, KV_WIDTH:].astype(bf16), preferred_element_type=f32)
        g = jax.nn.sigmoid(g_ref[0])
        o = g[:, 0:1] * oc_ref[...] + g[:, 1:2] * o_s + g[:, 2:3] * o_w
        row = lax.broadcasted_iota(jnp.int32, (SD_ROWS, HEAD_DIM), 0)
        o_ref[0] = jnp.where(row < SD_ROWS // NSA_KV_HEADS, o[:, :HEAD_DIM], o[:, HEAD_DIM:])


def nsa_decode(q, glog, kv_new, kcvc, pool_t, page_table, cache_win, thr, rel_bias, pos0):
    f32, bf16 = jnp.float32, jnp.bfloat16
    b, t = q.shape[:2]
    n_pages = page_table.shape[1]
    win_len = cache_win.shape[1]
    assert t * NSA_HEADS == SD_ROWS and pos0 == n_pages * 128 and n_pages % SD_PAGES == 0 and pos0 % SEL_BLOCK == 0
    assert win_len + t <= WIN_PAD and t <= SEL_BLOCK and SD_KEYS == LANES * SEL_BLOCK
    n_rows = pos0 + t
    n_cmp = (n_rows - CMP_BLOCK) // CMP_STRIDE + 1
    n_sel = -(-n_rows // SEL_BLOCK)
    n_c = kcvc.shape[1]
    sel_lanes = -(-n_sel // LANES) * LANES
    q5 = jnp.transpose(q.reshape(b, t, NSA_KV_HEADS, GRP, HEAD_DIM), (0, 2, 3, 1, 4))
    qz = (q5[:, :, :, :, None, :] * jnp.eye(NSA_KV_HEADS, dtype=f32)[None, :, None, None, :, None]).reshape(b, SD_ROWS, KV_WIDTH)
    g3 = jnp.transpose(glog.reshape(b, t, NSA_HEADS, 3), (0, 2, 1, 3)).reshape(b, SD_ROWS, 3)
    knew = jnp.pad(kv_new[:, :, 2:4].reshape(b, t, 2 * KV_WIDTH), ((0, 0), (0, LANES - t), (0, 0)))
    whist = jnp.concatenate([cache_win.reshape(b, win_len, 2 * KV_WIDTH), kv_new[:, :, 4:6].reshape(b, t, 2 * KV_WIDTH),
                             jnp.zeros((b, WIN_PAD - win_len - t, 2 * KV_WIDTH), f32)], axis=1)
    qp = pos0 + jnp.arange(t, dtype=jnp.int32)[:, None]
    tab = lambda dist: bias_tables(dist[None], thr, rel_bias).reshape(SD_ROWS, dist.shape[1])
    bias_c = tab(qp - (CMP_STRIDE * jnp.arange(n_c, dtype=jnp.int32)[None, :] + CMP_BLOCK - 1 - CMP_STRIDE))
    bias_s = tab(qp - jnp.arange(pos0, dtype=jnp.int32)[None, :])
    bias_n = tab(qp - (pos0 + jnp.arange(LANES, dtype=jnp.int32)[None, :]))
    bias_w = tab(qp - (pos0 - win_len + jnp.arange(WIN_PAD, dtype=jnp.int32)[None, :]))
    m_idx = jnp.arange(n_c)[:, None]
    c_start = CMP_STRIDE * (m_idx - 1)
    s_start = SEL_BLOCK * jnp.arange(sel_lanes)[None, :]
    cover = ((c_start < s_start + SEL_BLOCK) & (c_start + CMP_BLOCK > s_start) & (m_idx >= 1) & (m_idx <= n_cmp)
             & (jnp.arange(sel_lanes)[None, :] < n_sel)).astype(bf16)
    expand = (jnp.arange(SD_KEYS)[None, :] // SEL_BLOCK == jnp.arange(LANES)[:, None]).astype(bf16)
    page_spec = lambda k, blk: pl.BlockSpec((1, KV_WIDTH, 128), lambda bb, g, pt: (pt[bb * n_pages + g * SD_PAGES + k], blk, 0))
    per_seq = lambda a: pl.BlockSpec((1,) + a.shape[1:], lambda bb, g, pt: (bb,) + (0,) * (a.ndim - 1))
    full = lambda a: pl.BlockSpec(a.shape, lambda bb, g, pt: (0,) * a.ndim)
    kv_rows = NSA_KV_HEADS * t
    out = pl.pallas_call(
        functools.partial(_nsa_decode_body, pos0=pos0, n_cmp=n_cmp, n_sel=n_sel, n_new=t, win_len=win_len),
        grid_spec=pltpu.PrefetchScalarGridSpec(
            num_scalar_prefetch=1,
            grid=(b, n_pages // SD_PAGES),
            in_specs=[page_spec(k, blk) for blk in (2, 3) for k in range(SD_PAGES)]
            + [per_seq(qz), per_seq(g3), per_seq(kcvc), per_seq(knew), per_seq(whist), full(bias_c),
               pl.BlockSpec((SD_ROWS, SD_KEYS), lambda bb, g, pt: (0, g)), full(bias_n), full(bias_w), full(cover), full(expand)],
            out_specs=pl.BlockSpec((1, SD_ROWS, HEAD_DIM), lambda bb, g, pt: (bb, 0, 0)),
            scratch_shapes=[pltpu.VMEM((kv_rows, sel_lanes), f32), pltpu.VMEM((SD_ROWS, KV_WIDTH), f32),
                            pltpu.VMEM((SD_ROWS, 1), f32), pltpu.VMEM((SD_ROWS, 1), f32), pltpu.VMEM((SD_ROWS, KV_WIDTH), f32),
                            pltpu.VMEM((kv_rows, SD_KEYS), f32)],
        ),
        out_shape=jax.ShapeDtypeStruct((b, SD_ROWS, HEAD_DIM), f32),
        compiler_params=pltpu.CompilerParams(vmem_limit_bytes=VMEM_LIMIT),
        name="nsa_decode",
    )(page_table.reshape(-1).astype(jnp.int32), *([pool_t] * (2 * SD_PAGES)), qz, g3, kcvc, knew, whist,
      bias_c, bias_s, bias_n, bias_w, cover, expand)
    out = jnp.transpose(out.reshape(b, NSA_KV_HEADS, GRP, t, HEAD_DIM), (0, 3, 1, 2, 4))
    return out.reshape(b, t, NSA_WIDTH)


MOE_ROWS = 1280
MOE_SUB = 64
MOE_PATHS = (256, 1024, 1088, 1152, 1216, MOE_ROWS)
MOE_CHUNK = 256
MOE_TF = 256
MOE_COLS = 256


def _moe_body(e_ref, sub_ref, blk_ref, ok_ref, x_ref, wg_ref, wl_ref, bg_ref, bl_ref, wd_ref, bd_ref, o_ref, xs_ref):
    bf16 = jnp.bfloat16
    i, f = pl.program_id(0), pl.program_id(1)
    n_sub = sub_ref[i]
    d = o_ref.shape[1]

    @pl.when(f == 0)
    def _():
        o_ref[...] = jnp.broadcast_to(bd_ref[0], o_ref.shape)

    @pl.when((f == 0) & (n_sub > 0))
    def _():
        for j in range(MOE_ROWS // MOE_CHUNK):
            rows = slice(j * MOE_CHUNK, (j + 1) * MOE_CHUNK)
            w = lax.bitcast_convert_type(x_ref[rows, :], jnp.uint32)
            hi = lax.bitcast_convert_type(w & jnp.uint32(0xFFFF0000), jnp.float32)
            lo = lax.bitcast_convert_type(w << 16, jnp.float32)
            xs_ref[rows, :] = jnp.concatenate([hi, lo], axis=1).astype(bf16)

    def expert_rows(n_rows):
        x = xs_ref[0:n_rows, :]
        hg = jnp.dot(x, wg_ref[0].astype(bf16), preferred_element_type=jnp.float32) + bg_ref[0]
        hl = jnp.dot(x, wl_ref[0].astype(bf16), preferred_element_type=jnp.float32) + bl_ref[0]
        hg = jnp.minimum(hg, SWIGLU_LIMIT)
        hl = jnp.clip(hl, -SWIGLU_LIMIT, SWIGLU_LIMIT)
        act = (hg * jax.nn.sigmoid(SWIGLU_ALPHA * hg) * (hl + 1.0)).astype(bf16)
        for c in range(d // MOE_COLS):
            cols = slice(c * MOE_COLS, (c + 1) * MOE_COLS)
            o_ref[0:n_rows, cols] += jnp.dot(act, wd_ref[0, :, cols].astype(bf16), preferred_element_type=jnp.float32)

    below = 0
    for path_rows in MOE_PATHS:
        @pl.when((n_sub * MOE_SUB > below) & (n_sub * MOE_SUB <= path_rows))
        def _(path_rows=path_rows):
            expert_rows(path_rows)

        below = path_rows


def moe_experts(item_e, item_sub, item_blk, item_ok, xb, w_gu, b_gu, w_down, b_down):
    n_items = item_e.shape[0]
    n_f = D_FF // MOE_TF
    d = 2 * xb.shape[1]

    def col(f, ok, i):
        return f * ok[i] + (n_f - 1) * (1 - ok[i])

    return pl.pallas_call(
        _moe_body,
        grid_spec=pltpu.PrefetchScalarGridSpec(
            num_scalar_prefetch=4,
            grid=(n_items, n_f),
            in_specs=[pl.BlockSpec((MOE_ROWS, d // 2), lambda i, f, e, s, b, ok: (b[i], 0)),
                      pl.BlockSpec((1, d, MOE_TF), lambda i, f, e, s, b, ok: (e[i], 0, col(f, ok, i))),
                      pl.BlockSpec((1, d, MOE_TF), lambda i, f, e, s, b, ok: (e[i], 0, n_f + col(f, ok, i))),
                      pl.BlockSpec((1, 1, MOE_TF), lambda i, f, e, s, b, ok: (e[i], 0, col(f, ok, i))),
                      pl.BlockSpec((1, 1, MOE_TF), lambda i, f, e, s, b, ok: (e[i], 0, n_f + col(f, ok, i))),
                      pl.BlockSpec((1, MOE_TF, d), lambda i, f, e, s, b, ok: (e[i], col(f, ok, i), 0)),
                      pl.BlockSpec((1, 1, d), lambda i, f, e, s, b, ok: (e[i], 0, 0))],
            out_specs=pl.BlockSpec((MOE_ROWS, d), lambda i, f, e, s, b, ok: (i, 0)),
            scratch_shapes=[pltpu.VMEM((MOE_ROWS, d), jnp.bfloat16)],
        ),
        out_shape=jax.ShapeDtypeStruct((n_items * MOE_ROWS, d), jnp.float32),
        compiler_params=pltpu.CompilerParams(vmem_limit_bytes=VMEM_LIMIT),
        name="moe_experts",
    )(item_e, item_sub, item_blk, item_ok, xb, w_gu, w_gu, b_gu[:, None, :], b_gu[:, None, :], w_down, b_down[:, None, :])


ROW_TILE = 256
PROJ_TILE = 512
PROJ_ROWS = 512


def _rms(v, g):
    return v * lax.rsqrt(jnp.mean(v * v, axis=-1, keepdims=True) + RMS_EPS) * g


def _norm_proj_body(x_ref, g_ref, w_ref, o_ref, h_ref):
    @pl.when(pl.program_id(1) == 0)
    def _():
        h_ref[...] = _rms(x_ref[...], g_ref[...]).astype(jnp.bfloat16)

    o_ref[...] = jnp.dot(h_ref[...], w_ref[...], preferred_element_type=jnp.float32)


def norm_proj(x, gain, w):
    n, d = x.shape
    cols = w.shape[1]
    tm = min(n, PROJ_ROWS)
    return pl.pallas_call(
        _norm_proj_body,
        grid=(n // tm, cols // PROJ_TILE),
        in_specs=[pl.BlockSpec((tm, d), lambda i, j: (i, 0)),
                  pl.BlockSpec((1, d), lambda i, j: (0, 0)),
                  pl.BlockSpec((d, PROJ_TILE), lambda i, j: (0, j))],
        out_specs=pl.BlockSpec((tm, PROJ_TILE), lambda i, j: (i, j)),
        out_shape=jax.ShapeDtypeStruct((n, cols), jnp.float32),
        scratch_shapes=[pltpu.VMEM((tm, d), jnp.bfloat16)],
        compiler_params=pltpu.CompilerParams(vmem_limit_bytes=VMEM_LIMIT),
        name="norm_proj",
    )(x, gain.reshape(1, d), w)


def _mix_out_body(attn_ref, y_ref, x_ref, wglu_ref, bglu_ref, ga_ref, gs_ref, wout_ref, gf_ref, wr_ref,
                  x2_ref, hp_ref, lg_ref):
    f32, bf16 = jnp.float32, jnp.bfloat16
    y = y_ref[...]
    ssm = y * jax.nn.sigmoid(jnp.dot(y.astype(bf16), wglu_ref[...], preferred_element_type=f32) + bglu_ref[...])
    merged = jnp.concatenate([_rms(attn_ref[...], ga_ref[...]), _rms(ssm, gs_ref[...])], axis=1).astype(bf16)
    x2 = x_ref[...] + jnp.dot(merged, wout_ref[...], preferred_element_type=f32)
    x2_ref[...] = x2
    hm = _rms(x2, gf_ref[...]).astype(bf16)
    lg_ref[...] = jnp.dot(hm, wr_ref[...], preferred_element_type=f32)
    bits = lax.bitcast_convert_type(hm.astype(f32), jnp.uint32)
    half = bits.shape[1] // 2
    hp_ref[...] = lax.bitcast_convert_type(bits[:, :half] | (bits[:, half:] >> 16), f32)


def mix_out(attn, y, x, lw):
    n = x.shape[0]
    f32, bf16 = jnp.float32, jnp.bfloat16
    row = lambda a: a.astype(f32).reshape(1, -1)
    wr = jnp.pad(lw['router_w'], ((0, 0), (0, LANES - N_EXPERTS))).astype(bf16)
    consts = [lw['w_glu'].astype(bf16), row(lw['b_glu']), row(lw['norm_attn_out']), row(lw['norm_ssm_out']),
              lw['w_out'].astype(bf16), row(lw['norm_ffn']), wr]
    tile = lambda c: pl.BlockSpec((ROW_TILE, c), lambda i: (i, 0))
    full = lambda a: pl.BlockSpec(a.shape, lambda i: (0, 0))
    return pl.pallas_call(
        _mix_out_body,
        grid=(n // ROW_TILE,),
        in_specs=[tile(NSA_WIDTH), tile(SSM_WIDTH), tile(D_MODEL)] + [full(c) for c in consts],
        out_specs=[tile(D_MODEL), tile(D_MODEL // 2), tile(LANES)],
        out_shape=[jax.ShapeDtypeStruct((n, D_MODEL), f32), jax.ShapeDtypeStruct((n, D_MODEL // 2), f32),
                   jax.ShapeDtypeStruct((n, LANES), f32)],
        compiler_params=pltpu.CompilerParams(vmem_limit_bytes=VMEM_LIMIT),
        name="mix_out",
    )(attn, y, x, *consts)


def _combine_body(x_ref, y0_ref, y1_ref, y2_ref, y3_ref, gate_ref, g_ref, o_ref, *, normed):
    gate = gate_ref[...]
    moe = gate[:, 0:1] * y0_ref[...]
    for k, y_ref in enumerate((y1_ref, y2_ref, y3_ref), start=1):
        moe = moe + gate[:, k:k + 1] * y_ref[...]
    out = x_ref[...] + moe
    o_ref[...] = _rms(out, g_ref[...]) if normed else out


def combine_norm(x, ys, gate, gain):
    n, d = x.shape
    normed = gain is not None
    gain = jnp.ones((d,), jnp.float32) if gain is None else gain
    tile = lambda c: pl.BlockSpec((ROW_TILE, c), lambda i: (i, 0))
    return pl.pallas_call(
        functools.partial(_combine_body, normed=normed),
        grid=(n // ROW_TILE,),
        in_specs=[tile(d)] * (1 + TOP_K) + [tile(TOP_K), pl.BlockSpec((1, d), lambda i: (0, 0))],
        out_specs=tile(d),
        out_shape=jax.ShapeDtypeStruct((n, d), jnp.float32),
        name="combine_norm",
    )(x, *ys, gate, gain.astype(jnp.float32).reshape(1, d))


def moe_routed(hp, logits, router_b, w_gu, b_gu, w_down, b_down):
    n = hp.shape[0]
    top_val, top_idx = lax.top_k(logits[:, :N_EXPERTS] + router_b.astype(jnp.float32), TOP_K)
    gate = jax.nn.softmax(top_val, axis=-1)
    nk = n * TOP_K
    n_items = N_EXPERTS + nk // MOE_ROWS
    flat_e = top_idx.reshape(nk)
    onehot = (flat_e[:, None] == jnp.arange(N_EXPERTS, dtype=flat_e.dtype)[None, :]).astype(jnp.int32)
    running = jnp.cumsum(onehot, axis=0)
    counts = running[-1]
    pos_in_e = jnp.sum(onehot * running, axis=1) - 1
    items_e = (counts + MOE_ROWS - 1) // MOE_ROWS
    item_end = jnp.cumsum(items_e)
    item_start = item_end - items_e
    total = item_end[-1]
    dest = (item_start[flat_e] * MOE_ROWS + pos_in_e).astype(jnp.int32)
    row_tok = jnp.zeros(n_items * MOE_ROWS, jnp.int32).at[dest].set(jnp.arange(nk, dtype=jnp.int32) // TOP_K)
    item = jnp.minimum(jnp.arange(n_items, dtype=jnp.int32), total - 1)
    item_e = jnp.minimum(jnp.searchsorted(item_end, item, side='right'), N_EXPERTS - 1).astype(jnp.int32)
    rows = jnp.clip(counts[item_e] - (item - item_start[item_e]) * MOE_ROWS, 0, MOE_ROWS)
    item_sub = jnp.where(jnp.arange(n_items) < total, (rows + MOE_SUB - 1) // MOE_SUB, 0).astype(jnp.int32)
    item_ok = (jnp.arange(n_items) < total).astype(jnp.int32)
    yb = moe_experts(item_e, item_sub, item.astype(jnp.int32), item_ok, hp[row_tok], w_gu, b_gu, w_down, b_down)
    return yb, dest.reshape(n, TOP_K), gate


def mixer_layer(x, pos0, pool, page_table, win_buf, h0_re, h0_im, rel_bias, lw):
    b, t, _ = x.shape
    w_in = jnp.pad(lw['w_in'], ((0, 0), (0, -IN_WIDTH % PROJ_TILE))).astype(jnp.bfloat16)
    proj = norm_proj(x.reshape(b * t, D_MODEL), lw['norm_mix'].astype(jnp.float32), w_in).reshape(b, t, -1)
    o1 = NSA_WIDTH
    o2 = o1 + N_KV * KV_WIDTH
    o3 = o2 + 3 * NSA_HEADS
    q = proj[..., :o1].reshape(b, t, NSA_HEADS, HEAD_DIM)
    kv_new = proj[..., o1:o2].reshape(b, t, N_KV, NSA_KV_HEADS, HEAD_DIM)
    u = proj[..., o3:IN_WIDTH]
    paged_new = kv_new[:, :, :N_PAGED]
    win_new = kv_new[:, :, N_PAGED:]
    n_rows = pos0 + t
    n_cmp = (n_rows - CMP_BLOCK) // CMP_STRIDE + 1
    thr = bucket_thresholds()
    cmp_w = (lw['cmp_pe'], lw['cmp_w1'], lw['cmp_b1'], lw['cmp_w2'], lw['cmp_b2'])
    if pool is None:
        assert pos0 == 0 and t % 128 == 0
        pages = t // 128
        own_pool = paged_new.reshape(b * pages, 128, ROW_LANES)
        own_table = jnp.arange(b * pages, dtype=jnp.int32).reshape(b, pages)
        kcvc = compress_pages(own_pool, own_table, *cmp_w, n_g=pages, transposed=False)
        attn = nsa_prefill(q, proj[..., o2:o3], kcvc, kv_new, thr, rel_bias, n_cmp).astype(x.dtype)
        w_hist = win_new
    else:
        kcvc = compress_pages(pool, page_table, *cmp_w, n_g=CMP_PAGES, transposed=True)
        attn = nsa_decode(q, proj[..., o2:o3], kv_new, kcvc, pool, page_table, win_buf, thr, rel_bias, pos0).astype(x.dtype)
        w_hist = jnp.concatenate([win_buf, win_new], axis=1)
    new_win = w_hist[:, w_hist.shape[1] - min(WINDOW, n_rows):]
    y, h_re, h_im = s5_branch(u, h0_re, h0_im, lw, pool is None)
    x2, hp, logits = mix_out(attn.reshape(b * t, NSA_WIDTH), y.reshape(b * t, SSM_WIDTH), x.reshape(b * t, D_MODEL), lw)
    return (x2, hp, logits), paged_new, new_win, h_re, h_im


def s5_branch(u, h0_re, h0_im, lw, chained):
    bsz, t, _ = u.shape
    f32 = jnp.float32
    a, bcat, ccat = s5_discretize(lw['lam_re'], lw['lam_im'], lw['log_dt'], lw['b_re'], lw['b_im'], lw['c_re'], lw['c_im'])
    d_skip = lw['d_skip'].astype(f32)
    flat = lambda h: h.astype(f32).reshape(bsz, SSM_GROUPS * SSM_STATE)
    if chained:
        n_r = t // S5_CHUNK
        u4 = jnp.transpose(u.astype(f32).reshape(bsz, n_r, S5_CHUNK, SSM_WIDTH), (0, 2, 1, 3))
        y4, h_re, h_im = s5_scan(u4, flat(h0_re)[:, None], flat(h0_im)[:, None], a, bcat, ccat, d_skip, True)
        y = jnp.transpose(y4, (0, 2, 1, 3)).reshape(bsz, t, SSM_WIDTH)
        h_re, h_im = h_re[:, 0], h_im[:, 0]
    else:
        u4 = jnp.transpose(u.astype(f32), (1, 0, 2))[None]
        y4, h_re, h_im = s5_scan(u4, flat(h0_re)[None], flat(h0_im)[None], a, bcat, ccat, d_skip, False)
        y = jnp.transpose(y4[0], (1, 0, 2))
        h_re, h_im = h_re[0], h_im[0]
    st = lambda h: h.reshape(bsz, SSM_GROUPS, SSM_STATE).astype(u.dtype)
    return y, st(h_re), st(h_im)


def moe_residual(groups, lw, final_gain):
    sizes = [g[0].shape[0] for g in groups]
    hp = lax.optimization_barrier(jnp.concatenate([g[1] for g in groups], axis=0))
    logits = jnp.concatenate([g[2] for g in groups], axis=0)
    yb, dest, gate = moe_routed(hp, logits, lw['router_b'], lw['w_gu'], lw['b_gu'], lw['w_down'], lw['b_down'])
    outs, start = [], 0
    for (x2, _, _), n in zip(groups, sizes):
        rows = slice(start, start + n)
        outs.append(combine_norm(x2, [yb[dest[rows, k]] for k in range(TOP_K)], gate[rows], final_gain))
        start += n
    return outs


def kernel(x_prompt, x_sample, cache_nsa_kv, cache_win_kv, state_ssm_re, state_ssm_im, page_table, rel_bias,
           norm_mix, w_in, cmp_pe, cmp_w1, cmp_b1, cmp_w2, cmp_b2, ssm_lam_re, ssm_lam_im, ssm_log_dt,
           ssm_b_re, ssm_b_im, ssm_c_re, ssm_c_im, ssm_d, ssm_w_glu, ssm_b_glu, norm_attn_out, norm_ssm_out,
           w_out, norm_ffn, router_w, router_b, w_gu, b_gu, w_down, b_down, norm_final):
    n_seq, n_pages = page_table.shape
    past_len = n_pages * cache_nsa_kv.shape[2]
    xp, xs = x_prompt, x_sample
    kv_p, win_p, sre_p, sim_p = [], [], [], []
    kv_s, win_s, sre_s, sim_s = [], [], [], []
    for i in range(DEPTH):
        lw = dict(norm_mix=norm_mix[i], w_in=w_in[i], cmp_pe=cmp_pe[i], cmp_w1=cmp_w1[i], cmp_b1=cmp_b1[i],
                  cmp_w2=cmp_w2[i], cmp_b2=cmp_b2[i], lam_re=ssm_lam_re[i], lam_im=ssm_lam_im[i],
                  log_dt=ssm_log_dt[i], b_re=ssm_b_re[i], b_im=ssm_b_im[i], c_re=ssm_c_re[i], c_im=ssm_c_im[i],
                  d_skip=ssm_d[i], w_glu=ssm_w_glu[i], b_glu=ssm_b_glu[i], norm_attn_out=norm_attn_out[i],
                  norm_ssm_out=norm_ssm_out[i], w_out=w_out[i], norm_ffn=norm_ffn[i], router_w=router_w[i],
                  router_b=router_b[i], w_gu=w_gu[i], b_gu=b_gu[i], w_down=w_down[i], b_down=b_down[i])
        h0 = jnp.zeros((xp.shape[0], SSM_GROUPS, SSM_STATE), xp.dtype)
        gp, kv1, w1, r1, m1 = mixer_layer(xp, 0, None, None, None, h0, h0, rel_bias, lw)
        pool = jnp.transpose(cache_nsa_kv[i].reshape(cache_nsa_kv.shape[1], cache_nsa_kv.shape[2], ROW_LANES), (0, 2, 1))
        gs, kv2, w2, r2, m2 = mixer_layer(xs, past_len, pool, page_table, cache_win_kv[i], state_ssm_re[i], state_ssm_im[i], rel_bias, lw)
        op, os_ = moe_residual([gp, gs], lw, norm_final if i == DEPTH - 1 else None)
        xp, xs = op.reshape(xp.shape), os_.reshape(xs.shape)
        kv_p.append(kv1); win_p.append(w1); sre_p.append(r1); sim_p.append(m1)
        kv_s.append(kv2); win_s.append(w2); sre_s.append(r2); sim_s.append(m2)
    return (xp, xs, jnp.stack(kv_p), jnp.stack(win_p), jnp.stack(sre_p), jnp.stack(sim_p),
            jnp.stack(kv_s), jnp.stack(win_s), jnp.stack(sre_s), jnp.stack(sim_s))
```

```python
import functools
import math
import jax, jax.numpy as jnp
from jax import lax
from jax.experimental import pallas as pl
from jax.experimental.pallas import tpu as pltpu

D_MODEL = 2048
DEPTH = 1
NSA_HEADS = 16
NSA_KV_HEADS = 2
HEAD_DIM = 64
NSA_WIDTH = NSA_HEADS * HEAD_DIM
KV_WIDTH = NSA_KV_HEADS * HEAD_DIM
N_PAGED = 4
N_KV = 6
SSM_WIDTH = D_MODEL - NSA_WIDTH
SSM_CH = 16
SSM_GROUPS = SSM_WIDTH // SSM_CH
SSM_STATE = 64
IN_WIDTH = NSA_WIDTH + N_KV * KV_WIDTH + 3 * NSA_HEADS + SSM_WIDTH
CMP_BLOCK = 32
CMP_STRIDE = 16
CMP_HIDDEN = 2 * HEAD_DIM
SEL_BLOCK = 64
SEL_TOP_N = 16
WINDOW = 512
FORCE_SCORE = 1e6
REL_BUCKETS = 32
REL_MAX_DIST = 4096
N_EXPERTS = 32
TOP_K = 4
D_FF = D_MODEL
SWIGLU_LIMIT = 7.0
SWIGLU_ALPHA = 1.702
RMS_EPS = 1e-5
NEG_INF = -1e30


GRP = NSA_HEADS // NSA_KV_HEADS
TQ = 128
TK = 128
SEL_TILES = 4
WIN_TILES = 2
LANES = 128
BUCKET_TABLE_LEN = 32768
BIAS_TILE_ELEMS = 16384
VMEM_LIMIT = 56 * 1024 * 1024


def rel_bucket(dist):
    n = jnp.maximum(dist, 0)
    exact = REL_BUCKETS // 2
    nf = jnp.maximum(n, 1).astype(jnp.float32)
    large = exact + (jnp.log(nf / exact) / math.log(REL_MAX_DIST / exact) * (REL_BUCKETS - exact)).astype(jnp.int32)
    return jnp.where(n < exact, n, jnp.minimum(large, REL_BUCKETS - 1))


def bucket_thresholds():
    tab = rel_bucket(jnp.arange(BUCKET_TABLE_LEN, dtype=jnp.int32))
    return jnp.sum(tab[None, :] < jnp.arange(REL_BUCKETS, dtype=jnp.int32)[:, None], axis=1).astype(jnp.int32)


def _bias_table_body(thr_ref, rb_ref, d_ref, o_ref):
    n = jnp.maximum(d_ref[0], 0)
    for h in range(NSA_HEADS):
        val = jnp.full(n.shape, rb_ref[h], jnp.float32)
        for k in range(1, REL_BUCKETS):
            val = jnp.where(n >= thr_ref[k], rb_ref[k * NSA_HEADS + h], val)
        o_ref[h // GRP, 0, h % GRP] = val


def bias_tables(dist, thr, rel_bias):
    n, r, c = dist.shape
    ct = min(c, BIAS_TILE_ELEMS // r)
    assert c % ct == 0
    return pl.pallas_call(
        _bias_table_body,
        grid_spec=pltpu.PrefetchScalarGridSpec(
            num_scalar_prefetch=2,
            grid=(n, c // ct),
            in_specs=[pl.BlockSpec((1, r, ct), lambda i, j, *_: (i, 0, j))],
            out_specs=pl.BlockSpec((NSA_KV_HEADS, 1, GRP, r, ct), lambda i, j, *_: (0, i, 0, 0, j)),
        ),
        out_shape=jax.ShapeDtypeStruct((NSA_KV_HEADS, n, GRP, r, c), jnp.float32),
        name="bias_tables",
    )(thr, rel_bias.reshape(-1), dist)


def _nsa_prefill_body(q_ref, g_ref, kc_ref, vct_ref, ks_ref, vst_ref, kw_ref, vwt_ref, bt_ref, bc_ref, cov_ref, exp_ref,
                      o_ref, mask_ref, *, n_cmp, n_sel):
    f32, bf16 = jnp.float32, jnp.bfloat16
    qi = pl.program_id(2)
    nt_dims = (((1,), (1,)), ((), ()))
    q = (q_ref[0] * (HEAD_DIM ** -0.5)).reshape(GRP * TQ, HEAD_DIM).astype(bf16)
    lanes = [slice(g * TQ, (g + 1) * TQ) for g in range(GRP)]
    row = lax.broadcasted_iota(jnp.int32, (LANES, TQ), 0)
    qpos = qi * TQ + lax.broadcasted_iota(jnp.int32, (LANES, TQ), 1)

    ok_c = (qpos >= CMP_STRIDE * row + (CMP_BLOCK - 1)) & (row < n_cmp)
    s_c = lax.dot_general(kc_ref[0, 0].astype(bf16), q, nt_dims, preferred_element_type=f32)
    p_sum = jnp.zeros((LANES, TQ), f32)
    p_parts = []
    for g in range(GRP):
        p = _masked_softmax_rows(s_c[:, lanes[g]] + bc_ref[0, 0, g], ok_c)
        p_sum = p_sum + p
        p_parts.append(p.astype(bf16))
    o_c = jnp.dot(vct_ref[0, 0].astype(bf16), jnp.concatenate(p_parts, axis=1), preferred_element_type=f32)

    imp = jnp.dot(cov_ref[...], p_sum.astype(bf16), preferred_element_type=f32)
    rs = -(-n_sel // 8) * 8
    blk = lax.broadcasted_iota(jnp.int32, (rs, TQ), 0)
    qp = qi * TQ + lax.broadcasted_iota(jnp.int32, (rs, TQ), 1)
    cur = qp // SEL_BLOCK
    forced = (blk == 0) | (blk == cur) | (blk == cur - 1)
    score = jnp.where(forced, FORCE_SCORE, jnp.where(blk * SEL_BLOCK <= qp, imp[:rs], -1.0))
    score = jnp.where(blk < n_sel, score, -jnp.inf)
    sel = jnp.zeros((rs, TQ), f32)
    for s in range(n_sel):
        r = jnp.max(jnp.where(blk == s, score, -jnp.inf), axis=0, keepdims=True)
        beats = (score > r) | ((score == r) & (blk < s))
        rank = jnp.sum(jnp.where(beats, 1.0, 0.0), axis=0, keepdims=True)
        sel = jnp.where((blk == s) & (rank < min(SEL_TOP_N, n_sel)), 1.0, sel)
    sel = jnp.concatenate([sel, jnp.zeros((LANES - rs, TQ), f32)], axis=0)
    mask_ref[...] = jnp.dot(exp_ref[...], sel.astype(bf16), preferred_element_type=f32)

    init = (jnp.full((1, GRP * TQ), NEG_INF, f32), jnp.zeros((1, GRP * TQ), f32), jnp.zeros((HEAD_DIM, GRP * TQ), f32))
    n_diag = bt_ref.shape[1]

    def branch(k_ref, vt_ref, n_tiles, first_tile, keep):
        n_keys = n_tiles * TK
        key_i = lax.broadcasted_iota(jnp.int32, (n_keys, TQ), 0)
        qry_i = lax.broadcasted_iota(jnp.int32, (n_keys, TQ), 1)

        def step(kk, carry):
            m, l, acc = carry
            keys = pl.ds(pl.multiple_of(kk * n_keys, n_keys), n_keys)
            msk = keep(qi * TQ + qry_i - (kk * n_keys + key_i), keys)
            s_all = lax.dot_general(k_ref[0, 0, keys, :].astype(bf16), q, nt_dims, preferred_element_type=f32)
            diag = [jnp.clip(qi - (n_tiles * kk + j), 0, n_diag - 1) for j in range(n_tiles)]
            m_parts, l_parts, a_parts, p_parts = [], [], [], []
            for g in range(GRP):
                bias = jnp.concatenate([bt_ref[0, dg, g] for dg in diag], axis=0)
                s = jnp.where(msk, s_all[:, lanes[g]] + bias, NEG_INF)
                m_new = jnp.maximum(m[:, lanes[g]], jnp.max(s, axis=0, keepdims=True))
                p = jnp.where(msk, jnp.exp(s - m_new), 0.0)
                alpha = jnp.exp(m[:, lanes[g]] - m_new)
                m_parts.append(m_new)
                a_parts.append(alpha)
                l_parts.append(alpha * l[:, lanes[g]] + jnp.sum(p, axis=0, keepdims=True))
                p_parts.append(p.astype(bf16))
            pv = jnp.dot(vt_ref[0, 0, :, keys].astype(bf16), jnp.concatenate(p_parts, axis=1), preferred_element_type=f32)
            return (jnp.concatenate(m_parts, axis=1), jnp.concatenate(l_parts, axis=1),
                    jnp.concatenate(a_parts, axis=1) * acc + pv)

        _, l, acc = lax.fori_loop(first_tile // n_tiles, qi // n_tiles + 1, step, init)
        return acc / jnp.maximum(l, 1e-30)

    o_s = branch(ks_ref, vst_ref, SEL_TILES, 0, lambda dist, keys: (mask_ref[keys, :] > 0.5) & (dist >= 0))
    o_w = branch(kw_ref, vwt_ref, WIN_TILES, jnp.maximum(qi - WINDOW // TK, 0),
                 lambda dist, keys: (dist >= 0) & (dist < WINDOW))
    for g in range(GRP):
        gate = jax.nn.sigmoid(g_ref[0, 0, :, g, :])
        o_ref[0, g] = gate[0:1] * o_c[:, lanes[g]] + gate[1:2] * o_s[:, lanes[g]] + gate[2:3] * o_w[:, lanes[g]]


def _masked_softmax_rows(s, ok):
    s = jnp.where(ok, s, NEG_INF)
    e = jnp.where(ok, jnp.exp(s - jnp.max(s, axis=0, keepdims=True)), 0.0)
    return e / jnp.maximum(jnp.sum(e, axis=0, keepdims=True), 1e-30)


def nsa_prefill(q, glog, kcvc, kv_new, thr, rel_bias, n_cmp):
    f32, bf16 = jnp.float32, jnp.bfloat16
    b, t = q.shape[:2]
    nq = t // TQ
    n_sel = t // SEL_BLOCK
    assert t % (SEL_TILES * TK) == 0 and t % (WIN_TILES * TK) == 0 and n_sel <= LANES and n_cmp < LANES and kcvc.shape[1] == LANES and TQ == TK == LANES
    q4 = jnp.transpose(q, (0, 2, 1, 3))
    g5 = jnp.transpose(glog.reshape(b, t, NSA_KV_HEADS, GRP, 3), (0, 2, 4, 3, 1))
    cmp_tok = jnp.pad(kcvc[:, 1:], ((0, 0), (0, 1), (0, 0))).reshape(b, LANES, 2, NSA_KV_HEADS, HEAD_DIM)
    kc = jnp.transpose(cmp_tok[:, :, 0], (0, 2, 1, 3))
    vct = jnp.transpose(cmp_tok[:, :, 1], (0, 2, 3, 1))
    rows = lambda n: jnp.transpose(kv_new[:, :, n], (0, 2, 1, 3))
    cols = lambda n: jnp.transpose(kv_new[:, :, n], (0, 2, 3, 1))
    i = jnp.arange(TQ, dtype=jnp.int32)
    d_toep = jnp.arange(nq, dtype=jnp.int32)[:, None, None] * TQ + i[None, None, :] - i[None, :, None]
    c_end = CMP_STRIDE * jnp.arange(LANES, dtype=jnp.int32) + CMP_BLOCK - 1
    d_cmp = jnp.arange(t, dtype=jnp.int32).reshape(nq, 1, TQ) - c_end[None, :, None]
    tabs = bias_tables(jnp.concatenate([d_toep, d_cmp], axis=0), thr, rel_bias)
    bt, bc = tabs[:, :nq], tabs[:, nq:]
    s_start = SEL_BLOCK * jnp.arange(LANES)[:, None]
    c_start = CMP_STRIDE * jnp.arange(LANES)[None, :]
    cover = ((c_start < s_start + SEL_BLOCK) & (c_start + CMP_BLOCK > s_start)
             & (jnp.arange(LANES)[None, :] < n_cmp) & (jnp.arange(LANES)[:, None] < n_sel)).astype(bf16)
    expand = (jnp.arange(t)[:, None] // SEL_BLOCK == jnp.arange(LANES)[None, :]).astype(bf16)
    spec = lambda *blk: pl.BlockSpec((1, 1) + blk, lambda h, bb, qq: (bb, h) + (0,) * len(blk))
    out = pl.pallas_call(
        functools.partial(_nsa_prefill_body, n_cmp=n_cmp, n_sel=n_sel),
        grid=(NSA_KV_HEADS, b, nq),
        in_specs=[pl.BlockSpec((1, GRP, TQ, HEAD_DIM), lambda h, bb, qq: (bb, h, qq, 0)),
                  pl.BlockSpec((1, 1, 3, GRP, TQ), lambda h, bb, qq: (bb, h, 0, 0, qq)),
                  spec(LANES, HEAD_DIM), spec(HEAD_DIM, LANES),
                  spec(t, HEAD_DIM), spec(HEAD_DIM, t), spec(t, HEAD_DIM), spec(HEAD_DIM, t),
                  pl.BlockSpec((1, nq, GRP, TK, TQ), lambda h, bb, qq: (h, 0, 0, 0, 0)),
                  pl.BlockSpec((1, 1, GRP, LANES, TQ), lambda h, bb, qq: (h, qq, 0, 0, 0)),
                  pl.BlockSpec((LANES, LANES), lambda h, bb, qq: (0, 0)),
                  pl.BlockSpec((t, LANES), lambda h, bb, qq: (0, 0))],
        out_specs=pl.BlockSpec((1, GRP, HEAD_DIM, TQ), lambda h, bb, qq: (bb, h, 0, qq)),
        out_shape=jax.ShapeDtypeStruct((b, NSA_HEADS, HEAD_DIM, t), f32),
        scratch_shapes=[pltpu.VMEM((t, TQ), f32)],
        compiler_params=pltpu.CompilerParams(vmem_limit_bytes=VMEM_LIMIT),
        name="nsa_prefill",
    )(q4, g5, kc, vct, rows(2), cols(3), rows(4), cols(5), bt, bc, cover, expand)
    return jnp.transpose(out, (0, 3, 1, 2)).reshape(b, t, NSA_WIDTH)


S5_GB = 8
S5_CH = S5_GB * SSM_CH
S5_ST = S5_GB * SSM_STATE


def _cmul(a_re, a_im, b_re, b_im):
    return a_re * b_re - a_im * b_im, a_re * b_im + a_im * b_re


def _s5_body(u_ref, h0re_ref, h0im_ref, a_ref, bcat_ref, ccat_ref, d_ref, y_ref, hre_ref, him_ref, xs_ref, hin_ref,
             *, chained):
    f32, bf16 = jnp.float32, jnp.bfloat16
    n_l, n_r = u_ref.shape[1], u_ref.shape[2]
    u2 = u_ref[0].reshape(n_l * n_r, S5_CH)
    xs_ref[...] = jnp.dot(u2.astype(bf16), bcat_ref[0].astype(bf16),
                          preferred_element_type=f32).reshape(n_l, n_r, 2 * S5_ST)
    a_re, a_im = a_ref[0:1, :], a_ref[1:2, :]

    def scan_step(j, carry):
        h_re, h_im, p_re, p_im = carry
        x = xs_ref[j]
        t_re, t_im = _cmul(a_re, a_im, h_re, h_im)
        h_re, h_im = t_re + x[:, :S5_ST], t_im + x[:, S5_ST:]
        xs_ref[j] = jnp.concatenate([h_re, h_im], axis=1)
        return (h_re, h_im) + _cmul(a_re, a_im, p_re, p_im)

    if chained:
        start = (jnp.zeros((n_r, S5_ST), f32), jnp.zeros((n_r, S5_ST), f32))
    else:
        start = (h0re_ref[0], h0im_ref[0])
    ones = (jnp.ones((1, S5_ST), f32), jnp.zeros((1, S5_ST), f32))
    h_re, h_im, al_re, al_im = lax.fori_loop(0, n_l, scan_step, start + ones)

    if chained:
        hin_ref[0:1, :] = jnp.concatenate([h0re_ref[0], h0im_ref[0]], axis=1)

        def chain_step(c, carry):
            z = xs_ref[n_l - 1, pl.ds(c - 1, 1), :]
            t_re, t_im = _cmul(al_re, al_im, *carry)
            n_re, n_im = t_re + z[:, :S5_ST], t_im + z[:, S5_ST:]
            hin_ref[pl.ds(c, 1), :] = jnp.concatenate([n_re, n_im], axis=1)
            return n_re, n_im

        h_re, h_im = lax.fori_loop(1, n_r + 1, chain_step, (h0re_ref[0], h0im_ref[0]))

        def fix_step(j, carry):
            p_re, p_im = carry
            hin = hin_ref[0:n_r, :]
            t_re, t_im = _cmul(p_re, p_im, hin[:, :S5_ST], hin[:, S5_ST:])
            xs_ref[j] = xs_ref[j] + jnp.concatenate([t_re, t_im], axis=1)
            return _cmul(a_re, a_im, p_re, p_im)

        lax.fori_loop(0, n_l, fix_step, (a_re, a_im))

    hre_ref[0] = h_re
    him_ref[0] = h_im
    hs = xs_ref[...].reshape(n_l * n_r, 2 * S5_ST).astype(bf16)
    y = jnp.dot(hs, ccat_ref[0].astype(bf16), preferred_element_type=f32) + d_ref[...] * u2
    y_ref[0] = jax.nn.gelu(y).reshape(n_l, n_r, S5_CH)


def s5_discretize(lam_re, lam_im, log_dt, b_re, b_im, c_re, c_im):
    f32 = jnp.float32
    dt = jnp.exp(log_dt.astype(f32))[:, None]
    lr, li = lam_re.astype(f32), lam_im.astype(f32)
    mag = jnp.exp(lr * dt)
    a_re, a_im = mag * jnp.cos(li * dt), mag * jnp.sin(li * dt)
    den = lr * lr + li * li
    z_re = ((a_re - 1.0) * lr + a_im * li) / den
    z_im = (a_im * lr - (a_re - 1.0) * li) / den
    br, bim = b_re.astype(f32), b_im.astype(f32)
    bb_re = z_re[..., None] * br - z_im[..., None] * bim
    bb_im = z_re[..., None] * bim + z_im[..., None] * br
    ngb = SSM_GROUPS // S5_GB
    eye = jnp.eye(S5_GB, dtype=f32)

    def block_diag(w):
        wd = w[:, :, :, None, :] * eye[None, :, None, :, None]
        return wd.reshape(ngb, S5_GB * w.shape[2], S5_GB * w.shape[3])

    def pack_b(bb):
        return block_diag(jnp.swapaxes(bb.reshape(ngb, S5_GB, SSM_STATE, SSM_CH), 2, 3))

    def pack_c(cc):
        return block_diag(jnp.swapaxes(cc.reshape(ngb, S5_GB, SSM_CH, SSM_STATE), 2, 3))

    a = jnp.stack([a_re.reshape(-1), a_im.reshape(-1)])
    bcat = jnp.concatenate([pack_b(bb_re), pack_b(bb_im)], axis=2)
    ccat = jnp.concatenate([pack_c(c_re.astype(f32)), -pack_c(c_im.astype(f32))], axis=1)
    return a, bcat, ccat


def s5_scan(u4, h0_re, h0_im, a, bcat, ccat, d_skip, chained):
    nb, n_l, n_r, _ = u4.shape
    rh = h0_re.shape[1]
    ngb = SSM_GROUPS // S5_GB
    st_spec = pl.BlockSpec((1, rh, S5_ST), lambda i, j: (i, 0, j))
    return pl.pallas_call(
        functools.partial(_s5_body, chained=chained),
        grid=(nb, ngb),
        in_specs=[pl.BlockSpec((1, n_l, n_r, S5_CH), lambda i, j: (i, 0, 0, j)), st_spec, st_spec,
                  pl.BlockSpec((2, S5_ST), lambda i, j: (0, j)),
                  pl.BlockSpec((1, S5_CH, 2 * S5_ST), lambda i, j: (j, 0, 0)),
                  pl.BlockSpec((1, 2 * S5_ST, S5_CH), lambda i, j: (j, 0, 0)),
                  pl.BlockSpec((1, S5_CH), lambda i, j: (0, j))],
        out_specs=[pl.BlockSpec((1, n_l, n_r, S5_CH), lambda i, j: (i, 0, 0, j)), st_spec, st_spec],
        out_shape=[jax.ShapeDtypeStruct(u4.shape, jnp.float32),
                   jax.ShapeDtypeStruct(h0_re.shape, jnp.float32), jax.ShapeDtypeStruct(h0_re.shape, jnp.float32)],
        scratch_shapes=[pltpu.VMEM((n_l, n_r, 2 * S5_ST), jnp.float32), pltpu.VMEM((n_r + 8, 2 * S5_ST), jnp.float32)],
        compiler_params=pltpu.CompilerParams(vmem_limit_bytes=VMEM_LIMIT),
        name="s5_scan",
    )(u4, h0_re, h0_im, a, bcat, ccat, d_skip.reshape(1, SSM_WIDTH))


S5_CHUNK = 64


ROW_LANES = N_PAGED * KV_WIDTH
CHUNKS_PER_PAGE = 128 // CMP_STRIDE
CMP_HID2 = NSA_KV_HEADS * CMP_HIDDEN
CMP_PAGES = 64


def _compress_body(pt_ref, *refs, n_g, transposed):
    f32, bf16 = jnp.float32, jnp.bfloat16
    page_refs = (refs[:n_g], refs[n_g:2 * n_g])
    w1_ref, c1_ref, w2_ref, b2_ref, o_ref, carry_ref, rows_ref = refs[2 * n_g:]
    m_rows = n_g * CHUNKS_PER_PAGE
    first = pl.program_id(1) == 0
    row_id = lax.broadcasted_iota(jnp.int32, (m_rows, CMP_HID2), 0)
    outs = []
    for t in range(2):
        for k, r in enumerate(page_refs[t]):
            rows_ref[t, k * 128:(k + 1) * 128, :] = r[0].T if transposed else r[0]
        cols = [rows_ref[t, pl.ds(j, m_rows, stride=CMP_STRIDE), :] for j in range(CMP_STRIDE)]
        x = jnp.concatenate(cols, axis=1).astype(bf16)
        part = jnp.dot(x, w1_ref[t], preferred_element_type=f32)
        p0, p1 = part[:, :CMP_HID2], part[:, CMP_HID2:]
        prev = jnp.where(first, 0.0, carry_ref[t, 0:1, :])
        shifted = jnp.where(row_id == 0, prev, pltpu.roll(p0, 1, axis=0))
        carry_ref[t, 0:1, :] = p0[m_rows - 1:m_rows, :]
        h1 = (c1_ref[t] + shifted) + p1
        outs.append(jnp.dot(jax.nn.gelu(h1).astype(bf16), w2_ref[t], preferred_element_type=f32) + b2_ref[t])
    o_ref[0] = jnp.concatenate(outs, axis=1)


def compress_pages(pool, page_table, cmp_pe, cmp_w1, cmp_b1, cmp_w2, cmp_b2, n_g, transposed):
    n_seq, n_pages = page_table.shape
    assert n_pages % n_g == 0 and pool.shape[1:] == ((ROW_LANES, 128) if transposed else (128, ROW_LANES))
    f32, bf16 = jnp.float32, jnp.bfloat16
    r = CMP_BLOCK // CMP_STRIDE
    eye = jnp.eye(NSA_KV_HEADS, dtype=f32)
    w1 = cmp_w1.astype(f32).reshape(2, r, CMP_STRIDE, HEAD_DIM, CMP_HIDDEN)
    w1 = jnp.transpose(w1, (0, 2, 3, 1, 4))[:, :, None, :, :, None, :] * eye[None, None, :, None, None, :, None]
    w1 = w1.reshape(2, CMP_STRIDE * KV_WIDTH, r * CMP_HID2).astype(bf16)
    c1 = jnp.stack([jnp.einsum('ld,ldf->f', cmp_pe[t], cmp_w1[t]) + cmp_b1[t] for t in range(2)])
    c1 = jnp.tile(c1[:, None, :], (1, 1, NSA_KV_HEADS))
    w2 = (cmp_w2.astype(f32)[:, None, :, None, :] * eye[None, :, None, :, None]).reshape(2, CMP_HID2, KV_WIDTH).astype(bf16)
    b2 = jnp.tile(cmp_b2.astype(f32)[:, None, :], (1, 1, NSA_KV_HEADS))
    m_rows = n_g * CHUNKS_PER_PAGE
    page_spec = lambda k, t: pl.BlockSpec((1, 128, KV_WIDTH), lambda b, g, pt: (
        (pt[b * n_pages + g * n_g + k], t, 0) if transposed else (pt[b * n_pages + g * n_g + k], 0, t)))
    full = lambda a: pl.BlockSpec(a.shape, lambda b, g, pt: (0,) * a.ndim)
    return pl.pallas_call(
        functools.partial(_compress_body, n_g=n_g, transposed=transposed),
        grid_spec=pltpu.PrefetchScalarGridSpec(
            num_scalar_prefetch=1,
            grid=(n_seq, n_pages // n_g),
            in_specs=[page_spec(k, t) for t in range(2) for k in range(n_g)] + [full(w1), full(c1), full(w2), full(b2)],
            out_specs=pl.BlockSpec((1, m_rows, 2 * KV_WIDTH), lambda b, g, pt: (b, g, 0)),
            scratch_shapes=[pltpu.VMEM((2, 8, CMP_HID2), f32), pltpu.VMEM((2, n_g * 128, KV_WIDTH), f32)],
        ),
        out_shape=jax.ShapeDtypeStruct((n_seq, n_pages * CHUNKS_PER_PAGE, 2 * KV_WIDTH), f32),
        compiler_params=pltpu.CompilerParams(vmem_limit_bytes=VMEM_LIMIT),
        name="compress_pages",
    )(page_table.reshape(-1).astype(jnp.int32), *([pool] * (2 * n_g)), w1, c1, w2, b2)


SD_PAGES = 64
SD_KEYS = SD_PAGES * 128
SD_SUB = 16
SD_ROWS = NSA_HEADS * 8
WIN_PAD = 640


def _masked_softmax(s, ok):
    s = jnp.where(ok, s, NEG_INF)
    e = jnp.where(ok, jnp.exp(s - jnp.max(s, axis=-1, keepdims=True)), 0.0)
    return e / jnp.maximum(jnp.sum(e, axis=-1, keepdims=True), 1e-30)


def _rows_from_group(a):
    t = a.shape[0] // NSA_KV_HEADS
    a4 = jnp.broadcast_to(a.reshape(NSA_KV_HEADS, 1, t, a.shape[1]), (NSA_KV_HEADS, GRP, t, a.shape[1]))
    return a4.reshape(NSA_KV_HEADS * GRP * t, a.shape[1])


def _nsa_decode_body(pt_ref, *refs, pos0, n_cmp, n_sel, n_new, win_len):
    f32, bf16 = jnp.float32, jnp.bfloat16
    kt_refs, vt_refs = refs[:SD_PAGES], refs[SD_PAGES:2 * SD_PAGES]
    (q_ref, g_ref, kcvc_ref, knew_ref, wh_ref, bc_ref, bs_ref, bn_ref, bw_ref, cov_ref, exp_ref,
     o_ref, sel_ref, oc_ref, m_ref, l_ref, acc_ref, mask_ref) = refs[2 * SD_PAGES:]
    tile, n_tiles = pl.program_id(1), pl.num_programs(1)
    nt_dims = (((1,), (1,)), ((), ()))
    nn_dims = (((1,), (0,)), ((), ()))
    n_tok = SD_ROWS // NSA_HEADS
    q = (q_ref[0] * (HEAD_DIM ** -0.5)).astype(bf16)
    qpos = pos0 + lax.broadcasted_iota(jnp.int32, (SD_ROWS, 1), 0) % n_tok

    @pl.when(tile == 0)
    def _():
        n_c = kcvc_ref.shape[1]
        m_idx = lax.broadcasted_iota(jnp.int32, (SD_ROWS, n_c), 1)
        ok_c = (m_idx >= 1) & (m_idx <= n_cmp) & (qpos >= CMP_STRIDE * m_idx + (CMP_BLOCK - 1 - CMP_STRIDE))
        s_c = lax.dot_general(q, kcvc_ref[0, :, :KV_WIDTH].astype(bf16), nt_dims, preferred_element_type=f32) + bc_ref[...]
        p_c = _masked_softmax(s_c, ok_c)
        oc_ref[...] = jnp.dot(p_c.astype(bf16), kcvc_ref[0, :, KV_WIDTH:].astype(bf16), preferred_element_type=f32)
        p_sum = jnp.sum(p_c.reshape(NSA_KV_HEADS, GRP, n_tok, n_c), axis=1).reshape(NSA_KV_HEADS * n_tok, n_c)
        imp = jnp.dot(p_sum.astype(bf16), cov_ref[...], preferred_element_type=f32)
        n_l = imp.shape[1]
        lane = lax.broadcasted_iota(jnp.int32, (NSA_KV_HEADS * n_tok, n_l), 1)
        qp = pos0 + lax.broadcasted_iota(jnp.int32, (NSA_KV_HEADS * n_tok, n_l), 0) % n_tok
        cur = qp // SEL_BLOCK
        forced = (lane == 0) | (lane == cur) | (lane == cur - 1)
        score = jnp.where(forced, FORCE_SCORE, jnp.where(lane * SEL_BLOCK <= qp, imp, -1.0))
        score = jnp.where(lane < n_sel, score, -jnp.inf)
        sel = jnp.zeros(score.shape, f32)
        for _ in range(min(SEL_TOP_N, n_sel)):
            best = jnp.max(score, axis=-1, keepdims=True)
            lane_f = lane.astype(f32)
            hit = lane_f == jnp.min(jnp.where(score == best, lane_f, float(n_l)), axis=-1, keepdims=True)
            sel = jnp.where(hit, 1.0, sel)
            score = jnp.where(hit, -jnp.inf, score)
        sel_ref[...] = sel
        m_ref[...] = jnp.full(m_ref.shape, NEG_INF, f32)
        l_ref[...] = jnp.zeros(l_ref.shape, f32)
        acc_ref[...] = jnp.zeros(acc_ref.shape, f32)

    def online(s, ok, v, v_dims):
        m_old = m_ref[...]
        s = jnp.where(ok, s, NEG_INF)
        m_new = jnp.maximum(m_old, jnp.max(s, axis=-1, keepdims=True))
        p = jnp.where(ok, jnp.exp(s - m_new), 0.0)
        alpha = jnp.exp(m_old - m_new)
        l_ref[...] = alpha * l_ref[...] + jnp.sum(p, axis=-1, keepdims=True)
        acc_ref[...] = alpha * acc_ref[...] + lax.dot_general(p.astype(bf16), v, v_dims, preferred_element_type=f32)
        m_ref[...] = m_new

    sel_tile = sel_ref[:, pl.ds(pl.multiple_of(tile * LANES, LANES), LANES)]
    mask_ref[...] = jnp.dot(sel_tile.astype(bf16), exp_ref[...], preferred_element_type=f32)
    sub_keys = SD_SUB * 128
    for sub in range(SD_PAGES // SD_SUB):
        pages = slice(sub * SD_SUB, (sub + 1) * SD_SUB)
        kt = jnp.concatenate([r[0] for r in kt_refs[pages]], axis=1).astype(bf16)
        vt = jnp.concatenate([r[0] for r in vt_refs[pages]], axis=1).astype(bf16)
        s = jnp.dot(q, kt, preferred_element_type=f32) + bs_ref[:, sub * sub_keys:(sub + 1) * sub_keys]
        ok = _rows_from_group(mask_ref[:, sub * sub_keys:(sub + 1) * sub_keys]) > 0.5
        online(s, ok, vt, nt_dims)

    @pl.when(tile == n_tiles - 1)
    def _():
        lane = lax.broadcasted_iota(jnp.int32, (SD_ROWS, LANES), 1)
        new_blk = pos0 // SEL_BLOCK
        sel_new = _rows_from_group(sel_ref[:, new_blk:new_blk + 1]) > 0.5
        ok_n = sel_new & (lane < n_new) & (pos0 + lane <= qpos)
        s_n = lax.dot_general(q, knew_ref[0, :, :KV_WIDTH].astype(bf16), nt_dims, preferred_element_type=f32) + bn_ref[...]
        online(s_n, ok_n, knew_ref[0, :, KV_WIDTH:].astype(bf16), nn_dims)
        o_s = acc_ref[...] / jnp.maximum(l_ref[...], 1e-30)
        j = lax.broadcasted_iota(jnp.int32, (SD_ROWS, WIN_PAD), 1)
        kwpos = pos0 - win_len + j
        dist = qpos - kwpos
        ok_w = (dist >= 0) & (dist < WINDOW) & (kwpos >= 0) & (j < win_len + n_new)
        s_w = lax.dot_general(q, wh_ref[0, :, :KV_WIDTH].astype(bf16), nt_dims, preferred_element_type=f32) + bw_ref[...]
        o_w = jnp.dot(_masked_softmax(s_w, ok_w).astype(bf16), wh_ref[0, :, KV_WIDTH:].astype(bf16), preferred_element_type=f32)
        g = jax.nn.sigmoid(g_ref[0])
        o = g[:, 0:1] * oc_ref[...] + g[:, 1:2] * o_s + g[:, 2:3] * o_w
        row = lax.broadcasted_iota(jnp.int32, (SD_ROWS, HEAD_DIM), 0)
        o_ref[0] = jnp.where(row < SD_ROWS // NSA_KV_HEADS, o[:, :HEAD_DIM], o[:, HEAD_DIM:])


def nsa_decode(q, glog, kv_new, kcvc, pool_t, page_table, cache_win, thr, rel_bias, pos0):
    f32, bf16 = jnp.float32, jnp.bfloat16
    b, t = q.shape[:2]
    n_pages = page_table.shape[1]
    win_len = cache_win.shape[1]
    assert t * NSA_HEADS == SD_ROWS and pos0 == n_pages * 128 and n_pages % SD_PAGES == 0 and pos0 % SEL_BLOCK == 0
    assert win_len + t <= WIN_PAD and t <= SEL_BLOCK and SD_KEYS == LANES * SEL_BLOCK
    n_rows = pos0 + t
    n_cmp = (n_rows - CMP_BLOCK) // CMP_STRIDE + 1
    n_sel = -(-n_rows // SEL_BLOCK)
    n_c = kcvc.shape[1]
    sel_lanes = -(-n_sel // LANES) * LANES
    q5 = jnp.transpose(q.reshape(b, t, NSA_KV_HEADS, GRP, HEAD_DIM), (0, 2, 3, 1, 4))
    qz = (q5[:, :, :, :, None, :] * jnp.eye(NSA_KV_HEADS, dtype=f32)[None, :, None, None, :, None]).reshape(b, SD_ROWS, KV_WIDTH)
    g3 = jnp.transpose(glog.reshape(b, t, NSA_HEADS, 3), (0, 2, 1, 3)).reshape(b, SD_ROWS, 3)
    knew = jnp.pad(kv_new[:, :, 2:4].reshape(b, t, 2 * KV_WIDTH), ((0, 0), (0, LANES - t), (0, 0)))
    whist = jnp.concatenate([cache_win.reshape(b, win_len, 2 * KV_WIDTH), kv_new[:, :, 4:6].reshape(b, t, 2 * KV_WIDTH),
                             jnp.zeros((b, WIN_PAD - win_len - t, 2 * KV_WIDTH), f32)], axis=1)
    qp = pos0 + jnp.arange(t, dtype=jnp.int32)[:, None]
    tab = lambda dist: bias_tables(dist[None], thr, rel_bias).reshape(SD_ROWS, dist.shape[1])
    bias_c = tab(qp - (CMP_STRIDE * jnp.arange(n_c, dtype=jnp.int32)[None, :] + CMP_BLOCK - 1 - CMP_STRIDE))
    bias_s = tab(qp - jnp.arange(pos0, dtype=jnp.int32)[None, :])
    bias_n = tab(qp - (pos0 + jnp.arange(LANES, dtype=jnp.int32)[None, :]))
    bias_w = tab(qp - (pos0 - win_len + jnp.arange(WIN_PAD, dtype=jnp.int32)[None, :]))
    m_idx = jnp.arange(n_c)[:, None]
    c_start = CMP_STRIDE * (m_idx - 1)
    s_start = SEL_BLOCK * jnp.arange(sel_lanes)[None, :]
    cover = ((c_start < s_start + SEL_BLOCK) & (c_start + CMP_BLOCK > s_start) & (m_idx >= 1) & (m_idx <= n_cmp)
             & (jnp.arange(sel_lanes)[None, :] < n_sel)).astype(bf16)
    expand = (jnp.arange(SD_KEYS)[None, :] // SEL_BLOCK == jnp.arange(LANES)[:, None]).astype(bf16)
    page_spec = lambda k, blk: pl.BlockSpec((1, KV_WIDTH, 128), lambda bb, g, pt: (pt[bb * n_pages + g * SD_PAGES + k], blk, 0))
    per_seq = lambda a: pl.BlockSpec((1,) + a.shape[1:], lambda bb, g, pt: (bb,) + (0,) * (a.ndim - 1))
    full = lambda a: pl.BlockSpec(a.shape, lambda bb, g, pt: (0,) * a.ndim)
    kv_rows = NSA_KV_HEADS * t
    out = pl.pallas_call(
        functools.partial(_nsa_decode_body, pos0=pos0, n_cmp=n_cmp, n_sel=n_sel, n_new=t, win_len=win_len),
        grid_spec=pltpu.PrefetchScalarGridSpec(
            num_scalar_prefetch=1,
            grid=(b, n_pages // SD_PAGES),
            in_specs=[page_spec(k, blk) for blk in (2, 3) for k in range(SD_PAGES)]
            + [per_seq(qz), per_seq(g3), per_seq(kcvc), per_seq(knew), per_seq(whist), full(bias_c),
               pl.BlockSpec((SD_ROWS, SD_KEYS), lambda bb, g, pt: (0, g)), full(bias_n), full(bias_w), full(cover), full(expand)],
            out_specs=pl.BlockSpec((1, SD_ROWS, HEAD_DIM), lambda bb, g, pt: (bb, 0, 0)),
            scratch_shapes=[pltpu.VMEM((kv_rows, sel_lanes), f32), pltpu.VMEM((SD_ROWS, KV_WIDTH), f32),
                            pltpu.VMEM((SD_ROWS, 1), f32), pltpu.VMEM((SD_ROWS, 1), f32), pltpu.VMEM((SD_ROWS, KV_WIDTH), f32),
                            pltpu.VMEM((kv_rows, SD_KEYS), f32)],
        ),
        out_shape=jax.ShapeDtypeStruct((b, SD_ROWS, HEAD_DIM), f32),
        compiler_params=pltpu.CompilerParams(vmem_limit_bytes=VMEM_LIMIT),
        name="nsa_decode",
    )(page_table.reshape(-1).astype(jnp.int32), *([pool_t] * (2 * SD_PAGES)), qz, g3, kcvc, knew, whist,
      bias_c, bias_s, bias_n, bias_w, cover, expand)
    out = jnp.transpose(out.reshape(b, NSA_KV_HEADS, GRP, t, HEAD_DIM), (0, 3, 1, 2, 4))
    return out.reshape(b, t, NSA_WIDTH)


MOE_ROWS = 1280
MOE_SUB = 64
MOE_PATHS = (256, 1024, 1088, 1152, 1216, MOE_ROWS)
MOE_CHUNK = 256
MOE_TF = 256


def _moe_body(e_ref, sub_ref, blk_ref, ok_ref, x_ref, wg_ref, wl_ref, bg_ref, bl_ref, wd_ref, bd_ref, o_ref,
              xs_ref, act_ref):
    f32, bf16 = jnp.float32, jnp.bfloat16
    i, s = pl.program_id(0), pl.program_id(1)
    n_f = pl.num_programs(1) // 2
    n_sub = sub_ref[i]

    @pl.when((s == 0) & (n_sub > 0))
    def _():
        for j in range(MOE_ROWS // MOE_CHUNK):
            rows = slice(j * MOE_CHUNK, (j + 1) * MOE_CHUNK)
            w = lax.bitcast_convert_type(x_ref[rows, :], jnp.uint32)
            hi = lax.bitcast_convert_type(w & jnp.uint32(0xFFFF0000), f32)
            lo = lax.bitcast_convert_type(w << 16, f32)
            xs_ref[rows, :] = jnp.concatenate([hi, lo], axis=1).astype(bf16)

    @pl.when(s >= n_f)
    def _():
        o_ref[...] = jnp.broadcast_to(bd_ref[0], o_ref.shape)

    def hidden_tile(n_rows):
        x = xs_ref[0:n_rows, :]
        hg = jnp.dot(x, wg_ref[0].astype(bf16), preferred_element_type=f32) + bg_ref[0]
        hl = jnp.dot(x, wl_ref[0].astype(bf16), preferred_element_type=f32) + bl_ref[0]
        hg = jnp.minimum(hg, SWIGLU_LIMIT)
        hl = jnp.clip(hl, -SWIGLU_LIMIT, SWIGLU_LIMIT)
        act = hg * jax.nn.sigmoid(SWIGLU_ALPHA * hg) * (hl + 1.0)
        act_ref[0:n_rows, pl.ds(pl.multiple_of(s * MOE_TF, MOE_TF), MOE_TF)] = act.astype(bf16)

    def output_tile(n_rows):
        y = jnp.dot(act_ref[0:n_rows, :], wd_ref[0].astype(bf16), preferred_element_type=f32)
        o_ref[0:n_rows, :] = y + bd_ref[0]

    below = 0
    for path_rows in MOE_PATHS:
        fits = (n_sub * MOE_SUB > below) & (n_sub * MOE_SUB <= path_rows)

        @pl.when(fits & (s < n_f))
        def _(path_rows=path_rows):
            hidden_tile(path_rows)

        @pl.when(fits & (s >= n_f))
        def _(path_rows=path_rows):
            output_tile(path_rows)

        below = path_rows


def moe_experts(item_e, item_sub, item_blk, item_ok, xb, w_gu, b_gu, w_down, b_down):
    n_items = item_e.shape[0]
    n_f = D_FF // MOE_TF
    d = 2 * xb.shape[1]
    n_c = d // MOE_TF

    def hid(s, ok, i):
        return jnp.minimum(s, n_f - 1) * ok[i] + (n_f - 1) * (1 - ok[i])

    def col(s, ok, i):
        return jnp.maximum(s - n_f, 0) * ok[i] + (n_c - 1) * (1 - ok[i])

    return pl.pallas_call(
        _moe_body,
        grid_spec=pltpu.PrefetchScalarGridSpec(
            num_scalar_prefetch=4,
            grid=(n_items, n_f + n_c),
            in_specs=[pl.BlockSpec((MOE_ROWS, d // 2), lambda i, s, e, n, b, ok: (b[i], 0)),
                      pl.BlockSpec((1, d, MOE_TF), lambda i, s, e, n, b, ok: (e[i], 0, hid(s, ok, i))),
                      pl.BlockSpec((1, d, MOE_TF), lambda i, s, e, n, b, ok: (e[i], 0, n_f + hid(s, ok, i))),
                      pl.BlockSpec((1, 1, MOE_TF), lambda i, s, e, n, b, ok: (e[i], 0, hid(s, ok, i))),
                      pl.BlockSpec((1, 1, MOE_TF), lambda i, s, e, n, b, ok: (e[i], 0, n_f + hid(s, ok, i))),
                      pl.BlockSpec((1, D_FF, MOE_TF), lambda i, s, e, n, b, ok: (e[i], 0, col(s, ok, i))),
                      pl.BlockSpec((1, 1, MOE_TF), lambda i, s, e, n, b, ok: (e[i], 0, col(s, ok, i)))],
            out_specs=pl.BlockSpec((MOE_ROWS, MOE_TF), lambda i, s, e, n, b, ok: (i, jnp.maximum(s - n_f, 0))),
            scratch_shapes=[pltpu.VMEM((MOE_ROWS, d), jnp.bfloat16), pltpu.VMEM((MOE_ROWS, D_FF), jnp.bfloat16)],
        ),
        out_shape=jax.ShapeDtypeStruct((n_items * MOE_ROWS, d), jnp.float32),
        compiler_params=pltpu.CompilerParams(vmem_limit_bytes=VMEM_LIMIT),
        name="moe_experts",
    )(item_e, item_sub, item_blk, item_ok, xb, w_gu, w_gu, b_gu[:, None, :], b_gu[:, None, :], w_down, b_down[:, None, :])


ROW_TILE = 256
PROJ_TILE = 512
PROJ_ROWS = 512


def _rms(v, g):
    return v * lax.rsqrt(jnp.mean(v * v, axis=-1, keepdims=True) + RMS_EPS) * g


def _norm_proj_body(x_ref, g_ref, w_ref, o_ref, h_ref):
    @pl.when(pl.program_id(1) == 0)
    def _():
        h_ref[...] = _rms(x_ref[...], g_ref[...]).astype(jnp.bfloat16)

    o_ref[...] = jnp.dot(h_ref[...], w_ref[...], preferred_element_type=jnp.float32)


def norm_proj(x, gain, w):
    n, d = x.shape
    cols = w.shape[1]
    tm = min(n, PROJ_ROWS)
    return pl.pallas_call(
        _norm_proj_body,
        grid=(n // tm, cols // PROJ_TILE),
        in_specs=[pl.BlockSpec((tm, d), lambda i, j: (i, 0)),
                  pl.BlockSpec((1, d), lambda i, j: (0, 0)),
                  pl.BlockSpec((d, PROJ_TILE), lambda i, j: (0, j))],
        out_specs=pl.BlockSpec((tm, PROJ_TILE), lambda i, j: (i, j)),
        out_shape=jax.ShapeDtypeStruct((n, cols), jnp.float32),
        scratch_shapes=[pltpu.VMEM((tm, d), jnp.bfloat16)],
        compiler_params=pltpu.CompilerParams(vmem_limit_bytes=VMEM_LIMIT),
        name="norm_proj",
    )(x, gain.reshape(1, d), w)


def _mix_out_body(attn_ref, y_ref, x_ref, wglu_ref, bglu_ref, ga_ref, gs_ref, wout_ref, gf_ref, wr_ref,
                  x2_ref, hp_ref, lg_ref):
    f32, bf16 = jnp.float32, jnp.bfloat16
    y = y_ref[...]
    ssm = y * jax.nn.sigmoid(jnp.dot(y.astype(bf16), wglu_ref[...], preferred_element_type=f32) + bglu_ref[...])
    merged = jnp.concatenate([_rms(attn_ref[...], ga_ref[...]), _rms(ssm, gs_ref[...])], axis=1).astype(bf16)
    x2 = x_ref[...] + jnp.dot(merged, wout_ref[...], preferred_element_type=f32)
    x2_ref[...] = x2
    hm = _rms(x2, gf_ref[...]).astype(bf16)
    lg_ref[...] = jnp.dot(hm, wr_ref[...], preferred_element_type=f32)
    bits = lax.bitcast_convert_type(hm.astype(f32), jnp.uint32)
    half = bits.shape[1] // 2
    hp_ref[...] = lax.bitcast_convert_type(bits[:, :half] | (bits[:, half:] >> 16), f32)


def mix_out(attn, y, x, lw):
    n = x.shape[0]
    f32, bf16 = jnp.float32, jnp.bfloat16
    row = lambda a: a.astype(f32).reshape(1, -1)
    wr = jnp.pad(lw['router_w'], ((0, 0), (0, LANES - N_EXPERTS))).astype(bf16)
    consts = [lw['w_glu'].astype(bf16), row(lw['b_glu']), row(lw['norm_attn_out']), row(lw['norm_ssm_out']),
              lw['w_out'].astype(bf16), row(lw['norm_ffn']), wr]
    tile = lambda c: pl.BlockSpec((ROW_TILE, c), lambda i: (i, 0))
    full = lambda a: pl.BlockSpec(a.shape, lambda i: (0, 0))
    return pl.pallas_call(
        _mix_out_body,
        grid=(n // ROW_TILE,),
        in_specs=[tile(NSA_WIDTH), tile(SSM_WIDTH), tile(D_MODEL)] + [full(c) for c in consts],
        out_specs=[tile(D_MODEL), tile(D_MODEL // 2), tile(LANES)],
        out_shape=[jax.ShapeDtypeStruct((n, D_MODEL), f32), jax.ShapeDtypeStruct((n, D_MODEL // 2), f32),
                   jax.ShapeDtypeStruct((n, LANES), f32)],
        compiler_params=pltpu.CompilerParams(vmem_limit_bytes=VMEM_LIMIT),
        name="mix_out",
    )(attn, y, x, *consts)


def _combine_body(x_ref, y0_ref, y1_ref, y2_ref, y3_ref, gate_ref, g_ref, o_ref, *, normed):
    gate = gate_ref[...]
    moe = gate[:, 0:1] * y0_ref[...]
    for k, y_ref in enumerate((y1_ref, y2_ref, y3_ref), start=1):
        moe = moe + gate[:, k:k + 1] * y_ref[...]
    out = x_ref[...] + moe
    o_ref[...] = _rms(out, g_ref[...]) if normed else out


def combine_norm(x, ys, gate, gain):
    n, d = x.shape
    normed = gain is not None
    gain = jnp.ones((d,), jnp.float32) if gain is None else gain
    tile = lambda c: pl.BlockSpec((ROW_TILE, c), lambda i: (i, 0))
    return pl.pallas_call(
        functools.partial(_combine_body, normed=normed),
        grid=(n // ROW_TILE,),
        in_specs=[tile(d)] * (1 + TOP_K) + [tile(TOP_K), pl.BlockSpec((1, d), lambda i: (0, 0))],
        out_specs=tile(d),
        out_shape=jax.ShapeDtypeStruct((n, d), jnp.float32),
        name="combine_norm",
    )(x, *ys, gate, gain.astype(jnp.float32).reshape(1, d))


def moe_routed(hp, logits, router_b, w_gu, b_gu, w_down, b_down):
    n = hp.shape[0]
    top_val, top_idx = lax.top_k(logits[:, :N_EXPERTS] + router_b.astype(jnp.float32), TOP_K)
    gate = jax.nn.softmax(top_val, axis=-1)
    nk = n * TOP_K
    n_items = N_EXPERTS + nk // MOE_ROWS
    flat_e = top_idx.reshape(nk)
    onehot = (flat_e[:, None] == jnp.arange(N_EXPERTS, dtype=flat_e.dtype)[None, :]).astype(jnp.int32)
    running = jnp.cumsum(onehot, axis=0)
    counts = running[-1]
    pos_in_e = jnp.sum(onehot * running, axis=1) - 1
    items_e = (counts + MOE_ROWS - 1) // MOE_ROWS
    item_end = jnp.cumsum(items_e)
    item_start = item_end - items_e
    total = item_end[-1]
    dest = (item_start[flat_e] * MOE_ROWS + pos_in_e).astype(jnp.int32)
    row_tok = jnp.zeros(n_items * MOE_ROWS, jnp.int32).at[dest].set(jnp.arange(nk, dtype=jnp.int32) // TOP_K)
    item = jnp.minimum(jnp.arange(n_items, dtype=jnp.int32), total - 1)
    item_e = jnp.minimum(jnp.searchsorted(item_end, item, side='right'), N_EXPERTS - 1).astype(jnp.int32)
    rows = jnp.clip(counts[item_e] - (item - item_start[item_e]) * MOE_ROWS, 0, MOE_ROWS)
    item_sub = jnp.where(jnp.arange(n_items) < total, (rows + MOE_SUB - 1) // MOE_SUB, 0).astype(jnp.int32)
    item_ok = (jnp.arange(n_items) < total).astype(jnp.int32)
    yb = moe_experts(item_e, item_sub, item.astype(jnp.int32), item_ok, hp[row_tok], w_gu, b_gu, w_down, b_down)
    return yb, dest.reshape(n, TOP_K), gate


def mixer_layer(x, pos0, pool, page_table, win_buf, h0_re, h0_im, rel_bias, lw):
    b, t, _ = x.shape
    w_in = jnp.pad(lw['w_in'], ((0, 0), (0, -IN_WIDTH % PROJ_TILE))).astype(jnp.bfloat16)
    proj = norm_proj(x.reshape(b * t, D_MODEL), lw['norm_mix'].astype(jnp.float32), w_in).reshape(b, t, -1)
    o1 = NSA_WIDTH
    o2 = o1 + N_KV * KV_WIDTH
    o3 = o2 + 3 * NSA_HEADS
    q = proj[..., :o1].reshape(b, t, NSA_HEADS, HEAD_DIM)
    kv_new = proj[..., o1:o2].reshape(b, t, N_KV, NSA_KV_HEADS, HEAD_DIM)
    u = proj[..., o3:IN_WIDTH]
    paged_new = kv_new[:, :, :N_PAGED]
    win_new = kv_new[:, :, N_PAGED:]
    n_rows = pos0 + t
    n_cmp = (n_rows - CMP_BLOCK) // CMP_STRIDE + 1
    thr = bucket_thresholds()
    cmp_w = (lw['cmp_pe'], lw['cmp_w1'], lw['cmp_b1'], lw['cmp_w2'], lw['cmp_b2'])
    if pool is None:
        assert pos0 == 0 and t % 128 == 0
        pages = t // 128
        own_pool = paged_new.reshape(b * pages, 128, ROW_LANES)
        own_table = jnp.arange(b * pages, dtype=jnp.int32).reshape(b, pages)
        kcvc = compress_pages(own_pool, own_table, *cmp_w, n_g=pages, transposed=False)
        attn = nsa_prefill(q, proj[..., o2:o3], kcvc, kv_new, thr, rel_bias, n_cmp).astype(x.dtype)
        w_hist = win_new
    else:
        kcvc = compress_pages(pool, page_table, *cmp_w, n_g=CMP_PAGES, transposed=True)
        attn = nsa_decode(q, proj[..., o2:o3], kv_new, kcvc, pool, page_table, win_buf, thr, rel_bias, pos0).astype(x.dtype)
        w_hist = jnp.concatenate([win_buf, win_new], axis=1)
    new_win = w_hist[:, w_hist.shape[1] - min(WINDOW, n_rows):]
    y, h_re, h_im = s5_branch(u, h0_re, h0_im, lw, pool is None)
    x2, hp, logits = mix_out(attn.reshape(b * t, NSA_WIDTH), y.reshape(b * t, SSM_WIDTH), x.reshape(b * t, D_MODEL), lw)
    return (x2, hp, logits), paged_new, new_win, h_re, h_im


def s5_branch(u, h0_re, h0_im, lw, chained):
    bsz, t, _ = u.shape
    f32 = jnp.float32
    a, bcat, ccat = s5_discretize(lw['lam_re'], lw['lam_im'], lw['log_dt'], lw['b_re'], lw['b_im'], lw['c_re'], lw['c_im'])
    d_skip = lw['d_skip'].astype(f32)
    flat = lambda h: h.astype(f32).reshape(bsz, SSM_GROUPS * SSM_STATE)
    if chained:
        n_r = t // S5_CHUNK
        u4 = jnp.transpose(u.astype(f32).reshape(bsz, n_r, S5_CHUNK, SSM_WIDTH), (0, 2, 1, 3))
        y4, h_re, h_im = s5_scan(u4, flat(h0_re)[:, None], flat(h0_im)[:, None], a, bcat, ccat, d_skip, True)
        y = jnp.transpose(y4, (0, 2, 1, 3)).reshape(bsz, t, SSM_WIDTH)
        h_re, h_im = h_re[:, 0], h_im[:, 0]
    else:
        u4 = jnp.transpose(u.astype(f32), (1, 0, 2))[None]
        y4, h_re, h_im = s5_scan(u4, flat(h0_re)[None], flat(h0_im)[None], a, bcat, ccat, d_skip, False)
        y = jnp.transpose(y4[0], (1, 0, 2))
        h_re, h_im = h_re[0], h_im[0]
    st = lambda h: h.reshape(bsz, SSM_GROUPS, SSM_STATE).astype(u.dtype)
    return y, st(h_re), st(h_im)


def moe_residual(groups, lw, final_gain):
    sizes = [g[0].shape[0] for g in groups]
    hp = lax.optimization_barrier(jnp.concatenate([g[1] for g in groups], axis=0))
    logits = jnp.concatenate([g[2] for g in groups], axis=0)
    yb, dest, gate = moe_routed(hp, logits, lw['router_b'], lw['w_gu'], lw['b_gu'], lw['w_down'], lw['b_down'])
    outs, start = [], 0
    for (x2, _, _), n in zip(groups, sizes):
        rows = slice(start, start + n)
        outs.append(combine_norm(x2, [yb[dest[rows, k]] for k in range(TOP_K)], gate[rows], final_gain))
        start += n
    return outs


def kernel(x_prompt, x_sample, cache_nsa_kv, cache_win_kv, state_ssm_re, state_ssm_im, page_table, rel_bias,
           norm_mix, w_in, cmp_pe, cmp_w1, cmp_b1, cmp_w2, cmp_b2, ssm_lam_re, ssm_lam_im, ssm_log_dt,
           ssm_b_re, ssm_b_im, ssm_c_re, ssm_c_im, ssm_d, ssm_w_glu, ssm_b_glu, norm_attn_out, norm_ssm_out,
           w_out, norm_ffn, router_w, router_b, w_gu, b_gu, w_down, b_down, norm_final):
    n_seq, n_pages = page_table.shape
    past_len = n_pages * cache_nsa_kv.shape[2]
    xp, xs = x_prompt, x_sample
    kv_p, win_p, sre_p, sim_p = [], [], [], []
    kv_s, win_s, sre_s, sim_s = [], [], [], []
    for i in range(DEPTH):
        lw = dict(norm_mix=norm_mix[i], w_in=w_in[i], cmp_pe=cmp_pe[i], cmp_w1=cmp_w1[i], cmp_b1=cmp_b1[i],
                  cmp_w2=cmp_w2[i], cmp_b2=cmp_b2[i], lam_re=ssm_lam_re[i], lam_im=ssm_lam_im[i],
                  log_dt=ssm_log_dt[i], b_re=ssm_b_re[i], b_im=ssm_b_im[i], c_re=ssm_c_re[i], c_im=ssm_c_im[i],
                  d_skip=ssm_d[i], w_glu=ssm_w_glu[i], b_glu=ssm_b_glu[i], norm_attn_out=norm_attn_out[i],
                  norm_ssm_out=norm_ssm_out[i], w_out=w_out[i], norm_ffn=norm_ffn[i], router_w=router_w[i],
                  router_b=router_b[i], w_gu=w_gu[i], b_gu=b_gu[i], w_down=w_down[i], b_down=b_down[i])
        h0 = jnp.zeros((xp.shape[0], SSM_GROUPS, SSM_STATE), xp.dtype)
        gp, kv1, w1, r1, m1 = mixer_layer(xp, 0, None, None, None, h0, h0, rel_bias, lw)
        pool = jnp.transpose(cache_nsa_kv[i].reshape(cache_nsa_kv.shape[1], cache_nsa_kv.shape[2], ROW_LANES), (0, 2, 1))
        gs, kv2, w2, r2, m2 = mixer_layer(xs, past_len, pool, page_table, cache_win_kv[i], state_ssm_re[i], state_ssm_im[i], rel_bias, lw)
        op, os_ = moe_residual([gp, gs], lw, norm_final if i == DEPTH - 1 else None)
        xp, xs = op.reshape(xp.shape), os_.reshape(xs.shape)
        kv_p.append(kv1); win_p.append(w1); sre_p.append(r1); sim_p.append(m1)
        kv_s.append(kv2); win_s.append(w2); sre_s.append(r2); sim_s.append(m2)
    return (xp, xs, jnp.stack(kv_p), jnp.stack(win_p), jnp.stack(sre_p), jnp.stack(sim_p),
            jnp.stack(kv_s), jnp.stack(win_s), jnp.stack(sre_s), jnp.stack(sim_s))
```

```python
import functools
import math
import jax, jax.numpy as jnp
from jax import lax
from jax.experimental import pallas as pl
from jax.experimental.pallas import tpu as pltpu

D_MODEL = 2048
DEPTH = 1
NSA_HEADS = 16
NSA_KV_HEADS = 2
HEAD_DIM = 64
NSA_WIDTH = NSA_HEADS * HEAD_DIM
KV_WIDTH = NSA_KV_HEADS * HEAD_DIM
N_PAGED = 4
N_KV = 6
SSM_WIDTH = D_MODEL - NSA_WIDTH
SSM_CH = 16
SSM_GROUPS = SSM_WIDTH // SSM_CH
SSM_STATE = 64
IN_WIDTH = NSA_WIDTH + N_KV * KV_WIDTH + 3 * NSA_HEADS + SSM_WIDTH
CMP_BLOCK = 32
CMP_STRIDE = 16
CMP_HIDDEN = 2 * HEAD_DIM
SEL_BLOCK = 64
SEL_TOP_N = 16
WINDOW = 512
FORCE_SCORE = 1e6
REL_BUCKETS = 32
REL_MAX_DIST = 4096
N_EXPERTS = 32
TOP_K = 4
D_FF = D_MODEL
SWIGLU_LIMIT = 7.0
SWIGLU_ALPHA = 1.702
RMS_EPS = 1e-5
NEG_INF = -1e30


GRP = NSA_HEADS // NSA_KV_HEADS
TQ = 128
TK = 128
SEL_TILES = 4
WIN_TILES = 2
LANES = 128
BUCKET_TABLE_LEN = 32768
BIAS_TILE_ELEMS = 16384
VMEM_LIMIT = 56 * 1024 * 1024


def rel_bucket(dist):
    n = jnp.maximum(dist, 0)
    exact = REL_BUCKETS // 2
    nf = jnp.maximum(n, 1).astype(jnp.float32)
    large = exact + (jnp.log(nf / exact) / math.log(REL_MAX_DIST / exact) * (REL_BUCKETS - exact)).astype(jnp.int32)
    return jnp.where(n < exact, n, jnp.minimum(large, REL_BUCKETS - 1))


def bucket_thresholds():
    tab = rel_bucket(jnp.arange(BUCKET_TABLE_LEN, dtype=jnp.int32))
    return jnp.sum(tab[None, :] < jnp.arange(REL_BUCKETS, dtype=jnp.int32)[:, None], axis=1).astype(jnp.int32)


def _bias_table_body(thr_ref, rb_ref, d_ref, o_ref):
    n = jnp.maximum(d_ref[0], 0)
    for h in range(NSA_HEADS):
        val = jnp.full(n.shape, rb_ref[h], jnp.float32)
        for k in range(1, REL_BUCKETS):
            val = jnp.where(n >= thr_ref[k], rb_ref[k * NSA_HEADS + h], val)
        o_ref[h // GRP, 0, h % GRP] = val


def bias_tables(dist, thr, rel_bias):
    n, r, c = dist.shape
    ct = min(c, BIAS_TILE_ELEMS // r)
    assert c % ct == 0
    return pl.pallas_call(
        _bias_table_body,
        grid_spec=pltpu.PrefetchScalarGridSpec(
            num_scalar_prefetch=2,
            grid=(n, c // ct),
            in_specs=[pl.BlockSpec((1, r, ct), lambda i, j, *_: (i, 0, j))],
            out_specs=pl.BlockSpec((NSA_KV_HEADS, 1, GRP, r, ct), lambda i, j, *_: (0, i, 0, 0, j)),
        ),
        out_shape=jax.ShapeDtypeStruct((NSA_KV_HEADS, n, GRP, r, c), jnp.float32),
        name="bias_tables",
    )(thr, rel_bias.reshape(-1), dist)


def _nsa_prefill_body(q_ref, g_ref, kc_ref, vct_ref, ks_ref, vst_ref, kw_ref, vwt_ref, bt_ref, bc_ref, cov_ref, exp_ref,
                      o_ref, mask_ref, *, n_cmp, n_sel):
    f32, bf16 = jnp.float32, jnp.bfloat16
    qi = pl.program_id(2)
    nt_dims = (((1,), (1,)), ((), ()))
    q = (q_ref[0] * (HEAD_DIM ** -0.5)).reshape(GRP * TQ, HEAD_DIM).astype(bf16)
    lanes = [slice(g * TQ, (g + 1) * TQ) for g in range(GRP)]
    row = lax.broadcasted_iota(jnp.int32, (LANES, TQ), 0)
    qpos = qi * TQ + lax.broadcasted_iota(jnp.int32, (LANES, TQ), 1)

    ok_c = (qpos >= CMP_STRIDE * row + (CMP_BLOCK - 1)) & (row < n_cmp)
    s_c = lax.dot_general(kc_ref[0, 0].astype(bf16), q, nt_dims, preferred_element_type=f32)
    p_sum = jnp.zeros((LANES, TQ), f32)
    p_parts = []
    for g in range(GRP):
        p = _masked_softmax_rows(s_c[:, lanes[g]] + bc_ref[0, 0, g], ok_c)
        p_sum = p_sum + p
        p_parts.append(p.astype(bf16))
    o_c = jnp.dot(vct_ref[0, 0].astype(bf16), jnp.concatenate(p_parts, axis=1), preferred_element_type=f32)

    imp = jnp.dot(cov_ref[...], p_sum.astype(bf16), preferred_element_type=f32)
    rs = -(-n_sel // 8) * 8
    blk = lax.broadcasted_iota(jnp.int32, (rs, TQ), 0)
    qp = qi * TQ + lax.broadcasted_iota(jnp.int32, (rs, TQ), 1)
    cur = qp // SEL_BLOCK
    forced = (blk == 0) | (blk == cur) | (blk == cur - 1)
    score = jnp.where(forced, FORCE_SCORE, jnp.where(blk * SEL_BLOCK <= qp, imp[:rs], -1.0))
    score = jnp.where(blk < n_sel, score, -jnp.inf)
    sel = jnp.zeros((rs, TQ), f32)
    for s in range(n_sel):
        r = jnp.max(jnp.where(blk == s, score, -jnp.inf), axis=0, keepdims=True)
        beats = (score > r) | ((score == r) & (blk < s))
        rank = jnp.sum(jnp.where(beats, 1.0, 0.0), axis=0, keepdims=True)
        sel = jnp.where((blk == s) & (rank < min(SEL_TOP_N, n_sel)), 1.0, sel)
    sel = jnp.concatenate([sel, jnp.zeros((LANES - rs, TQ), f32)], axis=0)
    mask_ref[...] = jnp.dot(exp_ref[...], sel.astype(bf16), preferred_element_type=f32)

    init = (jnp.full((1, GRP * TQ), NEG_INF, f32), jnp.zeros((1, GRP * TQ), f32), jnp.zeros((HEAD_DIM, GRP * TQ), f32))
    n_diag = bt_ref.shape[1]

    def branch(k_ref, vt_ref, n_tiles, first_tile, keep):
        n_keys = n_tiles * TK
        key_i = lax.broadcasted_iota(jnp.int32, (n_keys, TQ), 0)
        qry_i = lax.broadcasted_iota(jnp.int32, (n_keys, TQ), 1)

        def step(kk, carry):
            m, l, acc = carry
            keys = pl.ds(pl.multiple_of(kk * n_keys, n_keys), n_keys)
            msk = keep(qi * TQ + qry_i - (kk * n_keys + key_i), keys)
            s_all = lax.dot_general(k_ref[0, 0, keys, :].astype(bf16), q, nt_dims, preferred_element_type=f32)
            diag = [jnp.clip(qi - (n_tiles * kk + j), 0, n_diag - 1) for j in range(n_tiles)]
            m_parts, l_parts, a_parts, p_parts = [], [], [], []
            for g in range(GRP):
                bias = jnp.concatenate([bt_ref[0, dg, g] for dg in diag], axis=0)
                s = jnp.where(msk, s_all[:, lanes[g]] + bias, NEG_INF)
                m_new = jnp.maximum(m[:, lanes[g]], jnp.max(s, axis=0, keepdims=True))
                p = jnp.where(msk, jnp.exp(s - m_new), 0.0)
                alpha = jnp.exp(m[:, lanes[g]] - m_new)
                m_parts.append(m_new)
                a_parts.append(alpha)
                l_parts.append(alpha * l[:, lanes[g]] + jnp.sum(p, axis=0, keepdims=True))
                p_parts.append(p.astype(bf16))
            pv = jnp.dot(vt_ref[0, 0, :, keys].astype(bf16), jnp.concatenate(p_parts, axis=1), preferred_element_type=f32)
            return (jnp.concatenate(m_parts, axis=1), jnp.concatenate(l_parts, axis=1),
                    jnp.concatenate(a_parts, axis=1) * acc + pv)

        _, l, acc = lax.fori_loop(first_tile // n_tiles, qi // n_tiles + 1, step, init)
        return acc / jnp.maximum(l, 1e-30)

    o_s = branch(ks_ref, vst_ref, SEL_TILES, 0, lambda dist, keys: (mask_ref[keys, :] > 0.5) & (dist >= 0))
    o_w = branch(kw_ref, vwt_ref, WIN_TILES, jnp.maximum(qi - WINDOW // TK, 0),
                 lambda dist, keys: (dist >= 0) & (dist < WINDOW))
    heads = []
    for g in range(GRP):
        gate = jax.nn.sigmoid(g_ref[0, 0, :, g, :])
        heads.append(gate[0:1] * o_c[:, lanes[g]] + gate[1:2] * o_s[:, lanes[g]] + gate[2:3] * o_w[:, lanes[g]])
    o_ref[0] = jnp.concatenate(heads, axis=0).T


def _masked_softmax_rows(s, ok):
    s = jnp.where(ok, s, NEG_INF)
    e = jnp.where(ok, jnp.exp(s - jnp.max(s, axis=0, keepdims=True)), 0.0)
    return e / jnp.maximum(jnp.sum(e, axis=0, keepdims=True), 1e-30)


def nsa_prefill(q, glog, kcvc, kv_new, thr, rel_bias, n_cmp):
    f32, bf16 = jnp.float32, jnp.bfloat16
    b, t = q.shape[:2]
    nq = t // TQ
    n_sel = t // SEL_BLOCK
    assert t % (SEL_TILES * TK) == 0 and t % (WIN_TILES * TK) == 0 and n_sel <= LANES and n_cmp < LANES and kcvc.shape[1] == LANES and TQ == TK == LANES
    q4 = jnp.transpose(q, (0, 2, 1, 3))
    g5 = jnp.transpose(glog.reshape(b, t, NSA_KV_HEADS, GRP, 3), (0, 2, 4, 3, 1))
    cmp_tok = jnp.pad(kcvc[:, 1:], ((0, 0), (0, 1), (0, 0))).reshape(b, LANES, 2, NSA_KV_HEADS, HEAD_DIM)
    kc = jnp.transpose(cmp_tok[:, :, 0], (0, 2, 1, 3))
    vct = jnp.transpose(cmp_tok[:, :, 1], (0, 2, 3, 1))
    rows = lambda n: jnp.transpose(kv_new[:, :, n], (0, 2, 1, 3))
    cols = lambda n: jnp.transpose(kv_new[:, :, n], (0, 2, 3, 1))
    i = jnp.arange(TQ, dtype=jnp.int32)
    d_toep = jnp.arange(nq, dtype=jnp.int32)[:, None, None] * TQ + i[None, None, :] - i[None, :, None]
    c_end = CMP_STRIDE * jnp.arange(LANES, dtype=jnp.int32) + CMP_BLOCK - 1
    d_cmp = jnp.arange(t, dtype=jnp.int32).reshape(nq, 1, TQ) - c_end[None, :, None]
    tabs = bias_tables(jnp.concatenate([d_toep, d_cmp], axis=0), thr, rel_bias)
    s_start = SEL_BLOCK * jnp.arange(LANES)[:, None]
    c_start = CMP_STRIDE * jnp.arange(LANES)[None, :]
    cover = ((c_start < s_start + SEL_BLOCK) & (c_start + CMP_BLOCK > s_start)
             & (jnp.arange(LANES)[None, :] < n_cmp) & (jnp.arange(LANES)[:, None] < n_sel)).astype(bf16)
    expand = (jnp.arange(t)[:, None] // SEL_BLOCK == jnp.arange(LANES)[None, :]).astype(bf16)
    spec = lambda *blk: pl.BlockSpec((1, 1) + blk, lambda h, bb, qq: (bb, h) + (0,) * len(blk))
    out = pl.pallas_call(
        functools.partial(_nsa_prefill_body, n_cmp=n_cmp, n_sel=n_sel),
        grid=(NSA_KV_HEADS, b, nq),
        in_specs=[pl.BlockSpec((1, GRP, TQ, HEAD_DIM), lambda h, bb, qq: (bb, h, qq, 0)),
                  pl.BlockSpec((1, 1, 3, GRP, TQ), lambda h, bb, qq: (bb, h, 0, 0, qq)),
                  spec(LANES, HEAD_DIM), spec(HEAD_DIM, LANES),
                  spec(t, HEAD_DIM), spec(HEAD_DIM, t), spec(t, HEAD_DIM), spec(HEAD_DIM, t),
                  pl.BlockSpec((1, nq, GRP, TK, TQ), lambda h, bb, qq: (h, 0, 0, 0, 0)),
                  pl.BlockSpec((1, 1, GRP, LANES, TQ), lambda h, bb, qq: (h, nq + qq, 0, 0, 0)),
                  pl.BlockSpec((LANES, LANES), lambda h, bb, qq: (0, 0)),
                  pl.BlockSpec((t, LANES), lambda h, bb, qq: (0, 0))],
        out_specs=pl.BlockSpec((1, TQ, GRP * HEAD_DIM), lambda h, bb, qq: (bb, qq, h)),
        out_shape=jax.ShapeDtypeStruct((b, t, NSA_WIDTH), f32),
        scratch_shapes=[pltpu.VMEM((t, TQ), f32)],
        compiler_params=pltpu.CompilerParams(vmem_limit_bytes=VMEM_LIMIT),
        name="nsa_prefill",
    )(q4, g5, kc, vct, rows(2), cols(3), rows(4), cols(5), tabs, tabs, cover, expand)
    return out


S5_GB = 8
S5_CH = S5_GB * SSM_CH
S5_ST = S5_GB * SSM_STATE


def _cmul(a_re, a_im, b_re, b_im):
    return a_re * b_re - a_im * b_im, a_re * b_im + a_im * b_re


def _s5_body(u_ref, h0re_ref, h0im_ref, a_ref, bcat_ref, ccat_ref, d_ref, y_ref, hre_ref, him_ref, xs_ref, hin_ref,
             *, chained):
    f32, bf16 = jnp.float32, jnp.bfloat16
    n_l, n_r = u_ref.shape[1], u_ref.shape[2]
    u2 = u_ref[0].reshape(n_l * n_r, S5_CH)
    xs_ref[...] = jnp.dot(u2.astype(bf16), bcat_ref[0].astype(bf16),
                          preferred_element_type=f32).reshape(n_l, n_r, 2 * S5_ST)
    a_re, a_im = a_ref[0:1, :], a_ref[1:2, :]

    def scan_step(j, carry):
        h_re, h_im, p_re, p_im = carry
        x = xs_ref[j]
        t_re, t_im = _cmul(a_re, a_im, h_re, h_im)
        h_re, h_im = t_re + x[:, :S5_ST], t_im + x[:, S5_ST:]
        xs_ref[j] = jnp.concatenate([h_re, h_im], axis=1)
        return (h_re, h_im) + _cmul(a_re, a_im, p_re, p_im)

    if chained:
        start = (jnp.zeros((n_r, S5_ST), f32), jnp.zeros((n_r, S5_ST), f32))
    else:
        start = (h0re_ref[0], h0im_ref[0])
    ones = (jnp.ones((1, S5_ST), f32), jnp.zeros((1, S5_ST), f32))
    h_re, h_im, al_re, al_im = lax.fori_loop(0, n_l, scan_step, start + ones)

    if chained:
        hin_ref[0:1, :] = jnp.concatenate([h0re_ref[0], h0im_ref[0]], axis=1)

        def chain_step(c, carry):
            z = xs_ref[n_l - 1, pl.ds(c - 1, 1), :]
            t_re, t_im = _cmul(al_re, al_im, *carry)
            n_re, n_im = t_re + z[:, :S5_ST], t_im + z[:, S5_ST:]
            hin_ref[pl.ds(c, 1), :] = jnp.concatenate([n_re, n_im], axis=1)
            return n_re, n_im

        h_re, h_im = lax.fori_loop(1, n_r + 1, chain_step, (h0re_ref[0], h0im_ref[0]))

        def fix_step(j, carry):
            p_re, p_im = carry
            hin = hin_ref[0:n_r, :]
            t_re, t_im = _cmul(p_re, p_im, hin[:, :S5_ST], hin[:, S5_ST:])
            xs_ref[j] = xs_ref[j] + jnp.concatenate([t_re, t_im], axis=1)
            return _cmul(a_re, a_im, p_re, p_im)

        lax.fori_loop(0, n_l, fix_step, (a_re, a_im))

    hre_ref[0] = h_re
    him_ref[0] = h_im
    hs = xs_ref[...].reshape(n_l * n_r, 2 * S5_ST).astype(bf16)
    y = jnp.dot(hs, ccat_ref[0].astype(bf16), preferred_element_type=f32) + d_ref[...] * u2
    y_ref[0] = jax.nn.gelu(y).reshape(n_l, n_r, S5_CH)


def s5_discretize(lam_re, lam_im, log_dt, b_re, b_im, c_re, c_im):
    f32 = jnp.float32
    dt = jnp.exp(log_dt.astype(f32))[:, None]
    lr, li = lam_re.astype(f32), lam_im.astype(f32)
    mag = jnp.exp(lr * dt)
    a_re, a_im = mag * jnp.cos(li * dt), mag * jnp.sin(li * dt)
    den = lr * lr + li * li
    z_re = ((a_re - 1.0) * lr + a_im * li) / den
    z_im = (a_im * lr - (a_re - 1.0) * li) / den
    br, bim = b_re.astype(f32), b_im.astype(f32)
    bb_re = z_re[..., None] * br - z_im[..., None] * bim
    bb_im = z_re[..., None] * bim + z_im[..., None] * br
    ngb = SSM_GROUPS // S5_GB
    eye = jnp.eye(S5_GB, dtype=f32)

    def block_diag(w):
        wd = w[:, :, :, None, :] * eye[None, :, None, :, None]
        return wd.reshape(ngb, S5_GB * w.shape[2], S5_GB * w.shape[3])

    def pack_b(bb):
        return block_diag(jnp.swapaxes(bb.reshape(ngb, S5_GB, SSM_STATE, SSM_CH), 2, 3))

    def pack_c(cc):
        return block_diag(jnp.swapaxes(cc.reshape(ngb, S5_GB, SSM_CH, SSM_STATE), 2, 3))

    a = jnp.stack([a_re.reshape(-1), a_im.reshape(-1)])
    bcat = jnp.concatenate([pack_b(bb_re), pack_b(bb_im)], axis=2)
    ccat = jnp.concatenate([pack_c(c_re.astype(f32)), -pack_c(c_im.astype(f32))], axis=1)
    return a, bcat, ccat


def s5_scan(u4, h0_re, h0_im, a, bcat, ccat, d_skip, chained):
    nb, n_l, n_r, _ = u4.shape
    rh = h0_re.shape[1]
    ngb = SSM_GROUPS // S5_GB
    st_spec = pl.BlockSpec((1, rh, S5_ST), lambda i, j: (i, 0, j))
    return pl.pallas_call(
        functools.partial(_s5_body, chained=chained),
        grid=(nb, ngb),
        in_specs=[pl.BlockSpec((1, n_l, n_r, S5_CH), lambda i, j: (i, 0, 0, j)), st_spec, st_spec,
                  pl.BlockSpec((2, S5_ST), lambda i, j: (0, j)),
                  pl.BlockSpec((1, S5_CH, 2 * S5_ST), lambda i, j: (j, 0, 0)),
                  pl.BlockSpec((1, 2 * S5_ST, S5_CH), lambda i, j: (j, 0, 0)),
                  pl.BlockSpec((1, S5_CH), lambda i, j: (0, j))],
        out_specs=[pl.BlockSpec((1, n_l, n_r, S5_CH), lambda i, j: (i, 0, 0, j)), st_spec, st_spec],
        out_shape=[jax.ShapeDtypeStruct(u4.shape, jnp.float32),
                   jax.ShapeDtypeStruct(h0_re.shape, jnp.float32), jax.ShapeDtypeStruct(h0_re.shape, jnp.float32)],
        scratch_shapes=[pltpu.VMEM((n_l, n_r, 2 * S5_ST), jnp.float32), pltpu.VMEM((n_r + 8, 2 * S5_ST), jnp.float32)],
        compiler_params=pltpu.CompilerParams(vmem_limit_bytes=VMEM_LIMIT),
        name="s5_scan",
    )(u4, h0_re, h0_im, a, bcat, ccat, d_skip.reshape(1, SSM_WIDTH))


S5_CHUNK = 64


ROW_LANES = N_PAGED * KV_WIDTH
CHUNKS_PER_PAGE = 128 // CMP_STRIDE
CMP_HID2 = NSA_KV_HEADS * CMP_HIDDEN
CMP_PAGES = 64


def _compress_body(pt_ref, *refs, n_g, transposed):
    f32, bf16 = jnp.float32, jnp.bfloat16
    page_refs = (refs[:n_g], refs[n_g:2 * n_g])
    w1_ref, c1_ref, w2_ref, b2_ref, o_ref, carry_ref, rows_ref = refs[2 * n_g:]
    m_rows = n_g * CHUNKS_PER_PAGE
    first = pl.program_id(1) == 0
    row_id = lax.broadcasted_iota(jnp.int32, (m_rows, CMP_HID2), 0)
    outs = []
    for t in range(2):
        for k, r in enumerate(page_refs[t]):
            rows_ref[t, k * 128:(k + 1) * 128, :] = r[0].T if transposed else r[0]
        cols = [rows_ref[t, pl.ds(j, m_rows, stride=CMP_STRIDE), :] for j in range(CMP_STRIDE)]
        x = jnp.concatenate(cols, axis=1).astype(bf16)
        part = jnp.dot(x, w1_ref[t], preferred_element_type=f32)
        p0, p1 = part[:, :CMP_HID2], part[:, CMP_HID2:]
        prev = jnp.where(first, 0.0, carry_ref[t, 0:1, :])
        shifted = jnp.where(row_id == 0, prev, pltpu.roll(p0, 1, axis=0))
        carry_ref[t, 0:1, :] = p0[m_rows - 1:m_rows, :]
        h1 = (c1_ref[t] + shifted) + p1
        outs.append(jnp.dot(jax.nn.gelu(h1).astype(bf16), w2_ref[t], preferred_element_type=f32) + b2_ref[t])
    o_ref[0] = jnp.concatenate(outs, axis=1)


def compress_pages(pool, page_table, cmp_pe, cmp_w1, cmp_b1, cmp_w2, cmp_b2, n_g, transposed):
    n_seq, n_pages = page_table.shape
    assert n_pages % n_g == 0 and pool.shape[1:] == ((ROW_LANES, 128) if transposed else (128, ROW_LANES))
    f32, bf16 = jnp.float32, jnp.bfloat16
    r = CMP_BLOCK // CMP_STRIDE
    eye = jnp.eye(NSA_KV_HEADS, dtype=f32)
    w1 = cmp_w1.astype(f32).reshape(2, r, CMP_STRIDE, HEAD_DIM, CMP_HIDDEN)
    w1 = jnp.transpose(w1, (0, 2, 3, 1, 4))[:, :, None, :, :, None, :] * eye[None, None, :, None, None, :, None]
    w1 = w1.reshape(2, CMP_STRIDE * KV_WIDTH, r * CMP_HID2).astype(bf16)
    c1 = jnp.stack([jnp.einsum('ld,ldf->f', cmp_pe[t], cmp_w1[t]) + cmp_b1[t] for t in range(2)])
    c1 = jnp.tile(c1[:, None, :], (1, 1, NSA_KV_HEADS))
    w2 = (cmp_w2.astype(f32)[:, None, :, None, :] * eye[None, :, None, :, None]).reshape(2, CMP_HID2, KV_WIDTH).astype(bf16)
    b2 = jnp.tile(cmp_b2.astype(f32)[:, None, :], (1, 1, NSA_KV_HEADS))
    m_rows = n_g * CHUNKS_PER_PAGE
    page_spec = lambda k, t: pl.BlockSpec((1, 128, KV_WIDTH), lambda b, g, pt: (
        (pt[b * n_pages + g * n_g + k], t, 0) if transposed else (pt[b * n_pages + g * n_g + k], 0, t)))
    full = lambda a: pl.BlockSpec(a.shape, lambda b, g, pt: (0,) * a.ndim)
    return pl.pallas_call(
        functools.partial(_compress_body, n_g=n_g, transposed=transposed),
        grid_spec=pltpu.PrefetchScalarGridSpec(
            num_scalar_prefetch=1,
            grid=(n_seq, n_pages // n_g),
            in_specs=[page_spec(k, t) for t in range(2) for k in range(n_g)] + [full(w1), full(c1), full(w2), full(b2)],
            out_specs=pl.BlockSpec((1, m_rows, 2 * KV_WIDTH), lambda b, g, pt: (b, g, 0)),
            scratch_shapes=[pltpu.VMEM((2, 8, CMP_HID2), f32), pltpu.VMEM((2, n_g * 128, KV_WIDTH), f32)],
        ),
        out_shape=jax.ShapeDtypeStruct((n_seq, n_pages * CHUNKS_PER_PAGE, 2 * KV_WIDTH), f32),
        compiler_params=pltpu.CompilerParams(vmem_limit_bytes=VMEM_LIMIT),
        name="compress_pages",
    )(page_table.reshape(-1).astype(jnp.int32), *([pool] * (2 * n_g)), w1, c1, w2, b2)


SD_PAGES = 64
SD_KEYS = SD_PAGES * 128
SD_SUB = 16
SD_ROWS = NSA_HEADS * 8
WIN_PAD = 640


def _masked_softmax(s, ok):
    s = jnp.where(ok, s, NEG_INF)
    e = jnp.where(ok, jnp.exp(s - jnp.max(s, axis=-1, keepdims=True)), 0.0)
    return e / jnp.maximum(jnp.sum(e, axis=-1, keepdims=True), 1e-30)


def _rows_from_group(a):
    t = a.shape[0] // NSA_KV_HEADS
    a4 = jnp.broadcast_to(a.reshape(NSA_KV_HEADS, 1, t, a.shape[1]), (NSA_KV_HEADS, GRP, t, a.shape[1]))
    return a4.reshape(NSA_KV_HEADS * GRP * t, a.shape[1])


def _nsa_decode_body(pt_ref, *refs, pos0, n_cmp, n_sel, n_new, win_len):
    f32, bf16 = jnp.float32, jnp.bfloat16
    kt_refs, vt_refs = refs[:SD_PAGES], refs[SD_PAGES:2 * SD_PAGES]
    (q_ref, g_ref, kcvc_ref, knew_ref, wh_ref, bc_ref, bs_ref, bn_ref, bw_ref, cov_ref, exp_ref,
     o_ref, sel_ref, oc_ref, m_ref, l_ref, acc_ref, mask_ref) = refs[2 * SD_PAGES:]
    tile, n_tiles = pl.program_id(1), pl.num_programs(1)
    nt_dims = (((1,), (1,)), ((), ()))
    nn_dims = (((1,), (0,)), ((), ()))
    n_tok = SD_ROWS // NSA_HEADS
    q = (q_ref[0] * (HEAD_DIM ** -0.5)).astype(bf16)
    qpos = pos0 + lax.broadcasted_iota(jnp.int32, (SD_ROWS, 1), 0) % n_tok

    @pl.when(tile == 0)
    def _():
        n_c = kcvc_ref.shape[1]
        m_idx = lax.broadcasted_iota(jnp.int32, (SD_ROWS, n_c), 1)
        ok_c = (m_idx >= 1) & (m_idx <= n_cmp) & (qpos >= CMP_STRIDE * m_idx + (CMP_BLOCK - 1 - CMP_STRIDE))
        s_c = lax.dot_general(q, kcvc_ref[0, :, :KV_WIDTH].astype(bf16), nt_dims, preferred_element_type=f32) + bc_ref[...]
        p_c = _masked_softmax(s_c, ok_c)
        oc_ref[...] = jnp.dot(p_c.astype(bf16), kcvc_ref[0, :, KV_WIDTH:].astype(bf16), preferred_element_type=f32)
        p_sum = jnp.sum(p_c.reshape(NSA_KV_HEADS, GRP, n_tok, n_c), axis=1).reshape(NSA_KV_HEADS * n_tok, n_c)
        imp = jnp.dot(p_sum.astype(bf16), cov_ref[...], preferred_element_type=f32)
        n_l = imp.shape[1]
        lane = lax.broadcasted_iota(jnp.int32, (NSA_KV_HEADS * n_tok, n_l), 1)
        qp = pos0 + lax.broadcasted_iota(jnp.int32, (NSA_KV_HEADS * n_tok, n_l), 0) % n_tok
        cur = qp // SEL_BLOCK
        forced = (lane == 0) | (lane == cur) | (lane == cur - 1)
        score = jnp.where(forced, FORCE_SCORE, jnp.where(lane * SEL_BLOCK <= qp, imp, -1.0))
        score = jnp.where(lane < n_sel, score, -jnp.inf)
        sel = jnp.zeros(score.shape, f32)
        for _ in range(min(SEL_TOP_N, n_sel)):
            best = jnp.max(score, axis=-1, keepdims=True)
            lane_f = lane.astype(f32)
            hit = lane_f == jnp.min(jnp.where(score == best, lane_f, float(n_l)), axis=-1, keepdims=True)
            sel = jnp.where(hit, 1.0, sel)
            score = jnp.where(hit, -jnp.inf, score)
        sel_ref[...] = sel
        m_ref[...] = jnp.full(m_ref.shape, NEG_INF, f32)
        l_ref[...] = jnp.zeros(l_ref.shape, f32)
        acc_ref[...] = jnp.zeros(acc_ref.shape, f32)

    def online(s, ok, v, v_dims):
        m_old = m_ref[...]
        s = jnp.where(ok, s, NEG_INF)
        m_new = jnp.maximum(m_old, jnp.max(s, axis=-1, keepdims=True))
        p = jnp.where(ok, jnp.exp(s - m_new), 0.0)
        alpha = jnp.exp(m_old - m_new)
        l_ref[...] = alpha * l_ref[...] + jnp.sum(p, axis=-1, keepdims=True)
        acc_ref[...] = alpha * acc_ref[...] + lax.dot_general(p.astype(bf16), v, v_dims, preferred_element_type=f32)
        m_ref[...] = m_new

    sel_tile = sel_ref[:, pl.ds(pl.multiple_of(tile * LANES, LANES), LANES)]
    mask_ref[...] = jnp.dot(sel_tile.astype(bf16), exp_ref[...], preferred_element_type=f32)
    sub_keys = SD_SUB * 128
    for sub in range(SD_PAGES // SD_SUB):
        pages = slice(sub * SD_SUB, (sub + 1) * SD_SUB)
        kt = jnp.concatenate([r[0] for r in kt_refs[pages]], axis=1).astype(bf16)
        vt = jnp.concatenate([r[0] for r in vt_refs[pages]], axis=1).astype(bf16)
        s = jnp.dot(q, kt, preferred_element_type=f32) + bs_ref[:, sub * sub_keys:(sub + 1) * sub_keys]
        ok = _rows_from_group(mask_ref[:, sub * sub_keys:(sub + 1) * sub_keys]) > 0.5
        online(s, ok, vt, nt_dims)

    @pl.when(tile == n_tiles - 1)
    def _():
        lane = lax.broadcasted_iota(jnp.int32, (SD_ROWS, LANES), 1)
        new_blk = pos0 // SEL_BLOCK
        sel_new = _rows_from_group(sel_ref[:, new_blk:new_blk + 1]) > 0.5
        ok_n = sel_new & (lane < n_new) & (pos0 + lane <= qpos)
        s_n = lax.dot_general(q, knew_ref[0, :, :KV_WIDTH].astype(bf16), nt_dims, preferred_element_type=f32) + bn_ref[...]
        online(s_n, ok_n, knew_ref[0, :, KV_WIDTH:].astype(bf16), nn_dims)
        o_s = acc_ref[...] / jnp.maximum(l_ref[...], 1e-30)
        j = lax.broadcasted_iota(jnp.int32, (SD_ROWS, WIN_PAD), 1)
        kwpos = pos0 - win_len + j
        dist = qpos - kwpos
        ok_w = (dist >= 0) & (dist < WINDOW) & (kwpos >= 0) & (j < win_len + n_new)
        s_w = lax.dot_general(q, wh_ref[0, :, :KV_WIDTH].astype(bf16), nt_dims, preferred_element_type=f32) + bw_ref[...]
        o_w = jnp.dot(_masked_softmax(s_w, ok_w).astype(bf16), wh_ref[0, :, KV_WIDTH:].astype(bf16), preferred_element_type=f32)
        g = jax.nn.sigmoid(g_ref[0])
        o = g[:, 0:1] * oc_ref[...] + g[:, 1:2] * o_s + g[:, 2:3] * o_w
        row = lax.broadcasted_iota(jnp.int32, (SD_ROWS, HEAD_DIM), 0)
        o_ref[0] = jnp.where(row < SD_ROWS // NSA_KV_HEADS, o[:, :HEAD_DIM], o[:, HEAD_DIM:])


def nsa_decode(q, glog, kv_new, kcvc, pool_t, page_table, cache_win, thr, rel_bias, pos0):
    f32, bf16 = jnp.float32, jnp.bfloat16
    b, t = q.shape[:2]
    n_pages = page_table.shape[1]
    win_len = cache_win.shape[1]
    assert t * NSA_HEADS == SD_ROWS and pos0 == n_pages * 128 and n_pages % SD_PAGES == 0 and pos0 % SEL_BLOCK == 0
    assert win_len + t <= WIN_PAD and t <= SEL_BLOCK and SD_KEYS == LANES * SEL_BLOCK
    n_rows = pos0 + t
    n_cmp = (n_rows - CMP_BLOCK) // CMP_STRIDE + 1
    n_sel = -(-n_rows // SEL_BLOCK)
    n_c = kcvc.shape[1]
    sel_lanes = -(-n_sel // LANES) * LANES
    q5 = jnp.transpose(q.reshape(b, t, NSA_KV_HEADS, GRP, HEAD_DIM), (0, 2, 3, 1, 4))
    qz = (q5[:, :, :, :, None, :] * jnp.eye(NSA_KV_HEADS, dtype=f32)[None, :, None, None, :, None]).reshape(b, SD_ROWS, KV_WIDTH)
    g3 = jnp.transpose(glog.reshape(b, t, NSA_HEADS, 3), (0, 2, 1, 3)).reshape(b, SD_ROWS, 3)
    knew = jnp.pad(kv_new[:, :, 2:4].reshape(b, t, 2 * KV_WIDTH), ((0, 0), (0, LANES - t), (0, 0)))
    whist = jnp.concatenate([cache_win.reshape(b, win_len, 2 * KV_WIDTH), kv_new[:, :, 4:6].reshape(b, t, 2 * KV_WIDTH),
                             jnp.zeros((b, WIN_PAD - win_len - t, 2 * KV_WIDTH), f32)], axis=1)
    qp = pos0 + jnp.arange(t, dtype=jnp.int32)[:, None]
    tab = lambda dist: bias_tables(dist[None], thr, rel_bias).reshape(SD_ROWS, dist.shape[1])
    bias_c = tab(qp - (CMP_STRIDE * jnp.arange(n_c, dtype=jnp.int32)[None, :] + CMP_BLOCK - 1 - CMP_STRIDE))
    bias_s = tab(qp - jnp.arange(pos0, dtype=jnp.int32)[None, :])
    bias_n = tab(qp - (pos0 + jnp.arange(LANES, dtype=jnp.int32)[None, :]))
    bias_w = tab(qp - (pos0 - win_len + jnp.arange(WIN_PAD, dtype=jnp.int32)[None, :]))
    m_idx = jnp.arange(n_c)[:, None]
    c_start = CMP_STRIDE * (m_idx - 1)
    s_start = SEL_BLOCK * jnp.arange(sel_lanes)[None, :]
    cover = ((c_start < s_start + SEL_BLOCK) & (c_start + CMP_BLOCK > s_start) & (m_idx >= 1) & (m_idx <= n_cmp)
             & (jnp.arange(sel_lanes)[None, :] < n_sel)).astype(bf16)
    expand = (jnp.arange(SD_KEYS)[None, :] // SEL_BLOCK == jnp.arange(LANES)[:, None]).astype(bf16)
    page_spec = lambda k, blk: pl.BlockSpec((1, KV_WIDTH, 128), lambda bb, g, pt: (pt[bb * n_pages + g * SD_PAGES + k], blk, 0))
    per_seq = lambda a: pl.BlockSpec((1,) + a.shape[1:], lambda bb, g, pt: (bb,) + (0,) * (a.ndim - 1))
    full = lambda a: pl.BlockSpec(a.shape, lambda bb, g, pt: (0,) * a.ndim)
    kv_rows = NSA_KV_HEADS * t
    out = pl.pallas_call(
        functools.partial(_nsa_decode_body, pos0=pos0, n_cmp=n_cmp, n_sel=n_sel, n_new=t, win_len=win_len),
        grid_spec=pltpu.PrefetchScalarGridSpec(
            num_scalar_prefetch=1,
            grid=(b, n_pages // SD_PAGES),
            in_specs=[page_spec(k, blk) for blk in (2, 3) for k in range(SD_PAGES)]
            + [per_seq(qz), per_seq(g3), per_seq(kcvc), per_seq(knew), per_seq(whist), full(bias_c),
               pl.BlockSpec((SD_ROWS, SD_KEYS), lambda bb, g, pt: (0, g)), full(bias_n), full(bias_w), full(cover), full(expand)],
            out_specs=pl.BlockSpec((1, SD_ROWS, HEAD_DIM), lambda bb, g, pt: (bb, 0, 0)),
            scratch_shapes=[pltpu.VMEM((kv_rows, sel_lanes), f32), pltpu.VMEM((SD_ROWS, KV_WIDTH), f32),
                            pltpu.VMEM((SD_ROWS, 1), f32), pltpu.VMEM((SD_ROWS, 1), f32), pltpu.VMEM((SD_ROWS, KV_WIDTH), f32),
                            pltpu.VMEM((kv_rows, SD_KEYS), f32)],
        ),
        out_shape=jax.ShapeDtypeStruct((b, SD_ROWS, HEAD_DIM), f32),
        compiler_params=pltpu.CompilerParams(vmem_limit_bytes=VMEM_LIMIT),
        name="nsa_decode",
    )(page_table.reshape(-1).astype(jnp.int32), *([pool_t] * (2 * SD_PAGES)), qz, g3, kcvc, knew, whist,
      bias_c, bias_s, bias_n, bias_w, cover, expand)
    out = jnp.transpose(out.reshape(b, NSA_KV_HEADS, GRP, t, HEAD_DIM), (0, 3, 1, 2, 4))
    return out.reshape(b, t, NSA_WIDTH)


MOE_ROWS = 1280
MOE_SUB = 64
MOE_PATHS = (256, 1024, 1088, 1152, 1216, MOE_ROWS)
MOE_CHUNK = 256
MOE_TF = 256
MOE_COLS = 256


def _moe_body(e_ref, sub_ref, blk_ref, ok_ref, x_ref, wg_ref, wl_ref, bg_ref, bl_ref, wd_ref, bd_ref, o_ref, xs_ref):
    bf16 = jnp.bfloat16
    i, f = pl.program_id(0), pl.program_id(1)
    n_sub = sub_ref[i]
    d = o_ref.shape[1]

    @pl.when(f == 0)
    def _():
        o_ref[...] = jnp.broadcast_to(bd_ref[0], o_ref.shape)

    @pl.when((f == 0) & (n_sub > 0))
    def _():
        for j in range(MOE_ROWS // MOE_CHUNK):
            rows = slice(j * MOE_CHUNK, (j + 1) * MOE_CHUNK)
            w = lax.bitcast_convert_type(x_ref[rows, :], jnp.uint32)
            hi = lax.bitcast_convert_type(w & jnp.uint32(0xFFFF0000), jnp.float32)
            lo = lax.bitcast_convert_type(w << 16, jnp.float32)
            xs_ref[rows, :] = jnp.concatenate([hi, lo], axis=1).astype(bf16)

    def expert_rows(n_rows):
        x = xs_ref[0:n_rows, :]
        hg = jnp.dot(x, wg_ref[0].astype(bf16), preferred_element_type=jnp.float32) + bg_ref[0]
        hl = jnp.dot(x, wl_ref[0].astype(bf16), preferred_element_type=jnp.float32) + bl_ref[0]
        hg = jnp.minimum(hg, SWIGLU_LIMIT)
        hl = jnp.clip(hl, -SWIGLU_LIMIT, SWIGLU_LIMIT)
        act = (hg * jax.nn.sigmoid(SWIGLU_ALPHA * hg) * (hl + 1.0)).astype(bf16)
        for c in range(d // MOE_COLS):
            cols = slice(c * MOE_COLS, (c + 1) * MOE_COLS)
            o_ref[0:n_rows, cols] += jnp.dot(act, wd_ref[0, :, cols].astype(bf16), preferred_element_type=jnp.float32)

    below = 0
    for path_rows in MOE_PATHS:
        @pl.when((n_sub * MOE_SUB > below) & (n_sub * MOE_SUB <= path_rows))
        def _(path_rows=path_rows):
            expert_rows(path_rows)

        below = path_rows


def moe_experts(item_e, item_sub, item_blk, item_ok, xb, w_gu, b_gu, w_down, b_down):
    n_items = item_e.shape[0]
    n_f = D_FF // MOE_TF
    d = 2 * xb.shape[1]

    def col(f, ok, i):
        return f * ok[i] + (n_f - 1) * (1 - ok[i])

    return pl.pallas_call(
        _moe_body,
        grid_spec=pltpu.PrefetchScalarGridSpec(
            num_scalar_prefetch=4,
            grid=(n_items, n_f),
            in_specs=[pl.BlockSpec((MOE_ROWS, d // 2), lambda i, f, e, s, b, ok: (b[i], 0)),
                      pl.BlockSpec((1, d, MOE_TF), lambda i, f, e, s, b, ok: (e[i], 0, col(f, ok, i))),
                      pl.BlockSpec((1, d, MOE_TF), lambda i, f, e, s, b, ok: (e[i], 0, n_f + col(f, ok, i))),
                      pl.BlockSpec((1, 1, MOE_TF), lambda i, f, e, s, b, ok: (e[i], 0, col(f, ok, i))),
                      pl.BlockSpec((1, 1, MOE_TF), lambda i, f, e, s, b, ok: (e[i], 0, n_f + col(f, ok, i))),
                      pl.BlockSpec((1, MOE_TF, d), lambda i, f, e, s, b, ok: (e[i], col(f, ok, i), 0)),
                      pl.BlockSpec((1, 1, d), lambda i, f, e, s, b, ok: (e[i], 0, 0))],
            out_specs=pl.BlockSpec((MOE_ROWS, d), lambda i, f, e, s, b, ok: (i, 0)),
            scratch_shapes=[pltpu.VMEM((MOE_ROWS, d), jnp.bfloat16)],
        ),
        out_shape=jax.ShapeDtypeStruct((n_items * MOE_ROWS, d), jnp.float32),
        compiler_params=pltpu.CompilerParams(vmem_limit_bytes=VMEM_LIMIT),
        name="moe_experts",
    )(item_e, item_sub, item_blk, item_ok, xb, w_gu, w_gu, b_gu[:, None, :], b_gu[:, None, :], w_down, b_down[:, None, :])


ROW_TILE = 256
PROJ_TILE = 512
PROJ_ROWS = 512


def _rms(v, g):
    return v * lax.rsqrt(jnp.mean(v * v, axis=-1, keepdims=True) + RMS_EPS) * g


def _norm_proj_body(x_ref, g_ref, w_ref, o_ref, h_ref):
    @pl.when(pl.program_id(1) == 0)
    def _():
        h_ref[...] = _rms(x_ref[...], g_ref[...]).astype(jnp.bfloat16)

    o_ref[...] = jnp.dot(h_ref[...], w_ref[...], preferred_element_type=jnp.float32)


def norm_proj(x, gain, w):
    n, d = x.shape
    cols = w.shape[1]
    tm = min(n, PROJ_ROWS)
    return pl.pallas_call(
        _norm_proj_body,
        grid=(n // tm, cols // PROJ_TILE),
        in_specs=[pl.BlockSpec((tm, d), lambda i, j: (i, 0)),
                  pl.BlockSpec((1, d), lambda i, j: (0, 0)),
                  pl.BlockSpec((d, PROJ_TILE), lambda i, j: (0, j))],
        out_specs=pl.BlockSpec((tm, PROJ_TILE), lambda i, j: (i, j)),
        out_shape=jax.ShapeDtypeStruct((n, cols), jnp.float32),
        scratch_shapes=[pltpu.VMEM((tm, d), jnp.bfloat16)],
        compiler_params=pltpu.CompilerParams(vmem_limit_bytes=VMEM_LIMIT),
        name="norm_proj",
    )(x, gain.reshape(1, d), w)


def _mix_out_body(attn_ref, y_ref, x_ref, wglu_ref, bglu_ref, ga_ref, gs_ref, wout_ref, gf_ref, wr_ref,
                  x2_ref, hp_ref, lg_ref):
    f32, bf16 = jnp.float32, jnp.bfloat16
    y = y_ref[...]
    ssm = y * jax.nn.sigmoid(jnp.dot(y.astype(bf16), wglu_ref[...], preferred_element_type=f32) + bglu_ref[...])
    merged = jnp.concatenate([_rms(attn_ref[...], ga_ref[...]), _rms(ssm, gs_ref[...])], axis=1).astype(bf16)
    x2 = x_ref[...] + jnp.dot(merged, wout_ref[...], preferred_element_type=f32)
    x2_ref[...] = x2
    hm = _rms(x2, gf_ref[...]).astype(bf16)
    lg_ref[...] = jnp.dot(hm, wr_ref[...], preferred_element_type=f32)
    bits = lax.bitcast_convert_type(hm.astype(f32), jnp.uint32)
    half = bits.shape[1] // 2
    hp_ref[...] = lax.bitcast_convert_type(bits[:, :half] | (bits[:, half:] >> 16), f32)


def mix_out(attn, y, x, lw):
    n = x.shape[0]
    f32, bf16 = jnp.float32, jnp.bfloat16
    row = lambda a: a.astype(f32).reshape(1, -1)
    wr = jnp.pad(lw['router_w'], ((0, 0), (0, LANES - N_EXPERTS))).astype(bf16)
    consts = [lw['w_glu'].astype(bf16), row(lw['b_glu']), row(lw['norm_attn_out']), row(lw['norm_ssm_out']),
              lw['w_out'].astype(bf16), row(lw['norm_ffn']), wr]
    tile = lambda c: pl.BlockSpec((ROW_TILE, c), lambda i: (i, 0))
    full = lambda a: pl.BlockSpec(a.shape, lambda i: (0, 0))
    return pl.pallas_call(
        _mix_out_body,
        grid=(n // ROW_TILE,),
        in_specs=[tile(NSA_WIDTH), tile(SSM_WIDTH), tile(D_MODEL)] + [full(c) for c in consts],
        out_specs=[tile(D_MODEL), tile(D_MODEL // 2), tile(LANES)],
        out_shape=[jax.ShapeDtypeStruct((n, D_MODEL), f32), jax.ShapeDtypeStruct((n, D_MODEL // 2), f32),
                   jax.ShapeDtypeStruct((n, LANES), f32)],
        compiler_params=pltpu.CompilerParams(vmem_limit_bytes=VMEM_LIMIT),
        name="mix_out",
    )(attn, y, x, *consts)


def _combine_body(x_ref, y0_ref, y1_ref, y2_ref, y3_ref, gate_ref, g_ref, o_ref, *, normed):
    gate = gate_ref[...]
    moe = gate[:, 0:1] * y0_ref[...]
    for k, y_ref in enumerate((y1_ref, y2_ref, y3_ref), start=1):
        moe = moe + gate[:, k:k + 1] * y_ref[...]
    out = x_ref[...] + moe
    o_ref[...] = _rms(out, g_ref[...]) if normed else out


def combine_norm(x, ys, gate, gain):
    n, d = x.shape
    normed = gain is not None
    gain = jnp.ones((d,), jnp.float32) if gain is None else gain
    tile = lambda c: pl.BlockSpec((ROW_TILE, c), lambda i: (i, 0))
    return pl.pallas_call(
        functools.partial(_combine_body, normed=normed),
        grid=(n // ROW_TILE,),
        in_specs=[tile(d)] * (1 + TOP_K) + [tile(TOP_K), pl.BlockSpec((1, d), lambda i: (0, 0))],
        out_specs=tile(d),
        out_shape=jax.ShapeDtypeStruct((n, d), jnp.float32),
        name="combine_norm",
    )(x, *ys, gate, gain.astype(jnp.float32).reshape(1, d))


def moe_routed(hp, logits, router_b, w_gu, b_gu, w_down, b_down):
    n = hp.shape[0]
    top_val, top_idx = lax.top_k(logits[:, :N_EXPERTS] + router_b.astype(jnp.float32), TOP_K)
    gate = jax.nn.softmax(top_val, axis=-1)
    nk = n * TOP_K
    n_items = N_EXPERTS + nk // MOE_ROWS
    flat_e = top_idx.reshape(nk)
    onehot = (flat_e[:, None] == jnp.arange(N_EXPERTS, dtype=flat_e.dtype)[None, :]).astype(jnp.int32)
    running = jnp.cumsum(onehot, axis=0)
    counts = running[-1]
    pos_in_e = jnp.sum(onehot * running, axis=1) - 1
    items_e = (counts + MOE_ROWS - 1) // MOE_ROWS
    item_end = jnp.cumsum(items_e)
    item_start = item_end - items_e
    total = item_end[-1]
    dest = (item_start[flat_e] * MOE_ROWS + pos_in_e).astype(jnp.int32)
    row_tok = jnp.zeros(n_items * MOE_ROWS, jnp.int32).at[dest].set(jnp.arange(nk, dtype=jnp.int32) // TOP_K)
    item = jnp.minimum(jnp.arange(n_items, dtype=jnp.int32), total - 1)
    item_e = jnp.minimum(jnp.searchsorted(item_end, item, side='right'), N_EXPERTS - 1).astype(jnp.int32)
    rows = jnp.clip(counts[item_e] - (item - item_start[item_e]) * MOE_ROWS, 0, MOE_ROWS)
    item_sub = jnp.where(jnp.arange(n_items) < total, (rows + MOE_SUB - 1) // MOE_SUB, 0).astype(jnp.int32)
    item_ok = (jnp.arange(n_items) < total).astype(jnp.int32)
    yb = moe_experts(item_e, item_sub, item.astype(jnp.int32), item_ok, hp[row_tok], w_gu, b_gu, w_down, b_down)
    return yb, dest.reshape(n, TOP_K), gate


def mixer_layer(x, pos0, pool, page_table, win_buf, h0_re, h0_im, rel_bias, lw):
    b, t, _ = x.shape
    w_in = jnp.pad(lw['w_in'], ((0, 0), (0, -IN_WIDTH % PROJ_TILE))).astype(jnp.bfloat16)
    proj = norm_proj(x.reshape(b * t, D_MODEL), lw['norm_mix'].astype(jnp.float32), w_in).reshape(b, t, -1)
    o1 = NSA_WIDTH
    o2 = o1 + N_KV * KV_WIDTH
    o3 = o2 + 3 * NSA_HEADS
    q = proj[..., :o1].reshape(b, t, NSA_HEADS, HEAD_DIM)
    kv_new = proj[..., o1:o2].reshape(b, t, N_KV, NSA_KV_HEADS, HEAD_DIM)
    u = proj[..., o3:IN_WIDTH]
    paged_new = kv_new[:, :, :N_PAGED]
    win_new = kv_new[:, :, N_PAGED:]
    n_rows = pos0 + t
    n_cmp = (n_rows - CMP_BLOCK) // CMP_STRIDE + 1
    thr = bucket_thresholds()
    cmp_w = (lw['cmp_pe'], lw['cmp_w1'], lw['cmp_b1'], lw['cmp_w2'], lw['cmp_b2'])
    if pool is None:
        assert pos0 == 0 and t % 128 == 0
        pages = t // 128
        own_pool = proj[..., o1:o1 + ROW_LANES].reshape(b * pages, 128, ROW_LANES)
        own_table = jnp.arange(b * pages, dtype=jnp.int32).reshape(b, pages)
        kcvc = compress_pages(own_pool, own_table, *cmp_w, n_g=pages, transposed=False)
        attn = nsa_prefill(q, proj[..., o2:o3], kcvc, kv_new, thr, rel_bias, n_cmp).astype(x.dtype)
        w_hist = win_new
    else:
        kcvc = compress_pages(pool, page_table, *cmp_w, n_g=CMP_PAGES, transposed=True)
        attn = nsa_decode(q, proj[..., o2:o3], kv_new, kcvc, pool, page_table, win_buf, thr, rel_bias, pos0).astype(x.dtype)
        w_hist = jnp.concatenate([win_buf, win_new], axis=1)
    new_win = w_hist[:, w_hist.shape[1] - min(WINDOW, n_rows):]
    y, h_re, h_im = s5_branch(u, h0_re, h0_im, lw, pool is None)
    x2, hp, logits = mix_out(attn.reshape(b * t, NSA_WIDTH), y.reshape(b * t, SSM_WIDTH), x.reshape(b * t, D_MODEL), lw)
    return (x2, hp, logits), paged_new, new_win, h_re, h_im


def s5_branch(u, h0_re, h0_im, lw, chained):
    bsz, t, _ = u.shape
    f32 = jnp.float32
    a, bcat, ccat = s5_discretize(lw['lam_re'], lw['lam_im'], lw['log_dt'], lw['b_re'], lw['b_im'], lw['c_re'], lw['c_im'])
    d_skip = lw['d_skip'].astype(f32)
    flat = lambda h: h.astype(f32).reshape(bsz, SSM_GROUPS * SSM_STATE)
    if chained:
        n_r = t // S5_CHUNK
        u4 = jnp.transpose(u.astype(f32).reshape(bsz, n_r, S5_CHUNK, SSM_WIDTH), (0, 2, 1, 3))
        y4, h_re, h_im = s5_scan(u4, flat(h0_re)[:, None], flat(h0_im)[:, None], a, bcat, ccat, d_skip, True)
        y = jnp.transpose(y4, (0, 2, 1, 3)).reshape(bsz, t, SSM_WIDTH)
        h_re, h_im = h_re[:, 0], h_im[:, 0]
    else:
        u4 = jnp.transpose(u.astype(f32), (1, 0, 2))[None]
        y4, h_re, h_im = s5_scan(u4, flat(h0_re)[None], flat(h0_im)[None], a, bcat, ccat, d_skip, False)
        y = jnp.transpose(y4[0], (1, 0, 2))
        h_re, h_im = h_re[0], h_im[0]
    st = lambda h: h.reshape(bsz, SSM_GROUPS, SSM_STATE).astype(u.dtype)
    return y, st(h_re), st(h_im)


def moe_residual(groups, lw, final_gain):
    sizes = [g[0].shape[0] for g in groups]
    hp = lax.optimization_barrier(jnp.concatenate([g[1] for g in groups], axis=0))
    logits = jnp.concatenate([g[2] for g in groups], axis=0)
    yb, dest, gate = moe_routed(hp, logits, lw['router_b'], lw['w_gu'], lw['b_gu'], lw['w_down'], lw['b_down'])
    outs, start = [], 0
    for (x2, _, _), n in zip(groups, sizes):
        rows = slice(start, start + n)
        outs.append(combine_norm(x2, [yb[dest[rows, k]] for k in range(TOP_K)], gate[rows], final_gain))
        start += n
    return outs


def kernel(x_prompt, x_sample, cache_nsa_kv, cache_win_kv, state_ssm_re, state_ssm_im, page_table, rel_bias,
           norm_mix, w_in, cmp_pe, cmp_w1, cmp_b1, cmp_w2, cmp_b2, ssm_lam_re, ssm_lam_im, ssm_log_dt,
           ssm_b_re, ssm_b_im, ssm_c_re, ssm_c_im, ssm_d, ssm_w_glu, ssm_b_glu, norm_attn_out, norm_ssm_out,
           w_out, norm_ffn, router_w, router_b, w_gu, b_gu, w_down, b_down, norm_final):
    n_seq, n_pages = page_table.shape
    past_len = n_pages * cache_nsa_kv.shape[2]
    xp, xs = x_prompt, x_sample
    kv_p, win_p, sre_p, sim_p = [], [], [], []
    kv_s, win_s, sre_s, sim_s = [], [], [], []
    for i in range(DEPTH):
        lw = dict(norm_mix=norm_mix[i], w_in=w_in[i], cmp_pe=cmp_pe[i], cmp_w1=cmp_w1[i], cmp_b1=cmp_b1[i],
                  cmp_w2=cmp_w2[i], cmp_b2=cmp_b2[i], lam_re=ssm_lam_re[i], lam_im=ssm_lam_im[i],
                  log_dt=ssm_log_dt[i], b_re=ssm_b_re[i], b_im=ssm_b_im[i], c_re=ssm_c_re[i], c_im=ssm_c_im[i],
                  d_skip=ssm_d[i], w_glu=ssm_w_glu[i], b_glu=ssm_b_glu[i], norm_attn_out=norm_attn_out[i],
                  norm_ssm_out=norm_ssm_out[i], w_out=w_out[i], norm_ffn=norm_ffn[i], router_w=router_w[i],
                  router_b=router_b[i], w_gu=w_gu[i], b_gu=b_gu[i], w_down=w_down[i], b_down=b_down[i])
        h0 = jnp.zeros((xp.shape[0], SSM_GROUPS, SSM_STATE), xp.dtype)
        gp, kv1, w1, r1, m1 = mixer_layer(xp, 0, None, None, None, h0, h0, rel_bias, lw)
        pool = jnp.transpose(cache_nsa_kv[i].reshape(cache_nsa_kv.shape[1], cache_nsa_kv.shape[2], ROW_LANES), (0, 2, 1))
        gs, kv2, w2, r2, m2 = mixer_layer(xs, past_len, pool, page_table, cache_win_kv[i], state_ssm_re[i], state_ssm_im[i], rel_bias, lw)
        op, os_ = moe_residual([gp, gs], lw, norm_final if i == DEPTH - 1 else None)
        xp, xs = op.reshape(xp.shape), os_.reshape(xs.shape)
        kv_p.append(kv1); win_p.append(w1); sre_p.append(r1); sim_p.append(m1)
        kv_s.append(kv2); win_s.append(w2); sre_s.append(r2); sim_s.append(m2)
    return (xp, xs, jnp.stack(kv_p), jnp.stack(win_p), jnp.stack(sre_p), jnp.stack(sim_p),
            jnp.stack(kv_s), jnp.stack(win_s), jnp.stack(sre_s), jnp.stack(sim_s))
```

```python
import functools
import math
import jax, jax.numpy as jnp
from jax import lax
from jax.experimental import pallas as pl
from jax.experimental.pallas import tpu as pltpu

D_MODEL = 2048
DEPTH = 1
NSA_HEADS = 16
NSA_KV_HEADS = 2
HEAD_DIM = 64
NSA_WIDTH = NSA_HEADS * HEAD_DIM
KV_WIDTH = NSA_KV_HEADS * HEAD_DIM
N_PAGED = 4
N_KV = 6
SSM_WIDTH = D_MODEL - NSA_WIDTH
SSM_CH = 16
SSM_GROUPS = SSM_WIDTH // SSM_CH
SSM_STATE = 64
IN_WIDTH = NSA_WIDTH + N_KV * KV_WIDTH + 3 * NSA_HEADS + SSM_WIDTH
CMP_BLOCK = 32
CMP_STRIDE = 16
CMP_HIDDEN = 2 * HEAD_DIM
SEL_BLOCK = 64
SEL_TOP_N = 16
WINDOW = 512
FORCE_SCORE = 1e6
REL_BUCKETS = 32
REL_MAX_DIST = 4096
N_EXPERTS = 32
TOP_K = 4
D_FF = D_MODEL
SWIGLU_LIMIT = 7.0
SWIGLU_ALPHA = 1.702
RMS_EPS = 1e-5
NEG_INF = -1e30


GRP = NSA_HEADS // NSA_KV_HEADS
TQ = 128
TK = 128
SEL_TILES = 4
WIN_TILES = 2
LANES = 128
BUCKET_TABLE_LEN = 32768
BIAS_TILE_ELEMS = 16384
VMEM_LIMIT = 56 * 1024 * 1024


def rel_bucket(dist):
    n = jnp.maximum(dist, 0)
    exact = REL_BUCKETS // 2
    nf = jnp.maximum(n, 1).astype(jnp.float32)
    large = exact + (jnp.log(nf / exact) / math.log(REL_MAX_DIST / exact) * (REL_BUCKETS - exact)).astype(jnp.int32)
    return jnp.where(n < exact, n, jnp.minimum(large, REL_BUCKETS - 1))


def bucket_thresholds():
    tab = rel_bucket(jnp.arange(BUCKET_TABLE_LEN, dtype=jnp.int32))
    return jnp.sum(tab[None, :] < jnp.arange(REL_BUCKETS, dtype=jnp.int32)[:, None], axis=1).astype(jnp.int32)


def _bias_table_body(thr_ref, rb_ref, d_ref, o_ref):
    n = jnp.maximum(d_ref[0], 0)
    for h in range(NSA_HEADS):
        val = jnp.full(n.shape, rb_ref[h], jnp.float32)
        for k in range(1, REL_BUCKETS):
            val = jnp.where(n >= thr_ref[k], rb_ref[k * NSA_HEADS + h], val)
        o_ref[h // GRP, 0, h % GRP] = val


def bias_tables(dist, thr, rel_bias):
    n, r, c = dist.shape
    ct = min(c, BIAS_TILE_ELEMS // r)
    assert c % ct == 0
    return pl.pallas_call(
        _bias_table_body,
        grid_spec=pltpu.PrefetchScalarGridSpec(
            num_scalar_prefetch=2,
            grid=(n, c // ct),
            in_specs=[pl.BlockSpec((1, r, ct), lambda i, j, *_: (i, 0, j))],
            out_specs=pl.BlockSpec((NSA_KV_HEADS, 1, GRP, r, ct), lambda i, j, *_: (0, i, 0, 0, j)),
        ),
        out_shape=jax.ShapeDtypeStruct((NSA_KV_HEADS, n, GRP, r, c), jnp.float32),
        name="bias_tables",
    )(thr, rel_bias.reshape(-1), dist)


def _nsa_prefill_body(q_ref, g_ref, kc_ref, vct_ref, ks_ref, vst_ref, kw_ref, vwt_ref, bt_ref, bc_ref, cov_ref, exp_ref,
                      o_ref, mask_ref, *, n_cmp, n_sel):
    f32, bf16 = jnp.float32, jnp.bfloat16
    qi = pl.program_id(2)
    nt_dims = (((1,), (1,)), ((), ()))
    q = (q_ref[0] * (HEAD_DIM ** -0.5)).reshape(GRP * TQ, HEAD_DIM).astype(bf16)
    lanes = [slice(g * TQ, (g + 1) * TQ) for g in range(GRP)]
    row = lax.broadcasted_iota(jnp.int32, (LANES, TQ), 0)
    qpos = qi * TQ + lax.broadcasted_iota(jnp.int32, (LANES, TQ), 1)

    ok_c = (qpos >= CMP_STRIDE * row + (CMP_BLOCK - 1)) & (row < n_cmp)
    s_c = lax.dot_general(kc_ref[0, 0].astype(bf16), q, nt_dims, preferred_element_type=f32)
    p_sum = jnp.zeros((LANES, TQ), f32)
    p_parts = []
    for g in range(GRP):
        p = _masked_softmax_rows(s_c[:, lanes[g]] + bc_ref[0, 0, g], ok_c)
        p_sum = p_sum + p
        p_parts.append(p.astype(bf16))
    o_c = jnp.dot(vct_ref[0, 0].astype(bf16), jnp.concatenate(p_parts, axis=1), preferred_element_type=f32)

    imp = jnp.dot(cov_ref[...], p_sum.astype(bf16), preferred_element_type=f32)
    rs = -(-n_sel // 8) * 8
    blk = lax.broadcasted_iota(jnp.int32, (rs, TQ), 0)
    qp = qi * TQ + lax.broadcasted_iota(jnp.int32, (rs, TQ), 1)
    cur = qp // SEL_BLOCK
    forced = (blk == 0) | (blk == cur) | (blk == cur - 1)
    score = jnp.where(forced, FORCE_SCORE, jnp.where(blk * SEL_BLOCK <= qp, imp[:rs], -1.0))
    score = jnp.where(blk < n_sel, score, -jnp.inf)
    sel = jnp.zeros((rs, TQ), f32)
    for s in range(n_sel):
        r = jnp.max(jnp.where(blk == s, score, -jnp.inf), axis=0, keepdims=True)
        beats = (score > r) | ((score == r) & (blk < s))
        rank = jnp.sum(jnp.where(beats, 1.0, 0.0), axis=0, keepdims=True)
        sel = jnp.where((blk == s) & (rank < min(SEL_TOP_N, n_sel)), 1.0, sel)
    sel = jnp.concatenate([sel, jnp.zeros((LANES - rs, TQ), f32)], axis=0)
    mask_ref[...] = jnp.dot(exp_ref[...], sel.astype(bf16), preferred_element_type=f32)

    init = (jnp.full((1, GRP * TQ), NEG_INF, f32), jnp.zeros((1, GRP * TQ), f32), jnp.zeros((HEAD_DIM, GRP * TQ), f32))
    n_diag = bt_ref.shape[1]

    def branch(k_ref, vt_ref, n_tiles, first_tile, keep):
        n_keys = n_tiles * TK
        key_i = lax.broadcasted_iota(jnp.int32, (n_keys, TQ), 0)
        qry_i = lax.broadcasted_iota(jnp.int32, (n_keys, TQ), 1)

        def step(kk, carry):
            m, l, acc = carry
            keys = pl.ds(pl.multiple_of(kk * n_keys, n_keys), n_keys)
            msk = keep(qi * TQ + qry_i - (kk * n_keys + key_i), keys)
            s_all = lax.dot_general(k_ref[0, 0, keys, :].astype(bf16), q, nt_dims, preferred_element_type=f32)
            diag = [jnp.clip(qi - (n_tiles * kk + j), 0, n_diag - 1) for j in range(n_tiles)]
            m_parts, l_parts, a_parts, p_parts = [], [], [], []
            for g in range(GRP):
                bias = jnp.concatenate([bt_ref[0, dg, g] for dg in diag], axis=0)
                s = jnp.where(msk, s_all[:, lanes[g]] + bias, NEG_INF)
                m_new = jnp.maximum(m[:, lanes[g]], jnp.max(s, axis=0, keepdims=True))
                p = jnp.where(msk, jnp.exp(s - m_new), 0.0)
                alpha = jnp.exp(m[:, lanes[g]] - m_new)
                m_parts.append(m_new)
                a_parts.append(alpha)
                l_parts.append(alpha * l[:, lanes[g]] + jnp.sum(p, axis=0, keepdims=True))
                p_parts.append(p.astype(bf16))
            pv = jnp.dot(vt_ref[0, 0, :, keys].astype(bf16), jnp.concatenate(p_parts, axis=1), preferred_element_type=f32)
            return (jnp.concatenate(m_parts, axis=1), jnp.concatenate(l_parts, axis=1),
                    jnp.concatenate(a_parts, axis=1) * acc + pv)

        _, l, acc = lax.fori_loop(first_tile // n_tiles, qi // n_tiles + 1, step, init)
        return acc / jnp.maximum(l, 1e-30)

    o_s = branch(ks_ref, vst_ref, SEL_TILES, 0, lambda dist, keys: (mask_ref[keys, :] > 0.5) & (dist >= 0))
    o_w = branch(kw_ref, vwt_ref, WIN_TILES, jnp.maximum(qi - WINDOW // TK, 0),
                 lambda dist, keys: (dist >= 0) & (dist < WINDOW))
    heads = []
    for g in range(GRP):
        gate = jax.nn.sigmoid(g_ref[0, 0, :, g, :])
        heads.append(gate[0:1] * o_c[:, lanes[g]] + gate[1:2] * o_s[:, lanes[g]] + gate[2:3] * o_w[:, lanes[g]])
    o_ref[0] = jnp.concatenate(heads, axis=0).T


def _masked_softmax_rows(s, ok):
    s = jnp.where(ok, s, NEG_INF)
    e = jnp.where(ok, jnp.exp(s - jnp.max(s, axis=0, keepdims=True)), 0.0)
    return e / jnp.maximum(jnp.sum(e, axis=0, keepdims=True), 1e-30)


def nsa_prefill(q, glog, kcvc, kv_new, thr, rel_bias, n_cmp):
    f32, bf16 = jnp.float32, jnp.bfloat16
    b, t = q.shape[:2]
    nq = t // TQ
    n_sel = t // SEL_BLOCK
    assert t % (SEL_TILES * TK) == 0 and t % (WIN_TILES * TK) == 0 and n_sel <= LANES and n_cmp < LANES and kcvc.shape[1] == LANES and TQ == TK == LANES
    q4 = jnp.transpose(q, (0, 2, 1, 3))
    g5 = jnp.transpose(glog.reshape(b, t, NSA_KV_HEADS, GRP, 3), (0, 2, 4, 3, 1))
    cmp_tok = jnp.pad(kcvc[:, 1:], ((0, 0), (0, 1), (0, 0))).reshape(b, LANES, 2, NSA_KV_HEADS, HEAD_DIM)
    kc = jnp.transpose(cmp_tok[:, :, 0], (0, 2, 1, 3))
    vct = jnp.transpose(cmp_tok[:, :, 1], (0, 2, 3, 1))
    rows = lambda n: jnp.transpose(kv_new[:, :, n], (0, 2, 1, 3))
    cols = lambda n: jnp.transpose(kv_new[:, :, n], (0, 2, 3, 1))
    i = jnp.arange(TQ, dtype=jnp.int32)
    d_toep = jnp.arange(nq, dtype=jnp.int32)[:, None, None] * TQ + i[None, None, :] - i[None, :, None]
    c_end = CMP_STRIDE * jnp.arange(LANES, dtype=jnp.int32) + CMP_BLOCK - 1
    d_cmp = jnp.arange(t, dtype=jnp.int32).reshape(nq, 1, TQ) - c_end[None, :, None]
    tabs = bias_tables(jnp.concatenate([d_toep, d_cmp], axis=0), thr, rel_bias)
    s_start = SEL_BLOCK * jnp.arange(LANES)[:, None]
    c_start = CMP_STRIDE * jnp.arange(LANES)[None, :]
    cover = ((c_start < s_start + SEL_BLOCK) & (c_start + CMP_BLOCK > s_start)
             & (jnp.arange(LANES)[None, :] < n_cmp) & (jnp.arange(LANES)[:, None] < n_sel)).astype(bf16)
    expand = (jnp.arange(t)[:, None] // SEL_BLOCK == jnp.arange(LANES)[None, :]).astype(bf16)
    spec = lambda *blk: pl.BlockSpec((1, 1) + blk, lambda h, bb, qq: (bb, h) + (0,) * len(blk))
    out = pl.pallas_call(
        functools.partial(_nsa_prefill_body, n_cmp=n_cmp, n_sel=n_sel),
        grid=(NSA_KV_HEADS, b, nq),
        in_specs=[pl.BlockSpec((1, GRP, TQ, HEAD_DIM), lambda h, bb, qq: (bb, h, qq, 0)),
                  pl.BlockSpec((1, 1, 3, GRP, TQ), lambda h, bb, qq: (bb, h, 0, 0, qq)),
                  spec(LANES, HEAD_DIM), spec(HEAD_DIM, LANES),
                  spec(t, HEAD_DIM), spec(HEAD_DIM, t), spec(t, HEAD_DIM), spec(HEAD_DIM, t),
                  pl.BlockSpec((1, nq, GRP, TK, TQ), lambda h, bb, qq: (h, 0, 0, 0, 0)),
                  pl.BlockSpec((1, 1, GRP, LANES, TQ), lambda h, bb, qq: (h, nq + qq, 0, 0, 0)),
                  pl.BlockSpec((LANES, LANES), lambda h, bb, qq: (0, 0)),
                  pl.BlockSpec((t, LANES), lambda h, bb, qq: (0, 0))],
        out_specs=pl.BlockSpec((1, TQ, GRP * HEAD_DIM), lambda h, bb, qq: (bb, qq, h)),
        out_shape=jax.ShapeDtypeStruct((b, t, NSA_WIDTH), f32),
        scratch_shapes=[pltpu.VMEM((t, TQ), f32)],
        compiler_params=pltpu.CompilerParams(vmem_limit_bytes=VMEM_LIMIT),
        name="nsa_prefill",
    )(q4, g5, kc, vct, rows(2), cols(3), rows(4), cols(5), tabs, tabs, cover, expand)
    return out


S5_GB = 8
S5_CH = S5_GB * SSM_CH
S5_ST = S5_GB * SSM_STATE


def _cmul(a_re, a_im, b_re, b_im):
    return a_re * b_re - a_im * b_im, a_re * b_im + a_im * b_re


def _s5_body(u_ref, h0re_ref, h0im_ref, a_ref, bcat_ref, ccat_ref, d_ref, y_ref, hre_ref, him_ref, xs_ref, hin_ref,
             *, chained):
    f32, bf16 = jnp.float32, jnp.bfloat16
    n_l, n_r = u_ref.shape[1], u_ref.shape[2]
    u2 = u_ref[0].reshape(n_l * n_r, S5_CH)
    xs_ref[...] = jnp.dot(u2.astype(bf16), bcat_ref[0].astype(bf16),
                          preferred_element_type=f32).reshape(n_l, n_r, 2 * S5_ST)
    a_re, a_im = a_ref[0:1, :], a_ref[1:2, :]

    def scan_step(j, carry):
        h_re, h_im, p_re, p_im = carry
        x = xs_ref[j]
        t_re, t_im = _cmul(a_re, a_im, h_re, h_im)
        h_re, h_im = t_re + x[:, :S5_ST], t_im + x[:, S5_ST:]
        xs_ref[j] = jnp.concatenate([h_re, h_im], axis=1)
        return (h_re, h_im) + _cmul(a_re, a_im, p_re, p_im)

    if chained:
        start = (jnp.zeros((n_r, S5_ST), f32), jnp.zeros((n_r, S5_ST), f32))
    else:
        start = (h0re_ref[0], h0im_ref[0])
    ones = (jnp.ones((1, S5_ST), f32), jnp.zeros((1, S5_ST), f32))
    h_re, h_im, al_re, al_im = lax.fori_loop(0, n_l, scan_step, start + ones)

    if chained:
        hin_ref[0:1, :] = jnp.concatenate([h0re_ref[0], h0im_ref[0]], axis=1)

        def chain_step(c, carry):
            z = xs_ref[n_l - 1, pl.ds(c - 1, 1), :]
            t_re, t_im = _cmul(al_re, al_im, *carry)
            n_re, n_im = t_re + z[:, :S5_ST], t_im + z[:, S5_ST:]
            hin_ref[pl.ds(c, 1), :] = jnp.concatenate([n_re, n_im], axis=1)
            return n_re, n_im

        h_re, h_im = lax.fori_loop(1, n_r + 1, chain_step, (h0re_ref[0], h0im_ref[0]))

        def fix_step(j, carry):
            p_re, p_im = carry
            hin = hin_ref[0:n_r, :]
            t_re, t_im = _cmul(p_re, p_im, hin[:, :S5_ST], hin[:, S5_ST:])
            xs_ref[j] = xs_ref[j] + jnp.concatenate([t_re, t_im], axis=1)
            return _cmul(a_re, a_im, p_re, p_im)

        lax.fori_loop(0, n_l, fix_step, (a_re, a_im))

    hre_ref[0] = h_re
    him_ref[0] = h_im
    hs = xs_ref[...].reshape(n_l * n_r, 2 * S5_ST).astype(bf16)
    y = jnp.dot(hs, ccat_ref[0].astype(bf16), preferred_element_type=f32) + d_ref[...] * u2
    y_ref[0] = jax.nn.gelu(y).reshape(n_l, n_r, S5_CH)


def s5_discretize(lam_re, lam_im, log_dt, b_re, b_im, c_re, c_im):
    f32 = jnp.float32
    dt = jnp.exp(log_dt.astype(f32))[:, None]
    lr, li = lam_re.astype(f32), lam_im.astype(f32)
    mag = jnp.exp(lr * dt)
    a_re, a_im = mag * jnp.cos(li * dt), mag * jnp.sin(li * dt)
    den = lr * lr + li * li
    z_re = ((a_re - 1.0) * lr + a_im * li) / den
    z_im = (a_im * lr - (a_re - 1.0) * li) / den
    br, bim = b_re.astype(f32), b_im.astype(f32)
    bb_re = z_re[..., None] * br - z_im[..., None] * bim
    bb_im = z_re[..., None] * bim + z_im[..., None] * br
    ngb = SSM_GROUPS // S5_GB
    eye = jnp.eye(S5_GB, dtype=f32)

    def block_diag(w):
        wd = w[:, :, :, None, :] * eye[None, :, None, :, None]
        return wd.reshape(ngb, S5_GB * w.shape[2], S5_GB * w.shape[3])

    def pack_b(bb):
        return block_diag(jnp.swapaxes(bb.reshape(ngb, S5_GB, SSM_STATE, SSM_CH), 2, 3))

    def pack_c(cc):
        return block_diag(jnp.swapaxes(cc.reshape(ngb, S5_GB, SSM_CH, SSM_STATE), 2, 3))

    a = jnp.stack([a_re.reshape(-1), a_im.reshape(-1)])
    bcat = jnp.concatenate([pack_b(bb_re), pack_b(bb_im)], axis=2)
    ccat = jnp.concatenate([pack_c(c_re.astype(f32)), -pack_c(c_im.astype(f32))], axis=1)
    return a, bcat, ccat


def s5_scan(u4, h0_re, h0_im, a, bcat, ccat, d_skip, chained):
    nb, n_l, n_r, _ = u4.shape
    rh = h0_re.shape[1]
    ngb = SSM_GROUPS // S5_GB
    st_spec = pl.BlockSpec((1, rh, S5_ST), lambda i, j: (i, 0, j))
    return pl.pallas_call(
        functools.partial(_s5_body, chained=chained),
        grid=(nb, ngb),
        in_specs=[pl.BlockSpec((1, n_l, n_r, S5_CH), lambda i, j: (i, 0, 0, j)), st_spec, st_spec,
                  pl.BlockSpec((2, S5_ST), lambda i, j: (0, j)),
                  pl.BlockSpec((1, S5_CH, 2 * S5_ST), lambda i, j: (j, 0, 0)),
                  pl.BlockSpec((1, 2 * S5_ST, S5_CH), lambda i, j: (j, 0, 0)),
                  pl.BlockSpec((1, S5_CH), lambda i, j: (0, j))],
        out_specs=[pl.BlockSpec((1, n_l, n_r, S5_CH), lambda i, j: (i, 0, 0, j)), st_spec, st_spec],
        out_shape=[jax.ShapeDtypeStruct(u4.shape, jnp.float32),
                   jax.ShapeDtypeStruct(h0_re.shape, jnp.float32), jax.ShapeDtypeStruct(h0_re.shape, jnp.float32)],
        scratch_shapes=[pltpu.VMEM((n_l, n_r, 2 * S5_ST), jnp.float32), pltpu.VMEM((n_r + 8, 2 * S5_ST), jnp.float32)],
        compiler_params=pltpu.CompilerParams(vmem_limit_bytes=VMEM_LIMIT),
        name="s5_scan",
    )(u4, h0_re, h0_im, a, bcat, ccat, d_skip.reshape(1, SSM_WIDTH))


S5_CHUNK = 64


ROW_LANES = N_PAGED * KV_WIDTH
CHUNKS_PER_PAGE = 128 // CMP_STRIDE
CMP_HID2 = NSA_KV_HEADS * CMP_HIDDEN
CMP_PAGES = 64


def _compress_body(pt_ref, *refs, n_g, transposed):
    f32, bf16 = jnp.float32, jnp.bfloat16
    page_refs = (refs[:n_g], refs[n_g:2 * n_g])
    w1_ref, c1_ref, w2_ref, b2_ref, o_ref, carry_ref, rows_ref = refs[2 * n_g:]
    m_rows = n_g * CHUNKS_PER_PAGE
    first = pl.program_id(1) == 0
    row_id = lax.broadcasted_iota(jnp.int32, (m_rows, CMP_HID2), 0)
    outs = []
    for t in range(2):
        for k, r in enumerate(page_refs[t]):
            rows_ref[t, k * 128:(k + 1) * 128, :] = r[0].T if transposed else r[0]
        cols = [rows_ref[t, pl.ds(j, m_rows, stride=CMP_STRIDE), :] for j in range(CMP_STRIDE)]
        x = jnp.concatenate(cols, axis=1).astype(bf16)
        part = jnp.dot(x, w1_ref[t], preferred_element_type=f32)
        p0, p1 = part[:, :CMP_HID2], part[:, CMP_HID2:]
        prev = jnp.where(first, 0.0, carry_ref[t, 0:1, :])
        shifted = jnp.where(row_id == 0, prev, pltpu.roll(p0, 1, axis=0))
        carry_ref[t, 0:1, :] = p0[m_rows - 1:m_rows, :]
        h1 = (c1_ref[t] + shifted) + p1
        outs.append(jnp.dot(jax.nn.gelu(h1).astype(bf16), w2_ref[t], preferred_element_type=f32) + b2_ref[t])
    o_ref[0] = jnp.concatenate(outs, axis=1)


def compress_pages(pool, page_table, cmp_pe, cmp_w1, cmp_b1, cmp_w2, cmp_b2, n_g, transposed):
    n_seq, n_pages = page_table.shape
    assert n_pages % n_g == 0 and pool.shape[1:] == ((ROW_LANES, 128) if transposed else (128, ROW_LANES))
    f32, bf16 = jnp.float32, jnp.bfloat16
    r = CMP_BLOCK // CMP_STRIDE
    eye = jnp.eye(NSA_KV_HEADS, dtype=f32)
    w1 = cmp_w1.astype(f32).reshape(2, r, CMP_STRIDE, HEAD_DIM, CMP_HIDDEN)
    w1 = jnp.transpose(w1, (0, 2, 3, 1, 4))[:, :, None, :, :, None, :] * eye[None, None, :, None, None, :, None]
    w1 = w1.reshape(2, CMP_STRIDE * KV_WIDTH, r * CMP_HID2).astype(bf16)
    c1 = jnp.stack([jnp.einsum('ld,ldf->f', cmp_pe[t], cmp_w1[t]) + cmp_b1[t] for t in range(2)])
    c1 = jnp.tile(c1[:, None, :], (1, 1, NSA_KV_HEADS))
    w2 = (cmp_w2.astype(f32)[:, None, :, None, :] * eye[None, :, None, :, None]).reshape(2, CMP_HID2, KV_WIDTH).astype(bf16)
    b2 = jnp.tile(cmp_b2.astype(f32)[:, None, :], (1, 1, NSA_KV_HEADS))
    m_rows = n_g * CHUNKS_PER_PAGE
    page_spec = lambda k, t: pl.BlockSpec((1, 128, KV_WIDTH), lambda b, g, pt: (
        (pt[b * n_pages + g * n_g + k], t, 0) if transposed else (pt[b * n_pages + g * n_g + k], 0, t)))
    full = lambda a: pl.BlockSpec(a.shape, lambda b, g, pt: (0,) * a.ndim)
    return pl.pallas_call(
        functools.partial(_compress_body, n_g=n_g, transposed=transposed),
        grid_spec=pltpu.PrefetchScalarGridSpec(
            num_scalar_prefetch=1,
            grid=(n_seq, n_pages // n_g),
            in_specs=[page_spec(k, t) for t in range(2) for k in range(n_g)] + [full(w1), full(c1), full(w2), full(b2)],
            out_specs=pl.BlockSpec((1, m_rows, 2 * KV_WIDTH), lambda b, g, pt: (b, g, 0)),
            scratch_shapes=[pltpu.VMEM((2, 8, CMP_HID2), f32), pltpu.VMEM((2, n_g * 128, KV_WIDTH), f32)],
        ),
        out_shape=jax.ShapeDtypeStruct((n_seq, n_pages * CHUNKS_PER_PAGE, 2 * KV_WIDTH), f32),
        compiler_params=pltpu.CompilerParams(vmem_limit_bytes=VMEM_LIMIT),
        name="compress_pages",
    )(page_table.reshape(-1).astype(jnp.int32), *([pool] * (2 * n_g)), w1, c1, w2, b2)


SD_PAGES = 64
SD_KEYS = SD_PAGES * 128
SD_SUB = 16
SD_ROWS = NSA_HEADS * 8
WIN_PAD = 640


def _masked_softmax(s, ok):
    s = jnp.where(ok, s, NEG_INF)
    e = jnp.where(ok, jnp.exp(s - jnp.max(s, axis=-1, keepdims=True)), 0.0)
    return e / jnp.maximum(jnp.sum(e, axis=-1, keepdims=True), 1e-30)


def _rows_from_group(a):
    t = a.shape[0] // NSA_KV_HEADS
    a4 = jnp.broadcast_to(a.reshape(NSA_KV_HEADS, 1, t, a.shape[1]), (NSA_KV_HEADS, GRP, t, a.shape[1]))
    return a4.reshape(NSA_KV_HEADS * GRP * t, a.shape[1])


def _nsa_decode_body(pt_ref, *refs, pos0, n_cmp, n_sel, n_new, win_len):
    f32, bf16 = jnp.float32, jnp.bfloat16
    kt_refs, vt_refs = refs[:SD_PAGES], refs[SD_PAGES:2 * SD_PAGES]
    (q_ref, g_ref, kcvc_ref, knew_ref, wh_ref, bc_ref, bs_ref, bn_ref, bw_ref, cov_ref, exp_ref,
     o_ref, sel_ref, oc_ref, m_ref, l_ref, acc_ref, mask_ref) = refs[2 * SD_PAGES:]
    tile, n_tiles = pl.program_id(1), pl.num_programs(1)
    nt_dims = (((1,), (1,)), ((), ()))
    nn_dims = (((1,), (0,)), ((), ()))
    n_tok = SD_ROWS // NSA_HEADS
    q = (q_ref[0] * (HEAD_DIM ** -0.5)).astype(bf16)
    qpos = pos0 + lax.broadcasted_iota(jnp.int32, (SD_ROWS, 1), 0) % n_tok

    @pl.when(tile == 0)
    def _():
        n_c = kcvc_ref.shape[1]
        m_idx = lax.broadcasted_iota(jnp.int32, (SD_ROWS, n_c), 1)
        ok_c = (m_idx >= 1) & (m_idx <= n_cmp) & (qpos >= CMP_STRIDE * m_idx + (CMP_BLOCK - 1 - CMP_STRIDE))
        s_c = lax.dot_general(q, kcvc_ref[0, :, :KV_WIDTH].astype(bf16), nt_dims, preferred_element_type=f32) + bc_ref[...]
        p_c = _masked_softmax(s_c, ok_c)
        oc_ref[...] = jnp.dot(p_c.astype(bf16), kcvc_ref[0, :, KV_WIDTH:].astype(bf16), preferred_element_type=f32)
        p_sum = jnp.sum(p_c.reshape(NSA_KV_HEADS, GRP, n_tok, n_c), axis=1).reshape(NSA_KV_HEADS * n_tok, n_c)
        imp = jnp.dot(p_sum.astype(bf16), cov_ref[...], preferred_element_type=f32)
        n_l = imp.shape[1]
        lane = lax.broadcasted_iota(jnp.int32, (NSA_KV_HEADS * n_tok, n_l), 1)
        qp = pos0 + lax.broadcasted_iota(jnp.int32, (NSA_KV_HEADS * n_tok, n_l), 0) % n_tok
        cur = qp // SEL_BLOCK
        forced = (lane == 0) | (lane == cur) | (lane == cur - 1)
        score = jnp.where(forced, FORCE_SCORE, jnp.where(lane * SEL_BLOCK <= qp, imp, -1.0))
        score = jnp.where(lane < n_sel, score, -jnp.inf)
        sel = jnp.zeros(score.shape, f32)
        for _ in range(min(SEL_TOP_N, n_sel)):
            best = jnp.max(score, axis=-1, keepdims=True)
            lane_f = lane.astype(f32)
            hit = lane_f == jnp.min(jnp.where(score == best, lane_f, float(n_l)), axis=-1, keepdims=True)
            sel = jnp.where(hit, 1.0, sel)
            score = jnp.where(hit, -jnp.inf, score)
        sel_ref[...] = sel
        m_ref[...] = jnp.full(m_ref.shape, NEG_INF, f32)
        l_ref[...] = jnp.zeros(l_ref.shape, f32)
        acc_ref[...] = jnp.zeros(acc_ref.shape, f32)

    def online(s, ok, v, v_dims):
        m_old = m_ref[...]
        s = jnp.where(ok, s, NEG_INF)
        m_new = jnp.maximum(m_old, jnp.max(s, axis=-1, keepdims=True))
        p = jnp.where(ok, jnp.exp(s - m_new), 0.0)
        alpha = jnp.exp(m_old - m_new)
        l_ref[...] = alpha * l_ref[...] + jnp.sum(p, axis=-1, keepdims=True)
        acc_ref[...] = alpha * acc_ref[...] + lax.dot_general(p.astype(bf16), v, v_dims, preferred_element_type=f32)
        m_ref[...] = m_new

    sel_tile = sel_ref[:, pl.ds(pl.multiple_of(tile * LANES, LANES), LANES)]
    mask_ref[...] = jnp.dot(sel_tile.astype(bf16), exp_ref[...], preferred_element_type=f32)
    sub_keys = SD_SUB * 128
    for sub in range(SD_PAGES // SD_SUB):
        pages = slice(sub * SD_SUB, (sub + 1) * SD_SUB)
        kt = jnp.concatenate([r[0] for r in kt_refs[pages]], axis=1).astype(bf16)
        vt = jnp.concatenate([r[0] for r in vt_refs[pages]], axis=1).astype(bf16)
        s = jnp.dot(q, kt, preferred_element_type=f32) + bs_ref[:, sub * sub_keys:(sub + 1) * sub_keys]
        ok = _rows_from_group(mask_ref[:, sub * sub_keys:(sub + 1) * sub_keys]) > 0.5
        online(s, ok, vt, nt_dims)

    @pl.when(tile == n_tiles - 1)
    def _():
        lane = lax.broadcasted_iota(jnp.int32, (SD_ROWS, LANES), 1)
        new_blk = pos0 // SEL_BLOCK
        sel_new = _rows_from_group(sel_ref[:, new_blk:new_blk + 1]) > 0.5
        ok_n = sel_new & (lane < n_new) & (pos0 + lane <= qpos)
        s_n = lax.dot_general(q, knew_ref[0, :, :KV_WIDTH].astype(bf16), nt_dims, preferred_element_type=f32) + bn_ref[...]
        online(s_n, ok_n, knew_ref[0, :, KV_WIDTH:].astype(bf16), nn_dims)
        o_s = acc_ref[...] / jnp.maximum(l_ref[...], 1e-30)
        j = lax.broadcasted_iota(jnp.int32, (SD_ROWS, WIN_PAD), 1)
        kwpos = pos0 - win_len + j
        dist = qpos - kwpos
        ok_w = (dist >= 0) & (dist < WINDOW) & (kwpos >= 0) & (j < win_len + n_new)
        s_w = lax.dot_general(q, wh_ref[0, :, :KV_WIDTH].astype(bf16), nt_dims, preferred_element_type=f32) + bw_ref[...]
        o_w = jnp.dot(_masked_softmax(s_w, ok_w).astype(bf16), wh_ref[0, :, KV_WIDTH:].astype(bf16), preferred_element_type=f32)
        g = jax.nn.sigmoid(g_ref[0])
        o = g[:, 0:1] * oc_ref[...] + g[:, 1:2] * o_s + g[:, 2:3] * o_w
        row = lax.broadcasted_iota(jnp.int32, (SD_ROWS, HEAD_DIM), 0)
        o_ref[0] = jnp.where(row < SD_ROWS // NSA_KV_HEADS, o[:, :HEAD_DIM], o[:, HEAD_DIM:])


def nsa_decode(q, glog, kv_new, kcvc, pool_t, page_table, cache_win, thr, rel_bias, pos0):
    f32, bf16 = jnp.float32, jnp.bfloat16
    b, t = q.shape[:2]
    n_pages = page_table.shape[1]
    win_len = cache_win.shape[1]
    assert t * NSA_HEADS == SD_ROWS and pos0 == n_pages * 128 and n_pages % SD_PAGES == 0 and pos0 % SEL_BLOCK == 0
    assert win_len + t <= WIN_PAD and t <= SEL_BLOCK and SD_KEYS == LANES * SEL_BLOCK
    n_rows = pos0 + t
    n_cmp = (n_rows - CMP_BLOCK) // CMP_STRIDE + 1
    n_sel = -(-n_rows // SEL_BLOCK)
    n_c = kcvc.shape[1]
    sel_lanes = -(-n_sel // LANES) * LANES
    q5 = jnp.transpose(q.reshape(b, t, NSA_KV_HEADS, GRP, HEAD_DIM), (0, 2, 3, 1, 4))
    qz = (q5[:, :, :, :, None, :] * jnp.eye(NSA_KV_HEADS, dtype=f32)[None, :, None, None, :, None]).reshape(b, SD_ROWS, KV_WIDTH)
    g3 = jnp.transpose(glog.reshape(b, t, NSA_HEADS, 3), (0, 2, 1, 3)).reshape(b, SD_ROWS, 3)
    knew = jnp.pad(kv_new[:, :, 2:4].reshape(b, t, 2 * KV_WIDTH), ((0, 0), (0, LANES - t), (0, 0)))
    whist = jnp.concatenate([cache_win.reshape(b, win_len, 2 * KV_WIDTH), kv_new[:, :, 4:6].reshape(b, t, 2 * KV_WIDTH),
                             jnp.zeros((b, WIN_PAD - win_len - t, 2 * KV_WIDTH), f32)], axis=1)
    qp = pos0 + jnp.arange(t, dtype=jnp.int32)[:, None]
    tab = lambda dist: bias_tables(dist[None], thr, rel_bias).reshape(SD_ROWS, dist.shape[1])
    bias_c = tab(qp - (CMP_STRIDE * jnp.arange(n_c, dtype=jnp.int32)[None, :] + CMP_BLOCK - 1 - CMP_STRIDE))
    bias_s = tab(qp - jnp.arange(pos0, dtype=jnp.int32)[None, :])
    bias_n = tab(qp - (pos0 + jnp.arange(LANES, dtype=jnp.int32)[None, :]))
    bias_w = tab(qp - (pos0 - win_len + jnp.arange(WIN_PAD, dtype=jnp.int32)[None, :]))
    m_idx = jnp.arange(n_c)[:, None]
    c_start = CMP_STRIDE * (m_idx - 1)
    s_start = SEL_BLOCK * jnp.arange(sel_lanes)[None, :]
    cover = ((c_start < s_start + SEL_BLOCK) & (c_start + CMP_BLOCK > s_start) & (m_idx >= 1) & (m_idx <= n_cmp)
             & (jnp.arange(sel_lanes)[None, :] < n_sel)).astype(bf16)
    expand = (jnp.arange(SD_KEYS)[None, :] // SEL_BLOCK == jnp.arange(LANES)[:, None]).astype(bf16)
    page_spec = lambda k, blk: pl.BlockSpec((1, KV_WIDTH, 128), lambda bb, g, pt: (pt[bb * n_pages + g * SD_PAGES + k], blk, 0))
    per_seq = lambda a: pl.BlockSpec((1,) + a.shape[1:], lambda bb, g, pt: (bb,) + (0,) * (a.ndim - 1))
    full = lambda a: pl.BlockSpec(a.shape, lambda bb, g, pt: (0,) * a.ndim)
    kv_rows = NSA_KV_HEADS * t
    out = pl.pallas_call(
        functools.partial(_nsa_decode_body, pos0=pos0, n_cmp=n_cmp, n_sel=n_sel, n_new=t, win_len=win_len),
        grid_spec=pltpu.PrefetchScalarGridSpec(
            num_scalar_prefetch=1,
            grid=(b, n_pages // SD_PAGES),
            in_specs=[page_spec(k, blk) for blk in (2, 3) for k in range(SD_PAGES)]
            + [per_seq(qz), per_seq(g3), per_seq(kcvc), per_seq(knew), per_seq(whist), full(bias_c),
               pl.BlockSpec((SD_ROWS, SD_KEYS), lambda bb, g, pt: (0, g)), full(bias_n), full(bias_w), full(cover), full(expand)],
            out_specs=pl.BlockSpec((1, SD_ROWS, HEAD_DIM), lambda bb, g, pt: (bb, 0, 0)),
            scratch_shapes=[pltpu.VMEM((kv_rows, sel_lanes), f32), pltpu.VMEM((SD_ROWS, KV_WIDTH), f32),
                            pltpu.VMEM((SD_ROWS, 1), f32), pltpu.VMEM((SD_ROWS, 1), f32), pltpu.VMEM((SD_ROWS, KV_WIDTH), f32),
                            pltpu.VMEM((kv_rows, SD_KEYS), f32)],
        ),
        out_shape=jax.ShapeDtypeStruct((b, SD_ROWS, HEAD_DIM), f32),
        compiler_params=pltpu.CompilerParams(vmem_limit_bytes=VMEM_LIMIT),
        name="nsa_decode",
    )(page_table.reshape(-1).astype(jnp.int32), *([pool_t] * (2 * SD_PAGES)), qz, g3, kcvc, knew, whist,
      bias_c, bias_s, bias_n, bias_w, cover, expand)
    out = jnp.transpose(out.reshape(b, NSA_KV_HEADS, GRP, t, HEAD_DIM), (0, 3, 1, 2, 4))
    return out.reshape(b, t, NSA_WIDTH)


MOE_ROWS = 1280
MOE_SUB = 64
MOE_PATHS = (256, 1024, 1088, 1152, 1216, MOE_ROWS)
MOE_CHUNK = 256
MOE_TF = 256
MOE_COLS = 256


def _moe_body(e_ref, sub_ref, blk_ref, ok_ref, x_ref, wg_ref, wl_ref, bg_ref, bl_ref, wd_ref, bd_ref, o_ref, xs_ref):
    bf16 = jnp.bfloat16
    i, f = pl.program_id(0), pl.program_id(1)
    n_sub = sub_ref[i]
    d = o_ref.shape[1]

    @pl.when(f == 0)
    def _():
        o_ref[...] = jnp.broadcast_to(bd_ref[0], o_ref.shape)

    @pl.when((f == 0) & (n_sub > 0))
    def _():
        for j in range(MOE_ROWS // MOE_CHUNK):
            rows = slice(j * MOE_CHUNK, (j + 1) * MOE_CHUNK)
            w = lax.bitcast_convert_type(x_ref[rows, :], jnp.uint32)
            hi = lax.bitcast_convert_type(w & jnp.uint32(0xFFFF0000), jnp.float32)
            lo = lax.bitcast_convert_type(w << 16, jnp.float32)
            xs_ref[rows, :] = jnp.concatenate([hi, lo], axis=1).astype(bf16)

    def expert_rows(n_rows):
        x = xs_ref[0:n_rows, :]
        hg = jnp.dot(x, wg_ref[0].astype(bf16), preferred_element_type=jnp.float32) + bg_ref[0]
        hl = jnp.dot(x, wl_ref[0].astype(bf16), preferred_element_type=jnp.float32) + bl_ref[0]
        hg = jnp.minimum(hg, SWIGLU_LIMIT)
        hl = jnp.clip(hl, -SWIGLU_LIMIT, SWIGLU_LIMIT)
        act = (hg * jax.nn.sigmoid(SWIGLU_ALPHA * hg) * (hl + 1.0)).astype(bf16)
        for c in range(d // MOE_COLS):
            cols = slice(c * MOE_COLS, (c + 1) * MOE_COLS)
            o_ref[0:n_rows, cols] += jnp.dot(act, wd_ref[0, :, cols].astype(bf16), preferred_element_type=jnp.float32)

    below = 0
    for path_rows in MOE_PATHS:
        @pl.when((n_sub * MOE_SUB > below) & (n_sub * MOE_SUB <= path_rows))
        def _(path_rows=path_rows):
            expert_rows(path_rows)

        below = path_rows


def moe_experts(item_e, item_sub, item_blk, item_ok, xb, w_gu, b_gu, w_down, b_down):
    n_items = item_e.shape[0]
    n_f = D_FF // MOE_TF
    d = 2 * xb.shape[1]

    def col(f, ok, i):
        return f * ok[i] + (n_f - 1) * (1 - ok[i])

    return pl.pallas_call(
        _moe_body,
        grid_spec=pltpu.PrefetchScalarGridSpec(
            num_scalar_prefetch=4,
            grid=(n_items, n_f),
            in_specs=[pl.BlockSpec((MOE_ROWS, d // 2), lambda i, f, e, s, b, ok: (b[i], 0)),
                      pl.BlockSpec((1, d, MOE_TF), lambda i, f, e, s, b, ok: (e[i], 0, col(f, ok, i))),
                      pl.BlockSpec((1, d, MOE_TF), lambda i, f, e, s, b, ok: (e[i], 0, n_f + col(f, ok, i))),
                      pl.BlockSpec((1, 1, MOE_TF), lambda i, f, e, s, b, ok: (e[i], 0, col(f, ok, i))),
                      pl.BlockSpec((1, 1, MOE_TF), lambda i, f, e, s, b, ok: (e[i], 0, n_f + col(f, ok, i))),
                      pl.BlockSpec((1, MOE_TF, d), lambda i, f, e, s, b, ok: (e[i], col(f, ok, i), 0)),
                      pl.BlockSpec((1, 1, d), lambda i, f, e, s, b, ok: (e[i], 0, 0))],
            out_specs=pl.BlockSpec((MOE_ROWS, d), lambda i, f, e, s, b, ok: (i, 0)),
            scratch_shapes=[pltpu.VMEM((MOE_ROWS, d), jnp.bfloat16)],
        ),
        out_shape=jax.ShapeDtypeStruct((n_items * MOE_ROWS, d), jnp.float32),
        compiler_params=pltpu.CompilerParams(vmem_limit_bytes=VMEM_LIMIT),
        name="moe_experts",
    )(item_e, item_sub, item_blk, item_ok, xb, w_gu, w_gu, b_gu[:, None, :], b_gu[:, None, :], w_down, b_down[:, None, :])


ROW_TILE = 256
PROJ_TILE = 1024
PROJ_ROWS = 512


def _rms(v, g):
    return v * lax.rsqrt(jnp.mean(v * v, axis=-1, keepdims=True) + RMS_EPS) * g


def _norm_proj_body(x_ref, g_ref, w_ref, o_ref, h_ref):
    @pl.when(pl.program_id(1) == 0)
    def _():
        h_ref[...] = _rms(x_ref[...], g_ref[...]).astype(jnp.bfloat16)

    o_ref[...] = jnp.dot(h_ref[...], w_ref[...], preferred_element_type=jnp.float32)


def norm_proj(x, gain, w):
    n, d = x.shape
    cols = w.shape[1]
    tm = min(n, PROJ_ROWS)
    return pl.pallas_call(
        _norm_proj_body,
        grid=(n // tm, cols // PROJ_TILE),
        in_specs=[pl.BlockSpec((tm, d), lambda i, j: (i, 0)),
                  pl.BlockSpec((1, d), lambda i, j: (0, 0)),
                  pl.BlockSpec((d, PROJ_TILE), lambda i, j: (0, j))],
        out_specs=pl.BlockSpec((tm, PROJ_TILE), lambda i, j: (i, j)),
        out_shape=jax.ShapeDtypeStruct((n, cols), jnp.float32),
        scratch_shapes=[pltpu.VMEM((tm, d), jnp.bfloat16)],
        compiler_params=pltpu.CompilerParams(vmem_limit_bytes=VMEM_LIMIT),
        name="norm_proj",
    )(x, gain.reshape(1, d), w)


def _mix_out_body(attn_ref, y_ref, x_ref, wglu_ref, bglu_ref, ga_ref, gs_ref, wout_ref, gf_ref, wr_ref,
                  x2_ref, hp_ref, lg_ref):
    f32, bf16 = jnp.float32, jnp.bfloat16
    y = y_ref[...]
    ssm = y * jax.nn.sigmoid(jnp.dot(y.astype(bf16), wglu_ref[...], preferred_element_type=f32) + bglu_ref[...])
    merged = jnp.concatenate([_rms(attn_ref[...], ga_ref[...]), _rms(ssm, gs_ref[...])], axis=1).astype(bf16)
    x2 = x_ref[...] + jnp.dot(merged, wout_ref[...], preferred_element_type=f32)
    x2_ref[...] = x2
    hm = _rms(x2, gf_ref[...]).astype(bf16)
    lg_ref[...] = jnp.dot(hm, wr_ref[...], preferred_element_type=f32)
    bits = lax.bitcast_convert_type(hm.astype(f32), jnp.uint32)
    half = bits.shape[1] // 2
    hp_ref[...] = lax.bitcast_convert_type(bits[:, :half] | (bits[:, half:] >> 16), f32)


def mix_out(attn, y, x, lw):
    n = x.shape[0]
    f32, bf16 = jnp.float32, jnp.bfloat16
    row = lambda a: a.astype(f32).reshape(1, -1)
    wr = jnp.pad(lw['router_w'], ((0, 0), (0, LANES - N_EXPERTS))).astype(bf16)
    consts = [lw['w_glu'].astype(bf16), row(lw['b_glu']), row(lw['norm_attn_out']), row(lw['norm_ssm_out']),
              lw['w_out'].astype(bf16), row(lw['norm_ffn']), wr]
    tile = lambda c: pl.BlockSpec((ROW_TILE, c), lambda i: (i, 0))
    full = lambda a: pl.BlockSpec(a.shape, lambda i: (0, 0))
    return pl.pallas_call(
        _mix_out_body,
        grid=(n // ROW_TILE,),
        in_specs=[tile(NSA_WIDTH), tile(SSM_WIDTH), tile(D_MODEL)] + [full(c) for c in consts],
        out_specs=[tile(D_MODEL), tile(D_MODEL // 2), tile(LANES)],
        out_shape=[jax.ShapeDtypeStruct((n, D_MODEL), f32), jax.ShapeDtypeStruct((n, D_MODEL // 2), f32),
                   jax.ShapeDtypeStruct((n, LANES), f32)],
        compiler_params=pltpu.CompilerParams(vmem_limit_bytes=VMEM_LIMIT),
        name="mix_out",
    )(attn, y, x, *consts)


def _combine_body(x_ref, y0_ref, y1_ref, y2_ref, y3_ref, gate_ref, g_ref, o_ref, *, normed):
    gate = gate_ref[...]
    moe = gate[:, 0:1] * y0_ref[...]
    for k, y_ref in enumerate((y1_ref, y2_ref, y3_ref), start=1):
        moe = moe + gate[:, k:k + 1] * y_ref[...]
    out = x_ref[...] + moe
    o_ref[...] = _rms(out, g_ref[...]) if normed else out


def combine_norm(x, ys, gate, gain):
    n, d = x.shape
    normed = gain is not None
    gain = jnp.ones((d,), jnp.float32) if gain is None else gain
    tile = lambda c: pl.BlockSpec((ROW_TILE, c), lambda i: (i, 0))
    return pl.pallas_call(
        functools.partial(_combine_body, normed=normed),
        grid=(n // ROW_TILE,),
        in_specs=[tile(d)] * (1 + TOP_K) + [tile(TOP_K), pl.BlockSpec((1, d), lambda i: (0, 0))],
        out_specs=tile(d),
        out_shape=jax.ShapeDtypeStruct((n, d), jnp.float32),
        name="combine_norm",
    )(x, *ys, gate, gain.astype(jnp.float32).reshape(1, d))


def moe_routed(hp, logits, router_b, w_gu, b_gu, w_down, b_down):
    n = hp.shape[0]
    top_val, top_idx = lax.top_k(logits[:, :N_EXPERTS] + router_b.astype(jnp.float32), TOP_K)
    gate = jax.nn.softmax(top_val, axis=-1)
    nk = n * TOP_K
    n_items = N_EXPERTS + nk // MOE_ROWS
    flat_e = top_idx.reshape(nk)
    onehot = (flat_e[:, None] == jnp.arange(N_EXPERTS, dtype=flat_e.dtype)[None, :]).astype(jnp.int32)
    running = jnp.cumsum(onehot, axis=0)
    counts = running[-1]
    pos_in_e = jnp.sum(onehot * running, axis=1) - 1
    items_e = (counts + MOE_ROWS - 1) // MOE_ROWS
    item_end = jnp.cumsum(items_e)
    item_start = item_end - items_e
    total = item_end[-1]
    dest = (item_start[flat_e] * MOE_ROWS + pos_in_e).astype(jnp.int32)
    row_tok = jnp.zeros(n_items * MOE_ROWS, jnp.int32).at[dest].set(jnp.arange(nk, dtype=jnp.int32) // TOP_K)
    item = jnp.minimum(jnp.arange(n_items, dtype=jnp.int32), total - 1)
    item_e = jnp.minimum(jnp.searchsorted(item_end, item, side='right'), N_EXPERTS - 1).astype(jnp.int32)
    rows = jnp.clip(counts[item_e] - (item - item_start[item_e]) * MOE_ROWS, 0, MOE_ROWS)
    item_sub = jnp.where(jnp.arange(n_items) < total, (rows + MOE_SUB - 1) // MOE_SUB, 0).astype(jnp.int32)
    item_ok = (jnp.arange(n_items) < total).astype(jnp.int32)
    yb = moe_experts(item_e, item_sub, item.astype(jnp.int32), item_ok, hp[row_tok], w_gu, b_gu, w_down, b_down)
    return yb, dest.reshape(n, TOP_K), gate


def mixer_layer(x, pos0, pool, page_table, win_buf, h0_re, h0_im, rel_bias, lw):
    b, t, _ = x.shape
    w_in = jnp.pad(lw['w_in'], ((0, 0), (0, -IN_WIDTH % PROJ_TILE))).astype(jnp.bfloat16)
    proj = norm_proj(x.reshape(b * t, D_MODEL), lw['norm_mix'].astype(jnp.float32), w_in).reshape(b, t, -1)
    o1 = NSA_WIDTH
    o2 = o1 + N_KV * KV_WIDTH
    o3 = o2 + 3 * NSA_HEADS
    q = proj[..., :o1].reshape(b, t, NSA_HEADS, HEAD_DIM)
    kv_new = proj[..., o1:o2].reshape(b, t, N_KV, NSA_KV_HEADS, HEAD_DIM)
    u = proj[..., o3:IN_WIDTH]
    paged_new = kv_new[:, :, :N_PAGED]
    win_new = kv_new[:, :, N_PAGED:]
    n_rows = pos0 + t
    n_cmp = (n_rows - CMP_BLOCK) // CMP_STRIDE + 1
    thr = bucket_thresholds()
    cmp_w = (lw['cmp_pe'], lw['cmp_w1'], lw['cmp_b1'], lw['cmp_w2'], lw['cmp_b2'])
    if pool is None:
        assert pos0 == 0 and t % 128 == 0
        pages = t // 128
        own_pool = proj[..., o1:o1 + ROW_LANES].reshape(b * pages, 128, ROW_LANES)
        own_table = jnp.arange(b * pages, dtype=jnp.int32).reshape(b, pages)
        kcvc = compress_pages(own_pool, own_table, *cmp_w, n_g=pages, transposed=False)
        attn = nsa_prefill(q, proj[..., o2:o3], kcvc, kv_new, thr, rel_bias, n_cmp).astype(x.dtype)
        w_hist = win_new
    else:
        kcvc = compress_pages(pool, page_table, *cmp_w, n_g=CMP_PAGES, transposed=True)
        attn = nsa_decode(q, proj[..., o2:o3], kv_new, kcvc, pool, page_table, win_buf, thr, rel_bias, pos0).astype(x.dtype)
        w_hist = jnp.concatenate([win_buf, win_new], axis=1)
    new_win = w_hist[:, w_hist.shape[1] - min(WINDOW, n_rows):]
    y, h_re, h_im = s5_branch(u, h0_re, h0_im, lw, pool is None)
    x2, hp, logits = mix_out(attn.reshape(b * t, NSA_WIDTH), y.reshape(b * t, SSM_WIDTH), x.reshape(b * t, D_MODEL), lw)
    return (x2, hp, logits), paged_new, new_win, h_re, h_im


def s5_branch(u, h0_re, h0_im, lw, chained):
    bsz, t, _ = u.shape
    f32 = jnp.float32
    a, bcat, ccat = s5_discretize(lw['lam_re'], lw['lam_im'], lw['log_dt'], lw['b_re'], lw['b_im'], lw['c_re'], lw['c_im'])
    d_skip = lw['d_skip'].astype(f32)
    flat = lambda h: h.astype(f32).reshape(bsz, SSM_GROUPS * SSM_STATE)
    if chained:
        n_r = t // S5_CHUNK
        u4 = jnp.transpose(u.astype(f32).reshape(bsz, n_r, S5_CHUNK, SSM_WIDTH), (0, 2, 1, 3))
        y4, h_re, h_im = s5_scan(u4, flat(h0_re)[:, None], flat(h0_im)[:, None], a, bcat, ccat, d_skip, True)
        y = jnp.transpose(y4, (0, 2, 1, 3)).reshape(bsz, t, SSM_WIDTH)
        h_re, h_im = h_re[:, 0], h_im[:, 0]
    else:
        u4 = jnp.transpose(u.astype(f32), (1, 0, 2))[None]
        y4, h_re, h_im = s5_scan(u4, flat(h0_re)[None], flat(h0_im)[None], a, bcat, ccat, d_skip, False)
        y = jnp.transpose(y4[0], (1, 0, 2))
        h_re, h_im = h_re[0], h_im[0]
    st = lambda h: h.reshape(bsz, SSM_GROUPS, SSM_STATE).astype(u.dtype)
    return y, st(h_re), st(h_im)


def moe_residual(groups, lw, final_gain):
    sizes = [g[0].shape[0] for g in groups]
    hp = lax.optimization_barrier(jnp.concatenate([g[1] for g in groups], axis=0))
    logits = jnp.concatenate([g[2] for g in groups], axis=0)
    yb, dest, gate = moe_routed(hp, logits, lw['router_b'], lw['w_gu'], lw['b_gu'], lw['w_down'], lw['b_down'])
    outs, start = [], 0
    for (x2, _, _), n in zip(groups, sizes):
        rows = slice(start, start + n)
        outs.append(combine_norm(x2, [yb[dest[rows, k]] for k in range(TOP_K)], gate[rows], final_gain))
        start += n
    return outs


def kernel(x_prompt, x_sample, cache_nsa_kv, cache_win_kv, state_ssm_re, state_ssm_im, page_table, rel_bias,
           norm_mix, w_in, cmp_pe, cmp_w1, cmp_b1, cmp_w2, cmp_b2, ssm_lam_re, ssm_lam_im, ssm_log_dt,
           ssm_b_re, ssm_b_im, ssm_c_re, ssm_c_im, ssm_d, ssm_w_glu, ssm_b_glu, norm_attn_out, norm_ssm_out,
           w_out, norm_ffn, router_w, router_b, w_gu, b_gu, w_down, b_down, norm_final):
    n_seq, n_pages = page_table.shape
    past_len = n_pages * cache_nsa_kv.shape[2]
    xp, xs = x_prompt, x_sample
    kv_p, win_p, sre_p, sim_p = [], [], [], []
    kv_s, win_s, sre_s, sim_s = [], [], [], []
    for i in range(DEPTH):
        lw = dict(norm_mix=norm_mix[i], w_in=w_in[i], cmp_pe=cmp_pe[i], cmp_w1=cmp_w1[i], cmp_b1=cmp_b1[i],
                  cmp_w2=cmp_w2[i], cmp_b2=cmp_b2[i], lam_re=ssm_lam_re[i], lam_im=ssm_lam_im[i],
                  log_dt=ssm_log_dt[i], b_re=ssm_b_re[i], b_im=ssm_b_im[i], c_re=ssm_c_re[i], c_im=ssm_c_im[i],
                  d_skip=ssm_d[i], w_glu=ssm_w_glu[i], b_glu=ssm_b_glu[i], norm_attn_out=norm_attn_out[i],
                  norm_ssm_out=norm_ssm_out[i], w_out=w_out[i], norm_ffn=norm_ffn[i], router_w=router_w[i],
                  router_b=router_b[i], w_gu=w_gu[i], b_gu=b_gu[i], w_down=w_down[i], b_down=b_down[i])
        h0 = jnp.zeros((xp.shape[0], SSM_GROUPS, SSM_STATE), xp.dtype)
        gp, kv1, w1, r1, m1 = mixer_layer(xp, 0, None, None, None, h0, h0, rel_bias, lw)
        pool = jnp.transpose(cache_nsa_kv[i].reshape(cache_nsa_kv.shape[1], cache_nsa_kv.shape[2], ROW_LANES), (0, 2, 1))
        gs, kv2, w2, r2, m2 = mixer_layer(xs, past_len, pool, page_table, cache_win_kv[i], state_ssm_re[i], state_ssm_im[i], rel_bias, lw)
        op, os_ = moe_residual([gp, gs], lw, norm_final if i == DEPTH - 1 else None)
        xp, xs = op.reshape(xp.shape), os_.reshape(xs.shape)
        kv_p.append(kv1); win_p.append(w1); sre_p.append(r1); sim_p.append(m1)
        kv_s.append(kv2); win_s.append(w2); sre_s.append(r2); sim_s.append(m2)
    return (xp, xs, jnp.stack(kv_p), jnp.stack(win_p), jnp.stack(sre_p), jnp.stack(sim_p),
            jnp.stack(kv_s), jnp.stack(win_s), jnp.stack(sre_s), jnp.stack(sim_s))
```

```python
import functools
import math
import jax, jax.numpy as jnp
from jax import lax
from jax.experimental import pallas as pl
from jax.experimental.pallas import tpu as pltpu

D_MODEL = 2048
DEPTH = 1
NSA_HEADS = 16
NSA_KV_HEADS = 2
HEAD_DIM = 64
NSA_WIDTH = NSA_HEADS * HEAD_DIM
KV_WIDTH = NSA_KV_HEADS * HEAD_DIM
N_PAGED = 4
N_KV = 6
SSM_WIDTH = D_MODEL - NSA_WIDTH
SSM_CH = 16
SSM_GROUPS = SSM_WIDTH // SSM_CH
SSM_STATE = 64
IN_WIDTH = NSA_WIDTH + N_KV * KV_WIDTH + 3 * NSA_HEADS + SSM_WIDTH
CMP_BLOCK = 32
CMP_STRIDE = 16
CMP_HIDDEN = 2 * HEAD_DIM
SEL_BLOCK = 64
SEL_TOP_N = 16
WINDOW = 512
FORCE_SCORE = 1e6
REL_BUCKETS = 32
REL_MAX_DIST = 4096
N_EXPERTS = 32
TOP_K = 4
D_FF = D_MODEL
SWIGLU_LIMIT = 7.0
SWIGLU_ALPHA = 1.702
RMS_EPS = 1e-5
NEG_INF = -1e30


GRP = NSA_HEADS // NSA_KV_HEADS
TQ = 128
TK = 128
SEL_TILES = 4
WIN_TILES = 2
LANES = 128
BUCKET_TABLE_LEN = 32768
BIAS_TILE_ELEMS = 16384
VMEM_LIMIT = 56 * 1024 * 1024


def rel_bucket(dist):
    n = jnp.maximum(dist, 0)
    exact = REL_BUCKETS // 2
    nf = jnp.maximum(n, 1).astype(jnp.float32)
    large = exact + (jnp.log(nf / exact) / math.log(REL_MAX_DIST / exact) * (REL_BUCKETS - exact)).astype(jnp.int32)
    return jnp.where(n < exact, n, jnp.minimum(large, REL_BUCKETS - 1))


def bucket_thresholds():
    tab = rel_bucket(jnp.arange(BUCKET_TABLE_LEN, dtype=jnp.int32))
    return jnp.sum(tab[None, :] < jnp.arange(REL_BUCKETS, dtype=jnp.int32)[:, None], axis=1).astype(jnp.int32)


def _bias_table_body(thr_ref, rb_ref, d_ref, o_ref):
    n = jnp.maximum(d_ref[0], 0)
    for h in range(NSA_HEADS):
        val = jnp.full(n.shape, rb_ref[h], jnp.float32)
        for k in range(1, REL_BUCKETS):
            val = jnp.where(n >= thr_ref[k], rb_ref[k * NSA_HEADS + h], val)
        o_ref[h // GRP, 0, h % GRP] = val


def bias_tables(dist, thr, rel_bias):
    n, r, c = dist.shape
    ct = min(c, BIAS_TILE_ELEMS // r)
    assert c % ct == 0
    return pl.pallas_call(
        _bias_table_body,
        grid_spec=pltpu.PrefetchScalarGridSpec(
            num_scalar_prefetch=2,
            grid=(n, c // ct),
            in_specs=[pl.BlockSpec((1, r, ct), lambda i, j, *_: (i, 0, j))],
            out_specs=pl.BlockSpec((NSA_KV_HEADS, 1, GRP, r, ct), lambda i, j, *_: (0, i, 0, 0, j)),
        ),
        out_shape=jax.ShapeDtypeStruct((NSA_KV_HEADS, n, GRP, r, c), jnp.float32),
        name="bias_tables",
    )(thr, rel_bias.reshape(-1), dist)


def _nsa_prefill_body(q_ref, g_ref, kc_ref, vct_ref, ks_ref, vst_ref, kw_ref, vwt_ref, bt_ref, bc_ref, cov_ref, exp_ref,
                      o_ref, mask_ref, *, n_cmp, n_sel):
    f32, bf16 = jnp.float32, jnp.bfloat16
    qi = pl.program_id(2)
    nt_dims = (((1,), (1,)), ((), ()))
    q = (q_ref[0] * (HEAD_DIM ** -0.5)).reshape(GRP * TQ, HEAD_DIM).astype(bf16)
    lanes = [slice(g * TQ, (g + 1) * TQ) for g in range(GRP)]
    row = lax.broadcasted_iota(jnp.int32, (LANES, TQ), 0)
    qpos = qi * TQ + lax.broadcasted_iota(jnp.int32, (LANES, TQ), 1)

    ok_c = (qpos >= CMP_STRIDE * row + (CMP_BLOCK - 1)) & (row < n_cmp)
    s_c = lax.dot_general(kc_ref[0, 0].astype(bf16), q, nt_dims, preferred_element_type=f32)
    p_sum = jnp.zeros((LANES, TQ), f32)
    p_parts = []
    for g in range(GRP):
        p = _masked_softmax_rows(s_c[:, lanes[g]] + bc_ref[0, 0, g], ok_c)
        p_sum = p_sum + p
        p_parts.append(p.astype(bf16))
    o_c = jnp.dot(vct_ref[0, 0].astype(bf16), jnp.concatenate(p_parts, axis=1), preferred_element_type=f32)

    imp = jnp.dot(cov_ref[...], p_sum.astype(bf16), preferred_element_type=f32)
    rs = -(-n_sel // 8) * 8
    blk = lax.broadcasted_iota(jnp.int32, (rs, TQ), 0)
    qp = qi * TQ + lax.broadcasted_iota(jnp.int32, (rs, TQ), 1)
    cur = qp // SEL_BLOCK
    forced = (blk == 0) | (blk == cur) | (blk == cur - 1)
    score = jnp.where(forced, FORCE_SCORE, jnp.where(blk * SEL_BLOCK <= qp, imp[:rs], -1.0))
    score = jnp.where(blk < n_sel, score, -jnp.inf)
    sel = jnp.zeros((rs, TQ), f32)
    for s in range(n_sel):
        r = jnp.max(jnp.where(blk == s, score, -jnp.inf), axis=0, keepdims=True)
        beats = (score > r) | ((score == r) & (blk < s))
        rank = jnp.sum(jnp.where(beats, 1.0, 0.0), axis=0, keepdims=True)
        sel = jnp.where((blk == s) & (rank < min(SEL_TOP_N, n_sel)), 1.0, sel)
    sel = jnp.concatenate([sel, jnp.zeros((LANES - rs, TQ), f32)], axis=0)
    mask_ref[...] = jnp.dot(exp_ref[...], sel.astype(bf16), preferred_element_type=f32)

    init = (jnp.full((1, GRP * TQ), NEG_INF, f32), jnp.zeros((1, GRP * TQ), f32), jnp.zeros((HEAD_DIM, GRP * TQ), f32))
    n_diag = bt_ref.shape[1]

    def branch(k_ref, vt_ref, n_tiles, first_tile, keep):
        n_keys = n_tiles * TK
        key_i = lax.broadcasted_iota(jnp.int32, (n_keys, TQ), 0)
        qry_i = lax.broadcasted_iota(jnp.int32, (n_keys, TQ), 1)

        def step(kk, carry):
            m, l, acc = carry
            keys = pl.ds(pl.multiple_of(kk * n_keys, n_keys), n_keys)
            msk = keep(qi * TQ + qry_i - (kk * n_keys + key_i), keys)
            s_all = lax.dot_general(k_ref[0, 0, keys, :].astype(bf16), q, nt_dims, preferred_element_type=f32)
            diag = [jnp.clip(qi - (n_tiles * kk + j), 0, n_diag - 1) for j in range(n_tiles)]
            m_parts, l_parts, a_parts, p_parts = [], [], [], []
            for g in range(GRP):
                bias = jnp.concatenate([bt_ref[0, dg, g] for dg in diag], axis=0)
                s = jnp.where(msk, s_all[:, lanes[g]] + bias, NEG_INF)
                m_new = jnp.maximum(m[:, lanes[g]], jnp.max(s, axis=0, keepdims=True))
                p = jnp.where(msk, jnp.exp(s - m_new), 0.0)
                alpha = jnp.exp(m[:, lanes[g]] - m_new)
                m_parts.append(m_new)
                a_parts.append(alpha)
                l_parts.append(alpha * l[:, lanes[g]] + jnp.sum(p, axis=0, keepdims=True))
                p_parts.append(p.astype(bf16))
            pv = jnp.dot(vt_ref[0, 0, :, keys].astype(bf16), jnp.concatenate(p_parts, axis=1), preferred_element_type=f32)
            return (jnp.concatenate(m_parts, axis=1), jnp.concatenate(l_parts, axis=1),
                    jnp.concatenate(a_parts, axis=1) * acc + pv)

        _, l, acc = lax.fori_loop(first_tile // n_tiles, qi // n_tiles + 1, step, init)
        return acc / jnp.maximum(l, 1e-30)

    o_s = branch(ks_ref, vst_ref, SEL_TILES, 0, lambda dist, keys: (mask_ref[keys, :] > 0.5) & (dist >= 0))
    o_w = branch(kw_ref, vwt_ref, WIN_TILES, jnp.maximum(qi - WINDOW // TK, 0),
                 lambda dist, keys: (dist >= 0) & (dist < WINDOW))
    heads = []
    for g in range(GRP):
        gate = jax.nn.sigmoid(g_ref[0, 0, :, g, :])
        heads.append(gate[0:1] * o_c[:, lanes[g]] + gate[1:2] * o_s[:, lanes[g]] + gate[2:3] * o_w[:, lanes[g]])
    o_ref[0] = jnp.concatenate(heads, axis=0).T


def _masked_softmax_rows(s, ok):
    s = jnp.where(ok, s, NEG_INF)
    e = jnp.where(ok, jnp.exp(s - jnp.max(s, axis=0, keepdims=True)), 0.0)
    return e / jnp.maximum(jnp.sum(e, axis=0, keepdims=True), 1e-30)


def nsa_prefill(q, glog, kcvc, kv_new, thr, rel_bias, n_cmp):
    f32, bf16 = jnp.float32, jnp.bfloat16
    b, t = q.shape[:2]
    nq = t // TQ
    n_sel = t // SEL_BLOCK
    assert t % (SEL_TILES * TK) == 0 and t % (WIN_TILES * TK) == 0 and n_sel <= LANES and n_cmp < LANES and kcvc.shape[1] == LANES and TQ == TK == LANES
    q4 = jnp.transpose(q, (0, 2, 1, 3))
    g5 = jnp.transpose(glog.reshape(b, t, NSA_KV_HEADS, GRP, 3), (0, 2, 4, 3, 1))
    cmp_tok = jnp.pad(kcvc[:, 1:], ((0, 0), (0, 1), (0, 0))).reshape(b, LANES, 2, NSA_KV_HEADS, HEAD_DIM)
    kc = jnp.transpose(cmp_tok[:, :, 0], (0, 2, 1, 3))
    vct = jnp.transpose(cmp_tok[:, :, 1], (0, 2, 3, 1))
    rows = lambda n: jnp.transpose(kv_new[:, :, n], (0, 2, 1, 3))
    cols = lambda n: jnp.transpose(kv_new[:, :, n], (0, 2, 3, 1))
    i = jnp.arange(TQ, dtype=jnp.int32)
    d_toep = jnp.arange(nq, dtype=jnp.int32)[:, None, None] * TQ + i[None, None, :] - i[None, :, None]
    c_end = CMP_STRIDE * jnp.arange(LANES, dtype=jnp.int32) + CMP_BLOCK - 1
    d_cmp = jnp.arange(t, dtype=jnp.int32).reshape(nq, 1, TQ) - c_end[None, :, None]
    tabs = bias_tables(jnp.concatenate([d_toep, d_cmp], axis=0), thr, rel_bias)
    s_start = SEL_BLOCK * jnp.arange(LANES)[:, None]
    c_start = CMP_STRIDE * jnp.arange(LANES)[None, :]
    cover = ((c_start < s_start + SEL_BLOCK) & (c_start + CMP_BLOCK > s_start)
             & (jnp.arange(LANES)[None, :] < n_cmp) & (jnp.arange(LANES)[:, None] < n_sel)).astype(bf16)
    expand = (jnp.arange(t)[:, None] // SEL_BLOCK == jnp.arange(LANES)[None, :]).astype(bf16)
    spec = lambda *blk: pl.BlockSpec((1, 1) + blk, lambda h, bb, qq: (bb, h) + (0,) * len(blk))
    out = pl.pallas_call(
        functools.partial(_nsa_prefill_body, n_cmp=n_cmp, n_sel=n_sel),
        grid=(NSA_KV_HEADS, b, nq),
        in_specs=[pl.BlockSpec((1, GRP, TQ, HEAD_DIM), lambda h, bb, qq: (bb, h, qq, 0)),
                  pl.BlockSpec((1, 1, 3, GRP, TQ), lambda h, bb, qq: (bb, h, 0, 0, qq)),
                  spec(LANES, HEAD_DIM), spec(HEAD_DIM, LANES),
                  spec(t, HEAD_DIM), spec(HEAD_DIM, t), spec(t, HEAD_DIM), spec(HEAD_DIM, t),
                  pl.BlockSpec((1, nq, GRP, TK, TQ), lambda h, bb, qq: (h, 0, 0, 0, 0)),
                  pl.BlockSpec((1, 1, GRP, LANES, TQ), lambda h, bb, qq: (h, nq + qq, 0, 0, 0)),
                  pl.BlockSpec((LANES, LANES), lambda h, bb, qq: (0, 0)),
                  pl.BlockSpec((t, LANES), lambda h, bb, qq: (0, 0))],
        out_specs=pl.BlockSpec((1, TQ, GRP * HEAD_DIM), lambda h, bb, qq: (bb, qq, h)),
        out_shape=jax.ShapeDtypeStruct((b, t, NSA_WIDTH), f32),
        scratch_shapes=[pltpu.VMEM((t, TQ), f32)],
        compiler_params=pltpu.CompilerParams(vmem_limit_bytes=VMEM_LIMIT),
        name="nsa_prefill",
    )(q4, g5, kc, vct, rows(2), cols(3), rows(4), cols(5), tabs, tabs, cover, expand)
    return out


S5_GB = 8
S5_CH = S5_GB * SSM_CH
S5_ST = S5_GB * SSM_STATE


def _cmul(a_re, a_im, b_re, b_im):
    return a_re * b_re - a_im * b_im, a_re * b_im + a_im * b_re


def _s5_body(u_ref, h0re_ref, h0im_ref, a_ref, bcat_ref, ccat_ref, d_ref, y_ref, hre_ref, him_ref, xs_ref, hin_ref,
             *, chained):
    f32, bf16 = jnp.float32, jnp.bfloat16
    n_l, n_r = u_ref.shape[1], u_ref.shape[2]
    u2 = u_ref[0].reshape(n_l * n_r, S5_CH)
    xs_ref[...] = jnp.dot(u2.astype(bf16), bcat_ref[0].astype(bf16),
                          preferred_element_type=f32).reshape(n_l, n_r, 2 * S5_ST)
    a_re, a_im = a_ref[0:1, :], a_ref[1:2, :]

    def scan_step(j, carry):
        h_re, h_im, p_re, p_im = carry
        x = xs_ref[j]
        t_re, t_im = _cmul(a_re, a_im, h_re, h_im)
        h_re, h_im = t_re + x[:, :S5_ST], t_im + x[:, S5_ST:]
        xs_ref[j] = jnp.concatenate([h_re, h_im], axis=1)
        return (h_re, h_im) + _cmul(a_re, a_im, p_re, p_im)

    if chained:
        start = (jnp.zeros((n_r, S5_ST), f32), jnp.zeros((n_r, S5_ST), f32))
    else:
        start = (h0re_ref[0], h0im_ref[0])
    ones = (jnp.ones((1, S5_ST), f32), jnp.zeros((1, S5_ST), f32))
    h_re, h_im, al_re, al_im = lax.fori_loop(0, n_l, scan_step, start + ones)

    if chained:
        hin_ref[0:1, :] = jnp.concatenate([h0re_ref[0], h0im_ref[0]], axis=1)

        def chain_step(c, carry):
            z = xs_ref[n_l - 1, pl.ds(c - 1, 1), :]
            t_re, t_im = _cmul(al_re, al_im, *carry)
            n_re, n_im = t_re + z[:, :S5_ST], t_im + z[:, S5_ST:]
            hin_ref[pl.ds(c, 1), :] = jnp.concatenate([n_re, n_im], axis=1)
            return n_re, n_im

        h_re, h_im = lax.fori_loop(1, n_r + 1, chain_step, (h0re_ref[0], h0im_ref[0]))

        def fix_step(j, carry):
            p_re, p_im = carry
            hin = hin_ref[0:n_r, :]
            t_re, t_im = _cmul(p_re, p_im, hin[:, :S5_ST], hin[:, S5_ST:])
            xs_ref[j] = xs_ref[j] + jnp.concatenate([t_re, t_im], axis=1)
            return _cmul(a_re, a_im, p_re, p_im)

        lax.fori_loop(0, n_l, fix_step, (a_re, a_im))

    hre_ref[0] = h_re
    him_ref[0] = h_im
    hs = xs_ref[...].reshape(n_l * n_r, 2 * S5_ST).astype(bf16)
    y = jnp.dot(hs, ccat_ref[0].astype(bf16), preferred_element_type=f32) + d_ref[...] * u2
    y_ref[0] = jax.nn.gelu(y).reshape(n_l, n_r, S5_CH)


def s5_discretize(lam_re, lam_im, log_dt, b_re, b_im, c_re, c_im):
    f32 = jnp.float32
    dt = jnp.exp(log_dt.astype(f32))[:, None]
    lr, li = lam_re.astype(f32), lam_im.astype(f32)
    mag = jnp.exp(lr * dt)
    a_re, a_im = mag * jnp.cos(li * dt), mag * jnp.sin(li * dt)
    den = lr * lr + li * li
    z_re = ((a_re - 1.0) * lr + a_im * li) / den
    z_im = (a_im * lr - (a_re - 1.0) * li) / den
    br, bim = b_re.astype(f32), b_im.astype(f32)
    bb_re = z_re[..., None] * br - z_im[..., None] * bim
    bb_im = z_re[..., None] * bim + z_im[..., None] * br
    ngb = SSM_GROUPS // S5_GB
    eye = jnp.eye(S5_GB, dtype=f32)

    def block_diag(w):
        wd = w[:, :, :, None, :] * eye[None, :, None, :, None]
        return wd.reshape(ngb, S5_GB * w.shape[2], S5_GB * w.shape[3])

    def pack_b(bb):
        return block_diag(jnp.swapaxes(bb.reshape(ngb, S5_GB, SSM_STATE, SSM_CH), 2, 3))

    def pack_c(cc):
        return block_diag(jnp.swapaxes(cc.reshape(ngb, S5_GB, SSM_CH, SSM_STATE), 2, 3))

    a = jnp.stack([a_re.reshape(-1), a_im.reshape(-1)])
    bcat = jnp.concatenate([pack_b(bb_re), pack_b(bb_im)], axis=2)
    ccat = jnp.concatenate([pack_c(c_re.astype(f32)), -pack_c(c_im.astype(f32))], axis=1)
    return a, bcat, ccat


def s5_scan(u4, h0_re, h0_im, a, bcat, ccat, d_skip, chained):
    nb, n_l, n_r, _ = u4.shape
    rh = h0_re.shape[1]
    ngb = SSM_GROUPS // S5_GB
    st_spec = pl.BlockSpec((1, rh, S5_ST), lambda i, j: (i, 0, j))
    return pl.pallas_call(
        functools.partial(_s5_body, chained=chained),
        grid=(nb, ngb),
        in_specs=[pl.BlockSpec((1, n_l, n_r, S5_CH), lambda i, j: (i, 0, 0, j)), st_spec, st_spec,
                  pl.BlockSpec((2, S5_ST), lambda i, j: (0, j)),
                  pl.BlockSpec((1, S5_CH, 2 * S5_ST), lambda i, j: (j, 0, 0)),
                  pl.BlockSpec((1, 2 * S5_ST, S5_CH), lambda i, j: (j, 0, 0)),
                  pl.BlockSpec((1, S5_CH), lambda i, j: (0, j))],
        out_specs=[pl.BlockSpec((1, n_l, n_r, S5_CH), lambda i, j: (i, 0, 0, j)), st_spec, st_spec],
        out_shape=[jax.ShapeDtypeStruct(u4.shape, jnp.float32),
                   jax.ShapeDtypeStruct(h0_re.shape, jnp.float32), jax.ShapeDtypeStruct(h0_re.shape, jnp.float32)],
        scratch_shapes=[pltpu.VMEM((n_l, n_r, 2 * S5_ST), jnp.float32), pltpu.VMEM((n_r + 8, 2 * S5_ST), jnp.float32)],
        compiler_params=pltpu.CompilerParams(vmem_limit_bytes=VMEM_LIMIT),
        name="s5_scan",
    )(u4, h0_re, h0_im, a, bcat, ccat, d_skip.reshape(1, SSM_WIDTH))


S5_CHUNK = 64


ROW_LANES = N_PAGED * KV_WIDTH
CHUNKS_PER_PAGE = 128 // CMP_STRIDE
CMP_HID2 = NSA_KV_HEADS * CMP_HIDDEN
CMP_PAGES = 64


def _compress_body(pt_ref, *refs, n_g, transposed):
    f32, bf16 = jnp.float32, jnp.bfloat16
    page_refs = (refs[:n_g], refs[n_g:2 * n_g])
    w1_ref, c1_ref, w2_ref, b2_ref, o_ref, carry_ref, rows_ref = refs[2 * n_g:]
    m_rows = n_g * CHUNKS_PER_PAGE
    first = pl.program_id(1) == 0
    row_id = lax.broadcasted_iota(jnp.int32, (m_rows, CMP_HID2), 0)
    outs = []
    for t in range(2):
        for k, r in enumerate(page_refs[t]):
            rows_ref[t, k * 128:(k + 1) * 128, :] = r[0].T if transposed else r[0]
        cols = [rows_ref[t, pl.ds(j, m_rows, stride=CMP_STRIDE), :] for j in range(CMP_STRIDE)]
        x = jnp.concatenate(cols, axis=1).astype(bf16)
        part = jnp.dot(x, w1_ref[t], preferred_element_type=f32)
        p0, p1 = part[:, :CMP_HID2], part[:, CMP_HID2:]
        prev = jnp.where(first, 0.0, carry_ref[t, 0:1, :])
        shifted = jnp.where(row_id == 0, prev, pltpu.roll(p0, 1, axis=0))
        carry_ref[t, 0:1, :] = p0[m_rows - 1:m_rows, :]
        h1 = (c1_ref[t] + shifted) + p1
        outs.append(jnp.dot(jax.nn.gelu(h1).astype(bf16), w2_ref[t], preferred_element_type=f32) + b2_ref[t])
    o_ref[0] = jnp.concatenate(outs, axis=1)


def compress_pages(pool, page_table, cmp_pe, cmp_w1, cmp_b1, cmp_w2, cmp_b2, n_g, transposed):
    n_seq, n_pages = page_table.shape
    assert n_pages % n_g == 0 and pool.shape[1:] == ((ROW_LANES, 128) if transposed else (128, ROW_LANES))
    f32, bf16 = jnp.float32, jnp.bfloat16
    r = CMP_BLOCK // CMP_STRIDE
    eye = jnp.eye(NSA_KV_HEADS, dtype=f32)
    w1 = cmp_w1.astype(f32).reshape(2, r, CMP_STRIDE, HEAD_DIM, CMP_HIDDEN)
    w1 = jnp.transpose(w1, (0, 2, 3, 1, 4))[:, :, None, :, :, None, :] * eye[None, None, :, None, None, :, None]
    w1 = w1.reshape(2, CMP_STRIDE * KV_WIDTH, r * CMP_HID2).astype(bf16)
    c1 = jnp.stack([jnp.einsum('ld,ldf->f', cmp_pe[t], cmp_w1[t]) + cmp_b1[t] for t in range(2)])
    c1 = jnp.tile(c1[:, None, :], (1, 1, NSA_KV_HEADS))
    w2 = (cmp_w2.astype(f32)[:, None, :, None, :] * eye[None, :, None, :, None]).reshape(2, CMP_HID2, KV_WIDTH).astype(bf16)
    b2 = jnp.tile(cmp_b2.astype(f32)[:, None, :], (1, 1, NSA_KV_HEADS))
    m_rows = n_g * CHUNKS_PER_PAGE
    page_spec = lambda k, t: pl.BlockSpec((1, 128, KV_WIDTH), lambda b, g, pt: (
        (pt[b * n_pages + g * n_g + k], t, 0) if transposed else (pt[b * n_pages + g * n_g + k], 0, t)))
    full = lambda a: pl.BlockSpec(a.shape, lambda b, g, pt: (0,) * a.ndim)
    return pl.pallas_call(
        functools.partial(_compress_body, n_g=n_g, transposed=transposed),
        grid_spec=pltpu.PrefetchScalarGridSpec(
            num_scalar_prefetch=1,
            grid=(n_seq, n_pages // n_g),
            in_specs=[page_spec(k, t) for t in range(2) for k in range(n_g)] + [full(w1), full(c1), full(w2), full(b2)],
            out_specs=pl.BlockSpec((1, m_rows, 2 * KV_WIDTH), lambda b, g, pt: (b, g, 0)),
            scratch_shapes=[pltpu.VMEM((2, 8, CMP_HID2), f32), pltpu.VMEM((2, n_g * 128, KV_WIDTH), f32)],
        ),
        out_shape=jax.ShapeDtypeStruct((n_seq, n_pages * CHUNKS_PER_PAGE, 2 * KV_WIDTH), f32),
        compiler_params=pltpu.CompilerParams(vmem_limit_bytes=VMEM_LIMIT),
        name="compress_pages",
    )(page_table.reshape(-1).astype(jnp.int32), *([pool] * (2 * n_g)), w1, c1, w2, b2)


SD_PAGES = 64
SD_KEYS = SD_PAGES * 128
SD_SUB = 16
SD_ROWS = NSA_HEADS * 8
WIN_PAD = 640


def _masked_softmax(s, ok):
    s = jnp.where(ok, s, NEG_INF)
    e = jnp.where(ok, jnp.exp(s - jnp.max(s, axis=-1, keepdims=True)), 0.0)
    return e / jnp.maximum(jnp.sum(e, axis=-1, keepdims=True), 1e-30)


def _rows_from_group(a):
    t = a.shape[0] // NSA_KV_HEADS
    a4 = jnp.broadcast_to(a.reshape(NSA_KV_HEADS, 1, t, a.shape[1]), (NSA_KV_HEADS, GRP, t, a.shape[1]))
    return a4.reshape(NSA_KV_HEADS * GRP * t, a.shape[1])


def _nsa_decode_body(pt_ref, *refs, pos0, n_cmp, n_sel, n_new, win_len):
    f32, bf16 = jnp.float32, jnp.bfloat16
    kt_refs, vt_refs = refs[:SD_PAGES], refs[SD_PAGES:2 * SD_PAGES]
    (q_ref, g_ref, kcvc_ref, knew_ref, wh_ref, bc_ref, bs_ref, bn_ref, bw_ref, cov_ref, exp_ref,
     o_ref, sel_ref, oc_ref, m_ref, l_ref, acc_ref, mask_ref) = refs[2 * SD_PAGES:]
    tile, n_tiles = pl.program_id(1), pl.num_programs(1)
    nt_dims = (((1,), (1,)), ((), ()))
    nn_dims = (((1,), (0,)), ((), ()))
    n_tok = SD_ROWS // NSA_HEADS
    q = (q_ref[0] * (HEAD_DIM ** -0.5)).astype(bf16)
    qpos = pos0 + lax.broadcasted_iota(jnp.int32, (SD_ROWS, 1), 0) % n_tok

    @pl.when(tile == 0)
    def _():
        n_c = kcvc_ref.shape[1]
        m_idx = lax.broadcasted_iota(jnp.int32, (SD_ROWS, n_c), 1)
        ok_c = (m_idx >= 1) & (m_idx <= n_cmp) & (qpos >= CMP_STRIDE * m_idx + (CMP_BLOCK - 1 - CMP_STRIDE))
        s_c = lax.dot_general(q, kcvc_ref[0, :, :KV_WIDTH].astype(bf16), nt_dims, preferred_element_type=f32) + bc_ref[...]
        p_c = _masked_softmax(s_c, ok_c)
        oc_ref[...] = jnp.dot(p_c.astype(bf16), kcvc_ref[0, :, KV_WIDTH:].astype(bf16), preferred_element_type=f32)
        p_sum = jnp.sum(p_c.reshape(NSA_KV_HEADS, GRP, n_tok, n_c), axis=1).reshape(NSA_KV_HEADS * n_tok, n_c)
        imp = jnp.dot(p_sum.astype(bf16), cov_ref[...], preferred_element_type=f32)
        n_l = imp.shape[1]
        lane = lax.broadcasted_iota(jnp.int32, (NSA_KV_HEADS * n_tok, n_l), 1)
        qp = pos0 + lax.broadcasted_iota(jnp.int32, (NSA_KV_HEADS * n_tok, n_l), 0) % n_tok
        cur = qp // SEL_BLOCK
        forced = (lane == 0) | (lane == cur) | (lane == cur - 1)
        score = jnp.where(forced, FORCE_SCORE, jnp.where(lane * SEL_BLOCK <= qp, imp, -1.0))
        score = jnp.where(lane < n_sel, score, -jnp.inf)
        sel = jnp.zeros(score.shape, f32)
        for _ in range(min(SEL_TOP_N, n_sel)):
            best = jnp.max(score, axis=-1, keepdims=True)
            lane_f = lane.astype(f32)
            hit = lane_f == jnp.min(jnp.where(score == best, lane_f, float(n_l)), axis=-1, keepdims=True)
            sel = jnp.where(hit, 1.0, sel)
            score = jnp.where(hit, -jnp.inf, score)
        sel_ref[...] = sel
        m_ref[...] = jnp.full(m_ref.shape, NEG_INF, f32)
        l_ref[...] = jnp.zeros(l_ref.shape, f32)
        acc_ref[...] = jnp.zeros(acc_ref.shape, f32)

    all_rows = slice(0, SD_ROWS)

    def online(rows, s, ok, v, v_dims):
        m_old = m_ref[rows, :]
        s = jnp.where(ok, s, NEG_INF)
        m_new = jnp.maximum(m_old, jnp.max(s, axis=-1, keepdims=True))
        p = jnp.where(ok, jnp.exp(s - m_new), 0.0)
        alpha = jnp.exp(m_old - m_new)
        l_ref[rows, :] = alpha * l_ref[rows, :] + jnp.sum(p, axis=-1, keepdims=True)
        acc_ref[rows, :] = alpha * acc_ref[rows, :] + lax.dot_general(p.astype(bf16), v, v_dims, preferred_element_type=f32)
        m_ref[rows, :] = m_new

    sel_tile = sel_ref[:, pl.ds(pl.multiple_of(tile * LANES, LANES), LANES)]
    mask_ref[...] = (jnp.dot(sel_tile.astype(bf16), exp_ref[...], preferred_element_type=f32) - 1.0) * (-NEG_INF)
    sub_keys = SD_SUB * 128
    for sub in range(SD_PAGES // SD_SUB):
        pages = slice(sub * SD_SUB, (sub + 1) * SD_SUB)
        cols = slice(sub * sub_keys, (sub + 1) * sub_keys)
        kt = jnp.concatenate([r[0] for r in kt_refs[pages]], axis=1).astype(bf16)
        vt = jnp.concatenate([r[0] for r in vt_refs[pages]], axis=1).astype(bf16)
        s = jnp.dot(q, kt, preferred_element_type=f32) + bs_ref[:, cols] + _rows_from_group(mask_ref[:, cols])
        m_old = m_ref[...]
        m_new = jnp.maximum(m_old, jnp.max(s, axis=-1, keepdims=True))
        p = jnp.exp(s - m_new)
        alpha = jnp.exp(m_old - m_new)
        l_ref[...] = alpha * l_ref[...] + jnp.sum(p, axis=-1, keepdims=True)
        acc_ref[...] = alpha * acc_ref[...] + lax.dot_general(p.astype(bf16), vt, nt_dims, preferred_element_type=f32)
        m_ref[...] = m_new

    @pl.when(tile == n_tiles - 1)
    def _():
        lane = lax.broadcasted_iota(jnp.int32, (SD_ROWS, LANES), 1)
        new_blk = pos0 // SEL_BLOCK
        sel_new = _rows_from_group(sel_ref[:, new_blk:new_blk + 1]) > 0.5
        ok_n = sel_new & (lane < n_new) & (pos0 + lane <= qpos)
        s_n = lax.dot_general(q, knew_ref[0, :, :KV_WIDTH].astype(bf16), nt_dims, preferred_element_type=f32) + bn_ref[...]
        online(all_rows, s_n, ok_n, knew_ref[0, :, KV_WIDTH:].astype(bf16), nn_dims)
        o_s = acc_ref[...] / jnp.maximum(l_ref[...], 1e-30)
        j = lax.broadcasted_iota(jnp.int32, (SD_ROWS, WIN_PAD), 1)
        kwpos = pos0 - win_len + j
        dist = qpos - kwpos
        ok_w = (dist >= 0) & (dist < WINDOW) & (kwpos >= 0) & (j < win_len + n_new)
        s_w = lax.dot_general(q, wh_ref[0, :, :KV_WIDTH].astype(bf16), nt_dims, preferred_element_type=f32) + bw_ref[...]
        o_w = jnp.dot(_masked_softmax(s_w, ok_w).astype(bf16), wh_ref[0, :, KV_WIDTH:].astype(bf16), preferred_element_type=f32)
        g = jax.nn.sigmoid(g_ref[0])
        o = g[:, 0:1] * oc_ref[...] + g[:, 1:2] * o_s + g[:, 2:3] * o_w
        row = lax.broadcasted_iota(jnp.int32, (SD_ROWS, HEAD_DIM), 0)
        o_ref[0] = jnp.where(row < SD_ROWS // NSA_KV_HEADS, o[:, :HEAD_DIM], o[:, HEAD_DIM:])


def nsa_decode(q, glog, kv_new, kcvc, pool_t, page_table, cache_win, thr, rel_bias, pos0):
    f32, bf16 = jnp.float32, jnp.bfloat16
    b, t = q.shape[:2]
    n_pages = page_table.shape[1]
    win_len = cache_win.shape[1]
    assert t * NSA_HEADS == SD_ROWS and pos0 == n_pages * 128 and n_pages % SD_PAGES == 0 and pos0 % SEL_BLOCK == 0
    assert win_len + t <= WIN_PAD and t <= SEL_BLOCK and SD_KEYS == LANES * SEL_BLOCK
    n_rows = pos0 + t
    n_cmp = (n_rows - CMP_BLOCK) // CMP_STRIDE + 1
    n_sel = -(-n_rows // SEL_BLOCK)
    n_c = kcvc.shape[1]
    sel_lanes = -(-n_sel // LANES) * LANES
    q5 = jnp.transpose(q.reshape(b, t, NSA_KV_HEADS, GRP, HEAD_DIM), (0, 2, 3, 1, 4))
    qz = (q5[:, :, :, :, None, :] * jnp.eye(NSA_KV_HEADS, dtype=f32)[None, :, None, None, :, None]).reshape(b, SD_ROWS, KV_WIDTH)
    g3 = jnp.transpose(glog.reshape(b, t, NSA_HEADS, 3), (0, 2, 1, 3)).reshape(b, SD_ROWS, 3)
    knew = jnp.pad(kv_new[:, :, 2:4].reshape(b, t, 2 * KV_WIDTH), ((0, 0), (0, LANES - t), (0, 0)))
    whist = jnp.concatenate([cache_win.reshape(b, win_len, 2 * KV_WIDTH), kv_new[:, :, 4:6].reshape(b, t, 2 * KV_WIDTH),
                             jnp.zeros((b, WIN_PAD - win_len - t, 2 * KV_WIDTH), f32)], axis=1)
    qp = pos0 + jnp.arange(t, dtype=jnp.int32)[:, None]
    tab = lambda dist: bias_tables(dist[None], thr, rel_bias).reshape(SD_ROWS, dist.shape[1])
    bias_c = tab(qp - (CMP_STRIDE * jnp.arange(n_c, dtype=jnp.int32)[None, :] + CMP_BLOCK - 1 - CMP_STRIDE))
    bias_s = tab(qp - jnp.arange(pos0, dtype=jnp.int32)[None, :])
    bias_n = tab(qp - (pos0 + jnp.arange(LANES, dtype=jnp.int32)[None, :]))
    bias_w = tab(qp - (pos0 - win_len + jnp.arange(WIN_PAD, dtype=jnp.int32)[None, :]))
    m_idx = jnp.arange(n_c)[:, None]
    c_start = CMP_STRIDE * (m_idx - 1)
    s_start = SEL_BLOCK * jnp.arange(sel_lanes)[None, :]
    cover = ((c_start < s_start + SEL_BLOCK) & (c_start + CMP_BLOCK > s_start) & (m_idx >= 1) & (m_idx <= n_cmp)
             & (jnp.arange(sel_lanes)[None, :] < n_sel)).astype(bf16)
    expand = (jnp.arange(SD_KEYS)[None, :] // SEL_BLOCK == jnp.arange(LANES)[:, None]).astype(bf16)
    page_spec = lambda k, blk: pl.BlockSpec((1, KV_WIDTH, 128), lambda bb, g, pt: (pt[bb * n_pages + g * SD_PAGES + k], blk, 0))
    per_seq = lambda a: pl.BlockSpec((1,) + a.shape[1:], lambda bb, g, pt: (bb,) + (0,) * (a.ndim - 1))
    full = lambda a: pl.BlockSpec(a.shape, lambda bb, g, pt: (0,) * a.ndim)
    kv_rows = NSA_KV_HEADS * t
    out = pl.pallas_call(
        functools.partial(_nsa_decode_body, pos0=pos0, n_cmp=n_cmp, n_sel=n_sel, n_new=t, win_len=win_len),
        grid_spec=pltpu.PrefetchScalarGridSpec(
            num_scalar_prefetch=1,
            grid=(b, n_pages // SD_PAGES),
            in_specs=[page_spec(k, blk) for blk in (2, 3) for k in range(SD_PAGES)]
            + [per_seq(qz), per_seq(g3), per_seq(kcvc), per_seq(knew), per_seq(whist), full(bias_c),
               pl.BlockSpec((SD_ROWS, SD_KEYS), lambda bb, g, pt: (0, g)), full(bias_n), full(bias_w), full(cover), full(expand)],
            out_specs=pl.BlockSpec((1, SD_ROWS, HEAD_DIM), lambda bb, g, pt: (bb, 0, 0)),
            scratch_shapes=[pltpu.VMEM((kv_rows, sel_lanes), f32), pltpu.VMEM((SD_ROWS, KV_WIDTH), f32),
                            pltpu.VMEM((SD_ROWS, 1), f32), pltpu.VMEM((SD_ROWS, 1), f32), pltpu.VMEM((SD_ROWS, KV_WIDTH), f32),
                            pltpu.VMEM((kv_rows, SD_KEYS), f32)],
        ),
        out_shape=jax.ShapeDtypeStruct((b, SD_ROWS, HEAD_DIM), f32),
        compiler_params=pltpu.CompilerParams(vmem_limit_bytes=VMEM_LIMIT),
        name="nsa_decode",
    )(page_table.reshape(-1).astype(jnp.int32), *([pool_t] * (2 * SD_PAGES)), qz, g3, kcvc, knew, whist,
      bias_c, bias_s, bias_n, bias_w, cover, expand)
    out = jnp.transpose(out.reshape(b, NSA_KV_HEADS, GRP, t, HEAD_DIM), (0, 3, 1, 2, 4))
    return out.reshape(b, t, NSA_WIDTH)


MOE_ROWS = 1280
MOE_SUB = 64
MOE_PATHS = (256, 1024, 1088, 1152, 1216, MOE_ROWS)
MOE_CHUNK = 256
MOE_TF = 256
MOE_COLS = 256


def _moe_body(e_ref, sub_ref, blk_ref, ok_ref, x_ref, wg_ref, wl_ref, bg_ref, bl_ref, wd_ref, bd_ref, o_ref, xs_ref):
    bf16 = jnp.bfloat16
    i, f = pl.program_id(0), pl.program_id(1)
    n_sub = sub_ref[i]
    d = o_ref.shape[1]

    @pl.when(f == 0)
    def _():
        o_ref[...] = jnp.broadcast_to(bd_ref[0], o_ref.shape)

    @pl.when((f == 0) & (n_sub > 0))
    def _():
        for j in range(MOE_ROWS // MOE_CHUNK):
            rows = slice(j * MOE_CHUNK, (j + 1) * MOE_CHUNK)
            w = lax.bitcast_convert_type(x_ref[rows, :], jnp.uint32)
            hi = lax.bitcast_convert_type(w & jnp.uint32(0xFFFF0000), jnp.float32)
            lo = lax.bitcast_convert_type(w << 16, jnp.float32)
            xs_ref[rows, :] = jnp.concatenate([hi, lo], axis=1).astype(bf16)

    def expert_rows(n_rows):
        x = xs_ref[0:n_rows, :]
        hg = jnp.dot(x, wg_ref[0].astype(bf16), preferred_element_type=jnp.float32) + bg_ref[0]
        hl = jnp.dot(x, wl_ref[0].astype(bf16), preferred_element_type=jnp.float32) + bl_ref[0]
        hg = jnp.minimum(hg, SWIGLU_LIMIT)
        hl = jnp.clip(hl, -SWIGLU_LIMIT, SWIGLU_LIMIT)
        act = (hg * jax.nn.sigmoid(SWIGLU_ALPHA * hg) * (hl + 1.0)).astype(bf16)
        for c in range(d // MOE_COLS):
            cols = slice(c * MOE_COLS, (c + 1) * MOE_COLS)
            o_ref[0:n_rows, cols] += jnp.dot(act, wd_ref[0, :, cols].astype(bf16), preferred_element_type=jnp.float32)

    below = 0
    for path_rows in MOE_PATHS:
        @pl.when((n_sub * MOE_SUB > below) & (n_sub * MOE_SUB <= path_rows))
        def _(path_rows=path_rows):
            expert_rows(path_rows)

        below = path_rows


def moe_experts(item_e, item_sub, item_blk, item_ok, xb, w_gu, b_gu, w_down, b_down):
    n_items = item_e.shape[0]
    n_f = D_FF // MOE_TF
    d = 2 * xb.shape[1]

    def col(f, ok, i):
        return f * ok[i] + (n_f - 1) * (1 - ok[i])

    return pl.pallas_call(
        _moe_body,
        grid_spec=pltpu.PrefetchScalarGridSpec(
            num_scalar_prefetch=4,
            grid=(n_items, n_f),
            in_specs=[pl.BlockSpec((MOE_ROWS, d // 2), lambda i, f, e, s, b, ok: (b[i], 0)),
                      pl.BlockSpec((1, d, MOE_TF), lambda i, f, e, s, b, ok: (e[i], 0, col(f, ok, i))),
                      pl.BlockSpec((1, d, MOE_TF), lambda i, f, e, s, b, ok: (e[i], 0, n_f + col(f, ok, i))),
                      pl.BlockSpec((1, 1, MOE_TF), lambda i, f, e, s, b, ok: (e[i], 0, col(f, ok, i))),
                      pl.BlockSpec((1, 1, MOE_TF), lambda i, f, e, s, b, ok: (e[i], 0, n_f + col(f, ok, i))),
                      pl.BlockSpec((1, MOE_TF, d), lambda i, f, e, s, b, ok: (e[i], col(f, ok, i), 0)),
                      pl.BlockSpec((1, 1, d), lambda i, f, e, s, b, ok: (e[i], 0, 0))],
            out_specs=pl.BlockSpec((MOE_ROWS, d), lambda i, f, e, s, b, ok: (i, 0)),
            scratch_shapes=[pltpu.VMEM((MOE_ROWS, d), jnp.bfloat16)],
        ),
        out_shape=jax.ShapeDtypeStruct((n_items * MOE_ROWS, d), jnp.float32),
        compiler_params=pltpu.CompilerParams(vmem_limit_bytes=VMEM_LIMIT),
        name="moe_experts",
    )(item_e, item_sub, item_blk, item_ok, xb, w_gu, w_gu, b_gu[:, None, :], b_gu[:, None, :], w_down, b_down[:, None, :])


ROW_TILE = 256
PROJ_TILE = 1024
PROJ_ROWS = 512


def _rms(v, g):
    return v * lax.rsqrt(jnp.mean(v * v, axis=-1, keepdims=True) + RMS_EPS) * g


def _norm_proj_body(x_ref, g_ref, w_ref, o_ref, h_ref):
    @pl.when(pl.program_id(1) == 0)
    def _():
        h_ref[...] = _rms(x_ref[...], g_ref[...]).astype(jnp.bfloat16)

    o_ref[...] = jnp.dot(h_ref[...], w_ref[...], preferred_element_type=jnp.float32)


def norm_proj(x, gain, w):
    n, d = x.shape
    cols = w.shape[1]
    tm = min(n, PROJ_ROWS)
    return pl.pallas_call(
        _norm_proj_body,
        grid=(n // tm, cols // PROJ_TILE),
        in_specs=[pl.BlockSpec((tm, d), lambda i, j: (i, 0)),
                  pl.BlockSpec((1, d), lambda i, j: (0, 0)),
                  pl.BlockSpec((d, PROJ_TILE), lambda i, j: (0, j))],
        out_specs=pl.BlockSpec((tm, PROJ_TILE), lambda i, j: (i, j)),
        out_shape=jax.ShapeDtypeStruct((n, cols), jnp.float32),
        scratch_shapes=[pltpu.VMEM((tm, d), jnp.bfloat16)],
        compiler_params=pltpu.CompilerParams(vmem_limit_bytes=VMEM_LIMIT),
        name="norm_proj",
    )(x, gain.reshape(1, d), w)


def _mix_out_body(attn_ref, y_ref, x_ref, wglu_ref, bglu_ref, ga_ref, gs_ref, wout_ref, gf_ref, wr_ref,
                  x2_ref, hp_ref, lg_ref):
    f32, bf16 = jnp.float32, jnp.bfloat16
    y = y_ref[...]
    ssm = y * jax.nn.sigmoid(jnp.dot(y.astype(bf16), wglu_ref[...], preferred_element_type=f32) + bglu_ref[...])
    merged = jnp.concatenate([_rms(attn_ref[...], ga_ref[...]), _rms(ssm, gs_ref[...])], axis=1).astype(bf16)
    x2 = x_ref[...] + jnp.dot(merged, wout_ref[...], preferred_element_type=f32)
    x2_ref[...] = x2
    hm = _rms(x2, gf_ref[...]).astype(bf16)
    lg_ref[...] = jnp.dot(hm, wr_ref[...], preferred_element_type=f32)
    bits = lax.bitcast_convert_type(hm.astype(f32), jnp.uint32)
    half = bits.shape[1] // 2
    hp_ref[...] = lax.bitcast_convert_type(bits[:, :half] | (bits[:, half:] >> 16), f32)


def mix_out(attn, y, x, lw):
    n = x.shape[0]
    f32, bf16 = jnp.float32, jnp.bfloat16
    row = lambda a: a.astype(f32).reshape(1, -1)
    wr = jnp.pad(lw['router_w'], ((0, 0), (0, LANES - N_EXPERTS))).astype(bf16)
    consts = [lw['w_glu'].astype(bf16), row(lw['b_glu']), row(lw['norm_attn_out']), row(lw['norm_ssm_out']),
              lw['w_out'].astype(bf16), row(lw['norm_ffn']), wr]
    tile = lambda c: pl.BlockSpec((ROW_TILE, c), lambda i: (i, 0))
    full = lambda a: pl.BlockSpec(a.shape, lambda i: (0, 0))
    return pl.pallas_call(
        _mix_out_body,
        grid=(n // ROW_TILE,),
        in_specs=[tile(NSA_WIDTH), tile(SSM_WIDTH), tile(D_MODEL)] + [full(c) for c in consts],
        out_specs=[tile(D_MODEL), tile(D_MODEL // 2), tile(LANES)],
        out_shape=[jax.ShapeDtypeStruct((n, D_MODEL), f32), jax.ShapeDtypeStruct((n, D_MODEL // 2), f32),
                   jax.ShapeDtypeStruct((n, LANES), f32)],
        compiler_params=pltpu.CompilerParams(vmem_limit_bytes=VMEM_LIMIT),
        name="mix_out",
    )(attn, y, x, *consts)


def _combine_body(x_ref, y0_ref, y1_ref, y2_ref, y3_ref, gate_ref, g_ref, o_ref, *, normed):
    gate = gate_ref[...]
    moe = gate[:, 0:1] * y0_ref[...]
    for k, y_ref in enumerate((y1_ref, y2_ref, y3_ref), start=1):
        moe = moe + gate[:, k:k + 1] * y_ref[...]
    out = x_ref[...] + moe
    o_ref[...] = _rms(out, g_ref[...]) if normed else out


def combine_norm(x, ys, gate, gain):
    n, d = x.shape
    normed = gain is not None
    gain = jnp.ones((d,), jnp.float32) if gain is None else gain
    tile = lambda c: pl.BlockSpec((ROW_TILE, c), lambda i: (i, 0))
    return pl.pallas_call(
        functools.partial(_combine_body, normed=normed),
        grid=(n // ROW_TILE,),
        in_specs=[tile(d)] * (1 + TOP_K) + [tile(TOP_K), pl.BlockSpec((1, d), lambda i: (0, 0))],
        out_specs=tile(d),
        out_shape=jax.ShapeDtypeStruct((n, d), jnp.float32),
        name="combine_norm",
    )(x, *ys, gate, gain.astype(jnp.float32).reshape(1, d))


def moe_routed(hp, logits, router_b, w_gu, b_gu, w_down, b_down):
    n = hp.shape[0]
    top_val, top_idx = lax.top_k(logits[:, :N_EXPERTS] + router_b.astype(jnp.float32), TOP_K)
    gate = jax.nn.softmax(top_val, axis=-1)
    nk = n * TOP_K
    n_items = N_EXPERTS + nk // MOE_ROWS
    flat_e = top_idx.reshape(nk)
    onehot = (flat_e[:, None] == jnp.arange(N_EXPERTS, dtype=flat_e.dtype)[None, :]).astype(jnp.int32)
    running = jnp.cumsum(onehot, axis=0)
    counts = running[-1]
    pos_in_e = jnp.sum(onehot * running, axis=1) - 1
    items_e = (counts + MOE_ROWS - 1) // MOE_ROWS
    item_end = jnp.cumsum(items_e)
    item_start = item_end - items_e
    total = item_end[-1]
    dest = (item_start[flat_e] * MOE_ROWS + pos_in_e).astype(jnp.int32)
    row_tok = jnp.zeros(n_items * MOE_ROWS, jnp.int32).at[dest].set(jnp.arange(nk, dtype=jnp.int32) // TOP_K)
    item = jnp.minimum(jnp.arange(n_items, dtype=jnp.int32), total - 1)
    item_e = jnp.minimum(jnp.searchsorted(item_end, item, side='right'), N_EXPERTS - 1).astype(jnp.int32)
    rows = jnp.clip(counts[item_e] - (item - item_start[item_e]) * MOE_ROWS, 0, MOE_ROWS)
    item_sub = jnp.where(jnp.arange(n_items) < total, (rows + MOE_SUB - 1) // MOE_SUB, 0).astype(jnp.int32)
    item_ok = (jnp.arange(n_items) < total).astype(jnp.int32)
    yb = moe_experts(item_e, item_sub, item.astype(jnp.int32), item_ok, hp[row_tok], w_gu, b_gu, w_down, b_down)
    return yb, dest.reshape(n, TOP_K), gate


def mixer_layer(x, pos0, pool, page_table, win_buf, h0_re, h0_im, rel_bias, lw):
    b, t, _ = x.shape
    w_in = jnp.pad(lw['w_in'], ((0, 0), (0, -IN_WIDTH % PROJ_TILE))).astype(jnp.bfloat16)
    proj = norm_proj(x.reshape(b * t, D_MODEL), lw['norm_mix'].astype(jnp.float32), w_in).reshape(b, t, -1)
    o1 = NSA_WIDTH
    o2 = o1 + N_KV * KV_WIDTH
    o3 = o2 + 3 * NSA_HEADS
    q = proj[..., :o1].reshape(b, t, NSA_HEADS, HEAD_DIM)
    kv_new = proj[..., o1:o2].reshape(b, t, N_KV, NSA_KV_HEADS, HEAD_DIM)
    u = proj[..., o3:IN_WIDTH]
    paged_new = kv_new[:, :, :N_PAGED]
    win_new = kv_new[:, :, N_PAGED:]
    n_rows = pos0 + t
    n_cmp = (n_rows - CMP_BLOCK) // CMP_STRIDE + 1
    thr = bucket_thresholds()
    cmp_w = (lw['cmp_pe'], lw['cmp_w1'], lw['cmp_b1'], lw['cmp_w2'], lw['cmp_b2'])
    if pool is None:
        assert pos0 == 0 and t % 128 == 0
        pages = t // 128
        own_pool = proj[..., o1:o1 + ROW_LANES].reshape(b * pages, 128, ROW_LANES)
        own_table = jnp.arange(b * pages, dtype=jnp.int32).reshape(b, pages)
        kcvc = compress_pages(own_pool, own_table, *cmp_w, n_g=pages, transposed=False)
        attn = nsa_prefill(q, proj[..., o2:o3], kcvc, kv_new, thr, rel_bias, n_cmp).astype(x.dtype)
        w_hist = win_new
    else:
        kcvc = compress_pages(pool, page_table, *cmp_w, n_g=CMP_PAGES, transposed=True)
        attn = nsa_decode(q, proj[..., o2:o3], kv_new, kcvc, pool, page_table, win_buf, thr, rel_bias, pos0).astype(x.dtype)
        w_hist = jnp.concatenate([win_buf, win_new], axis=1)
    new_win = w_hist[:, w_hist.shape[1] - min(WINDOW, n_rows):]
    y, h_re, h_im = s5_branch(u, h0_re, h0_im, lw, pool is None)
    x2, hp, logits = mix_out(attn.reshape(b * t, NSA_WIDTH), y.reshape(b * t, SSM_WIDTH), x.reshape(b * t, D_MODEL), lw)
    return (x2, hp, logits), paged_new, new_win, h_re, h_im


def s5_branch(u, h0_re, h0_im, lw, chained):
    bsz, t, _ = u.shape
    f32 = jnp.float32
    a, bcat, ccat = s5_discretize(lw['lam_re'], lw['lam_im'], lw['log_dt'], lw['b_re'], lw['b_im'], lw['c_re'], lw['c_im'])
    d_skip = lw['d_skip'].astype(f32)
    flat = lambda h: h.astype(f32).reshape(bsz, SSM_GROUPS * SSM_STATE)
    if chained:
        n_r = t // S5_CHUNK
        u4 = jnp.transpose(u.astype(f32).reshape(bsz, n_r, S5_CHUNK, SSM_WIDTH), (0, 2, 1, 3))
        y4, h_re, h_im = s5_scan(u4, flat(h0_re)[:, None], flat(h0_im)[:, None], a, bcat, ccat, d_skip, True)
        y = jnp.transpose(y4, (0, 2, 1, 3)).reshape(bsz, t, SSM_WIDTH)
        h_re, h_im = h_re[:, 0], h_im[:, 0]
    else:
        u4 = jnp.transpose(u.astype(f32), (1, 0, 2))[None]
        y4, h_re, h_im = s5_scan(u4, flat(h0_re)[None], flat(h0_im)[None], a, bcat, ccat, d_skip, False)
        y = jnp.transpose(y4[0], (1, 0, 2))
        h_re, h_im = h_re[0], h_im[0]
    st = lambda h: h.reshape(bsz, SSM_GROUPS, SSM_STATE).astype(u.dtype)
    return y, st(h_re), st(h_im)


def moe_residual(groups, lw, final_gain):
    sizes = [g[0].shape[0] for g in groups]
    hp = lax.optimization_barrier(jnp.concatenate([g[1] for g in groups], axis=0))
    logits = jnp.concatenate([g[2] for g in groups], axis=0)
    yb, dest, gate = moe_routed(hp, logits, lw['router_b'], lw['w_gu'], lw['b_gu'], lw['w_down'], lw['b_down'])
    outs, start = [], 0
    for (x2, _, _), n in zip(groups, sizes):
        rows = slice(start, start + n)
        outs.append(combine_norm(x2, [yb[dest[rows, k]] for k in range(TOP_K)], gate[rows], final_gain))
        start += n
    return outs


def kernel(x_prompt, x_sample, cache_nsa_kv, cache_win_kv, state_ssm_re, state_ssm_im, page_table, rel_bias,
           norm_mix, w_in, cmp_pe, cmp_w1, cmp_b1, cmp_w2, cmp_b2, ssm_lam_re, ssm_lam_im, ssm_log_dt,
           ssm_b_re, ssm_b_im, ssm_c_re, ssm_c_im, ssm_d, ssm_w_glu, ssm_b_glu, norm_attn_out, norm_ssm_out,
           w_out, norm_ffn, router_w, router_b, w_gu, b_gu, w_down, b_down, norm_final):
    n_seq, n_pages = page_table.shape
    past_len = n_pages * cache_nsa_kv.shape[2]
    xp, xs = x_prompt, x_sample
    kv_p, win_p, sre_p, sim_p = [], [], [], []
    kv_s, win_s, sre_s, sim_s = [], [], [], []
    for i in range(DEPTH):
        lw = dict(norm_mix=norm_mix[i], w_in=w_in[i], cmp_pe=cmp_pe[i], cmp_w1=cmp_w1[i], cmp_b1=cmp_b1[i],
                  cmp_w2=cmp_w2[i], cmp_b2=cmp_b2[i], lam_re=ssm_lam_re[i], lam_im=ssm_lam_im[i],
                  log_dt=ssm_log_dt[i], b_re=ssm_b_re[i], b_im=ssm_b_im[i], c_re=ssm_c_re[i], c_im=ssm_c_im[i],
                  d_skip=ssm_d[i], w_glu=ssm_w_glu[i], b_glu=ssm_b_glu[i], norm_attn_out=norm_attn_out[i],
                  norm_ssm_out=norm_ssm_out[i], w_out=w_out[i], norm_ffn=norm_ffn[i], router_w=router_w[i],
                  router_b=router_b[i], w_gu=w_gu[i], b_gu=b_gu[i], w_down=w_down[i], b_down=b_down[i])
        h0 = jnp.zeros((xp.shape[0], SSM_GROUPS, SSM_STATE), xp.dtype)
        gp, kv1, w1, r1, m1 = mixer_layer(xp, 0, None, None, None, h0, h0, rel_bias, lw)
        pool = jnp.transpose(cache_nsa_kv[i].reshape(cache_nsa_kv.shape[1], cache_nsa_kv.shape[2], ROW_LANES), (0, 2, 1))
        gs, kv2, w2, r2, m2 = mixer_layer(xs, past_len, pool, page_table, cache_win_kv[i], state_ssm_re[i], state_ssm_im[i], rel_bias, lw)
        op, os_ = moe_residual([gp, gs], lw, norm_final if i == DEPTH - 1 else None)
        xp, xs = op.reshape(xp.shape), os_.reshape(xs.shape)
        kv_p.append(kv1); win_p.append(w1); sre_p.append(r1); sim_p.append(m1)
        kv_s.append(kv2); win_s.append(w2); sre_s.append(r2); sim_s.append(m2)
    return (xp, xs, jnp.stack(kv_p), jnp.stack(win_p), jnp.stack(sre_p), jnp.stack(sim_p),
            jnp.stack(kv_s), jnp.stack(win_s), jnp.stack(sre_s), jnp.stack(sim_s))
```
